```python
import math
import jax, jax.numpy as jnp
from jax import lax
import numpy as np

D_MODEL = 1024
BATCH = 16
SEQ = 256
DEPTH = 2
DEC_BATCH = 4
DEC_SEQ = 1024
PAST_LEN = 256

GRID_W = 64
ATTN_HEADS = 4
ATTN_HD = 64
ATTN_VD = 2 * ATTN_HD
ATTN_W = ATTN_HEADS * 2 * ATTN_HD
N_ROPE_FREQ = ATTN_HD // 4
ROPE_BASE = 10000.0
CONV_W = 512
CONV_K = 3
RET_HEADS = 4
RET_HD = 128
RET_W = RET_HEADS * RET_HD
RET_CHUNK = 128
N_BRANCH = 3
N_EXPERTS = 16
EC_CAPACITY_FACTOR = 2
EXPERT_FF = 2048
N_MOD = 6
EPS = 1e-6
IN_SIZES = [ATTN_W, ATTN_W, ATTN_W, CONV_W, CONV_W, CONV_W, RET_W, RET_W, RET_W, RET_W, N_BRANCH * D_MODEL]
IN_COLS = sum(IN_SIZES)
SPLIT_IDX = [int(v) for v in np.cumsum(IN_SIZES)[:-1]]

kernel_name = "hybrid_diffattn_conv_retnet_ec_dit_step"


def rmsnorm(x, w):
    x32 = x.astype(jnp.float32)
    n = x32 * lax.rsqrt(jnp.mean(x32 * x32, axis=-1, keepdims=True) + EPS)
    return (n * w.astype(jnp.float32)).astype(x.dtype)


def rope_1d(x, ang):
    n = x.shape[-1] // 2
    x1, x2 = x[..., :n], x[..., n:]
    cos = jnp.cos(ang).astype(x.dtype)
    sin = jnp.sin(ang).astype(x.dtype)
    return jnp.concatenate([x1 * cos - x2 * sin, x1 * sin + x2 * cos], axis=-1)


def axial_rope(x, ang_r, ang_c):
    half = x.shape[-1] // 2
    return jnp.concatenate([rope_1d(x[..., :half], ang_r), rope_1d(x[..., half:], ang_c)], axis=-1)


def short_conv(u, w):
    up = jnp.pad(u, ((0, 0), (1, 1), (0, 0)))
    return up[:, :-2] * w[0] + up[:, 1:-1] * w[1] + up[:, 2:] * w[2]


def retention_scan(q, k, v, log_gamma, s0):
    b, h, t, d = q.shape
    n = t // RET_CHUNK
    f32 = jnp.float32

    def to_chunks(a):
        return jnp.moveaxis(a.astype(f32).reshape(b, h, n, RET_CHUNK, d), 2, 0)

    qc, kc, vc = to_chunks(q), to_chunks(k), to_chunks(v)
    lg = log_gamma.astype(f32)
    idx = jnp.arange(RET_CHUNK, dtype=f32)
    diff = idx[:, None] - idx[None, :]
    dmask = jnp.where(diff >= 0, jnp.exp(lg[:, None, None] * jnp.maximum(diff, 0.0)), 0.0)
    q_decay = jnp.exp(lg[:, None] * (idx + 1.0))[:, :, None]
    k_decay = jnp.exp(lg[:, None] * (RET_CHUNK - 1.0 - idx))[:, :, None]
    chunk_decay = jnp.exp(lg * RET_CHUNK)[:, None, None]

    def step(s, inp):
        qi, ki, vi = inp
        inner = jnp.einsum('bhid,bhjd->bhij', qi, ki) * dmask
        o = jnp.einsum('bhij,bhjv->bhiv', inner, vi) + jnp.einsum('bhid,bhdv->bhiv', qi * q_decay, s)
        s = s * chunk_decay + jnp.einsum('bhjd,bhjv->bhdv', ki * k_decay, vi)
        return s, o

    s_final, o = lax.scan(step, s0.astype(f32), (qc, kc, vc))
    o = jnp.moveaxis(o, 0, 2).reshape(b, h, t, d)
    return o.astype(q.dtype), s_final


def expert_choice_ffn(tok, w_router, w_gate, w_up, w_down):
    n = tok.shape[0]
    cap = EC_CAPACITY_FACTOR * n // N_EXPERTS
    aff = jax.nn.softmax((tok @ w_router).astype(jnp.float32), axis=-1)
    g, idx = lax.top_k(aff.T, cap)
    xe = tok[idx]
    hdn = jax.nn.silu(jnp.einsum('ecd,edf->ecf', xe, w_gate)) * jnp.einsum('ecd,edf->ecf', xe, w_up)
    ye = jnp.einsum('ecf,efd->ecd', hdn, w_down) * g[..., None].astype(tok.dtype)
    return jnp.zeros_like(tok).at[idx.reshape(-1)].add(ye.reshape(-1, tok.shape[1]))


def trunk_layer(x, mod, lam_init, ang, ctx, norm1_w, norm2_w, w_in, attn_lambda, attn_subln_w, conv_w,
                ret_decay_fwd, ret_decay_bwd, ret_norm_w, w_br_attn, w_br_conv, w_br_ret, w_out,
                w_router, w_exp_gate, w_exp_up, w_exp_down):
    b, t, _ = x.shape
    shift1, scale1, gate1, shift2, scale2, gate2 = [mod[:, i][:, None, :] for i in range(N_MOD)]

    h = rmsnorm(x, norm1_w) * (1.0 + scale1) + shift1
    z = h @ w_in
    aq, ak, av, cb, cc, cx, rq, rk, rv, rg, mg = jnp.split(z, SPLIT_IDX, axis=-1)

    q = aq.reshape(b, t, ATTN_HEADS, 2, ATTN_HD).transpose(0, 2, 1, 3, 4)
    k = ak.reshape(b, t, ATTN_HEADS, 2, ATTN_HD).transpose(0, 2, 1, 3, 4)
    v = av.reshape(b, t, ATTN_HEADS, ATTN_VD).transpose(0, 2, 1, 3)
    if ctx is None:
        k_all, v_all = k, v
    else:
        ctx_k, ctx_v, s0_f, s0_b = ctx
        ang_r, ang_c = ang
        q = axial_rope(q, ang_r, ang_c)
        k = axial_rope(k, ang_r, ang_c)
        ck = ctx_k.reshape(b, ATTN_HEADS, ctx_k.shape[2], 2, ATTN_HD)
        k_all = jnp.concatenate([ck.astype(k.dtype), k], axis=2)
        v_all = jnp.concatenate([ctx_v.astype(v.dtype), v], axis=2)
    lv = attn_lambda.astype(jnp.float32)
    lam = jnp.exp(jnp.sum(lv[0] * lv[1])) - jnp.exp(jnp.sum(lv[2] * lv[3])) + lam_init
    s = jnp.einsum('bhqcd,bhkcd->bhcqk', q, k_all).astype(jnp.float32) * (ATTN_HD ** -0.5)
    p = jax.nn.softmax(s, axis=-1)
    a = p[:, :, 0] - lam * p[:, :, 1]
    o = jnp.einsum('bhqk,bhkv->bhqv', a.astype(v_all.dtype), v_all)
    o = rmsnorm(o, attn_subln_w) * (1.0 - lam_init)
    attn_o = o.transpose(0, 2, 1, 3).reshape(b, t, ATTN_W)

    conv_o = cb * short_conv(cc * cx, conv_w)

    rq_h = rq.reshape(b, t, RET_HEADS, RET_HD).transpose(0, 2, 1, 3)
    rk_h = rk.reshape(b, t, RET_HEADS, RET_HD).transpose(0, 2, 1, 3) * (RET_HD ** -0.5)
    rv_h = rv.reshape(b, t, RET_HEADS, RET_HD).transpose(0, 2, 1, 3)
    lg_f = jax.nn.log_sigmoid(ret_decay_fwd.astype(jnp.float32))
    lg_b = jax.nn.log_sigmoid(ret_decay_bwd.astype(jnp.float32))
    if ctx is None:
        s0_f = jnp.zeros((b, RET_HEADS, RET_HD, RET_HD), jnp.float32)
        s0_b = jnp.zeros((b, RET_HEADS, RET_HD, RET_HD), jnp.float32)
    o_f, s_f = retention_scan(rq_h, rk_h, rv_h, lg_f, s0_f)
    o_b, s_b = retention_scan(jnp.flip(rq_h, 2), jnp.flip(rk_h, 2), jnp.flip(rv_h, 2), lg_b, s0_b)
    ro = o_f + jnp.flip(o_b, 2)
    ro = rmsnorm(ro, ret_norm_w.reshape(RET_HEADS, 1, RET_HD))
    ret_o = jax.nn.silu(rg) * ro.transpose(0, 2, 1, 3).reshape(b, t, RET_W)

    g = jax.nn.sigmoid(mg.reshape(b, t, N_BRANCH, D_MODEL))
    merged = (g[:, :, 0] * (attn_o @ w_br_attn) + g[:, :, 1] * (conv_o @ w_br_conv)
              + g[:, :, 2] * (ret_o @ w_br_ret))
    x = x + gate1 * (merged @ w_out)

    h2 = rmsnorm(x, norm2_w) * (1.0 + scale2) + shift2
    y = expert_choice_ffn(h2.reshape(b * t, D_MODEL), w_router, w_exp_gate, w_exp_up, w_exp_down)
    x = x + gate2 * y.reshape(b, t, D_MODEL)

    if ctx is None:
        new_k = k.reshape(b, ATTN_HEADS, t, 2 * ATTN_HD)
        return x, (new_k, v, s_f.astype(x.dtype), s_b.astype(x.dtype))
    return x, None


def setup_inputs(seed: int = 0) -> dict:
    key = jax.random.key(seed)
    ks = jax.random.split(key, 32)
    f32 = jnp.float32
    nrm = lambda k, shape, s: jax.random.normal(k, shape, f32) * s
    eps = 2.0 ** (-5.0 - jnp.arange(RET_HEADS, dtype=f32))
    decay_logit = jnp.log1p(-eps) - jnp.log(eps)
    return {
        'x_prompt': nrm(ks[0], (BATCH, SEQ, D_MODEL), 1.0),
        'x_sample': nrm(ks[1], (DEC_BATCH, DEC_SEQ, D_MODEL), 1.0),
        'c': nrm(ks[2], (DEC_BATCH, D_MODEL), 1.0),
        'cache_attn_k': nrm(ks[3], (DEC_BATCH, DEPTH, ATTN_HEADS, PAST_LEN, 2 * ATTN_HD), 1.0),
        'cache_attn_v': nrm(ks[4], (DEC_BATCH, DEPTH, ATTN_HEADS, PAST_LEN, ATTN_VD), 1.0),
        'state_ret_fwd': nrm(ks[5], (DEC_BATCH, DEPTH, RET_HEADS, RET_HD, RET_HD), 1.0),
        'state_ret_bwd': nrm(ks[6], (DEC_BATCH, DEPTH, RET_HEADS, RET_HD, RET_HD), 1.0),
        'c_ctx': nrm(ks[7], (D_MODEL,), 1.0),
        'w_ada': nrm(ks[8], (DEPTH, D_MODEL, N_MOD * D_MODEL), D_MODEL ** -0.5),
        'b_ada': nrm(ks[9], (DEPTH, N_MOD * D_MODEL), 0.01),
        'norm1_w': 1.0 + nrm(ks[10], (DEPTH, D_MODEL), 0.01),
        'norm2_w': 1.0 + nrm(ks[11], (DEPTH, D_MODEL), 0.01),
        'w_in': nrm(ks[12], (DEPTH, D_MODEL, IN_COLS), D_MODEL ** -0.5),
        'attn_lambda': nrm(ks[13], (DEPTH, 4, ATTN_HD), 0.1),
        'attn_subln_w': 1.0 + nrm(ks[14], (DEPTH, ATTN_VD), 0.01),
        'conv_w': nrm(ks[15], (DEPTH, CONV_K, CONV_W), CONV_K ** -0.5),
        'ret_decay_fwd': decay_logit[None, :] + nrm(ks[16], (DEPTH, RET_HEADS), 0.1),
        'ret_decay_bwd': decay_logit[None, :] + nrm(ks[17], (DEPTH, RET_HEADS), 0.1),
        'ret_norm_w': 1.0 + nrm(ks[18], (DEPTH, RET_W), 0.01),
        'w_br_attn': nrm(ks[19], (DEPTH, ATTN_W, D_MODEL), ATTN_W ** -0.5),
        'w_br_conv': nrm(ks[20], (DEPTH, CONV_W, D_MODEL), CONV_W ** -0.5),
        'w_br_ret': nrm(ks[21], (DEPTH, RET_W, D_MODEL), RET_W ** -0.5),
        'w_out': nrm(ks[22], (DEPTH, D_MODEL, D_MODEL), D_MODEL ** -0.5),
        'w_router': nrm(ks[23], (DEPTH, D_MODEL, N_EXPERTS), D_MODEL ** -0.5),
        'w_exp_gate': nrm(ks[24], (DEPTH, N_EXPERTS, D_MODEL, EXPERT_FF), D_MODEL ** -0.5),
        'w_exp_up': nrm(ks[25], (DEPTH, N_EXPERTS, D_MODEL, EXPERT_FF), D_MODEL ** -0.5),
        'w_exp_down': nrm(ks[26], (DEPTH, N_EXPERTS, EXPERT_FF, D_MODEL), EXPERT_FF ** -0.5),
        'final_norm_w': 1.0 + nrm(ks[27], (D_MODEL,), 0.01),
    }


def reference(x_prompt, x_sample, c, cache_attn_k, cache_attn_v, state_ret_fwd, state_ret_bwd, c_ctx,
              w_ada, b_ada, norm1_w, norm2_w, w_in, attn_lambda, attn_subln_w, conv_w, ret_decay_fwd,
              ret_decay_bwd, ret_norm_w, w_br_attn, w_br_conv, w_br_ret, w_out, w_router, w_exp_gate,
              w_exp_up, w_exp_down, final_norm_w):
    n_lat = x_sample.shape[1]
    rows = n_lat // GRID_W
    row = jnp.repeat(jnp.arange(rows), GRID_W).astype(jnp.float32)
    col = jnp.tile(jnp.arange(GRID_W), rows).astype(jnp.float32)
    inv = ROPE_BASE ** (-jnp.arange(N_ROPE_FREQ, dtype=jnp.float32) / N_ROPE_FREQ)
    ang_r = (row[:, None] * inv)[:, None, :]
    ang_c = (col[:, None] * inv)[:, None, :]

    xp, xs = x_prompt, x_sample
    ks_l, vs_l, sf_l, sb_l = [], [], [], []
    for l in range(DEPTH):
        lam_init = 0.8 - 0.6 * math.exp(-0.3 * l)
        params = (norm1_w[l], norm2_w[l], w_in[l], attn_lambda[l], attn_subln_w[l], conv_w[l],
                  ret_decay_fwd[l], ret_decay_bwd[l], ret_norm_w[l], w_br_attn[l], w_br_conv[l],
                  w_br_ret[l], w_out[l], w_router[l], w_exp_gate[l], w_exp_up[l], w_exp_down[l])
        mod_ctx = (jax.nn.silu(c_ctx) @ w_ada[l] + b_ada[l]).reshape(1, N_MOD, D_MODEL)
        mod_lat = (jax.nn.silu(c) @ w_ada[l] + b_ada[l]).reshape(c.shape[0], N_MOD, D_MODEL)
        xp, (k_l, v_l, s_f, s_b) = trunk_layer(xp, mod_ctx, lam_init, None, None, *params)
        ks_l.append(k_l)
        vs_l.append(v_l)
        sf_l.append(s_f)
        sb_l.append(s_b)
        ctx = (cache_attn_k[:, l], cache_attn_v[:, l], state_ret_fwd[:, l], state_ret_bwd[:, l])
        xs, _ = trunk_layer(xs, mod_lat, lam_init, (ang_r, ang_c), ctx, *params)

    y_prompt = rmsnorm(xp, final_norm_w)
    y_sample = rmsnorm(xs, final_norm_w)
    new_attn_k = jnp.stack(ks_l, axis=1)
    new_attn_v = jnp.stack(vs_l, axis=1)
    new_ret_fwd = jnp.stack(sf_l, axis=1)
    new_ret_bwd = jnp.stack(sb_l, axis=1)
    return (y_prompt, y_sample, new_attn_k, new_attn_v, new_ret_fwd, new_ret_bwd)
```

```python
import functools
import math

import jax
import jax.numpy as jnp
import numpy as np
from jax import lax
from jax.experimental import pallas as pl
from jax.experimental.pallas import tpu as pltpu

F32 = jnp.float32
BF16 = jnp.bfloat16

D = 1024
DEPTH = 2
N_CTX_SEQ = 16
T_CTX = 256
N_LAT_SEQ = 4
T_LAT = 1024
PAST = 256
N_CTX = N_CTX_SEQ * T_CTX
N_LAT = N_LAT_SEQ * T_LAT
N_TOK = N_CTX + N_LAT
TM = 256
N_TILES = N_TOK // TM
CTX_TILES = N_CTX // TM
LAT_TILES_PER_SEQ = T_LAT // TM
HEADS = 4
HD = 128
GRID_W = 64
N_ROPE_FREQ = 16
ROPE_BASE = 10000.0
IN_COLS = 8192
N_MOD = 6
N_EXPERTS = 16
CAP = 512
FF = 2048
TF = 512
EPS = 1e-6
RET_SCALE = HD ** -0.5
ATTN_SCALE = 64 ** -0.5
H2W = D + 128
VMEM_LIMIT = 56 * 1024 * 1024

QB, KB, VB = 0, 4, 8
RQB, RKB, RVB, RGB = 24, 28, 32, 36


def _sigmoid(x):
    return 1.0 / (1.0 + jnp.exp(-x))


def _log_sigmoid(x):
    return jnp.minimum(x, 0.0) - jnp.log(1.0 + jnp.exp(-jnp.abs(x)))


def _rms(x):
    return x * lax.rsqrt(jnp.mean(x * x, axis=-1, keepdims=True) + EPS)


def _dot(a, b):
    return jnp.dot(a, b, preferred_element_type=F32)


def _dot_nt(a, b):
    return lax.dot_general(a, b, (((1,), (1,)), ((), ())), preferred_element_type=F32)


def _dot_tn(a, b):
    return lax.dot_general(a, b, (((0,), (0,)), ((), ())), preferred_element_type=F32)


def _mod_row(i):
    return jnp.where(i < CTX_TILES, 0, 1 + (i - CTX_TILES) // LAT_TILES_PER_SEQ)


def _params(n_axes):
    return pltpu.CompilerParams(
        dimension_semantics=("arbitrary",) * n_axes, vmem_limit_bytes=VMEM_LIMIT)


def _mod_kernel(c_ref, w_ref, b_ref, o_ref):
    c = c_ref[...]
    s = (c * _sigmoid(c)).astype(BF16)
    o_ref[...] = _dot(s, w_ref[...].astype(BF16)) + b_ref[...]


def _modulation(cvec, w_ada, b_ada):
    tn = 1024
    return pl.pallas_call(
        _mod_kernel,
        grid=(DEPTH, N_MOD * D // tn),
        in_specs=[
            pl.BlockSpec((8, D), lambda l, n: (0, 0)),
            pl.BlockSpec((None, D, tn), lambda l, n: (l, 0, n)),
            pl.BlockSpec((None, 1, tn), lambda l, n: (l, 0, n)),
        ],
        out_specs=pl.BlockSpec((None, 8, tn), lambda l, n: (l, 0, n)),
        out_shape=jax.ShapeDtypeStruct((DEPTH, 8, N_MOD * D), F32),
        compiler_params=_params(2),
        name="modulation",
    )(cvec, w_ada, b_ada.reshape(DEPTH, 1, N_MOD * D))


def _inproj_kernel(has_y, *refs):
    if has_y:
        x_ref, y_ref, modp_ref, mod_ref, n1_ref, w_ref, zb_ref, zf_ref, xc_ref = refs
        x = x_ref[...] + modp_ref[0][:, 5 * D:6 * D] * y_ref[...]
        xc_ref[...] = x
    else:
        x_ref, mod_ref, n1_ref, w_ref, zb_ref, zf_ref = refs
        x = x_ref[...]
    m = mod_ref[0]
    shift1 = m[:, 0:D]
    scale1 = m[:, D:2 * D]
    h = ((_rms(x) * n1_ref[...]) * (1.0 + scale1) + shift1).astype(BF16)
    cw = 1024
    for c in range(IN_COLS // cw):
        z = _dot(h, w_ref[:, c * cw:(c + 1) * cw])
        zb_ref[:, c * cw:(c + 1) * cw] = z.astype(BF16)
        if c == 0:
            zf_ref[:, 0:cw] = z
        if c == 1:
            zf_ref[:, cw:cw + 512] = z[:, 0:512]


def _inproj(l, x, y, mod, norm1_w, w_in_bf):
    has_y = y is not None
    tile = pl.BlockSpec((TM, D), lambda i: (i, 0))
    in_specs = [tile]
    args = [x]
    if has_y:
        in_specs += [tile, pl.BlockSpec((1, 1, N_MOD * D), lambda i: ((l - 1) * 8 + _mod_row(i), 0, 0))]
        args += [y, mod]
    in_specs += [
        pl.BlockSpec((1, 1, N_MOD * D), lambda i: (l * 8 + _mod_row(i), 0, 0)),
        pl.BlockSpec((None, 1, D), lambda i: (l, 0, 0)),
        pl.BlockSpec((None, D, IN_COLS), lambda i: (l, 0, 0), pipeline_mode=pl.Buffered(1)),
    ]
    args += [mod, norm1_w, w_in_bf]
    out_specs = [pl.BlockSpec((TM, IN_COLS), lambda i: (i, 0)),
                 pl.BlockSpec((TM, 1536), lambda i: (i, 0))]
    out_shape = [jax.ShapeDtypeStruct((N_TOK, IN_COLS), BF16),
                 jax.ShapeDtypeStruct((N_TOK, 1536), F32)]
    if has_y:
        out_specs.append(tile)
        out_shape.append(jax.ShapeDtypeStruct((N_TOK, D), F32))
    return pl.pallas_call(
        functools.partial(_inproj_kernel, has_y),
        grid=(N_TILES,),
        in_specs=in_specs,
        out_specs=out_specs,
        out_shape=out_shape,
        compiler_params=_params(1),
        name="inproj",
    )(*args)


def _lambda(lam_ref, lam_init):
    lv = lam_ref[...]
    a = jnp.sum(lv[0:1] * lv[1:2], axis=-1, keepdims=True)
    b = jnp.sum(lv[2:3] * lv[3:4], axis=-1, keepdims=True)
    return jnp.exp(a) - jnp.exp(b) + lam_init


def _diff_attention(lam_init, q, k_all, v_all, lam, sw):
    lane = lax.broadcasted_iota(jnp.int32, q.shape, 1)
    zero = jnp.zeros_like(q)

    def softmax_map(qm):
        s = _dot_nt(qm, k_all) * ATTN_SCALE
        e = jnp.exp(s - jnp.max(s, axis=-1, keepdims=True))
        return e * (1.0 / jnp.sum(e, axis=-1, keepdims=True))

    a = softmax_map(jnp.where(lane < 64, q, zero)) - lam * softmax_map(jnp.where(lane >= 64, q, zero))
    o = _dot(a.astype(BF16), v_all)
    return (_rms(o) * sw) * (1.0 - lam_init)


def _attn_ctx_kernel(lam_init, q_ref, k_ref, v_ref, lam_ref, sw_ref, o_ref):
    lam = _lambda(lam_ref, lam_init)
    o = _diff_attention(lam_init, q_ref[...], k_ref[...], v_ref[...], lam, sw_ref[...])
    o_ref[...] = o.astype(BF16)


def _rope(x, cos, sin_signed):
    lane = lax.broadcasted_iota(jnp.int32, x.shape, 1)
    partner = jnp.where(lane % 32 < 16, pltpu.roll(x, 112, 1), pltpu.roll(x, 16, 1))
    return x * cos + partner * sin_signed


def _attn_lat_kernel(lam_init, q_ref, k_ref, v_ref, ck_ref, cv_ref, cos_ref, sin_ref,
                     cosq_ref, sinq_ref, lam_ref, sw_ref, o_ref, kall, vall):
    @pl.when(pl.program_id(2) == 0)
    def _():
        kall[0:PAST, :] = ck_ref[...].astype(BF16)
        kall[PAST:, :] = _rope(k_ref[...], cos_ref[...], sin_ref[...]).astype(BF16)
        vall[0:PAST, :] = cv_ref[...].astype(BF16)
        vall[PAST:, :] = v_ref[...]

    q = _rope(q_ref[...], cosq_ref[...], sinq_ref[...]).astype(BF16)
    lam = _lambda(lam_ref, lam_init)
    o = _diff_attention(lam_init, q, kall[...], vall[...], lam, sw_ref[...])
    o_ref[...] = o.astype(BF16)


def _attention(l, lam_init, zb, zf, cache_k, cache_v, cos, sin, attn_lambda, subln_w):
    lam_spec2 = pl.BlockSpec((None, 4, 64), lambda b, h: (l, 0, 0))
    sw_spec2 = pl.BlockSpec((None, 1, HD), lambda b, h: (l, 0, 0))
    ao = pl.pallas_call(
        functools.partial(_attn_ctx_kernel, lam_init),
        grid=(N_CTX_SEQ, HEADS),
        in_specs=[
            pl.BlockSpec((T_CTX, HD), lambda b, h: (b, QB + h)),
            pl.BlockSpec((T_CTX, HD), lambda b, h: (b, KB + h)),
            pl.BlockSpec((T_CTX, HD), lambda b, h: (b, VB + h)),
            lam_spec2, sw_spec2,
        ],
        out_specs=pl.BlockSpec((T_CTX, HD), lambda b, h: (b, h)),
        out_shape=jax.ShapeDtypeStruct((N_CTX, HEADS * HD), BF16),
        compiler_params=_params(2),
        name="attn_ctx",
    )(zb, zb, zb, attn_lambda, subln_w)

    lat_row = lambda b, j: CTX_TILES + LAT_TILES_PER_SEQ * b + j
    seq_row = lambda b: N_CTX // T_LAT + b
    ao_lat = pl.pallas_call(
        functools.partial(_attn_lat_kernel, lam_init),
        grid=(N_LAT_SEQ, HEADS, LAT_TILES_PER_SEQ),
        in_specs=[
            pl.BlockSpec((TM, HD), lambda b, h, j: (lat_row(b, j), QB + h)),
            pl.BlockSpec((T_LAT, HD), lambda b, h, j: (seq_row(b), KB + h)),
            pl.BlockSpec((T_LAT, HD), lambda b, h, j: (seq_row(b), VB + h)),
            pl.BlockSpec((None, None, None, PAST, HD), lambda b, h, j: (b, l, h, 0, 0)),
            pl.BlockSpec((None, None, None, PAST, HD), lambda b, h, j: (b, l, h, 0, 0)),
            pl.BlockSpec((T_LAT, HD), lambda b, h, j: (0, 0)),
            pl.BlockSpec((T_LAT, HD), lambda b, h, j: (0, 0)),
            pl.BlockSpec((TM, HD), lambda b, h, j: (j, 0)),
            pl.BlockSpec((TM, HD), lambda b, h, j: (j, 0)),
            pl.BlockSpec((None, 4, 64), lambda b, h, j: (l, 0, 0)),
            pl.BlockSpec((None, 1, HD), lambda b, h, j: (l, 0, 0)),
        ],
        out_specs=pl.BlockSpec((TM, HD), lambda b, h, j: (LAT_TILES_PER_SEQ * b + j, h)),
        out_shape=jax.ShapeDtypeStruct((N_LAT, HEADS * HD), BF16),
        scratch_shapes=[pltpu.VMEM((PAST + T_LAT, HD), BF16), pltpu.VMEM((PAST + T_LAT, HD), BF16)],
        compiler_params=_params(3),
        name="attn_lat",
    )(zf, zf, zb, cache_k, cache_v, cos, sin, cos, sin, attn_lambda, subln_w)
    return ao, ao_lat


def _decay_matrix(lgf, lgb, row0, tq, tk):
    i = row0 + lax.broadcasted_iota(jnp.int32, (tq, tk), 0)
    j = lax.broadcasted_iota(jnp.int32, (tq, tk), 1)
    d = (i - j).astype(F32)
    fwd = jnp.where(d >= 0.0, jnp.exp(lgf * jnp.maximum(d, 0.0)), 0.0)
    bwd = jnp.where(d <= 0.0, jnp.exp(lgb * jnp.maximum(-d, 0.0)), 0.0)
    return fwd + bwd


def _ret_finish(o, g_ref, nw_ref, o_ref):
    g = g_ref[...].astype(F32)
    o_ref[...] = ((g * _sigmoid(g)) * (_rms(o) * nw_ref[0])).astype(BF16)


def _ret_ctx_kernel(q_ref, k_ref, v_ref, g_ref, df_ref, db_ref, nw_ref, o_ref, sf_ref, sb_ref):
    q = q_ref[...]
    k = k_ref[...]
    v = v_ref[...]
    lgf = _log_sigmoid(df_ref[0])
    lgb = _log_sigmoid(db_ref[0])
    s = (_dot_nt(q, k) * RET_SCALE) * _decay_matrix(lgf, lgb, 0, T_CTX, T_CTX)
    o = _dot(s.astype(BF16), v)
    _ret_finish(o, g_ref, nw_ref, o_ref)
    j = lax.broadcasted_iota(jnp.int32, (T_CTX, 1), 0).astype(F32)
    kf = k.astype(F32) * RET_SCALE
    sf_ref[...] = _dot_tn((kf * jnp.exp(lgf * (T_CTX - 1.0 - j))).astype(BF16), v)
    sb_ref[...] = _dot_tn((kf * jnp.exp(lgb * j)).astype(BF16), v)


def _ret_lat_kernel(q_ref, k_ref, v_ref, g_ref, s0f_ref, s0b_ref, df_ref, db_ref, nw_ref, o_ref):
    row0 = pl.program_id(2) * TM
    q = q_ref[...]
    lgf = _log_sigmoid(df_ref[0])
    lgb = _log_sigmoid(db_ref[0])
    s = (_dot_nt(q, k_ref[...]) * RET_SCALE) * _decay_matrix(lgf, lgb, row0, TM, T_LAT)
    o = _dot(s.astype(BF16), v_ref[...])
    i = (row0 + lax.broadcasted_iota(jnp.int32, (TM, 1), 0)).astype(F32)
    o = o + jnp.exp(lgf * (i + 1.0)) * _dot(q, s0f_ref[...].astype(BF16))
    o = o + jnp.exp(lgb * (T_LAT - i)) * _dot(q, s0b_ref[...].astype(BF16))
    _ret_finish(o, g_ref, nw_ref, o_ref)


def _retention(l, zb, state_f, state_b, decay_f, decay_b, ret_norm_w):
    dec2 = pl.BlockSpec((1, 1, 1), lambda b, h: (l * HEADS + h, 0, 0))
    nw2 = pl.BlockSpec((1, 1, HD), lambda b, h: (l * HEADS + h, 0, 0))
    st_spec = pl.BlockSpec((None, None, HD, HD), lambda b, h: (b, h, 0, 0))
    st_shape = jax.ShapeDtypeStruct((N_CTX_SEQ, HEADS, HD, HD), F32)
    ro, sf, sb = pl.pallas_call(
        _ret_ctx_kernel,
        grid=(N_CTX_SEQ, HEADS),
        in_specs=[
            pl.BlockSpec((T_CTX, HD), lambda b, h: (b, RQB + h)),
            pl.BlockSpec((T_CTX, HD), lambda b, h: (b, RKB + h)),
            pl.BlockSpec((T_CTX, HD), lambda b, h: (b, RVB + h)),
            pl.BlockSpec((T_CTX, HD), lambda b, h: (b, RGB + h)),
            dec2, dec2, nw2,
        ],
        out_specs=[pl.BlockSpec((T_CTX, HD), lambda b, h: (b, h)), st_spec, st_spec],
        out_shape=[jax.ShapeDtypeStruct((N_CTX, HEADS * HD), BF16), st_shape, st_shape],
        compiler_params=_params(2),
        name="ret_ctx",
    )(zb, zb, zb, zb, decay_f, decay_b, ret_norm_w)

    lat_row = lambda b, j: CTX_TILES + LAT_TILES_PER_SEQ * b + j
    seq_row = lambda b: N_CTX // T_LAT + b
    dec3 = pl.BlockSpec((1, 1, 1), lambda b, h, j: (l * HEADS + h, 0, 0))
    nw3 = pl.BlockSpec((1, 1, HD), lambda b, h, j: (l * HEADS + h, 0, 0))
    s0_spec = pl.BlockSpec((None, None, None, HD, HD), lambda b, h, j: (b, l, h, 0, 0))
    ro_lat = pl.pallas_call(
        _ret_lat_kernel,
        grid=(N_LAT_SEQ, HEADS, LAT_TILES_PER_SEQ),
        in_specs=[
            pl.BlockSpec((TM, HD), lambda b, h, j: (lat_row(b, j), RQB + h)),
            pl.BlockSpec((T_LAT, HD), lambda b, h, j: (seq_row(b), RKB + h)),
            pl.BlockSpec((T_LAT, HD), lambda b, h, j: (seq_row(b), RVB + h)),
            pl.BlockSpec((TM, HD), lambda b, h, j: (lat_row(b, j), RGB + h)),
            s0_spec, s0_spec, dec3, dec3, nw3,
        ],
        out_specs=pl.BlockSpec((TM, HD), lambda b, h, j: (LAT_TILES_PER_SEQ * b + j, h)),
        out_shape=jax.ShapeDtypeStruct((N_LAT, HEADS * HD), BF16),
        compiler_params=_params(3),
        name="ret_lat",
    )(zb, zb, zb, zb, state_f, state_b, decay_f, decay_b, ret_norm_w)
    return ro, ro_lat, sf, sb


def _merge_kernel(x_ref, aoc_ref, aol_ref, roc_ref, rol_ref, cb_ref, cc_ref, cx_ref, ccp_ref, cxp_ref,
                  ccn_ref, cxn_ref, mg0_ref, mg1_ref, mg2_ref, mod_ref, cw_ref, wa_ref, wc_ref, wr_ref,
                  wo_ref, n2_ref, wrt_ref, x1_ref, h2_ref, aff_ref):
    i = pl.program_id(0)
    is_ctx = i < CTX_TILES
    ao = jnp.where(is_ctx, aoc_ref[...], aol_ref[...])
    ro = jnp.where(is_ctx, roc_ref[...], rol_ref[...])
    j = (i - CTX_TILES) % LAT_TILES_PER_SEQ
    seq_first = jnp.logical_or(i < CTX_TILES, j == 0)
    seq_last = jnp.logical_or(i < CTX_TILES, j == LAT_TILES_PER_SEQ - 1)

    u = cc_ref[...].astype(F32) * cx_ref[...].astype(F32)
    up = (ccp_ref[...].astype(F32) * cxp_ref[...].astype(F32))[15:16, :] * jnp.where(seq_first, 0.0, 1.0)
    dn = (ccn_ref[...].astype(F32) * cxn_ref[...].astype(F32))[0:1, :] * jnp.where(seq_last, 0.0, 1.0)
    r = lax.broadcasted_iota(jnp.int32, u.shape, 0)
    u_prev = jnp.where(r == 0, up, pltpu.roll(u, 1, 0))
    u_next = jnp.where(r == TM - 1, dn, pltpu.roll(u, TM - 1, 0))
    cw = cw_ref[...]
    conv = u_prev * cw[0:1, :] + u * cw[1:2, :] + u_next * cw[2:3, :]
    conv_o = (cb_ref[...].astype(F32) * conv).astype(BF16)

    merged = _sigmoid(mg0_ref[...].astype(F32)) * _dot(ao, wa_ref[...])
    merged = merged + _sigmoid(mg1_ref[...].astype(F32)) * _dot(conv_o, wc_ref[...])
    merged = merged + _sigmoid(mg2_ref[...].astype(F32)) * _dot(ro, wr_ref[...])

    m = mod_ref[0]
    gate1 = m[:, 2 * D:3 * D]
    shift2 = m[:, 3 * D:4 * D]
    scale2 = m[:, 4 * D:5 * D]
    x1 = x_ref[...] + gate1 * _dot(merged.astype(BF16), wo_ref[...])
    x1_ref[...] = x1
    h2 = (_rms(x1) * n2_ref[...]) * (1.0 + scale2) + shift2
    h2_ref[:, 0:D] = h2

    logits = _dot(h2.astype(BF16), wrt_ref[...])
    lane = lax.broadcasted_iota(jnp.int32, logits.shape, 1)
    valid = lane < N_EXPERTS
    lmax = jnp.max(jnp.where(valid, logits, -jnp.inf), axis=-1, keepdims=True)
    e = jnp.where(valid, jnp.exp(logits - lmax), 0.0)
    aff = e * (1.0 / jnp.sum(e, axis=-1, keepdims=True))
    h2_ref[:, D:H2W] = aff
    aff_ref[...] = aff.T[0:N_EXPERTS, :]


def _merge(l, x, ao, ao_lat, ro, ro_lat, zb, mod, conv_w, w_br_attn, w_br_conv, w_br_ret, w_out, norm2_w,
           w_router_pad):
    n16 = N_TOK // 16
    ctx_br = pl.BlockSpec((TM, 512), lambda i: (jnp.minimum(i, CTX_TILES - 1), 0))
    lat_br = pl.BlockSpec((TM, 512), lambda i: (jnp.maximum(i - CTX_TILES, 0), 0))
    col = lambda c: pl.BlockSpec((TM, 512), lambda i: (i, c))
    halo_p = lambda c: pl.BlockSpec((16, 512), lambda i: (jnp.maximum(i * (TM // 16) - 1, 0), c))
    halo_n = lambda c: pl.BlockSpec((16, 512), lambda i: (jnp.minimum((i + 1) * (TM // 16), n16 - 1), c))
    mgs = lambda c: pl.BlockSpec((TM, D), lambda i: (i, c))
    wbr = pl.BlockSpec((None, 512, D), lambda i: (l, 0, 0))
    return pl.pallas_call(
        _merge_kernel,
        grid=(N_TILES,),
        in_specs=[
            pl.BlockSpec((TM, D), lambda i: (i, 0)),
            ctx_br, lat_br, ctx_br, lat_br,
            col(3), col(4), col(5), halo_p(4), halo_p(5), halo_n(4), halo_n(5),
            mgs(5), mgs(6), mgs(7),
            pl.BlockSpec((1, 1, N_MOD * D), lambda i: (l * 8 + _mod_row(i), 0, 0)),
            pl.BlockSpec((None, 3, 512), lambda i: (l, 0, 0)),
            wbr, wbr, wbr,
            pl.BlockSpec((None, D, D), lambda i: (l, 0, 0)),
            pl.BlockSpec((None, 1, D), lambda i: (l, 0, 0)),
            pl.BlockSpec((None, D, 128), lambda i: (l, 0, 0)),
        ],
        out_specs=[
            pl.BlockSpec((TM, D), lambda i: (i, 0)),
            pl.BlockSpec((TM, H2W), lambda i: (i, 0)),
            pl.BlockSpec((N_EXPERTS, TM), lambda i: (0, i)),
        ],
        out_shape=[
            jax.ShapeDtypeStruct((N_TOK, D), F32),
            jax.ShapeDtypeStruct((N_TOK, H2W), F32),
            jax.ShapeDtypeStruct((N_EXPERTS, N_TOK), F32),
        ],
        compiler_params=_params(1),
        name="merge",
    )(x, ao, ao_lat, ro, ro_lat, zb, zb, zb, zb, zb, zb, zb, zb, zb, zb, mod, conv_w,
      w_br_attn, w_br_conv, w_br_ret, w_out, norm2_w, w_router_pad)


def _cumsum_lanes(x, tri):
    run = jnp.zeros((x.shape[0], 1), F32)
    outs = []
    for b in range(x.shape[1] // 128):
        cs = _dot(x[:, b * 128:(b + 1) * 128].astype(BF16), tri) + run
        run = cs[:, 127:128]
        outs.append(cs)
    return jnp.concatenate(outs, axis=-1)


def _topk_kernel(a_ref, tmat_ref, idx_ref):
    a = a_ref[...]
    n = a.shape[1]
    kf = float(CAP)

    def count_gt(thr):
        return jnp.sum(jnp.where(a > thr, 1.0, 0.0), axis=-1, keepdims=True)

    def span(lo, hi):
        inside = jnp.logical_and(a > lo, a <= hi)
        cmax = jnp.max(jnp.where(inside, a, -jnp.inf), axis=-1, keepdims=True)
        cmin = jnp.min(jnp.where(inside, a, jnp.inf), axis=-1, keepdims=True)
        return cmax, cmin

    def cond(c):
        return jnp.logical_and(c[2] > 0, c[3] < 400)

    def body(c):
        lo, hi, _, it = c
        mid = 0.5 * (lo + hi)
        ge = count_gt(mid) >= kf
        lo = jnp.where(ge, mid, lo)
        hi = jnp.where(ge, hi, mid)
        cmax, cmin = span(lo, hi)
        open_rows = jnp.max(jnp.where(cmax != cmin, 1, 0))
        return lo, hi, open_rows, it + 1

    lo0 = jnp.full((N_EXPERTS, 1), -1.0, F32)
    hi0 = jnp.max(a, axis=-1, keepdims=True)
    cmax0, cmin0 = span(lo0, hi0)
    lo, hi, _, _ = lax.while_loop(
        cond, body, (lo0, hi0, jnp.max(jnp.where(cmax0 != cmin0, 1, 0)), jnp.int32(0)))
    thr, _ = span(lo, hi)

    r = lax.broadcasted_iota(jnp.int32, (128, 128), 0)
    c = lax.broadcasted_iota(jnp.int32, (128, 128), 1)
    tri = jnp.where(r <= c, 1.0, 0.0).astype(BF16)
    gt = a > thr
    eq = jnp.where(a == thr, 1.0, 0.0)
    need = kf - count_gt(thr)
    eq_before = _cumsum_lanes(eq, tri) - eq
    sel = jnp.where(jnp.logical_or(gt, jnp.logical_and(eq > 0.0, eq_before < need)), 1.0, 0.0)
    pos = _cumsum_lanes(sel, tri) - 1.0
    slot = jnp.where(sel > 0.0, pos, -1.0).astype(jnp.int32)

    p_iota = lax.broadcasted_iota(jnp.int32, (CAP, 1024), 0)
    for e in range(N_EXPERTS):
        acc = jnp.zeros((CAP, 128), F32)
        for cb in range(n // 1024):
            onehot = jnp.where(p_iota == slot[e:e + 1, cb * 1024:(cb + 1) * 1024], 1.0, 0.0).astype(BF16)
            acc = acc + _dot(onehot, tmat_ref[cb * 1024:(cb + 1) * 1024, :])
        idx_ref[0, e] = (acc[:, 0:1] * 64.0 + acc[:, 1:2]).astype(jnp.int32)


def _topk(aff_t, tmat):
    return pl.pallas_call(
        _topk_kernel,
        grid=(2,),
        in_specs=[
            pl.BlockSpec((N_EXPERTS, N_CTX), lambda s: (0, s)),
            pl.BlockSpec((N_CTX, 128), lambda s: (0, 0)),
        ],
        out_specs=pl.BlockSpec((1, N_EXPERTS, CAP, 1), lambda s: (s, 0, 0, 0)),
        out_shape=jax.ShapeDtypeStruct((2, N_EXPERTS, CAP, 1), jnp.int32),
        compiler_params=_params(1),
        name="topk",
    )(aff_t, tmat)


def _ffn_kernel(l, idx_ref, h2_hbm, wg_ref, wu_ref, wd_ref, ye_ref, xg, xb, acc, sem):
    e = pl.program_id(0)
    f = pl.program_id(1)

    @pl.when(f == 0)
    def _():
        for s in range(2):
            def issue(p, carry):
                row = idx_ref[(s * N_EXPERTS + e) * CAP + p] + s * N_CTX
                pltpu.make_async_copy(h2_hbm.at[pl.ds(row, 1), :],
                                      xg.at[pl.ds(s * CAP + p, 1), :], sem).start()
                return carry
            lax.fori_loop(0, CAP, issue, 0)
        pltpu.make_async_copy(h2_hbm.at[pl.ds(0, 2 * CAP), :], xg, sem).wait()
        xb[...] = xg[:, 0:D].astype(BF16)
        acc[...] = jnp.zeros_like(acc)

    x = xb[...]
    hg = _dot(x, wg_ref[...].astype(BF16))
    hu = _dot(x, wu_ref[...].astype(BF16))
    hdn = ((hg * _sigmoid(hg)) * hu).astype(BF16)
    acc[...] += _dot(hdn, wd_ref[...].astype(BF16))

    @pl.when(f == FF // TF - 1)
    def _():
        tail = xg[:, D:H2W]
        lane = lax.broadcasted_iota(jnp.int32, tail.shape, 1)
        gate = jnp.sum(jnp.where(lane == e, tail, 0.0), axis=-1, keepdims=True)
        y = acc[...] * gate
        ye_ref[0] = y[0:CAP]
        ye_ref[1] = y[CAP:2 * CAP]


def _expert_ffn(l, idx_flat, h2ext, w_gate, w_up, w_down):
    grid_spec = pltpu.PrefetchScalarGridSpec(
        num_scalar_prefetch=1,
        grid=(N_EXPERTS, FF // TF),
        in_specs=[
            pl.BlockSpec(memory_space=pl.ANY),
            pl.BlockSpec((None, None, D, TF), lambda e, f, idx: (l, e, 0, f)),
            pl.BlockSpec((None, None, D, TF), lambda e, f, idx: (l, e, 0, f)),
            pl.BlockSpec((None, None, TF, D), lambda e, f, idx: (l, e, f, 0)),
        ],
        out_specs=pl.BlockSpec((2, None, CAP, D), lambda e, f, idx: (0, e, 0, 0)),
        scratch_shapes=[
            pltpu.VMEM((2 * CAP, H2W), F32),
            pltpu.VMEM((2 * CAP, D), BF16),
            pltpu.VMEM((2 * CAP, D), F32),
            pltpu.SemaphoreType.DMA,
        ],
    )
    return pl.pallas_call(
        functools.partial(_ffn_kernel, l),
        grid_spec=grid_spec,
        out_shape=jax.ShapeDtypeStruct((2, N_EXPERTS, CAP, D), F32),
        compiler_params=_params(2),
        name="expert_ffn",
    )(idx_flat, h2ext, w_gate, w_up, w_down)


SCATTER_UNROLL = 4


def _combine_kernel(idx_ref, ye_ref, y_ref):
    s = pl.program_id(0)
    e = pl.program_id(1)

    @pl.when(e == 0)
    def _():
        y_ref[...] = jnp.zeros_like(y_ref)

    base = (s * N_EXPERTS + e) * CAP

    def group(g, carry):
        p0 = g * SCATTER_UNROLL
        rows = [idx_ref[base + p0 + u] for u in range(SCATTER_UNROLL)]
        vals = [y_ref[pl.ds(rows[u], 1), :] + ye_ref[pl.ds(p0 + u, 1), :] for u in range(SCATTER_UNROLL)]
        for u in range(SCATTER_UNROLL):
            y_ref[pl.ds(rows[u], 1), :] = vals[u]
        return carry

    lax.fori_loop(0, CAP // SCATTER_UNROLL, group, 0)


def _combine(idx_flat, ye):
    grid_spec = pltpu.PrefetchScalarGridSpec(
        num_scalar_prefetch=1,
        grid=(2, N_EXPERTS),
        in_specs=[pl.BlockSpec((None, None, CAP, D), lambda s, e, idx: (s, e, 0, 0))],
        out_specs=pl.BlockSpec((N_CTX, D), lambda s, e, idx: (s, 0)),
    )
    return pl.pallas_call(
        _combine_kernel,
        grid_spec=grid_spec,
        out_shape=jax.ShapeDtypeStruct((N_TOK, D), F32),
        compiler_params=_params(2),
        name="combine",
    )(idx_flat, ye)


def _final_kernel(x_ref, y_ref, mod_ref, w_ref, o_ref):
    x = x_ref[...] + mod_ref[0][:, 5 * D:6 * D] * y_ref[...]
    o_ref[...] = _rms(x) * w_ref[...]


def _final(x1, y, mod, final_norm_w):
    tile = pl.BlockSpec((TM, D), lambda i: (i, 0))
    return pl.pallas_call(
        _final_kernel,
        grid=(N_TILES,),
        in_specs=[tile, tile,
                  pl.BlockSpec((1, 1, N_MOD * D), lambda i: ((DEPTH - 1) * 8 + _mod_row(i), 0, 0)),
                  pl.BlockSpec((1, D), lambda i: (0, 0))],
        out_specs=tile,
        out_shape=jax.ShapeDtypeStruct((N_TOK, D), F32),
        compiler_params=_params(1),
        name="final_norm",
    )(x1, y, mod, final_norm_w.reshape(1, D))


def _rope_tables():
    t = np.arange(T_LAT)
    row = (t // GRID_W).astype(np.float32)
    col = (t % GRID_W).astype(np.float32)
    inv = jnp.asarray(ROPE_BASE, F32) ** (-jnp.arange(N_ROPE_FREQ, dtype=F32) / N_ROPE_FREQ)
    ang_r = jnp.asarray(row)[:, None] * inv
    ang_c = jnp.asarray(col)[:, None] * inv
    def group(ang):
        return jnp.concatenate([ang, ang], axis=-1)
    ang = jnp.concatenate([group(ang_r), group(ang_c), group(ang_r), group(ang_c)], axis=-1)
    sign = np.where(np.arange(HD) % 32 < 16, -1.0, 1.0).astype(np.float32)
    return jnp.cos(ang), jnp.sin(ang) * sign


def _index_table():
    t = np.arange(N_CTX)
    tm = np.zeros((N_CTX, 128), np.float32)
    tm[:, 0] = t // 64
    tm[:, 1] = t % 64
    return jnp.asarray(tm, BF16)


def kernel(x_prompt, x_sample, c, cache_attn_k, cache_attn_v, state_ret_fwd, state_ret_bwd, c_ctx, w_ada, b_ada, norm1_w, norm2_w, w_in, attn_lambda, attn_subln_w, conv_w, ret_decay_fwd, ret_decay_bwd, ret_norm_w, w_br_attn, w_br_conv, w_br_ret, w_out, w_router, w_exp_gate, w_exp_up, w_exp_down, final_norm_w):
    x = jnp.concatenate([x_prompt.reshape(N_CTX, D), x_sample.reshape(N_LAT, D)], axis=0)
    cvec = jnp.concatenate([c_ctx[None, :], c, jnp.zeros((3, D), F32)], axis=0)
    mod = _modulation(cvec, w_ada, b_ada).reshape(DEPTH * 8, 1, N_MOD * D)

    w_in_bf = w_in.astype(BF16)
    w_br_attn_bf = w_br_attn.astype(BF16)
    w_br_conv_bf = w_br_conv.astype(BF16)
    w_br_ret_bf = w_br_ret.astype(BF16)
    w_out_bf = w_out.astype(BF16)
    w_router_pad = jnp.pad(w_router, ((0, 0), (0, 0), (0, 128 - N_EXPERTS))).astype(BF16)
    norm1 = norm1_w.reshape(DEPTH, 1, D)
    norm2 = norm2_w.reshape(DEPTH, 1, D)
    subln = attn_subln_w.reshape(DEPTH, 1, HD)
    decay_f = ret_decay_fwd.reshape(DEPTH * HEADS, 1, 1)
    decay_b = ret_decay_bwd.reshape(DEPTH * HEADS, 1, 1)
    ret_nw = ret_norm_w.reshape(DEPTH * HEADS, 1, HD)
    cos, sin = _rope_tables()
    tmat = _index_table()

    y = None
    ks, vs, sfs, sbs = [], [], [], []
    for l in range(DEPTH):
        lam_init = 0.8 - 0.6 * math.exp(-0.3 * l)
        if y is None:
            zb, zf = _inproj(l, x, None, mod, norm1, w_in_bf)
        else:
            zb, zf, x = _inproj(l, x, y, mod, norm1, w_in_bf)
        ao, ao_lat = _attention(l, lam_init, zb, zf, cache_attn_k, cache_attn_v, cos, sin, attn_lambda, subln)
        ro, ro_lat, sf, sb = _retention(l, zb, state_ret_fwd, state_ret_bwd, decay_f, decay_b, ret_nw)
        x, h2ext, aff_t = _merge(l, x, ao, ao_lat, ro, ro_lat, zb, mod, conv_w, w_br_attn_bf, w_br_conv_bf,
                                 w_br_ret_bf, w_out_bf, norm2, w_router_pad)
        idx_flat = _topk(aff_t, tmat).reshape(2 * N_EXPERTS * CAP)
        ye = _expert_ffn(l, idx_flat, h2ext, w_exp_gate, w_exp_up, w_exp_down)
        y = _combine(idx_flat, ye)

        kv = zf[:N_CTX].reshape(N_CTX_SEQ, T_CTX, 3, HEADS, HD)
        ks.append(kv[:, :, 1].transpose(0, 2, 1, 3))
        vs.append(kv[:, :, 2].transpose(0, 2, 1, 3))
        sfs.append(sf)
        sbs.append(sb)

    out = _final(x, y, mod, final_norm_w)
    y_prompt = out[:N_CTX].reshape(N_CTX_SEQ, T_CTX, D)
    y_sample = out[N_CTX:].reshape(N_LAT_SEQ, T_LAT, D)
    return (y_prompt, y_sample, jnp.stack(ks, axis=1), jnp.stack(vs, axis=1),
            jnp.stack(sfs, axis=1), jnp.stack(sbs, axis=1))
```

```python
import functools
import math

import jax
import jax.numpy as jnp
import numpy as np
from jax import lax
from jax.experimental import pallas as pl
from jax.experimental.pallas import tpu as pltpu

F32 = jnp.float32
BF16 = jnp.bfloat16

D = 1024
DEPTH = 2
N_CTX_SEQ = 16
T_CTX = 256
N_LAT_SEQ = 4
T_LAT = 1024
PAST = 256
N_CTX = N_CTX_SEQ * T_CTX
N_LAT = N_LAT_SEQ * T_LAT
N_TOK = N_CTX + N_LAT
TM = 256
N_TILES = N_TOK // TM
CTX_TILES = N_CTX // TM
LAT_TILES_PER_SEQ = T_LAT // TM
HEADS = 4
HD = 128
GRID_W = 64
N_ROPE_FREQ = 16
ROPE_BASE = 10000.0
IN_COLS = 8192
N_MOD = 6
N_EXPERTS = 16
CAP = 512
FF = 2048
TF = 512
EPS = 1e-6
RET_SCALE = HD ** -0.5
ATTN_SCALE = 64 ** -0.5
H2W = D + 128
VMEM_LIMIT = 56 * 1024 * 1024

QB, KB, VB = 0, 4, 8
RQB, RKB, RVB, RGB = 24, 28, 32, 36


def _sigmoid(x):
    return 1.0 / (1.0 + jnp.exp(-x))


def _log_sigmoid(x):
    return jnp.minimum(x, 0.0) - jnp.log(1.0 + jnp.exp(-jnp.abs(x)))


def _rms(x):
    return x * lax.rsqrt(jnp.mean(x * x, axis=-1, keepdims=True) + EPS)


def _dot(a, b):
    return jnp.dot(a, b, preferred_element_type=F32)


def _dot_nt(a, b):
    return lax.dot_general(a, b, (((1,), (1,)), ((), ())), preferred_element_type=F32)


def _dot_tn(a, b):
    return lax.dot_general(a, b, (((0,), (0,)), ((), ())), preferred_element_type=F32)


def _mod_row(i):
    return jnp.where(i < CTX_TILES, 0, 1 + (i - CTX_TILES) // LAT_TILES_PER_SEQ)


def _params(n_axes):
    return pltpu.CompilerParams(
        dimension_semantics=("arbitrary",) * n_axes, vmem_limit_bytes=VMEM_LIMIT)


def _mod_kernel(c_ref, w_ref, b_ref, o_ref):
    c = c_ref[...]
    s = (c * _sigmoid(c)).astype(BF16)
    o_ref[...] = _dot(s, w_ref[...].astype(BF16)) + b_ref[...]


def _modulation(cvec, w_ada, b_ada):
    tn = 1024
    return pl.pallas_call(
        _mod_kernel,
        grid=(DEPTH, N_MOD * D // tn),
        in_specs=[
            pl.BlockSpec((8, D), lambda l, n: (0, 0)),
            pl.BlockSpec((None, D, tn), lambda l, n: (l, 0, n)),
            pl.BlockSpec((None, 1, tn), lambda l, n: (l, 0, n)),
        ],
        out_specs=pl.BlockSpec((None, 8, tn), lambda l, n: (l, 0, n)),
        out_shape=jax.ShapeDtypeStruct((DEPTH, 8, N_MOD * D), F32),
        compiler_params=_params(2),
        name="modulation",
    )(cvec, w_ada, b_ada.reshape(DEPTH, 1, N_MOD * D))


def _inproj_kernel(has_y, n_prev, *refs):
    refs = list(refs)
    x_ref = refs.pop(0)
    if has_y:
        y_ref, modp_ref = refs.pop(0), refs.pop(0)
    mod_ref, n1_ref, w_ref = refs.pop(0), refs.pop(0), refs.pop(0)
    if n_prev:
        kp_ref, vp_ref = refs.pop(0), refs.pop(0)
    zb_ref, zq_ref, kn_ref, vn_ref = refs[0:4]
    x = x_ref[...]
    if has_y:
        x = x + modp_ref[0][:, 5 * D:6 * D] * y_ref[...]
        refs[4][...] = x
    is_ctx = pl.program_id(0) < CTX_TILES
    m = mod_ref[0]
    shift1 = m[:, 0:D]
    scale1 = m[:, D:2 * D]
    h = ((_rms(x) * n1_ref[...]) * (1.0 + scale1) + shift1).astype(BF16)
    cw = 1024
    for c in range(IN_COLS // cw):
        z = _dot(h, w_ref[:, c * cw:(c + 1) * cw])
        zb_ref[:, c * cw:(c + 1) * cw] = z.astype(BF16)
        if c == 0:
            @pl.when(jnp.logical_not(is_ctx))
            def _():
                zq_ref[...] = z

            @pl.when(is_ctx)
            def _():
                for hh in range(HEADS):
                    kn_ref[n_prev, hh] = z[:, 512 + hh * HD:512 + (hh + 1) * HD]
        if c == 1:
            @pl.when(is_ctx)
            def _():
                for hh in range(HEADS):
                    vn_ref[n_prev, hh] = z[:, hh * HD:(hh + 1) * HD]
    if n_prev:
        @pl.when(is_ctx)
        def _():
            kn_ref[0:n_prev] = kp_ref[...]
            vn_ref[0:n_prev] = vp_ref[...]


def _inproj(l, x, y, mod, norm1_w, w_in_bf, k_prev, v_prev):
    has_y = y is not None
    tile = pl.BlockSpec((TM, D), lambda i: (i, 0))
    ctx_i = lambda i: jnp.minimum(i, CTX_TILES - 1)
    lat_i = lambda i: jnp.maximum(i - CTX_TILES, 0)
    kv_spec = lambda n: pl.BlockSpec((None, n, HEADS, T_CTX, HD), lambda i: (ctx_i(i), 0, 0, 0, 0))
    kv_shape = jax.ShapeDtypeStruct((N_CTX_SEQ, l + 1, HEADS, T_CTX, HD), F32)
    in_specs = [tile]
    args = [x]
    if has_y:
        in_specs += [tile, pl.BlockSpec((1, 1, N_MOD * D), lambda i: ((l - 1) * 8 + _mod_row(i), 0, 0))]
        args += [y, mod]
    in_specs += [
        pl.BlockSpec((1, 1, N_MOD * D), lambda i: (l * 8 + _mod_row(i), 0, 0)),
        pl.BlockSpec((None, 1, D), lambda i: (l, 0, 0)),
        pl.BlockSpec((None, D, IN_COLS), lambda i: (l, 0, 0), pipeline_mode=pl.Buffered(1)),
    ]
    args += [mod, norm1_w, w_in_bf]
    if l:
        in_specs += [kv_spec(l), kv_spec(l)]
        args += [k_prev, v_prev]
    out_specs = [pl.BlockSpec((TM, IN_COLS), lambda i: (i, 0)),
                 pl.BlockSpec((TM, 1024), lambda i: (lat_i(i), 0)),
                 kv_spec(l + 1), kv_spec(l + 1)]
    out_shape = [jax.ShapeDtypeStruct((N_TOK, IN_COLS), BF16),
                 jax.ShapeDtypeStruct((N_LAT, 1024), F32),
                 kv_shape, kv_shape]
    if has_y:
        out_specs.append(tile)
        out_shape.append(jax.ShapeDtypeStruct((N_TOK, D), F32))
    return pl.pallas_call(
        functools.partial(_inproj_kernel, has_y, l),
        grid=(N_TILES,),
        in_specs=in_specs,
        out_specs=out_specs,
        out_shape=out_shape,
        compiler_params=_params(1),
        name="inproj",
    )(*args)


def _lambda(lam_ref, lam_init):
    lv = lam_ref[...]
    a = jnp.sum(lv[0:1] * lv[1:2], axis=-1, keepdims=True)
    b = jnp.sum(lv[2:3] * lv[3:4], axis=-1, keepdims=True)
    return jnp.exp(a) - jnp.exp(b) + lam_init


def _diff_attention(lam_init, q, k_all, v_all, lam, sw):
    lane = lax.broadcasted_iota(jnp.int32, q.shape, 1)
    zero = jnp.zeros_like(q)

    def softmax_map(qm):
        s = _dot_nt(qm, k_all) * ATTN_SCALE
        e = jnp.exp(s - jnp.max(s, axis=-1, keepdims=True))
        return e * (1.0 / jnp.sum(e, axis=-1, keepdims=True))

    a = softmax_map(jnp.where(lane < 64, q, zero)) - lam * softmax_map(jnp.where(lane >= 64, q, zero))
    o = _dot(a.astype(BF16), v_all)
    return (_rms(o) * sw) * (1.0 - lam_init)


def _attn_ctx_kernel(lam_init, q_ref, k_ref, v_ref, lam_ref, sw_ref, o_ref):
    lam = _lambda(lam_ref, lam_init)
    for h in range(HEADS):
        sl = slice(h * HD, (h + 1) * HD)
        o = _diff_attention(lam_init, q_ref[:, sl], k_ref[:, sl], v_ref[:, sl], lam, sw_ref[...])
        o_ref[:, sl] = o.astype(BF16)


def _rope(x, cos, sin_signed):
    lane = lax.broadcasted_iota(jnp.int32, x.shape, 1)
    partner = jnp.where(lane % 32 < 16, pltpu.roll(x, 112, 1), pltpu.roll(x, 16, 1))
    return x * cos + partner * sin_signed


def _attn_lat_kernel(lam_init, q_ref, k_ref, v_ref, ck_ref, cv_ref, cos_ref, sin_ref,
                     cosq_ref, sinq_ref, lam_ref, sw_ref, o_ref, kall, vall):
    @pl.when(pl.program_id(2) == 0)
    def _():
        kall[0:PAST, :] = ck_ref[...].astype(BF16)
        kall[PAST:, :] = _rope(k_ref[...], cos_ref[...], sin_ref[...]).astype(BF16)
        vall[0:PAST, :] = cv_ref[...].astype(BF16)
        vall[PAST:, :] = v_ref[...]

    q = _rope(q_ref[...], cosq_ref[...], sinq_ref[...]).astype(BF16)
    lam = _lambda(lam_ref, lam_init)
    o = _diff_attention(lam_init, q, kall[...], vall[...], lam, sw_ref[...])
    o_ref[...] = o.astype(BF16)


def _attention(l, lam_init, zb, zq, cache_k, cache_v, cos, sin, attn_lambda, subln_w):
    width = HEADS * HD
    ao = pl.pallas_call(
        functools.partial(_attn_ctx_kernel, lam_init),
        grid=(N_CTX_SEQ,),
        in_specs=[
            pl.BlockSpec((T_CTX, width), lambda b: (b, QB // HEADS)),
            pl.BlockSpec((T_CTX, width), lambda b: (b, KB // HEADS)),
            pl.BlockSpec((T_CTX, width), lambda b: (b, VB // HEADS)),
            pl.BlockSpec((None, 4, 64), lambda b: (l, 0, 0)),
            pl.BlockSpec((None, 1, HD), lambda b: (l, 0, 0)),
        ],
        out_specs=pl.BlockSpec((T_CTX, width), lambda b: (b, 0)),
        out_shape=jax.ShapeDtypeStruct((N_CTX, width), BF16),
        compiler_params=_params(1),
        name="attn_ctx",
    )(zb, zb, zb, attn_lambda, subln_w)

    lat_row = lambda b, j: CTX_TILES + LAT_TILES_PER_SEQ * b + j
    seq_row = lambda b: N_CTX // T_LAT + b
    ao_lat = pl.pallas_call(
        functools.partial(_attn_lat_kernel, lam_init),
        grid=(N_LAT_SEQ, HEADS, LAT_TILES_PER_SEQ),
        in_specs=[
            pl.BlockSpec((TM, HD), lambda b, h, j: (LAT_TILES_PER_SEQ * b + j, QB + h)),
            pl.BlockSpec((T_LAT, HD), lambda b, h, j: (b, KB + h)),
            pl.BlockSpec((T_LAT, HD), lambda b, h, j: (seq_row(b), VB + h)),
            pl.BlockSpec((None, None, None, PAST, HD), lambda b, h, j: (b, l, h, 0, 0)),
            pl.BlockSpec((None, None, None, PAST, HD), lambda b, h, j: (b, l, h, 0, 0)),
            pl.BlockSpec((T_LAT, HD), lambda b, h, j: (0, 0)),
            pl.BlockSpec((T_LAT, HD), lambda b, h, j: (0, 0)),
            pl.BlockSpec((TM, HD), lambda b, h, j: (j, 0)),
            pl.BlockSpec((TM, HD), lambda b, h, j: (j, 0)),
            pl.BlockSpec((None, 4, 64), lambda b, h, j: (l, 0, 0)),
            pl.BlockSpec((None, 1, HD), lambda b, h, j: (l, 0, 0)),
        ],
        out_specs=pl.BlockSpec((TM, HD), lambda b, h, j: (LAT_TILES_PER_SEQ * b + j, h)),
        out_shape=jax.ShapeDtypeStruct((N_LAT, HEADS * HD), BF16),
        scratch_shapes=[pltpu.VMEM((PAST + T_LAT, HD), BF16), pltpu.VMEM((PAST + T_LAT, HD), BF16)],
        compiler_params=_params(3),
        name="attn_lat",
    )(zq, zq, zb, cache_k, cache_v, cos, sin, cos, sin, attn_lambda, subln_w)
    return ao, ao_lat


def _decay_matrix(lgf, lgb, row0, tq, tk):
    i = row0 + lax.broadcasted_iota(jnp.int32, (tq, tk), 0)
    j = lax.broadcasted_iota(jnp.int32, (tq, tk), 1)
    d = (i - j).astype(F32)
    fwd = jnp.where(d >= 0.0, jnp.exp(lgf * jnp.maximum(d, 0.0)), 0.0)
    bwd = jnp.where(d <= 0.0, jnp.exp(lgb * jnp.maximum(-d, 0.0)), 0.0)
    return fwd + bwd


def _ret_finish(o, g, nw):
    g = g.astype(F32)
    return ((g * _sigmoid(g)) * (_rms(o) * nw)).astype(BF16)


def _ret_ctx_kernel(q_ref, k_ref, v_ref, g_ref, df_ref, db_ref, nw_ref, o_ref, sf_ref, sb_ref, dmat):
    @pl.when(pl.program_id(0) == 0)
    def _():
        for h in range(HEADS):
            dmat[h] = RET_SCALE * _decay_matrix(_log_sigmoid(df_ref[h]), _log_sigmoid(db_ref[h]), 0, T_CTX, T_CTX)

    j = lax.broadcasted_iota(jnp.int32, (T_CTX, 1), 0).astype(F32)
    for h in range(HEADS):
        sl = slice(h * HD, (h + 1) * HD)
        q = q_ref[:, sl]
        k = k_ref[:, sl]
        v = v_ref[:, sl]
        s = _dot_nt(q, k) * dmat[h]
        o = _dot(s.astype(BF16), v)
        o_ref[:, sl] = _ret_finish(o, g_ref[:, sl], nw_ref[h])
        lgf = _log_sigmoid(df_ref[h])
        lgb = _log_sigmoid(db_ref[h])
        kf = k.astype(F32) * RET_SCALE
        sf_ref[h] = _dot_tn((kf * jnp.exp(lgf * (T_CTX - 1.0 - j))).astype(BF16), v)
        sb_ref[h] = _dot_tn((kf * jnp.exp(lgb * j)).astype(BF16), v)


def _ret_lat_kernel(q_ref, k_ref, v_ref, g_ref, s0f_ref, s0b_ref, df_ref, db_ref, nw_ref, o_ref, dmat):
    row0 = pl.program_id(1) * TM
    lgf = _log_sigmoid(df_ref[0])
    lgb = _log_sigmoid(db_ref[0])

    @pl.when(pl.program_id(2) == 0)
    def _():
        dmat[...] = RET_SCALE * _decay_matrix(lgf, lgb, row0, TM, T_LAT)

    q = q_ref[...]
    s = _dot_nt(q, k_ref[...]) * dmat[...]
    o = _dot(s.astype(BF16), v_ref[...])
    i = (row0 + lax.broadcasted_iota(jnp.int32, (TM, 1), 0)).astype(F32)
    o = o + jnp.exp(lgf * (i + 1.0)) * _dot(q, s0f_ref[...].astype(BF16))
    o = o + jnp.exp(lgb * (T_LAT - i)) * _dot(q, s0b_ref[...].astype(BF16))
    o_ref[...] = _ret_finish(o, g_ref[...], nw_ref[0])


def _retention(l, zb, state_f, state_b, decay_f, decay_b, ret_norm_w):
    width = HEADS * HD
    dec1 = pl.BlockSpec((HEADS, 1, 1), lambda b: (l, 0, 0))
    nw1 = pl.BlockSpec((HEADS, 1, HD), lambda b: (l, 0, 0))
    st_spec = pl.BlockSpec((None, HEADS, HD, HD), lambda b: (b, 0, 0, 0))
    st_shape = jax.ShapeDtypeStruct((N_CTX_SEQ, HEADS, HD, HD), F32)
    ro, sf, sb = pl.pallas_call(
        _ret_ctx_kernel,
        grid=(N_CTX_SEQ,),
        in_specs=[
            pl.BlockSpec((T_CTX, width), lambda b: (b, RQB // HEADS)),
            pl.BlockSpec((T_CTX, width), lambda b: (b, RKB // HEADS)),
            pl.BlockSpec((T_CTX, width), lambda b: (b, RVB // HEADS)),
            pl.BlockSpec((T_CTX, width), lambda b: (b, RGB // HEADS)),
            dec1, dec1, nw1,
        ],
        out_specs=[pl.BlockSpec((T_CTX, width), lambda b: (b, 0)), st_spec, st_spec],
        out_shape=[jax.ShapeDtypeStruct((N_CTX, width), BF16), st_shape, st_shape],
        scratch_shapes=[pltpu.VMEM((HEADS, T_CTX, T_CTX), F32)],
        compiler_params=_params(1),
        name="ret_ctx",
    )(zb, zb, zb, zb, decay_f, decay_b, ret_norm_w)

    lat_row = lambda b, j: CTX_TILES + LAT_TILES_PER_SEQ * b + j
    seq_row = lambda b: N_CTX // T_LAT + b
    dec3 = pl.BlockSpec((1, 1, 1), lambda h, j, b: (l * HEADS + h, 0, 0))
    nw3 = pl.BlockSpec((1, 1, HD), lambda h, j, b: (l * HEADS + h, 0, 0))
    s0_spec = pl.BlockSpec((None, None, None, HD, HD), lambda h, j, b: (b, l, h, 0, 0))
    ro_lat = pl.pallas_call(
        _ret_lat_kernel,
        grid=(HEADS, LAT_TILES_PER_SEQ, N_LAT_SEQ),
        in_specs=[
            pl.BlockSpec((TM, HD), lambda h, j, b: (lat_row(b, j), RQB + h)),
            pl.BlockSpec((T_LAT, HD), lambda h, j, b: (seq_row(b), RKB + h)),
            pl.BlockSpec((T_LAT, HD), lambda h, j, b: (seq_row(b), RVB + h)),
            pl.BlockSpec((TM, HD), lambda h, j, b: (lat_row(b, j), RGB + h)),
            s0_spec, s0_spec, dec3, dec3, nw3,
        ],
        out_specs=pl.BlockSpec((TM, HD), lambda h, j, b: (LAT_TILES_PER_SEQ * b + j, h)),
        out_shape=jax.ShapeDtypeStruct((N_LAT, width), BF16),
        scratch_shapes=[pltpu.VMEM((TM, T_LAT), F32)],
        compiler_params=_params(3),
        name="ret_lat",
    )(zb, zb, zb, zb, state_f, state_b, decay_f, decay_b, ret_norm_w)
    return ro, ro_lat, sf, sb


def _merge_kernel(x_ref, aoc_ref, aol_ref, roc_ref, rol_ref, cb_ref, cc_ref, cx_ref, ccp_ref, cxp_ref,
                  ccn_ref, cxn_ref, mg0_ref, mg1_ref, mg2_ref, mod_ref, cw_ref, wa_ref, wc_ref, wr_ref,
                  wo_ref, n2_ref, wrt_ref, x1_ref, h2_ref, aff_ref):
    i = pl.program_id(0)
    is_ctx = i < CTX_TILES
    ao = jnp.where(is_ctx, aoc_ref[...], aol_ref[...])
    ro = jnp.where(is_ctx, roc_ref[...], rol_ref[...])
    j = (i - CTX_TILES) % LAT_TILES_PER_SEQ
    seq_first = jnp.logical_or(i < CTX_TILES, j == 0)
    seq_last = jnp.logical_or(i < CTX_TILES, j == LAT_TILES_PER_SEQ - 1)

    u = cc_ref[...].astype(F32) * cx_ref[...].astype(F32)
    up = (ccp_ref[...].astype(F32) * cxp_ref[...].astype(F32))[15:16, :] * jnp.where(seq_first, 0.0, 1.0)
    dn = (ccn_ref[...].astype(F32) * cxn_ref[...].astype(F32))[0:1, :] * jnp.where(seq_last, 0.0, 1.0)
    r = lax.broadcasted_iota(jnp.int32, u.shape, 0)
    u_prev = jnp.where(r == 0, up, pltpu.roll(u, 1, 0))
    u_next = jnp.where(r == TM - 1, dn, pltpu.roll(u, TM - 1, 0))
    cw = cw_ref[...]
    conv = u_prev * cw[0:1, :] + u * cw[1:2, :] + u_next * cw[2:3, :]
    conv_o = (cb_ref[...].astype(F32) * conv).astype(BF16)

    merged = _sigmoid(mg0_ref[...].astype(F32)) * _dot(ao, wa_ref[...])
    merged = merged + _sigmoid(mg1_ref[...].astype(F32)) * _dot(conv_o, wc_ref[...])
    merged = merged + _sigmoid(mg2_ref[...].astype(F32)) * _dot(ro, wr_ref[...])

    m = mod_ref[0]
    gate1 = m[:, 2 * D:3 * D]
    shift2 = m[:, 3 * D:4 * D]
    scale2 = m[:, 4 * D:5 * D]
    x1 = x_ref[...] + gate1 * _dot(merged.astype(BF16), wo_ref[...])
    x1_ref[...] = x1
    h2 = (_rms(x1) * n2_ref[...]) * (1.0 + scale2) + shift2
    h2_ref[:, 0:D] = h2

    logits = _dot(h2.astype(BF16), wrt_ref[...])
    lane = lax.broadcasted_iota(jnp.int32, logits.shape, 1)
    valid = lane < N_EXPERTS
    lmax = jnp.max(jnp.where(valid, logits, -jnp.inf), axis=-1, keepdims=True)
    e = jnp.where(valid, jnp.exp(logits - lmax), 0.0)
    aff = e * (1.0 / jnp.sum(e, axis=-1, keepdims=True))
    h2_ref[:, D:H2W] = aff
    aff_ref[...] = aff.T[0:N_EXPERTS, :]


def _merge(l, x, ao, ao_lat, ro, ro_lat, zb, mod, conv_w, w_br_attn, w_br_conv, w_br_ret, w_out, norm2_w,
           w_router_pad):
    n16 = N_TOK // 16
    ctx_br = pl.BlockSpec((TM, 512), lambda i: (jnp.minimum(i, CTX_TILES - 1), 0))
    lat_br = pl.BlockSpec((TM, 512), lambda i: (jnp.maximum(i - CTX_TILES, 0), 0))
    col = lambda c: pl.BlockSpec((TM, 512), lambda i: (i, c))
    halo_p = lambda c: pl.BlockSpec((16, 512), lambda i: (jnp.maximum(i * (TM // 16) - 1, 0), c))
    halo_n = lambda c: pl.BlockSpec((16, 512), lambda i: (jnp.minimum((i + 1) * (TM // 16), n16 - 1), c))
    mgs = lambda c: pl.BlockSpec((TM, D), lambda i: (i, c))
    wbr = pl.BlockSpec((None, 512, D), lambda i: (l, 0, 0))
    return pl.pallas_call(
        _merge_kernel,
        grid=(N_TILES,),
        in_specs=[
            pl.BlockSpec((TM, D), lambda i: (i, 0)),
            ctx_br, lat_br, ctx_br, lat_br,
            col(3), col(4), col(5), halo_p(4), halo_p(5), halo_n(4), halo_n(5),
            mgs(5), mgs(6), mgs(7),
            pl.BlockSpec((1, 1, N_MOD * D), lambda i: (l * 8 + _mod_row(i), 0, 0)),
            pl.BlockSpec((None, 3, 512), lambda i: (l, 0, 0)),
            wbr, wbr, wbr,
            pl.BlockSpec((None, D, D), lambda i: (l, 0, 0)),
            pl.BlockSpec((None, 1, D), lambda i: (l, 0, 0)),
            pl.BlockSpec((None, D, 128), lambda i: (l, 0, 0)),
        ],
        out_specs=[
            pl.BlockSpec((TM, D), lambda i: (i, 0)),
            pl.BlockSpec((TM, H2W), lambda i: (i, 0)),
            pl.BlockSpec((N_EXPERTS, TM), lambda i: (0, i)),
        ],
        out_shape=[
            jax.ShapeDtypeStruct((N_TOK, D), F32),
            jax.ShapeDtypeStruct((N_TOK, H2W), F32),
            jax.ShapeDtypeStruct((N_EXPERTS, N_TOK), F32),
        ],
        compiler_params=_params(1),
        name="merge",
    )(x, ao, ao_lat, ro, ro_lat, zb, zb, zb, zb, zb, zb, zb, zb, zb, zb, mod, conv_w,
      w_br_attn, w_br_conv, w_br_ret, w_out, norm2_w, w_router_pad)


def _cumsum_lanes(x, tri):
    run = jnp.zeros((x.shape[0], 1), F32)
    outs = []
    for b in range(x.shape[1] // 128):
        cs = _dot(x[:, b * 128:(b + 1) * 128].astype(BF16), tri) + run
        run = cs[:, 127:128]
        outs.append(cs)
    return jnp.concatenate(outs, axis=-1)


def _topk_kernel(a_ref, tmat_ref, idx_ref):
    a = a_ref[...]
    n = a.shape[1]
    kf = float(CAP)

    def count_gt(thr):
        return jnp.sum(jnp.where(a > thr, 1.0, 0.0), axis=-1, keepdims=True)

    def span(lo, hi):
        inside = jnp.logical_and(a > lo, a <= hi)
        cmax = jnp.max(jnp.where(inside, a, -jnp.inf), axis=-1, keepdims=True)
        cmin = jnp.min(jnp.where(inside, a, jnp.inf), axis=-1, keepdims=True)
        return cmax, cmin

    def cond(c):
        return jnp.logical_and(c[2] > 0, c[3] < 400)

    def body(c):
        lo, hi, _, it = c
        mid = 0.5 * (lo + hi)
        ge = count_gt(mid) >= kf
        lo = jnp.where(ge, mid, lo)
        hi = jnp.where(ge, hi, mid)
        cmax, cmin = span(lo, hi)
        open_rows = jnp.max(jnp.where(cmax != cmin, 1, 0))
        return lo, hi, open_rows, it + 1

    lo0 = jnp.full((N_EXPERTS, 1), -1.0, F32)
    hi0 = jnp.max(a, axis=-1, keepdims=True)
    cmax0, cmin0 = span(lo0, hi0)
    lo, hi, _, _ = lax.while_loop(
        cond, body, (lo0, hi0, jnp.max(jnp.where(cmax0 != cmin0, 1, 0)), jnp.int32(0)))
    thr, _ = span(lo, hi)

    r = lax.broadcasted_iota(jnp.int32, (128, 128), 0)
    c = lax.broadcasted_iota(jnp.int32, (128, 128), 1)
    tri = jnp.where(r <= c, 1.0, 0.0).astype(BF16)
    gt = a > thr
    eq = jnp.where(a == thr, 1.0, 0.0)
    need = kf - count_gt(thr)
    eq_before = _cumsum_lanes(eq, tri) - eq
    sel = jnp.where(jnp.logical_or(gt, jnp.logical_and(eq > 0.0, eq_before < need)), 1.0, 0.0)
    pos = _cumsum_lanes(sel, tri) - 1.0
    slot = jnp.where(sel > 0.0, pos, -1.0).astype(jnp.int32)

    p_iota = lax.broadcasted_iota(jnp.int32, (CAP, 1024), 0)
    for e in range(N_EXPERTS):
        acc = jnp.zeros((CAP, 128), F32)
        for cb in range(n // 1024):
            onehot = jnp.where(p_iota == slot[e:e + 1, cb * 1024:(cb + 1) * 1024], 1.0, 0.0).astype(BF16)
            acc = acc + _dot(onehot, tmat_ref[cb * 1024:(cb + 1) * 1024, :])
        idx_ref[0, e] = (acc[:, 0:1] * 64.0 + acc[:, 1:2]).astype(jnp.int32)


def _topk(aff_t, tmat):
    return pl.pallas_call(
        _topk_kernel,
        grid=(2,),
        in_specs=[
            pl.BlockSpec((N_EXPERTS, N_CTX), lambda s: (0, s)),
            pl.BlockSpec((N_CTX, 128), lambda s: (0, 0)),
        ],
        out_specs=pl.BlockSpec((1, N_EXPERTS, CAP, 1), lambda s: (s, 0, 0, 0)),
        out_shape=jax.ShapeDtypeStruct((2, N_EXPERTS, CAP, 1), jnp.int32),
        compiler_params=_params(1),
        name="topk",
    )(aff_t, tmat)


N_FT = FF // TF
GATHER_ROWS = 2 * CAP
ROWS_PER_STEP = GATHER_ROWS // N_FT


def _ffn_kernel(l, idx_ref, h2_hbm, wg_ref, wu_ref, wd_ref, ye_ref, xg, xb, acc, sems):
    e = pl.program_id(0)
    f = pl.program_id(1)
    slot = e % 2

    def row_copy(expert, s, p, dst_slot):
        row = idx_ref[(s * N_EXPERTS + expert) * CAP + p] + s * N_CTX
        return pltpu.make_async_copy(h2_hbm.at[pl.ds(row, 1), :],
                                     xg.at[dst_slot, pl.ds(s * CAP + p, 1), :], sems.at[dst_slot])

    def slot_wait(dst_slot):
        pltpu.make_async_copy(h2_hbm.at[pl.ds(0, GATHER_ROWS), :], xg.at[dst_slot], sems.at[dst_slot]).wait()

    @pl.when(jnp.logical_and(e == 0, f == 0))
    def _():
        def issue(r, carry):
            row_copy(0, r // CAP, r % CAP, 0).start()
            return carry
        lax.fori_loop(0, GATHER_ROWS, issue, 0)

    @pl.when(f == 0)
    def _():
        slot_wait(slot)
        xb[...] = xg[slot, :, 0:D].astype(BF16)
        acc[...] = jnp.zeros_like(acc)

    nxt = jnp.minimum(e + 1, N_EXPERTS - 1)
    s_nxt = f // (N_FT // 2)
    p0 = (f % (N_FT // 2)) * ROWS_PER_STEP
    for u in range(ROWS_PER_STEP):
        row_copy(nxt, s_nxt, p0 + u, 1 - slot).start()

    x = xb[...]
    hg = _dot(x, wg_ref[...].astype(BF16))
    hu = _dot(x, wu_ref[...].astype(BF16))
    hdn = ((hg * _sigmoid(hg)) * hu).astype(BF16)
    tail = xg[slot, :, D:H2W]
    lane = lax.broadcasted_iota(jnp.int32, tail.shape, 1)
    gate = jnp.sum(jnp.where(lane == e, tail, 0.0), axis=-1, keepdims=True)
    acc[...] += _dot(hdn, wd_ref[...].astype(BF16)) * gate

    @pl.when(f == N_FT - 1)
    def _():
        ye_ref[0] = acc[0:CAP, :]
        ye_ref[1] = acc[CAP:2 * CAP, :]

    @pl.when(jnp.logical_and(e == N_EXPERTS - 1, f == N_FT - 1))
    def _():
        slot_wait(1 - slot)


def _expert_ffn(l, idx_flat, h2ext, w_gate, w_up, w_down):
    grid_spec = pltpu.PrefetchScalarGridSpec(
        num_scalar_prefetch=1,
        grid=(N_EXPERTS, FF // TF),
        in_specs=[
            pl.BlockSpec(memory_space=pl.ANY),
            pl.BlockSpec((None, None, D, TF), lambda e, f, idx: (l, e, 0, f)),
            pl.BlockSpec((None, None, D, TF), lambda e, f, idx: (l, e, 0, f)),
            pl.BlockSpec((None, None, TF, D), lambda e, f, idx: (l, e, f, 0)),
        ],
        out_specs=pl.BlockSpec((2, None, CAP, D), lambda e, f, idx: (0, e, 0, 0)),
        scratch_shapes=[
            pltpu.VMEM((2, GATHER_ROWS, H2W), F32),
            pltpu.VMEM((GATHER_ROWS, D), BF16),
            pltpu.VMEM((GATHER_ROWS, D), F32),
            pltpu.SemaphoreType.DMA((2,)),
        ],
    )
    return pl.pallas_call(
        functools.partial(_ffn_kernel, l),
        grid_spec=grid_spec,
        out_shape=jax.ShapeDtypeStruct((2, N_EXPERTS, CAP, D), F32),
        compiler_params=_params(2),
        name="expert_ffn",
    )(idx_flat, h2ext, w_gate, w_up, w_down)


SCATTER_UNROLL = 4


def _combine_kernel(idx_ref, ye_ref, y_ref):
    s = pl.program_id(0)
    e = pl.program_id(1)

    @pl.when(e == 0)
    def _():
        y_ref[...] = jnp.zeros_like(y_ref)

    base = (s * N_EXPERTS + e) * CAP

    def group(g, carry):
        p0 = g * SCATTER_UNROLL
        rows = [idx_ref[base + p0 + u] for u in range(SCATTER_UNROLL)]
        vals = [y_ref[pl.ds(rows[u], 1), :] + ye_ref[pl.ds(p0 + u, 1), :] for u in range(SCATTER_UNROLL)]
        for u in range(SCATTER_UNROLL):
            y_ref[pl.ds(rows[u], 1), :] = vals[u]
        return carry

    lax.fori_loop(0, CAP // SCATTER_UNROLL, group, 0)


def _combine(idx_flat, ye):
    grid_spec = pltpu.PrefetchScalarGridSpec(
        num_scalar_prefetch=1,
        grid=(2, N_EXPERTS),
        in_specs=[pl.BlockSpec((None, None, CAP, D), lambda s, e, idx: (s, e, 0, 0))],
        out_specs=pl.BlockSpec((N_CTX, D), lambda s, e, idx: (s, 0)),
    )
    return pl.pallas_call(
        _combine_kernel,
        grid_spec=grid_spec,
        out_shape=jax.ShapeDtypeStruct((N_TOK, D), F32),
        compiler_params=_params(2),
        name="combine",
    )(idx_flat, ye)


def _final_kernel(x_ref, y_ref, mod_ref, w_ref, op_ref, os_ref):
    x = x_ref[...] + mod_ref[0][:, 5 * D:6 * D] * y_ref[...]
    out = _rms(x) * w_ref[...]
    is_ctx = pl.program_id(0) < CTX_TILES

    @pl.when(is_ctx)
    def _():
        op_ref[...] = out

    @pl.when(jnp.logical_not(is_ctx))
    def _():
        os_ref[...] = out


def _final(x1, y, mod, final_norm_w):
    tile = pl.BlockSpec((TM, D), lambda i: (i, 0))
    return pl.pallas_call(
        _final_kernel,
        grid=(N_TILES,),
        in_specs=[tile, tile,
                  pl.BlockSpec((1, 1, N_MOD * D), lambda i: ((DEPTH - 1) * 8 + _mod_row(i), 0, 0)),
                  pl.BlockSpec((1, D), lambda i: (0, 0))],
        out_specs=[pl.BlockSpec((TM, D), lambda i: (jnp.minimum(i, CTX_TILES - 1), 0)),
                   pl.BlockSpec((TM, D), lambda i: (jnp.maximum(i - CTX_TILES, 0), 0))],
        out_shape=[jax.ShapeDtypeStruct((N_CTX, D), F32), jax.ShapeDtypeStruct((N_LAT, D), F32)],
        compiler_params=_params(1),
        name="final_norm",
    )(x1, y, mod, final_norm_w.reshape(1, D))


def _rope_tables():
    t = np.arange(T_LAT)
    row = (t // GRID_W).astype(np.float32)
    col = (t % GRID_W).astype(np.float32)
    inv = jnp.asarray(ROPE_BASE, F32) ** (-jnp.arange(N_ROPE_FREQ, dtype=F32) / N_ROPE_FREQ)
    ang_r = jnp.asarray(row)[:, None] * inv
    ang_c = jnp.asarray(col)[:, None] * inv
    def group(ang):
        return jnp.concatenate([ang, ang], axis=-1)
    ang = jnp.concatenate([group(ang_r), group(ang_c), group(ang_r), group(ang_c)], axis=-1)
    sign = np.where(np.arange(HD) % 32 < 16, -1.0, 1.0).astype(np.float32)
    return jnp.cos(ang), jnp.sin(ang) * sign


def _index_table():
    t = np.arange(N_CTX)
    tm = np.zeros((N_CTX, 128), np.float32)
    tm[:, 0] = t // 64
    tm[:, 1] = t % 64
    return jnp.asarray(tm, BF16)


def kernel(x_prompt, x_sample, c, cache_attn_k, cache_attn_v, state_ret_fwd, state_ret_bwd, c_ctx, w_ada, b_ada, norm1_w, norm2_w, w_in, attn_lambda, attn_subln_w, conv_w, ret_decay_fwd, ret_decay_bwd, ret_norm_w, w_br_attn, w_br_conv, w_br_ret, w_out, w_router, w_exp_gate, w_exp_up, w_exp_down, final_norm_w):
    x = jnp.concatenate([x_prompt.reshape(N_CTX, D), x_sample.reshape(N_LAT, D)], axis=0)
    cvec = jnp.concatenate([c_ctx[None, :], c, jnp.zeros((3, D), F32)], axis=0)
    mod = _modulation(cvec, w_ada, b_ada).reshape(DEPTH * 8, 1, N_MOD * D)

    w_in_bf = w_in.astype(BF16)
    w_br_attn_bf = w_br_attn.astype(BF16)
    w_br_conv_bf = w_br_conv.astype(BF16)
    w_br_ret_bf = w_br_ret.astype(BF16)
    w_out_bf = w_out.astype(BF16)
    w_router_pad = jnp.pad(w_router, ((0, 0), (0, 0), (0, 128 - N_EXPERTS))).astype(BF16)
    norm1 = norm1_w.reshape(DEPTH, 1, D)
    norm2 = norm2_w.reshape(DEPTH, 1, D)
    subln = attn_subln_w.reshape(DEPTH, 1, HD)
    decay_f = ret_decay_fwd.reshape(DEPTH * HEADS, 1, 1)
    decay_b = ret_decay_bwd.reshape(DEPTH * HEADS, 1, 1)
    ret_nw = ret_norm_w.reshape(DEPTH * HEADS, 1, HD)
    cos, sin = _rope_tables()
    tmat = _index_table()

    y = None
    new_k = new_v = None
    sfs, sbs = [], []
    for l in range(DEPTH):
        lam_init = 0.8 - 0.6 * math.exp(-0.3 * l)
        if y is None:
            zb, zq, new_k, new_v = _inproj(l, x, None, mod, norm1, w_in_bf, new_k, new_v)
        else:
            zb, zq, new_k, new_v, x = _inproj(l, x, y, mod, norm1, w_in_bf, new_k, new_v)
        ao, ao_lat = _attention(l, lam_init, zb, zq, cache_attn_k, cache_attn_v, cos, sin, attn_lambda, subln)
        ro, ro_lat, sf, sb = _retention(l, zb, state_ret_fwd, state_ret_bwd, decay_f, decay_b, ret_nw)
        x, h2ext, aff_t = _merge(l, x, ao, ao_lat, ro, ro_lat, zb, mod, conv_w, w_br_attn_bf, w_br_conv_bf,
                                 w_br_ret_bf, w_out_bf, norm2, w_router_pad)
        idx_flat = _topk(aff_t, tmat).reshape(2 * N_EXPERTS * CAP)
        ye = _expert_ffn(l, idx_flat, h2ext, w_exp_gate, w_exp_up, w_exp_down)
        y = _combine(idx_flat, ye)

        sfs.append(sf)
        sbs.append(sb)

    y_prompt, y_sample = _final(x, y, mod, final_norm_w)
    return (y_prompt.reshape(N_CTX_SEQ, T_CTX, D), y_sample.reshape(N_LAT_SEQ, T_LAT, D), new_k, new_v,
            jnp.stack(sfs, axis=1), jnp.stack(sbs, axis=1))
```

```python
import functools
import math

import jax
import jax.numpy as jnp
import numpy as np
from jax import lax
from jax.experimental import pallas as pl
from jax.experimental.pallas import tpu as pltpu

F32 = jnp.float32
BF16 = jnp.bfloat16

D = 1024
DEPTH = 2
N_CTX_SEQ = 16
T_CTX = 256
N_LAT_SEQ = 4
T_LAT = 1024
PAST = 256
N_CTX = N_CTX_SEQ * T_CTX
N_LAT = N_LAT_SEQ * T_LAT
N_TOK = N_CTX + N_LAT
TM = 256
N_TILES = N_TOK // TM
CTX_TILES = N_CTX // TM
LAT_TILES_PER_SEQ = T_LAT // TM
HEADS = 4
HD = 128
GRID_W = 64
N_ROPE_FREQ = 16
ROPE_BASE = 10000.0
IN_COLS = 8192
N_MOD = 6
N_EXPERTS = 16
CAP = 512
FF = 2048
TF = 512
EPS = 1e-6
RET_SCALE = HD ** -0.5
ATTN_SCALE = 64 ** -0.5
SLAB = D // 128
VMEM_LIMIT = 56 * 1024 * 1024

QB, KB, VB = 0, 4, 8
RQB, RKB, RVB, RGB = 24, 28, 32, 36


def _sigmoid(x):
    return 1.0 / (1.0 + jnp.exp(-x))


def _log_sigmoid(x):
    return jnp.minimum(x, 0.0) - jnp.log(1.0 + jnp.exp(-jnp.abs(x)))


def _rms(x):
    return x * lax.rsqrt(jnp.mean(x * x, axis=-1, keepdims=True) + EPS)


def _dot(a, b):
    return jnp.dot(a, b, preferred_element_type=F32)


def _dot_nt(a, b):
    return lax.dot_general(a, b, (((1,), (1,)), ((), ())), preferred_element_type=F32)


def _dot_tn(a, b):
    return lax.dot_general(a, b, (((0,), (0,)), ((), ())), preferred_element_type=F32)


def _to_slabs(ref, x):
    n = x.shape[0]
    for s in range(SLAB):
        ref[pl.ds(s, n, stride=SLAB), :] = x[:, s * 128:(s + 1) * 128]


def _from_slabs(ref):
    n = ref.shape[0] // SLAB
    return jnp.concatenate([ref[pl.ds(s, n, stride=SLAB), :] for s in range(SLAB)], axis=-1)


def _mod_row(i):
    return jnp.where(i < CTX_TILES, 0, 1 + (i - CTX_TILES) // LAT_TILES_PER_SEQ)


def _params(n_axes):
    return pltpu.CompilerParams(
        dimension_semantics=("arbitrary",) * n_axes, vmem_limit_bytes=VMEM_LIMIT)


def _mod_kernel(c_ref, w_ref, b_ref, o_ref):
    c = c_ref[...]
    s = (c * _sigmoid(c)).astype(BF16)
    o_ref[...] = _dot(s, w_ref[...].astype(BF16)) + b_ref[...]


def _modulation(cvec, w_ada, b_ada):
    tn = 1024
    return pl.pallas_call(
        _mod_kernel,
        grid=(DEPTH, N_MOD * D // tn),
        in_specs=[
            pl.BlockSpec((8, D), lambda l, n: (0, 0)),
            pl.BlockSpec((None, D, tn), lambda l, n: (l, 0, n)),
            pl.BlockSpec((None, 1, tn), lambda l, n: (l, 0, n)),
        ],
        out_specs=pl.BlockSpec((None, 8, tn), lambda l, n: (l, 0, n)),
        out_shape=jax.ShapeDtypeStruct((DEPTH, 8, N_MOD * D), F32),
        compiler_params=_params(2),
        name="modulation",
    )(cvec, w_ada, b_ada.reshape(DEPTH, 1, N_MOD * D))


def _inproj_kernel(has_y, n_prev, *refs):
    refs = list(refs)
    x_ref = refs.pop(0)
    if has_y:
        y_ref, modp_ref = refs.pop(0), refs.pop(0)
    mod_ref, n1_ref, w_ref = refs.pop(0), refs.pop(0), refs.pop(0)
    if n_prev:
        kp_ref, vp_ref = refs.pop(0), refs.pop(0)
    zb_ref, zq_ref, kn_ref, vn_ref = refs[0:4]
    x = x_ref[...]
    if has_y:
        x = x + modp_ref[0][:, 5 * D:6 * D] * _from_slabs(y_ref)
        refs[4][...] = x
    m = mod_ref[0]
    shift1 = m[:, 0:D]
    scale1 = m[:, D:2 * D]
    h = ((_rms(x) * n1_ref[...]) * (1.0 + scale1) + shift1).astype(BF16)
    cw = 1024
    for c in range(IN_COLS // cw):
        z = _dot(h, w_ref[:, c * cw:(c + 1) * cw])
        zb_ref[:, c * cw:(c + 1) * cw] = z.astype(BF16)
        if c == 0:
            zq_ref[...] = z
            for hh in range(HEADS):
                kn_ref[n_prev, hh] = z[:, 512 + hh * HD:512 + (hh + 1) * HD]
        if c == 1:
            for hh in range(HEADS):
                vn_ref[n_prev, hh] = z[:, hh * HD:(hh + 1) * HD]
    if n_prev:
        kn_ref[0:n_prev] = kp_ref[...]
        vn_ref[0:n_prev] = vp_ref[...]


def _inproj(l, x, y, mod, norm1_w, w_in_bf, k_prev, v_prev):
    has_y = y is not None
    t = lambda i: (i + CTX_TILES) % N_TILES
    tile = pl.BlockSpec((TM, D), lambda i: (t(i), 0))
    ctx_i = lambda i: jnp.maximum(i - (N_TILES - CTX_TILES), 0)
    kv_spec = lambda n: pl.BlockSpec((None, n, HEADS, T_CTX, HD), lambda i: (ctx_i(i), 0, 0, 0, 0))
    kv_shape = jax.ShapeDtypeStruct((N_CTX_SEQ, l + 1, HEADS, T_CTX, HD), F32)
    in_specs = [tile]
    args = [x]
    if has_y:
        in_specs += [pl.BlockSpec((TM * SLAB, 128), lambda i: (t(i), 0)),
                     pl.BlockSpec((1, 1, N_MOD * D), lambda i: ((l - 1) * 8 + _mod_row(t(i)), 0, 0))]
        args += [y, mod]
    in_specs += [
        pl.BlockSpec((1, 1, N_MOD * D), lambda i: (l * 8 + _mod_row(t(i)), 0, 0)),
        pl.BlockSpec((None, 1, D), lambda i: (l, 0, 0)),
        pl.BlockSpec((None, D, IN_COLS), lambda i: (l, 0, 0), pipeline_mode=pl.Buffered(1)),
    ]
    args += [mod, norm1_w, w_in_bf]
    if l:
        in_specs += [kv_spec(l), kv_spec(l)]
        args += [k_prev, v_prev]
    out_specs = [pl.BlockSpec((TM, IN_COLS), lambda i: (t(i), 0)),
                 pl.BlockSpec((TM, 1024), lambda i: (t(i), 0)),
                 kv_spec(l + 1), kv_spec(l + 1)]
    out_shape = [jax.ShapeDtypeStruct((N_TOK, IN_COLS), BF16),
                 jax.ShapeDtypeStruct((N_TOK, 1024), F32),
                 kv_shape, kv_shape]
    if has_y:
        out_specs.append(tile)
        out_shape.append(jax.ShapeDtypeStruct((N_TOK, D), F32))
    return pl.pallas_call(
        functools.partial(_inproj_kernel, has_y, l),
        grid=(N_TILES,),
        in_specs=in_specs,
        out_specs=out_specs,
        out_shape=out_shape,
        compiler_params=_params(1),
        name="inproj",
    )(*args)


def _lambda(lam_ref, lam_init):
    lv = lam_ref[...]
    a = jnp.sum(lv[0:1] * lv[1:2], axis=-1, keepdims=True)
    b = jnp.sum(lv[2:3] * lv[3:4], axis=-1, keepdims=True)
    return jnp.exp(a) - jnp.exp(b) + lam_init


def _diff_attention(lam_init, q, k_all, v_all, lam, sw):
    lane = lax.broadcasted_iota(jnp.int32, q.shape, 1)
    zero = jnp.zeros_like(q)

    def softmax_map(qm):
        s = _dot_nt(qm, k_all) * ATTN_SCALE
        e = jnp.exp(s - jnp.max(s, axis=-1, keepdims=True))
        return e * (1.0 / jnp.sum(e, axis=-1, keepdims=True))

    a = softmax_map(jnp.where(lane < 64, q, zero)) - lam * softmax_map(jnp.where(lane >= 64, q, zero))
    o = _dot(a.astype(BF16), v_all)
    return (_rms(o) * sw) * (1.0 - lam_init)


def _attn_ctx_kernel(lam_init, q_ref, k_ref, v_ref, lam_ref, sw_ref, o_ref):
    lam = _lambda(lam_ref, lam_init)
    for h in range(HEADS):
        sl = slice(h * HD, (h + 1) * HD)
        o = _diff_attention(lam_init, q_ref[:, sl], k_ref[:, sl], v_ref[:, sl], lam, sw_ref[...])
        o_ref[:, sl] = o.astype(BF16)


def _rope(x, cos, sin_signed):
    lane = lax.broadcasted_iota(jnp.int32, x.shape, 1)
    partner = jnp.where(lane % 32 < 16, pltpu.roll(x, 112, 1), pltpu.roll(x, 16, 1))
    return x * cos + partner * sin_signed


def _attn_lat_kernel(lam_init, q_ref, k_ref, v_ref, ck_ref, cv_ref, cos_ref, sin_ref,
                     cosq_ref, sinq_ref, lam_ref, sw_ref, o_ref, kall, vall):
    @pl.when(pl.program_id(2) == 0)
    def _():
        kall[0:PAST, :] = ck_ref[...].astype(BF16)
        kall[PAST:, :] = _rope(k_ref[...], cos_ref[...], sin_ref[...]).astype(BF16)
        vall[0:PAST, :] = cv_ref[...].astype(BF16)
        vall[PAST:, :] = v_ref[...]

    q = _rope(q_ref[...], cosq_ref[...], sinq_ref[...]).astype(BF16)
    lam = _lambda(lam_ref, lam_init)
    o = _diff_attention(lam_init, q, kall[...], vall[...], lam, sw_ref[...])
    o_ref[...] = o.astype(BF16)


def _attention(l, lam_init, zb, zq, cache_k, cache_v, cos, sin, attn_lambda, subln_w):
    width = HEADS * HD
    ao = pl.pallas_call(
        functools.partial(_attn_ctx_kernel, lam_init),
        grid=(N_CTX_SEQ,),
        in_specs=[
            pl.BlockSpec((T_CTX, width), lambda b: (b, QB // HEADS)),
            pl.BlockSpec((T_CTX, width), lambda b: (b, KB // HEADS)),
            pl.BlockSpec((T_CTX, width), lambda b: (b, VB // HEADS)),
            pl.BlockSpec((None, 4, 64), lambda b: (l, 0, 0)),
            pl.BlockSpec((None, 1, HD), lambda b: (l, 0, 0)),
        ],
        out_specs=pl.BlockSpec((T_CTX, width), lambda b: (b, 0)),
        out_shape=jax.ShapeDtypeStruct((N_CTX, width), BF16),
        compiler_params=_params(1),
        name="attn_ctx",
    )(zb, zb, zb, attn_lambda, subln_w)

    lat_row = lambda b, j: CTX_TILES + LAT_TILES_PER_SEQ * b + j
    seq_row = lambda b: N_CTX // T_LAT + b
    ao_lat = pl.pallas_call(
        functools.partial(_attn_lat_kernel, lam_init),
        grid=(N_LAT_SEQ, HEADS, LAT_TILES_PER_SEQ),
        in_specs=[
            pl.BlockSpec((TM, HD), lambda b, h, j: (lat_row(b, j), QB + h)),
            pl.BlockSpec((T_LAT, HD), lambda b, h, j: (seq_row(b), KB + h)),
            pl.BlockSpec((T_LAT, HD), lambda b, h, j: (seq_row(b), VB + h)),
            pl.BlockSpec((None, None, None, PAST, HD), lambda b, h, j: (b, l, h, 0, 0)),
            pl.BlockSpec((None, None, None, PAST, HD), lambda b, h, j: (b, l, h, 0, 0)),
            pl.BlockSpec((T_LAT, HD), lambda b, h, j: (0, 0)),
            pl.BlockSpec((T_LAT, HD), lambda b, h, j: (0, 0)),
            pl.BlockSpec((TM, HD), lambda b, h, j: (j, 0)),
            pl.BlockSpec((TM, HD), lambda b, h, j: (j, 0)),
            pl.BlockSpec((None, 4, 64), lambda b, h, j: (l, 0, 0)),
            pl.BlockSpec((None, 1, HD), lambda b, h, j: (l, 0, 0)),
        ],
        out_specs=pl.BlockSpec((TM, HD), lambda b, h, j: (LAT_TILES_PER_SEQ * b + j, h)),
        out_shape=jax.ShapeDtypeStruct((N_LAT, HEADS * HD), BF16),
        scratch_shapes=[pltpu.VMEM((PAST + T_LAT, HD), BF16), pltpu.VMEM((PAST + T_LAT, HD), BF16)],
        compiler_params=_params(3),
        name="attn_lat",
    )(zq, zq, zb, cache_k, cache_v, cos, sin, cos, sin, attn_lambda, subln_w)
    return ao, ao_lat


def _decay_matrix(lgf, lgb, row0, tq, tk):
    i = row0 + lax.broadcasted_iota(jnp.int32, (tq, tk), 0)
    j = lax.broadcasted_iota(jnp.int32, (tq, tk), 1)
    d = (i - j).astype(F32)
    fwd = jnp.where(d >= 0.0, jnp.exp(lgf * jnp.maximum(d, 0.0)), 0.0)
    bwd = jnp.where(d <= 0.0, jnp.exp(lgb * jnp.maximum(-d, 0.0)), 0.0)
    return fwd + bwd


def _ret_finish(o, g, nw):
    g = g.astype(F32)
    return ((g * _sigmoid(g)) * (_rms(o) * nw)).astype(BF16)


def _ret_ctx_kernel(q_ref, k_ref, v_ref, g_ref, df_ref, db_ref, nw_ref, o_ref, sf_ref, sb_ref, dmat):
    @pl.when(pl.program_id(0) == 0)
    def _():
        for h in range(HEADS):
            dmat[h] = RET_SCALE * _decay_matrix(_log_sigmoid(df_ref[h]), _log_sigmoid(db_ref[h]), 0, T_CTX, T_CTX)

    j = lax.broadcasted_iota(jnp.int32, (T_CTX, 1), 0).astype(F32)
    for h in range(HEADS):
        sl = slice(h * HD, (h + 1) * HD)
        q = q_ref[:, sl]
        k = k_ref[:, sl]
        v = v_ref[:, sl]
        s = _dot_nt(q, k) * dmat[h]
        o = _dot(s.astype(BF16), v)
        o_ref[:, sl] = _ret_finish(o, g_ref[:, sl], nw_ref[h])
        lgf = _log_sigmoid(df_ref[h])
        lgb = _log_sigmoid(db_ref[h])
        kf = k.astype(F32) * RET_SCALE
        sf_ref[h] = _dot_tn((kf * jnp.exp(lgf * (T_CTX - 1.0 - j))).astype(BF16), v)
        sb_ref[h] = _dot_tn((kf * jnp.exp(lgb * j)).astype(BF16), v)


def _ret_lat_kernel(q_ref, k_ref, v_ref, g_ref, s0f_ref, s0b_ref, df_ref, db_ref, nw_ref, o_ref, dmat):
    row0 = pl.program_id(1) * TM
    lgf = _log_sigmoid(df_ref[0])
    lgb = _log_sigmoid(db_ref[0])

    @pl.when(pl.program_id(2) == 0)
    def _():
        dmat[...] = RET_SCALE * _decay_matrix(lgf, lgb, row0, TM, T_LAT)

    q = q_ref[...]
    s = _dot_nt(q, k_ref[...]) * dmat[...]
    o = _dot(s.astype(BF16), v_ref[...])
    i = (row0 + lax.broadcasted_iota(jnp.int32, (TM, 1), 0)).astype(F32)
    o = o + jnp.exp(lgf * (i + 1.0)) * _dot(q, s0f_ref[...].astype(BF16))
    o = o + jnp.exp(lgb * (T_LAT - i)) * _dot(q, s0b_ref[...].astype(BF16))
    o_ref[...] = _ret_finish(o, g_ref[...], nw_ref[0])


def _retention(l, zb, state_f, state_b, decay_f, decay_b, ret_norm_w):
    width = HEADS * HD
    dec1 = pl.BlockSpec((HEADS, 1, 1), lambda b: (l, 0, 0))
    nw1 = pl.BlockSpec((HEADS, 1, HD), lambda b: (l, 0, 0))
    st_spec = pl.BlockSpec((None, HEADS, HD, HD), lambda b: (b, 0, 0, 0))
    st_shape = jax.ShapeDtypeStruct((N_CTX_SEQ, HEADS, HD, HD), F32)
    ro, sf, sb = pl.pallas_call(
        _ret_ctx_kernel,
        grid=(N_CTX_SEQ,),
        in_specs=[
            pl.BlockSpec((T_CTX, width), lambda b: (b, RQB // HEADS)),
            pl.BlockSpec((T_CTX, width), lambda b: (b, RKB // HEADS)),
            pl.BlockSpec((T_CTX, width), lambda b: (b, RVB // HEADS)),
            pl.BlockSpec((T_CTX, width), lambda b: (b, RGB // HEADS)),
            dec1, dec1, nw1,
        ],
        out_specs=[pl.BlockSpec((T_CTX, width), lambda b: (b, 0)), st_spec, st_spec],
        out_shape=[jax.ShapeDtypeStruct((N_CTX, width), BF16), st_shape, st_shape],
        scratch_shapes=[pltpu.VMEM((HEADS, T_CTX, T_CTX), F32)],
        compiler_params=_params(1),
        name="ret_ctx",
    )(zb, zb, zb, zb, decay_f, decay_b, ret_norm_w)

    lat_row = lambda b, j: CTX_TILES + LAT_TILES_PER_SEQ * b + j
    seq_row = lambda b: N_CTX // T_LAT + b
    dec3 = pl.BlockSpec((1, 1, 1), lambda h, j, b: (l * HEADS + h, 0, 0))
    nw3 = pl.BlockSpec((1, 1, HD), lambda h, j, b: (l * HEADS + h, 0, 0))
    s0_spec = pl.BlockSpec((None, None, None, HD, HD), lambda h, j, b: (b, l, h, 0, 0))
    ro_lat = pl.pallas_call(
        _ret_lat_kernel,
        grid=(HEADS, LAT_TILES_PER_SEQ, N_LAT_SEQ),
        in_specs=[
            pl.BlockSpec((TM, HD), lambda h, j, b: (lat_row(b, j), RQB + h)),
            pl.BlockSpec((T_LAT, HD), lambda h, j, b: (seq_row(b), RKB + h)),
            pl.BlockSpec((T_LAT, HD), lambda h, j, b: (seq_row(b), RVB + h)),
            pl.BlockSpec((TM, HD), lambda h, j, b: (lat_row(b, j), RGB + h)),
            s0_spec, s0_spec, dec3, dec3, nw3,
        ],
        out_specs=pl.BlockSpec((TM, HD), lambda h, j, b: (LAT_TILES_PER_SEQ * b + j, h)),
        out_shape=jax.ShapeDtypeStruct((N_LAT, width), BF16),
        scratch_shapes=[pltpu.VMEM((TM, T_LAT), F32)],
        compiler_params=_params(3),
        name="ret_lat",
    )(zb, zb, zb, zb, state_f, state_b, decay_f, decay_b, ret_norm_w)
    return ro, ro_lat, sf, sb


def _merge_kernel(x_ref, aoc_ref, aol_ref, roc_ref, rol_ref, cb_ref, cc_ref, cx_ref, ccp_ref, cxp_ref,
                  ccn_ref, cxn_ref, mg0_ref, mg1_ref, mg2_ref, mod_ref, cw_ref, wa_ref, wc_ref, wr_ref,
                  wo_ref, n2_ref, wrt_ref, x1_ref, h2_ref, aff_ref):
    i = pl.program_id(0)
    is_ctx = i < CTX_TILES
    ao = jnp.where(is_ctx, aoc_ref[...], aol_ref[...])
    ro = jnp.where(is_ctx, roc_ref[...], rol_ref[...])
    j = (i - CTX_TILES) % LAT_TILES_PER_SEQ
    seq_first = jnp.logical_or(i < CTX_TILES, j == 0)
    seq_last = jnp.logical_or(i < CTX_TILES, j == LAT_TILES_PER_SEQ - 1)

    u = cc_ref[...].astype(F32) * cx_ref[...].astype(F32)
    up = (ccp_ref[...].astype(F32) * cxp_ref[...].astype(F32))[15:16, :] * jnp.where(seq_first, 0.0, 1.0)
    dn = (ccn_ref[...].astype(F32) * cxn_ref[...].astype(F32))[0:1, :] * jnp.where(seq_last, 0.0, 1.0)
    r = lax.broadcasted_iota(jnp.int32, u.shape, 0)
    u_prev = jnp.where(r == 0, up, pltpu.roll(u, 1, 0))
    u_next = jnp.where(r == TM - 1, dn, pltpu.roll(u, TM - 1, 0))
    cw = cw_ref[...]
    conv = u_prev * cw[0:1, :] + u * cw[1:2, :] + u_next * cw[2:3, :]
    conv_o = (cb_ref[...].astype(F32) * conv).astype(BF16)

    merged = _sigmoid(mg0_ref[...].astype(F32)) * _dot(ao, wa_ref[...])
    merged = merged + _sigmoid(mg1_ref[...].astype(F32)) * _dot(conv_o, wc_ref[...])
    merged = merged + _sigmoid(mg2_ref[...].astype(F32)) * _dot(ro, wr_ref[...])

    m = mod_ref[0]
    gate1 = m[:, 2 * D:3 * D]
    shift2 = m[:, 3 * D:4 * D]
    scale2 = m[:, 4 * D:5 * D]
    x1 = x_ref[...] + gate1 * _dot(merged.astype(BF16), wo_ref[...])
    x1_ref[...] = x1
    h2 = (_rms(x1) * n2_ref[...]) * (1.0 + scale2) + shift2
    _to_slabs(h2_ref, h2)

    logits = _dot(h2.astype(BF16), wrt_ref[...])
    lane = lax.broadcasted_iota(jnp.int32, logits.shape, 1)
    valid = lane < N_EXPERTS
    lmax = jnp.max(jnp.where(valid, logits, -jnp.inf), axis=-1, keepdims=True)
    e = jnp.where(valid, jnp.exp(logits - lmax), 0.0)
    aff_ref[...] = e * (1.0 / jnp.sum(e, axis=-1, keepdims=True))


def _merge(l, x, ao, ao_lat, ro, ro_lat, zb, mod, conv_w, w_br_attn, w_br_conv, w_br_ret, w_out, norm2_w,
           w_router_pad):
    n16 = N_TOK // 16
    ctx_br = pl.BlockSpec((TM, 512), lambda i: (jnp.minimum(i, CTX_TILES - 1), 0))
    lat_br = pl.BlockSpec((TM, 512), lambda i: (jnp.maximum(i - CTX_TILES, 0), 0))
    col = lambda c: pl.BlockSpec((TM, 512), lambda i: (i, c))
    halo_p = lambda c: pl.BlockSpec((16, 512), lambda i: (jnp.maximum(i * (TM // 16) - 1, 0), c))
    halo_n = lambda c: pl.BlockSpec((16, 512), lambda i: (jnp.minimum((i + 1) * (TM // 16), n16 - 1), c))
    mgs = lambda c: pl.BlockSpec((TM, D), lambda i: (i, c))
    wbr = pl.BlockSpec((None, 512, D), lambda i: (l, 0, 0))
    return pl.pallas_call(
        _merge_kernel,
        grid=(N_TILES,),
        in_specs=[
            pl.BlockSpec((TM, D), lambda i: (i, 0)),
            ctx_br, lat_br, ctx_br, lat_br,
            col(3), col(4), col(5), halo_p(4), halo_p(5), halo_n(4), halo_n(5),
            mgs(5), mgs(6), mgs(7),
            pl.BlockSpec((1, 1, N_MOD * D), lambda i: (l * 8 + _mod_row(i), 0, 0)),
            pl.BlockSpec((None, 3, 512), lambda i: (l, 0, 0)),
            wbr, wbr, wbr,
            pl.BlockSpec((None, D, D), lambda i: (l, 0, 0)),
            pl.BlockSpec((None, 1, D), lambda i: (l, 0, 0)),
            pl.BlockSpec((None, D, 128), lambda i: (l, 0, 0)),
        ],
        out_specs=[
            pl.BlockSpec((TM, D), lambda i: (i, 0)),
            pl.BlockSpec((TM * SLAB, 128), lambda i: (i, 0)),
            pl.BlockSpec((TM, 128), lambda i: (i, 0)),
        ],
        out_shape=[
            jax.ShapeDtypeStruct((N_TOK, D), F32),
            jax.ShapeDtypeStruct((N_TOK * SLAB, 128), F32),
            jax.ShapeDtypeStruct((N_TOK, 128), F32),
        ],
        compiler_params=_params(1),
        name="merge",
    )(x, ao, ao_lat, ro, ro_lat, zb, zb, zb, zb, zb, zb, zb, zb, zb, zb, mod, conv_w,
      w_br_attn, w_br_conv, w_br_ret, w_out, norm2_w, w_router_pad)


def _cumsum_lanes(x, tri):
    run = jnp.zeros((x.shape[0], 1), F32)
    outs = []
    for b in range(x.shape[1] // 128):
        cs = _dot(x[:, b * 128:(b + 1) * 128].astype(BF16), tri) + run
        run = cs[:, 127:128]
        outs.append(cs)
    return jnp.concatenate(outs, axis=-1)


GATE_LANE = (0, 16, 32)
IDX_LANE = 48


def _topk_kernel(an_ref, tmat_ref, idx_ref, gate_ref):
    an = an_ref[...]
    a = an.T[0:N_EXPERTS, :]
    n = a.shape[1]
    kf = float(CAP)

    hi = an.astype(BF16).astype(F32)
    mid = (an - hi).astype(BF16).astype(F32)
    lo = ((an - hi) - mid).astype(BF16).astype(F32)
    table = (tmat_ref[...].astype(F32) + hi + pltpu.roll(mid, GATE_LANE[1], 1)
             + pltpu.roll(lo, GATE_LANE[2], 1)).astype(BF16)

    def count_gt(thr):
        return jnp.sum(jnp.where(a > thr, 1.0, 0.0), axis=-1, keepdims=True)

    def span(lo, hi):
        inside = jnp.logical_and(a > lo, a <= hi)
        cmax = jnp.max(jnp.where(inside, a, -jnp.inf), axis=-1, keepdims=True)
        cmin = jnp.min(jnp.where(inside, a, jnp.inf), axis=-1, keepdims=True)
        return cmax, cmin

    def cond(c):
        return jnp.logical_and(c[2] > 0, c[3] < 400)

    def body(c):
        lo, hi, _, it = c
        mid = 0.5 * (lo + hi)
        ge = count_gt(mid) >= kf
        lo = jnp.where(ge, mid, lo)
        hi = jnp.where(ge, hi, mid)
        cmax, cmin = span(lo, hi)
        open_rows = jnp.max(jnp.where(cmax != cmin, 1, 0))
        return lo, hi, open_rows, it + 1

    lo0 = jnp.full((N_EXPERTS, 1), -1.0, F32)
    hi0 = jnp.max(a, axis=-1, keepdims=True)
    cmax0, cmin0 = span(lo0, hi0)
    lo, hi, _, _ = lax.while_loop(
        cond, body, (lo0, hi0, jnp.max(jnp.where(cmax0 != cmin0, 1, 0)), jnp.int32(0)))
    thr, _ = span(lo, hi)

    r = lax.broadcasted_iota(jnp.int32, (128, 128), 0)
    c = lax.broadcasted_iota(jnp.int32, (128, 128), 1)
    tri = jnp.where(r <= c, 1.0, 0.0).astype(BF16)
    gt = a > thr
    eq = jnp.where(a == thr, 1.0, 0.0)
    need = kf - count_gt(thr)
    eq_before = _cumsum_lanes(eq, tri) - eq
    sel = jnp.where(jnp.logical_or(gt, jnp.logical_and(eq > 0.0, eq_before < need)), 1.0, 0.0)
    pos = _cumsum_lanes(sel, tri) - 1.0
    slot = jnp.where(sel > 0.0, pos, -1.0).astype(jnp.int32)

    p_iota = lax.broadcasted_iota(jnp.int32, (CAP, 1024), 0)
    lane = lax.broadcasted_iota(jnp.int32, (CAP, 128), 1)
    for e in range(N_EXPERTS):
        acc = jnp.zeros((CAP, 128), F32)
        for cb in range(n // 1024):
            onehot = jnp.where(p_iota == slot[e:e + 1, cb * 1024:(cb + 1) * 1024], 1.0, 0.0).astype(BF16)
            acc = acc + _dot(onehot, table[cb * 1024:(cb + 1) * 1024, :])
        idx_ref[0, e] = (acc[:, IDX_LANE:IDX_LANE + 1] * 64.0 + acc[:, IDX_LANE + 1:IDX_LANE + 2]).astype(jnp.int32)
        g = jnp.zeros((CAP, 1), F32)
        for off in GATE_LANE:
            g = g + jnp.sum(jnp.where(lane == off + e, acc, 0.0), axis=-1, keepdims=True)
        gate_ref[0, e] = g


def _topk(aff_n, tmat):
    out_spec = pl.BlockSpec((1, N_EXPERTS, CAP, 1), lambda s: (s, 0, 0, 0))
    return pl.pallas_call(
        _topk_kernel,
        grid=(2,),
        in_specs=[
            pl.BlockSpec((N_CTX, 128), lambda s: (s, 0)),
            pl.BlockSpec((N_CTX, 128), lambda s: (0, 0)),
        ],
        out_specs=[out_spec, out_spec],
        out_shape=[jax.ShapeDtypeStruct((2, N_EXPERTS, CAP, 1), jnp.int32),
                   jax.ShapeDtypeStruct((2, N_EXPERTS, CAP, 1), F32)],
        compiler_params=_params(1),
        name="topk",
    )(aff_n, tmat)


N_FT = FF // TF
GATHER_ROWS = 2 * CAP
ROWS_PER_STEP = GATHER_ROWS // N_FT


def _ffn_kernel(l, idx_ref, h2_hbm, g_ref, wg_ref, wu_ref, wd_ref, ye_ref, xg, xb, acc, sems):
    e = pl.program_id(0)
    f = pl.program_id(1)
    slot = e % 2

    def row_copy(expert, s, p, dst_slot):
        row = idx_ref[(s * N_EXPERTS + expert) * CAP + p] + s * N_CTX
        src = h2_hbm.at[pl.ds(pl.multiple_of(row * SLAB, SLAB), SLAB), :]
        dst = xg.at[dst_slot, pl.ds(pl.multiple_of((s * CAP + p) * SLAB, SLAB), SLAB), :]
        return pltpu.make_async_copy(src, dst, sems.at[dst_slot])

    def slot_wait(dst_slot):
        pltpu.make_async_copy(h2_hbm.at[pl.ds(0, GATHER_ROWS * SLAB), :], xg.at[dst_slot],
                              sems.at[dst_slot]).wait()

    @pl.when(jnp.logical_and(e == 0, f == 0))
    def _():
        def issue(r, carry):
            row_copy(0, r // CAP, r % CAP, 0).start()
            return carry
        lax.fori_loop(0, GATHER_ROWS, issue, 0)

    @pl.when(f == 0)
    def _():
        slot_wait(slot)
        xb[...] = _from_slabs(xg.at[slot]).astype(BF16)
        acc[...] = jnp.zeros_like(acc)

    nxt = jnp.minimum(e + 1, N_EXPERTS - 1)
    s_nxt = f // (N_FT // 2)
    p0 = (f % (N_FT // 2)) * ROWS_PER_STEP
    for u in range(ROWS_PER_STEP):
        row_copy(nxt, s_nxt, p0 + u, 1 - slot).start()

    x = xb[...]
    hg = _dot(x, wg_ref[...].astype(BF16))
    hu = _dot(x, wu_ref[...].astype(BF16))
    hdn = ((hg * _sigmoid(hg)) * hu).astype(BF16)
    acc[...] += _dot(hdn, wd_ref[...].astype(BF16))

    @pl.when(f == N_FT - 1)
    def _():
        for s in range(2):
            _to_slabs(ye_ref.at[s], acc[s * CAP:(s + 1) * CAP, :] * g_ref[s])

    @pl.when(jnp.logical_and(e == N_EXPERTS - 1, f == N_FT - 1))
    def _():
        slot_wait(1 - slot)


def _expert_ffn(l, idx_flat, h2s, gates, w_gate, w_up, w_down):
    grid_spec = pltpu.PrefetchScalarGridSpec(
        num_scalar_prefetch=1,
        grid=(N_EXPERTS, FF // TF),
        in_specs=[
            pl.BlockSpec(memory_space=pl.ANY),
            pl.BlockSpec((2, None, CAP, 1), lambda e, f, idx: (0, e, 0, 0)),
            pl.BlockSpec((None, None, D, TF), lambda e, f, idx: (l, e, 0, f)),
            pl.BlockSpec((None, None, D, TF), lambda e, f, idx: (l, e, 0, f)),
            pl.BlockSpec((None, None, TF, D), lambda e, f, idx: (l, e, f, 0)),
        ],
        out_specs=pl.BlockSpec((2, None, CAP * SLAB, 128), lambda e, f, idx: (0, e, 0, 0)),
        scratch_shapes=[
            pltpu.VMEM((2, GATHER_ROWS * SLAB, 128), F32),
            pltpu.VMEM((GATHER_ROWS, D), BF16),
            pltpu.VMEM((GATHER_ROWS, D), F32),
            pltpu.SemaphoreType.DMA((2,)),
        ],
    )
    return pl.pallas_call(
        functools.partial(_ffn_kernel, l),
        grid_spec=grid_spec,
        out_shape=jax.ShapeDtypeStruct((2, N_EXPERTS, CAP * SLAB, 128), F32),
        compiler_params=_params(2),
        name="expert_ffn",
    )(idx_flat, h2s, gates, w_gate, w_up, w_down)


SCATTER_UNROLL = 8


def _combine_kernel(idx_ref, ye_ref, y_ref):
    s = pl.program_id(0)
    e = pl.program_id(1)

    @pl.when(e == 0)
    def _():
        y_ref[...] = jnp.zeros_like(y_ref)

    base = (s * N_EXPERTS + e) * CAP

    def group(g, carry):
        p0 = g * SCATTER_UNROLL
        tile = lambda r: pl.ds(pl.multiple_of(r * SLAB, SLAB), SLAB)
        rows = [idx_ref[base + p0 + u] for u in range(SCATTER_UNROLL)]
        vals = [y_ref[tile(rows[u]), :] + ye_ref[tile(p0 + u), :] for u in range(SCATTER_UNROLL)]
        for u in range(SCATTER_UNROLL):
            y_ref[tile(rows[u]), :] = vals[u]
        return carry

    lax.fori_loop(0, CAP // SCATTER_UNROLL, group, 0)


def _combine(idx_flat, ye):
    grid_spec = pltpu.PrefetchScalarGridSpec(
        num_scalar_prefetch=1,
        grid=(2, N_EXPERTS),
        in_specs=[pl.BlockSpec((None, None, CAP * SLAB, 128), lambda s, e, idx: (s, e, 0, 0))],
        out_specs=pl.BlockSpec((N_CTX * SLAB, 128), lambda s, e, idx: (s, 0)),
    )
    return pl.pallas_call(
        _combine_kernel,
        grid_spec=grid_spec,
        out_shape=jax.ShapeDtypeStruct((N_TOK * SLAB, 128), F32),
        compiler_params=_params(2),
        name="combine",
    )(idx_flat, ye)


def _final_kernel(x_ref, y_ref, mod_ref, w_ref, op_ref, os_ref):
    x = x_ref[...] + mod_ref[0][:, 5 * D:6 * D] * _from_slabs(y_ref)
    out = _rms(x) * w_ref[...]
    is_ctx = pl.program_id(0) < CTX_TILES

    @pl.when(is_ctx)
    def _():
        op_ref[...] = out

    @pl.when(jnp.logical_not(is_ctx))
    def _():
        os_ref[...] = out


def _final(x1, y, mod, final_norm_w):
    tile = pl.BlockSpec((TM, D), lambda i: (i, 0))
    return pl.pallas_call(
        _final_kernel,
        grid=(N_TILES,),
        in_specs=[tile, pl.BlockSpec((TM * SLAB, 128), lambda i: (i, 0)),
                  pl.BlockSpec((1, 1, N_MOD * D), lambda i: ((DEPTH - 1) * 8 + _mod_row(i), 0, 0)),
                  pl.BlockSpec((1, D), lambda i: (0, 0))],
        out_specs=[pl.BlockSpec((TM, D), lambda i: (jnp.minimum(i, CTX_TILES - 1), 0)),
                   pl.BlockSpec((TM, D), lambda i: (jnp.maximum(i - CTX_TILES, 0), 0))],
        out_shape=[jax.ShapeDtypeStruct((N_CTX, D), F32), jax.ShapeDtypeStruct((N_LAT, D), F32)],
        compiler_params=_params(1),
        name="final_norm",
    )(x1, y, mod, final_norm_w.reshape(1, D))


def _rope_tables():
    t = np.arange(T_LAT)
    row = (t // GRID_W).astype(np.float32)
    col = (t % GRID_W).astype(np.float32)
    inv = jnp.asarray(ROPE_BASE, F32) ** (-jnp.arange(N_ROPE_FREQ, dtype=F32) / N_ROPE_FREQ)
    ang_r = jnp.asarray(row)[:, None] * inv
    ang_c = jnp.asarray(col)[:, None] * inv
    def group(ang):
        return jnp.concatenate([ang, ang], axis=-1)
    ang = jnp.concatenate([group(ang_r), group(ang_c), group(ang_r), group(ang_c)], axis=-1)
    sign = np.where(np.arange(HD) % 32 < 16, -1.0, 1.0).astype(np.float32)
    return jnp.cos(ang), jnp.sin(ang) * sign


def _index_table():
    t = np.arange(N_CTX)
    tm = np.zeros((N_CTX, 128), np.float32)
    tm[:, IDX_LANE] = t // 64
    tm[:, IDX_LANE + 1] = t % 64
    return jnp.asarray(tm, BF16)


def kernel(x_prompt, x_sample, c, cache_attn_k, cache_attn_v, state_ret_fwd, state_ret_bwd, c_ctx, w_ada, b_ada, norm1_w, norm2_w, w_in, attn_lambda, attn_subln_w, conv_w, ret_decay_fwd, ret_decay_bwd, ret_norm_w, w_br_attn, w_br_conv, w_br_ret, w_out, w_router, w_exp_gate, w_exp_up, w_exp_down, final_norm_w):
    x = jnp.concatenate([x_prompt.reshape(N_CTX, D), x_sample.reshape(N_LAT, D)], axis=0)
    cvec = jnp.concatenate([c_ctx[None, :], c, jnp.zeros((3, D), F32)], axis=0)
    mod = _modulation(cvec, w_ada, b_ada).reshape(DEPTH * 8, 1, N_MOD * D)

    w_in_bf = w_in.astype(BF16)
    w_br_attn_bf = w_br_attn.astype(BF16)
    w_br_conv_bf = w_br_conv.astype(BF16)
    w_br_ret_bf = w_br_ret.astype(BF16)
    w_out_bf = w_out.astype(BF16)
    w_router_pad = jnp.pad(w_router, ((0, 0), (0, 0), (0, 128 - N_EXPERTS))).astype(BF16)
    norm1 = norm1_w.reshape(DEPTH, 1, D)
    norm2 = norm2_w.reshape(DEPTH, 1, D)
    subln = attn_subln_w.reshape(DEPTH, 1, HD)
    decay_f = ret_decay_fwd.reshape(DEPTH * HEADS, 1, 1)
    decay_b = ret_decay_bwd.reshape(DEPTH * HEADS, 1, 1)
    ret_nw = ret_norm_w.reshape(DEPTH * HEADS, 1, HD)
    cos, sin = _rope_tables()
    tmat = _index_table()

    y = None
    new_k = new_v = None
    sfs, sbs = [], []
    for l in range(DEPTH):
        lam_init = 0.8 - 0.6 * math.exp(-0.3 * l)
        if y is None:
            zb, zq, new_k, new_v = _inproj(l, x, None, mod, norm1, w_in_bf, new_k, new_v)
        else:
            zb, zq, new_k, new_v, x = _inproj(l, x, y, mod, norm1, w_in_bf, new_k, new_v)
        ao, ao_lat = _attention(l, lam_init, zb, zq, cache_attn_k, cache_attn_v, cos, sin, attn_lambda, subln)
        ro, ro_lat, sf, sb = _retention(l, zb, state_ret_fwd, state_ret_bwd, decay_f, decay_b, ret_nw)
        x, h2s, aff_n = _merge(l, x, ao, ao_lat, ro, ro_lat, zb, mod, conv_w, w_br_attn_bf, w_br_conv_bf,
                               w_br_ret_bf, w_out_bf, norm2, w_router_pad)
        idx, gates = _topk(aff_n, tmat)
        idx_flat = idx.reshape(2 * N_EXPERTS * CAP)
        ye = _expert_ffn(l, idx_flat, h2s, gates, w_exp_gate, w_exp_up, w_exp_down)
        y = _combine(idx_flat, ye)

        sfs.append(sf)
        sbs.append(sb)

    y_prompt, y_sample = _final(x, y, mod, final_norm_w)
    return (y_prompt.reshape(N_CTX_SEQ, T_CTX, D), y_sample.reshape(N_LAT_SEQ, T_LAT, D), new_k, new_v,
            jnp.stack(sfs, axis=1), jnp.stack(sbs, axis=1))
```

```python
import functools
import math

import jax
import jax.numpy as jnp
import numpy as np
from jax import lax
from jax.experimental import pallas as pl
from jax.experimental.pallas import tpu as pltpu

F32 = jnp.float32
BF16 = jnp.bfloat16

D = 1024
DEPTH = 2
N_CTX_SEQ = 16
T_CTX = 256
N_LAT_SEQ = 4
T_LAT = 1024
PAST = 256
N_CTX = N_CTX_SEQ * T_CTX
N_LAT = N_LAT_SEQ * T_LAT
N_TOK = N_CTX + N_LAT
TM = 256
N_TILES = N_TOK // TM
CTX_TILES = N_CTX // TM
LAT_TILES_PER_SEQ = T_LAT // TM
HEADS = 4
HD = 128
GRID_W = 64
N_ROPE_FREQ = 16
ROPE_BASE = 10000.0
IN_COLS = 8192
N_MOD = 6
N_EXPERTS = 16
CAP = 512
FF = 2048
TF = 512
EPS = 1e-6
RET_SCALE = HD ** -0.5
ATTN_SCALE = 64 ** -0.5
SLAB = D // 128
VMEM_LIMIT = 56 * 1024 * 1024

QB, KB, VB = 0, 4, 8
RQB, RKB, RVB, RGB = 24, 28, 32, 36


def _sigmoid(x):
    return 0.5 * jnp.tanh(0.5 * x) + 0.5


def _log_sigmoid(x):
    return jnp.minimum(x, 0.0) - jnp.log(1.0 + jnp.exp(-jnp.abs(x)))


def _rms(x):
    return x * lax.rsqrt(jnp.mean(x * x, axis=-1, keepdims=True) + EPS)


def _dot(a, b):
    return jnp.dot(a, b, preferred_element_type=F32)


def _dot_nt(a, b):
    return lax.dot_general(a, b, (((1,), (1,)), ((), ())), preferred_element_type=F32)


def _dot_tn(a, b):
    return lax.dot_general(a, b, (((0,), (0,)), ((), ())), preferred_element_type=F32)


def _to_slabs(ref, x):
    n = x.shape[0]
    for s in range(SLAB):
        ref[pl.ds(s, n, stride=SLAB), :] = x[:, s * 128:(s + 1) * 128]


def _from_slabs(ref):
    n = ref.shape[0] // SLAB
    return jnp.concatenate([ref[pl.ds(s, n, stride=SLAB), :] for s in range(SLAB)], axis=-1)


def _mod_row(i):
    return jnp.where(i < CTX_TILES, 0, 1 + (i - CTX_TILES) // LAT_TILES_PER_SEQ)


def _params(n_axes):
    return pltpu.CompilerParams(
        dimension_semantics=("arbitrary",) * n_axes, vmem_limit_bytes=VMEM_LIMIT)


def _mod_kernel(c_ref, w_ref, b_ref, o_ref):
    c = c_ref[...]
    s = (c * _sigmoid(c)).astype(BF16)
    o_ref[...] = _dot(s, w_ref[...].astype(BF16)) + b_ref[...]


def _modulation(cvec, w_ada, b_ada):
    tn = 1024
    return pl.pallas_call(
        _mod_kernel,
        grid=(DEPTH, N_MOD * D // tn),
        in_specs=[
            pl.BlockSpec((8, D), lambda l, n: (0, 0)),
            pl.BlockSpec((None, D, tn), lambda l, n: (l, 0, n)),
            pl.BlockSpec((None, 1, tn), lambda l, n: (l, 0, n)),
        ],
        out_specs=pl.BlockSpec((None, 8, tn), lambda l, n: (l, 0, n)),
        out_shape=jax.ShapeDtypeStruct((DEPTH, 8, N_MOD * D), F32),
        compiler_params=_params(2),
        name="modulation",
    )(cvec, w_ada, b_ada.reshape(DEPTH, 1, N_MOD * D))


def _inproj_kernel(has_y, x_pair, n_prev, *refs):
    refs = list(refs)
    x_ref = refs.pop(0)
    if x_pair:
        xs_ref = refs.pop(0)
    if has_y:
        y_ref, modp_ref = refs.pop(0), refs.pop(0)
    mod_ref, n1_ref, w_ref = refs.pop(0), refs.pop(0), refs.pop(0)
    if n_prev:
        kp_ref, vp_ref = refs.pop(0), refs.pop(0)
    zb_ref, zq_ref, kn_ref, vn_ref = refs[0:4]
    x = x_ref[...]
    if x_pair:
        x = jnp.where(pl.program_id(0) < N_TILES - CTX_TILES, xs_ref[...], x)
    if has_y:
        x = x + modp_ref[0][:, 5 * D:6 * D] * _from_slabs(y_ref)
        refs[4][...] = x
    m = mod_ref[0]
    shift1 = m[:, 0:D]
    scale1 = m[:, D:2 * D]
    h = ((_rms(x) * n1_ref[...]) * (1.0 + scale1) + shift1).astype(BF16)
    cw = 1024
    for c in range(IN_COLS // cw):
        z = _dot(h, w_ref[:, c * cw:(c + 1) * cw])
        zb_ref[:, c * cw:(c + 1) * cw] = z.astype(BF16)
        if c == 0:
            zq_ref[...] = z
            for hh in range(HEADS):
                kn_ref[n_prev, hh] = z[:, 512 + hh * HD:512 + (hh + 1) * HD]
        if c == 1:
            for hh in range(HEADS):
                vn_ref[n_prev, hh] = z[:, hh * HD:(hh + 1) * HD]
    if n_prev:
        kn_ref[0:n_prev] = kp_ref[...]
        vn_ref[0:n_prev] = vp_ref[...]


def _inproj(l, x, y, mod, norm1_w, w_in_bf, k_prev, v_prev):
    has_y = y is not None
    t = lambda i: (i + CTX_TILES) % N_TILES
    tile = pl.BlockSpec((TM, D), lambda i: (t(i), 0))
    ctx_i = lambda i: jnp.maximum(i - (N_TILES - CTX_TILES), 0)
    kv_spec = lambda n: pl.BlockSpec((None, n, HEADS, T_CTX, HD), lambda i: (ctx_i(i), 0, 0, 0, 0))
    kv_shape = jax.ShapeDtypeStruct((N_CTX_SEQ, l + 1, HEADS, T_CTX, HD), F32)
    x_pair = isinstance(x, tuple)
    if x_pair:
        n_lat = N_TILES - CTX_TILES
        in_specs = [pl.BlockSpec((TM, D), lambda i: (jnp.maximum(i - n_lat, 0), 0)),
                    pl.BlockSpec((TM, D), lambda i: (jnp.minimum(i, n_lat - 1), 0))]
        args = list(x)
    else:
        in_specs = [tile]
        args = [x]
    if has_y:
        in_specs += [pl.BlockSpec((TM * SLAB, 128), lambda i: (t(i), 0)),
                     pl.BlockSpec((1, 1, N_MOD * D), lambda i: ((l - 1) * 8 + _mod_row(t(i)), 0, 0))]
        args += [y, mod]
    in_specs += [
        pl.BlockSpec((1, 1, N_MOD * D), lambda i: (l * 8 + _mod_row(t(i)), 0, 0)),
        pl.BlockSpec((None, 1, D), lambda i: (l, 0, 0)),
        pl.BlockSpec((None, D, IN_COLS), lambda i: (l, 0, 0), pipeline_mode=pl.Buffered(1)),
    ]
    args += [mod, norm1_w, w_in_bf]
    if l:
        in_specs += [kv_spec(l), kv_spec(l)]
        args += [k_prev, v_prev]
    out_specs = [pl.BlockSpec((TM, IN_COLS), lambda i: (t(i), 0)),
                 pl.BlockSpec((TM, 1024), lambda i: (t(i), 0)),
                 kv_spec(l + 1), kv_spec(l + 1)]
    out_shape = [jax.ShapeDtypeStruct((N_TOK, IN_COLS), BF16),
                 jax.ShapeDtypeStruct((N_TOK, 1024), F32),
                 kv_shape, kv_shape]
    if has_y:
        out_specs.append(tile)
        out_shape.append(jax.ShapeDtypeStruct((N_TOK, D), F32))
    return pl.pallas_call(
        functools.partial(_inproj_kernel, has_y, x_pair, l),
        grid=(N_TILES,),
        in_specs=in_specs,
        out_specs=out_specs,
        out_shape=out_shape,
        compiler_params=_params(1),
        name="inproj",
    )(*args)


def _lambda(lam_ref, lam_init):
    lv = lam_ref[...]
    a = jnp.sum(lv[0:1] * lv[1:2], axis=-1, keepdims=True)
    b = jnp.sum(lv[2:3] * lv[3:4], axis=-1, keepdims=True)
    return jnp.exp(a) - jnp.exp(b) + lam_init


def _diff_attention(lam_init, q, k_t, v_all, lam, sw):
    lane = lax.broadcasted_iota(jnp.int32, q.shape, 1)
    zero = jnp.zeros_like(q)

    def exp_map(qm):
        s = _dot(qm, k_t)
        e = jnp.exp(s - jnp.max(s, axis=-1, keepdims=True))
        return e, 1.0 / jnp.sum(e, axis=-1, keepdims=True)

    e0, r0 = exp_map(jnp.where(lane < 64, q, zero))
    e1, r1 = exp_map(jnp.where(lane >= 64, q, zero))
    a = e0 * r0 - e1 * (lam * r1)
    o = _dot(a.astype(BF16), v_all)
    return (_rms(o) * sw) * (1.0 - lam_init)


def _scaled_q(q):
    return (q.astype(F32) * ATTN_SCALE).astype(BF16)


def _attn_ctx_kernel(lam_init, q_ref, k_ref, v_ref, lam_ref, sw_ref, o_ref):
    lam = _lambda(lam_ref, lam_init)
    for h in range(HEADS):
        sl = slice(h * HD, (h + 1) * HD)
        k_t = k_ref[:, sl].astype(F32).T.astype(BF16)
        o = _diff_attention(lam_init, _scaled_q(q_ref[:, sl]), k_t, v_ref[:, sl], lam, sw_ref[...])
        o_ref[:, sl] = o.astype(BF16)


def _rope(x, cos, sin_signed):
    lane = lax.broadcasted_iota(jnp.int32, x.shape, 1)
    partner = jnp.where(lane % 32 < 16, pltpu.roll(x, 112, 1), pltpu.roll(x, 16, 1))
    return x * cos + partner * sin_signed


def _attn_lat_kernel(lam_init, q_ref, k_ref, v_ref, ck_ref, cv_ref, cos_ref, sin_ref,
                     cosq_ref, sinq_ref, lam_ref, sw_ref, o_ref, k_t, vall):
    @pl.when(pl.program_id(1) == 0)
    def _():
        for h in range(HEADS):
            sl = slice(h * HD, (h + 1) * HD)
            k_t[h, :, 0:PAST] = ck_ref[h].T.astype(BF16)
            k_t[h, :, PAST:] = _rope(k_ref[:, sl], cos_ref[...], sin_ref[...]).T.astype(BF16)
            vall[h, 0:PAST, :] = cv_ref[h].astype(BF16)
            vall[h, PAST:, :] = v_ref[:, sl]

    lam = _lambda(lam_ref, lam_init)
    for h in range(HEADS):
        sl = slice(h * HD, (h + 1) * HD)
        q = _scaled_q(_rope(q_ref[:, sl], cosq_ref[...], sinq_ref[...]))
        o = _diff_attention(lam_init, q, k_t[h], vall[h], lam, sw_ref[...])
        o_ref[:, sl] = o.astype(BF16)


def _attention(l, lam_init, zb, zq, cache_k, cache_v, cos, sin, attn_lambda, subln_w):
    width = HEADS * HD
    ao = pl.pallas_call(
        functools.partial(_attn_ctx_kernel, lam_init),
        grid=(N_CTX_SEQ,),
        in_specs=[
            pl.BlockSpec((T_CTX, width), lambda b: (b, QB // HEADS)),
            pl.BlockSpec((T_CTX, width), lambda b: (b, KB // HEADS)),
            pl.BlockSpec((T_CTX, width), lambda b: (b, VB // HEADS)),
            pl.BlockSpec((None, 4, 64), lambda b: (l, 0, 0)),
            pl.BlockSpec((None, 1, HD), lambda b: (l, 0, 0)),
        ],
        out_specs=pl.BlockSpec((T_CTX, width), lambda b: (b, 0)),
        out_shape=jax.ShapeDtypeStruct((N_CTX, width), BF16),
        compiler_params=_params(1),
        name="attn_ctx",
    )(zb, zb, zb, attn_lambda, subln_w)

    lat_row = lambda b, j: CTX_TILES + LAT_TILES_PER_SEQ * b + j
    seq_row = lambda b: N_CTX // T_LAT + b
    ao_lat = pl.pallas_call(
        functools.partial(_attn_lat_kernel, lam_init),
        grid=(N_LAT_SEQ, LAT_TILES_PER_SEQ),
        in_specs=[
            pl.BlockSpec((TM, width), lambda b, j: (lat_row(b, j), QB // HEADS)),
            pl.BlockSpec((T_LAT, width), lambda b, j: (seq_row(b), KB // HEADS)),
            pl.BlockSpec((T_LAT, width), lambda b, j: (seq_row(b), VB // HEADS)),
            pl.BlockSpec((None, None, HEADS, PAST, HD), lambda b, j: (b, l, 0, 0, 0)),
            pl.BlockSpec((None, None, HEADS, PAST, HD), lambda b, j: (b, l, 0, 0, 0)),
            pl.BlockSpec((T_LAT, HD), lambda b, j: (0, 0)),
            pl.BlockSpec((T_LAT, HD), lambda b, j: (0, 0)),
            pl.BlockSpec((TM, HD), lambda b, j: (j, 0)),
            pl.BlockSpec((TM, HD), lambda b, j: (j, 0)),
            pl.BlockSpec((None, 4, 64), lambda b, j: (l, 0, 0)),
            pl.BlockSpec((None, 1, HD), lambda b, j: (l, 0, 0)),
        ],
        out_specs=pl.BlockSpec((TM, width), lambda b, j: (LAT_TILES_PER_SEQ * b + j, 0)),
        out_shape=jax.ShapeDtypeStruct((N_LAT, width), BF16),
        scratch_shapes=[pltpu.VMEM((HEADS, HD, PAST + T_LAT), BF16),
                        pltpu.VMEM((HEADS, PAST + T_LAT, HD), BF16)],
        compiler_params=_params(2),
        name="attn_lat",
    )(zq, zq, zb, cache_k, cache_v, cos, sin, cos, sin, attn_lambda, subln_w)
    return ao, ao_lat


def _decay_matrix(lgf, lgb, row0, tq, tk):
    i = row0 + lax.broadcasted_iota(jnp.int32, (tq, tk), 0)
    j = lax.broadcasted_iota(jnp.int32, (tq, tk), 1)
    d = (i - j).astype(F32)
    fwd = jnp.where(d >= 0.0, jnp.exp(lgf * jnp.maximum(d, 0.0)), 0.0)
    bwd = jnp.where(d <= 0.0, jnp.exp(lgb * jnp.maximum(-d, 0.0)), 0.0)
    return fwd + bwd


def _ret_finish(o, g, nw):
    g = g.astype(F32)
    return ((g * _sigmoid(g)) * (_rms(o) * nw)).astype(BF16)


def _ret_ctx_kernel(n_prev, *refs):
    refs = list(refs)
    q_ref, k_ref, v_ref, g_ref, df_ref, db_ref, nw_ref = refs[0:7]
    refs = refs[7:]
    if n_prev:
        sfp_ref, sbp_ref = refs.pop(0), refs.pop(0)
    o_ref, sf_ref, sb_ref, dmat = refs
    @pl.when(pl.program_id(0) == 0)
    def _():
        for h in range(HEADS):
            dmat[h] = RET_SCALE * _decay_matrix(_log_sigmoid(df_ref[h]), _log_sigmoid(db_ref[h]), 0, T_CTX, T_CTX)

    if n_prev:
        sf_ref[0:n_prev] = sfp_ref[...]
        sb_ref[0:n_prev] = sbp_ref[...]
    j = lax.broadcasted_iota(jnp.int32, (T_CTX, 1), 0).astype(F32)
    for h in range(HEADS):
        sl = slice(h * HD, (h + 1) * HD)
        q = q_ref[:, sl]
        k = k_ref[:, sl]
        v = v_ref[:, sl]
        s = _dot_nt(q, k) * dmat[h]
        o = _dot(s.astype(BF16), v)
        o_ref[:, sl] = _ret_finish(o, g_ref[:, sl], nw_ref[h])
        lgf = _log_sigmoid(df_ref[h])
        lgb = _log_sigmoid(db_ref[h])
        kf = k.astype(F32) * RET_SCALE
        sf_ref[n_prev, h] = _dot_tn((kf * jnp.exp(lgf * (T_CTX - 1.0 - j))).astype(BF16), v)
        sb_ref[n_prev, h] = _dot_tn((kf * jnp.exp(lgb * j)).astype(BF16), v)


N_DCHUNK = 2 * LAT_TILES_PER_SEQ - 1


def _ret_lat_kernel(q_ref, k_ref, v_ref, g_ref, s0f_ref, s0b_ref, df_ref, db_ref, nw_ref, o_ref, strip, k_t):
    b = pl.program_id(0)
    j = pl.program_id(1)

    @pl.when(jnp.logical_and(b == 0, j == 0))
    def _():
        for h in range(HEADS):
            lgf = _log_sigmoid(df_ref[h])
            lgb = _log_sigmoid(db_ref[h])
            for c in range(N_DCHUNK):
                strip[h, c] = RET_SCALE * _decay_matrix(lgf, lgb, T_LAT - TM - c * TM, TM, TM)

    @pl.when(j == 0)
    def _():
        for h in range(HEADS):
            k_t[h] = k_ref[:, h * HD:(h + 1) * HD].astype(F32).T.astype(BF16)

    i = (j * TM + lax.broadcasted_iota(jnp.int32, (TM, 1), 0)).astype(F32)
    c0 = LAT_TILES_PER_SEQ - 1 - j
    for h in range(HEADS):
        sl = slice(h * HD, (h + 1) * HD)
        lgf = _log_sigmoid(df_ref[h])
        lgb = _log_sigmoid(db_ref[h])
        q = q_ref[:, sl]
        dmat = jnp.concatenate([strip[h, c0 + c] for c in range(LAT_TILES_PER_SEQ)], axis=-1)
        s = _dot(q, k_t[h]) * dmat
        o = _dot(s.astype(BF16), v_ref[:, sl])
        o = o + jnp.exp(lgf * (i + 1.0)) * _dot(q, s0f_ref[h].astype(BF16))
        o = o + jnp.exp(lgb * (T_LAT - i)) * _dot(q, s0b_ref[h].astype(BF16))
        o_ref[:, sl] = _ret_finish(o, g_ref[:, sl], nw_ref[h])


def _retention(l, zb, state_f, state_b, decay_f, decay_b, ret_norm_w, sf_prev, sb_prev):
    width = HEADS * HD
    dec1 = pl.BlockSpec((HEADS, 1, 1), lambda b: (l, 0, 0))
    nw1 = pl.BlockSpec((HEADS, 1, HD), lambda b: (l, 0, 0))
    st_spec = lambda n: pl.BlockSpec((None, n, HEADS, HD, HD), lambda b: (b, 0, 0, 0, 0))
    st_shape = jax.ShapeDtypeStruct((N_CTX_SEQ, l + 1, HEADS, HD, HD), F32)
    in_specs = [
        pl.BlockSpec((T_CTX, width), lambda b: (b, RQB // HEADS)),
        pl.BlockSpec((T_CTX, width), lambda b: (b, RKB // HEADS)),
        pl.BlockSpec((T_CTX, width), lambda b: (b, RVB // HEADS)),
        pl.BlockSpec((T_CTX, width), lambda b: (b, RGB // HEADS)),
        dec1, dec1, nw1,
    ]
    args = [zb, zb, zb, zb, decay_f, decay_b, ret_norm_w]
    if l:
        in_specs += [st_spec(l), st_spec(l)]
        args += [sf_prev, sb_prev]
    ro, sf, sb = pl.pallas_call(
        functools.partial(_ret_ctx_kernel, l),
        grid=(N_CTX_SEQ,),
        in_specs=in_specs,
        out_specs=[pl.BlockSpec((T_CTX, width), lambda b: (b, 0)), st_spec(l + 1), st_spec(l + 1)],
        out_shape=[jax.ShapeDtypeStruct((N_CTX, width), BF16), st_shape, st_shape],
        scratch_shapes=[pltpu.VMEM((HEADS, T_CTX, T_CTX), F32)],
        compiler_params=_params(1),
        name="ret_ctx",
    )(*args)

    lat_row = lambda b, j: CTX_TILES + LAT_TILES_PER_SEQ * b + j
    seq_row = lambda b: N_CTX // T_LAT + b
    dec2 = pl.BlockSpec((HEADS, 1, 1), lambda b, j: (l, 0, 0))
    nw2 = pl.BlockSpec((HEADS, 1, HD), lambda b, j: (l, 0, 0))
    s0_spec = pl.BlockSpec((None, None, HEADS, HD, HD), lambda b, j: (b, l, 0, 0, 0))
    ro_lat = pl.pallas_call(
        _ret_lat_kernel,
        grid=(N_LAT_SEQ, LAT_TILES_PER_SEQ),
        in_specs=[
            pl.BlockSpec((TM, width), lambda b, j: (lat_row(b, j), RQB // HEADS)),
            pl.BlockSpec((T_LAT, width), lambda b, j: (seq_row(b), RKB // HEADS)),
            pl.BlockSpec((T_LAT, width), lambda b, j: (seq_row(b), RVB // HEADS)),
            pl.BlockSpec((TM, width), lambda b, j: (lat_row(b, j), RGB // HEADS)),
            s0_spec, s0_spec, dec2, dec2, nw2,
        ],
        out_specs=pl.BlockSpec((TM, width), lambda b, j: (LAT_TILES_PER_SEQ * b + j, 0)),
        out_shape=jax.ShapeDtypeStruct((N_LAT, width), BF16),
        scratch_shapes=[pltpu.VMEM((HEADS, N_DCHUNK, TM, TM), F32), pltpu.VMEM((HEADS, HD, T_LAT), BF16)],
        compiler_params=_params(2),
        name="ret_lat",
    )(zb, zb, zb, zb, state_f, state_b, decay_f, decay_b, ret_norm_w)
    return ro, ro_lat, sf, sb


def _merge_kernel(x_pair, *refs):
    refs = list(refs)
    x_ref = refs.pop(0)
    xs_ref = refs.pop(0) if x_pair else None
    (aoc_ref, aol_ref, roc_ref, rol_ref, cb_ref, cc_ref, cx_ref, ccp_ref, cxp_ref, ccn_ref, cxn_ref,
     mg0_ref, mg1_ref, mg2_ref, mod_ref, cw_ref, wa_ref, wc_ref, wr_ref, wo_ref, n2_ref, wrt_ref,
     x1_ref, h2_ref, aff_ref) = refs
    i = pl.program_id(0)
    is_ctx = i < CTX_TILES
    x_in = x_ref[...]
    if x_pair:
        x_in = jnp.where(is_ctx, x_in, xs_ref[...])
    ao = jnp.where(is_ctx, aoc_ref[...], aol_ref[...])
    ro = jnp.where(is_ctx, roc_ref[...], rol_ref[...])
    j = (i - CTX_TILES) % LAT_TILES_PER_SEQ
    seq_first = jnp.logical_or(i < CTX_TILES, j == 0)
    seq_last = jnp.logical_or(i < CTX_TILES, j == LAT_TILES_PER_SEQ - 1)

    u = cc_ref[...].astype(F32) * cx_ref[...].astype(F32)
    up = (ccp_ref[...].astype(F32) * cxp_ref[...].astype(F32))[15:16, :] * jnp.where(seq_first, 0.0, 1.0)
    dn = (ccn_ref[...].astype(F32) * cxn_ref[...].astype(F32))[0:1, :] * jnp.where(seq_last, 0.0, 1.0)
    r = lax.broadcasted_iota(jnp.int32, u.shape, 0)
    u_prev = jnp.where(r == 0, up, pltpu.roll(u, 1, 0))
    u_next = jnp.where(r == TM - 1, dn, pltpu.roll(u, TM - 1, 0))
    cw = cw_ref[...]
    conv = u_prev * cw[0:1, :] + u * cw[1:2, :] + u_next * cw[2:3, :]
    conv_o = (cb_ref[...].astype(F32) * conv).astype(BF16)

    merged = _sigmoid(mg0_ref[...].astype(F32)) * _dot(ao, wa_ref[...])
    merged = merged + _sigmoid(mg1_ref[...].astype(F32)) * _dot(conv_o, wc_ref[...])
    merged = merged + _sigmoid(mg2_ref[...].astype(F32)) * _dot(ro, wr_ref[...])

    m = mod_ref[0]
    gate1 = m[:, 2 * D:3 * D]
    shift2 = m[:, 3 * D:4 * D]
    scale2 = m[:, 4 * D:5 * D]
    x1 = x_in + gate1 * _dot(merged.astype(BF16), wo_ref[...])
    x1_ref[...] = x1
    h2 = (_rms(x1) * n2_ref[...]) * (1.0 + scale2) + shift2
    _to_slabs(h2_ref, h2)

    logits = _dot(h2.astype(BF16), wrt_ref[...])
    lane = lax.broadcasted_iota(jnp.int32, logits.shape, 1)
    valid = lane < N_EXPERTS
    lmax = jnp.max(jnp.where(valid, logits, -jnp.inf), axis=-1, keepdims=True)
    e = jnp.where(valid, jnp.exp(logits - lmax), 0.0)
    aff_ref[...] = e * (1.0 / jnp.sum(e, axis=-1, keepdims=True))


def _merge(l, x, ao, ao_lat, ro, ro_lat, zb, mod, conv_w, w_br_attn, w_br_conv, w_br_ret, w_out, norm2_w,
           w_router_pad):
    n16 = N_TOK // 16
    ctx_br = pl.BlockSpec((TM, 512), lambda i: (jnp.minimum(i, CTX_TILES - 1), 0))
    lat_br = pl.BlockSpec((TM, 512), lambda i: (jnp.maximum(i - CTX_TILES, 0), 0))
    col = lambda c: pl.BlockSpec((TM, 512), lambda i: (i, c))
    halo_p = lambda c: pl.BlockSpec((16, 512), lambda i: (jnp.maximum(i * (TM // 16) - 1, 0), c))
    halo_n = lambda c: pl.BlockSpec((16, 512), lambda i: (jnp.minimum((i + 1) * (TM // 16), n16 - 1), c))
    mgs = lambda c: pl.BlockSpec((TM, D), lambda i: (i, c))
    wbr = pl.BlockSpec((None, 512, D), lambda i: (l, 0, 0))
    x_pair = isinstance(x, tuple)
    if x_pair:
        x_specs = [pl.BlockSpec((TM, D), lambda i: (jnp.minimum(i, CTX_TILES - 1), 0)),
                   pl.BlockSpec((TM, D), lambda i: (jnp.maximum(i - CTX_TILES, 0), 0))]
        x_args = list(x)
    else:
        x_specs = [pl.BlockSpec((TM, D), lambda i: (i, 0))]
        x_args = [x]
    return pl.pallas_call(
        functools.partial(_merge_kernel, x_pair),
        grid=(N_TILES,),
        in_specs=x_specs + [
            ctx_br, lat_br, ctx_br, lat_br,
            col(3), col(4), col(5), halo_p(4), halo_p(5), halo_n(4), halo_n(5),
            mgs(5), mgs(6), mgs(7),
            pl.BlockSpec((1, 1, N_MOD * D), lambda i: (l * 8 + _mod_row(i), 0, 0)),
            pl.BlockSpec((None, 3, 512), lambda i: (l, 0, 0)),
            wbr, wbr, wbr,
            pl.BlockSpec((None, D, D), lambda i: (l, 0, 0)),
            pl.BlockSpec((None, 1, D), lambda i: (l, 0, 0)),
            pl.BlockSpec((None, D, 128), lambda i: (l, 0, 0)),
        ],
        out_specs=[
            pl.BlockSpec((TM, D), lambda i: (i, 0)),
            pl.BlockSpec((TM * SLAB, 128), lambda i: (i, 0)),
            pl.BlockSpec((TM, 128), lambda i: (i, 0)),
        ],
        out_shape=[
            jax.ShapeDtypeStruct((N_TOK, D), F32),
            jax.ShapeDtypeStruct((N_TOK * SLAB, 128), F32),
            jax.ShapeDtypeStruct((N_TOK, 128), F32),
        ],
        compiler_params=_params(1),
        name="merge",
    )(*x_args, ao, ao_lat, ro, ro_lat, zb, zb, zb, zb, zb, zb, zb, zb, zb, zb, mod, conv_w,
      w_br_attn, w_br_conv, w_br_ret, w_out, norm2_w, w_router_pad)


def _cumsum_lanes(x, tri):
    run = jnp.zeros((x.shape[0], 1), F32)
    outs = []
    for b in range(x.shape[1] // 128):
        cs = _dot(x[:, b * 128:(b + 1) * 128].astype(BF16), tri) + run
        run = cs[:, 127:128]
        outs.append(cs)
    return jnp.concatenate(outs, axis=-1)


GATE_LANE = (0, 16, 32)
IDX_LANE = 48


def _topk_kernel(an_ref, tmat_ref, idx_ref, gate_ref):
    an = an_ref[...]
    a = an.T[0:N_EXPERTS, :]
    n = a.shape[1]
    kf = float(CAP)

    hi = an.astype(BF16).astype(F32)
    mid = (an - hi).astype(BF16).astype(F32)
    lo = ((an - hi) - mid).astype(BF16).astype(F32)
    table = (tmat_ref[...].astype(F32) + hi + pltpu.roll(mid, GATE_LANE[1], 1)
             + pltpu.roll(lo, GATE_LANE[2], 1)).astype(BF16)

    def count_gt(thr):
        return jnp.sum(jnp.where(a > thr, 1.0, 0.0), axis=-1, keepdims=True)

    def span(lo, hi):
        inside = jnp.logical_and(a > lo, a <= hi)
        cmax = jnp.max(jnp.where(inside, a, -jnp.inf), axis=-1, keepdims=True)
        cmin = jnp.min(jnp.where(inside, a, jnp.inf), axis=-1, keepdims=True)
        return cmax, cmin

    def cond(c):
        return jnp.logical_and(c[2] > 0, c[3] < 400)

    def body(c):
        lo, hi, _, it = c
        mid = 0.5 * (lo + hi)
        ge = count_gt(mid) >= kf
        lo = jnp.where(ge, mid, lo)
        hi = jnp.where(ge, hi, mid)
        cmax, cmin = span(lo, hi)
        open_rows = jnp.max(jnp.where(cmax != cmin, 1, 0))
        return lo, hi, open_rows, it + 1

    lo0 = jnp.full((N_EXPERTS, 1), -1.0, F32)
    hi0 = jnp.max(a, axis=-1, keepdims=True)
    cmax0, cmin0 = span(lo0, hi0)
    lo, hi, _, _ = lax.while_loop(
        cond, body, (lo0, hi0, jnp.max(jnp.where(cmax0 != cmin0, 1, 0)), jnp.int32(0)))
    thr, _ = span(lo, hi)

    r = lax.broadcasted_iota(jnp.int32, (128, 128), 0)
    c = lax.broadcasted_iota(jnp.int32, (128, 128), 1)
    tri = jnp.where(r <= c, 1.0, 0.0).astype(BF16)
    gt = a > thr
    eq = jnp.where(a == thr, 1.0, 0.0)
    need = kf - count_gt(thr)
    eq_before = _cumsum_lanes(eq, tri) - eq
    sel = jnp.where(jnp.logical_or(gt, jnp.logical_and(eq > 0.0, eq_before < need)), 1.0, 0.0)
    pos = _cumsum_lanes(sel, tri) - 1.0
    slot = jnp.where(sel > 0.0, pos, -1.0).astype(jnp.int32)

    p_iota = lax.broadcasted_iota(jnp.int32, (CAP, 1024), 0)
    lane = lax.broadcasted_iota(jnp.int32, (CAP, 128), 1)
    for e in range(N_EXPERTS):
        acc = jnp.zeros((CAP, 128), F32)
        for cb in range(n // 1024):
            onehot = jnp.where(p_iota == slot[e:e + 1, cb * 1024:(cb + 1) * 1024], 1.0, 0.0).astype(BF16)
            acc = acc + _dot(onehot, table[cb * 1024:(cb + 1) * 1024, :])
        idx_ref[0, e] = (acc[:, IDX_LANE:IDX_LANE + 1] * 64.0 + acc[:, IDX_LANE + 1:IDX_LANE + 2]).astype(jnp.int32)
        g = jnp.zeros((CAP, 1), F32)
        for off in GATE_LANE:
            g = g + jnp.sum(jnp.where(lane == off + e, acc, 0.0), axis=-1, keepdims=True)
        gate_ref[0, e] = g


def _topk(aff_n, tmat):
    out_spec = pl.BlockSpec((1, N_EXPERTS, CAP, 1), lambda s: (s, 0, 0, 0))
    return pl.pallas_call(
        _topk_kernel,
        grid=(2,),
        in_specs=[
            pl.BlockSpec((N_CTX, 128), lambda s: (s, 0)),
            pl.BlockSpec((N_CTX, 128), lambda s: (0, 0)),
        ],
        out_specs=[out_spec, out_spec],
        out_shape=[jax.ShapeDtypeStruct((2, N_EXPERTS, CAP, 1), jnp.int32),
                   jax.ShapeDtypeStruct((2, N_EXPERTS, CAP, 1), F32)],
        compiler_params=_params(1),
        name="topk",
    )(aff_n, tmat)


N_FT = FF // TF
GATHER_ROWS = 2 * CAP
ROWS_PER_STEP = GATHER_ROWS // N_FT


def _ffn_kernel(l, idx_ref, h2_hbm, g_ref, wg_ref, wu_ref, wd_ref, ye_ref, xg, xb, acc, sems):
    e = pl.program_id(0)
    f = pl.program_id(1)
    slot = e % 2

    def row_copy(expert, s, p, dst_slot):
        row = idx_ref[(s * N_EXPERTS + expert) * CAP + p] + s * N_CTX
        src = h2_hbm.at[pl.ds(pl.multiple_of(row * SLAB, SLAB), SLAB), :]
        dst = xg.at[dst_slot, pl.ds(pl.multiple_of((s * CAP + p) * SLAB, SLAB), SLAB), :]
        return pltpu.make_async_copy(src, dst, sems.at[dst_slot])

    def slot_wait(dst_slot):
        pltpu.make_async_copy(h2_hbm.at[pl.ds(0, GATHER_ROWS * SLAB), :], xg.at[dst_slot],
                              sems.at[dst_slot]).wait()

    @pl.when(jnp.logical_and(e == 0, f == 0))
    def _():
        def issue(r, carry):
            row_copy(0, r // CAP, r % CAP, 0).start()
            return carry
        lax.fori_loop(0, GATHER_ROWS, issue, 0)

    @pl.when(f == 0)
    def _():
        slot_wait(slot)
        xb[...] = _from_slabs(xg.at[slot]).astype(BF16)
        acc[...] = jnp.zeros_like(acc)

    nxt = jnp.minimum(e + 1, N_EXPERTS - 1)
    s_nxt = f // (N_FT // 2)
    p0 = (f % (N_FT // 2)) * ROWS_PER_STEP
    for u in range(ROWS_PER_STEP):
        row_copy(nxt, s_nxt, p0 + u, 1 - slot).start()

    x = xb[...]
    hg = _dot(x, wg_ref[...].astype(BF16))
    hu = _dot(x, wu_ref[...].astype(BF16))
    hdn = ((hg * _sigmoid(hg)) * hu).astype(BF16)
    acc[...] += _dot(hdn, wd_ref[...].astype(BF16))

    @pl.when(f == N_FT - 1)
    def _():
        for s in range(2):
            _to_slabs(ye_ref.at[s], acc[s * CAP:(s + 1) * CAP, :] * g_ref[s])

    @pl.when(jnp.logical_and(e == N_EXPERTS - 1, f == N_FT - 1))
    def _():
        slot_wait(1 - slot)


def _expert_ffn(l, idx_flat, h2s, gates, w_gate, w_up, w_down):
    grid_spec = pltpu.PrefetchScalarGridSpec(
        num_scalar_prefetch=1,
        grid=(N_EXPERTS, FF // TF),
        in_specs=[
            pl.BlockSpec(memory_space=pl.ANY),
            pl.BlockSpec((2, None, CAP, 1), lambda e, f, idx: (0, e, 0, 0)),
            pl.BlockSpec((None, None, D, TF), lambda e, f, idx: (l, e, 0, f)),
            pl.BlockSpec((None, None, D, TF), lambda e, f, idx: (l, e, 0, f)),
            pl.BlockSpec((None, None, TF, D), lambda e, f, idx: (l, e, f, 0)),
        ],
        out_specs=pl.BlockSpec((2, None, CAP * SLAB, 128), lambda e, f, idx: (0, e, 0, 0)),
        scratch_shapes=[
            pltpu.VMEM((2, GATHER_ROWS * SLAB, 128), F32),
            pltpu.VMEM((GATHER_ROWS, D), BF16),
            pltpu.VMEM((GATHER_ROWS, D), F32),
            pltpu.SemaphoreType.DMA((2,)),
        ],
    )
    return pl.pallas_call(
        functools.partial(_ffn_kernel, l),
        grid_spec=grid_spec,
        out_shape=jax.ShapeDtypeStruct((2, N_EXPERTS, CAP * SLAB, 128), F32),
        compiler_params=_params(2),
        name="expert_ffn",
    )(idx_flat, h2s, gates, w_gate, w_up, w_down)


SCATTER_UNROLL = 8


def _combine_kernel(idx_ref, ye_ref, y_ref):
    s = pl.program_id(0)
    e = pl.program_id(1)

    @pl.when(e == 0)
    def _():
        y_ref[...] = jnp.zeros_like(y_ref)

    base = (s * N_EXPERTS + e) * CAP

    def group(g, carry):
        p0 = g * SCATTER_UNROLL
        tile = lambda r: pl.ds(pl.multiple_of(r * SLAB, SLAB), SLAB)
        rows = [idx_ref[base + p0 + u] for u in range(SCATTER_UNROLL)]
        vals = [y_ref[tile(rows[u]), :] + ye_ref[tile(p0 + u), :] for u in range(SCATTER_UNROLL)]
        for u in range(SCATTER_UNROLL):
            y_ref[tile(rows[u]), :] = vals[u]
        return carry

    lax.fori_loop(0, CAP // SCATTER_UNROLL, group, 0)


def _combine(idx_flat, ye):
    grid_spec = pltpu.PrefetchScalarGridSpec(
        num_scalar_prefetch=1,
        grid=(2, N_EXPERTS),
        in_specs=[pl.BlockSpec((None, None, CAP * SLAB, 128), lambda s, e, idx: (s, e, 0, 0))],
        out_specs=pl.BlockSpec((N_CTX * SLAB, 128), lambda s, e, idx: (s, 0)),
    )
    return pl.pallas_call(
        _combine_kernel,
        grid_spec=grid_spec,
        out_shape=jax.ShapeDtypeStruct((N_TOK * SLAB, 128), F32),
        compiler_params=_params(2),
        name="combine",
    )(idx_flat, ye)


def _final_kernel(x_ref, y_ref, mod_ref, w_ref, op_ref, os_ref):
    x = x_ref[...] + mod_ref[0][:, 5 * D:6 * D] * _from_slabs(y_ref)
    out = _rms(x) * w_ref[...]
    is_ctx = pl.program_id(0) < CTX_TILES

    @pl.when(is_ctx)
    def _():
        op_ref[...] = out

    @pl.when(jnp.logical_not(is_ctx))
    def _():
        os_ref[...] = out


def _final(x1, y, mod, final_norm_w):
    tile = pl.BlockSpec((TM, D), lambda i: (i, 0))
    return pl.pallas_call(
        _final_kernel,
        grid=(N_TILES,),
        in_specs=[tile, pl.BlockSpec((TM * SLAB, 128), lambda i: (i, 0)),
                  pl.BlockSpec((1, 1, N_MOD * D), lambda i: ((DEPTH - 1) * 8 + _mod_row(i), 0, 0)),
                  pl.BlockSpec((1, D), lambda i: (0, 0))],
        out_specs=[pl.BlockSpec((TM, D), lambda i: (jnp.minimum(i, CTX_TILES - 1), 0)),
                   pl.BlockSpec((TM, D), lambda i: (jnp.maximum(i - CTX_TILES, 0), 0))],
        out_shape=[jax.ShapeDtypeStruct((N_CTX, D), F32), jax.ShapeDtypeStruct((N_LAT, D), F32)],
        compiler_params=_params(1),
        name="final_norm",
    )(x1, y, mod, final_norm_w.reshape(1, D))


def _rope_tables():
    t = np.arange(T_LAT)
    row = (t // GRID_W).astype(np.float32)
    col = (t % GRID_W).astype(np.float32)
    inv = jnp.asarray(ROPE_BASE, F32) ** (-jnp.arange(N_ROPE_FREQ, dtype=F32) / N_ROPE_FREQ)
    ang_r = jnp.asarray(row)[:, None] * inv
    ang_c = jnp.asarray(col)[:, None] * inv
    def group(ang):
        return jnp.concatenate([ang, ang], axis=-1)
    ang = jnp.concatenate([group(ang_r), group(ang_c), group(ang_r), group(ang_c)], axis=-1)
    sign = np.where(np.arange(HD) % 32 < 16, -1.0, 1.0).astype(np.float32)
    return jnp.cos(ang), jnp.sin(ang) * sign


def _index_table():
    t = np.arange(N_CTX)
    tm = np.zeros((N_CTX, 128), np.float32)
    tm[:, IDX_LANE] = t // 64
    tm[:, IDX_LANE + 1] = t % 64
    return jnp.asarray(tm, BF16)


def kernel(x_prompt, x_sample, c, cache_attn_k, cache_attn_v, state_ret_fwd, state_ret_bwd, c_ctx, w_ada, b_ada, norm1_w, norm2_w, w_in, attn_lambda, attn_subln_w, conv_w, ret_decay_fwd, ret_decay_bwd, ret_norm_w, w_br_attn, w_br_conv, w_br_ret, w_out, w_router, w_exp_gate, w_exp_up, w_exp_down, final_norm_w):
    x = (x_prompt.reshape(N_CTX, D), x_sample.reshape(N_LAT, D))
    cvec = jnp.concatenate([c_ctx[None, :], c, jnp.zeros((3, D), F32)], axis=0)
    mod = _modulation(cvec, w_ada, b_ada).reshape(DEPTH * 8, 1, N_MOD * D)

    w_in_bf = w_in.astype(BF16)
    w_br_attn_bf = w_br_attn.astype(BF16)
    w_br_conv_bf = w_br_conv.astype(BF16)
    w_br_ret_bf = w_br_ret.astype(BF16)
    w_out_bf = w_out.astype(BF16)
    w_router_pad = jnp.pad(w_router, ((0, 0), (0, 0), (0, 128 - N_EXPERTS))).astype(BF16)
    norm1 = norm1_w.reshape(DEPTH, 1, D)
    norm2 = norm2_w.reshape(DEPTH, 1, D)
    subln = attn_subln_w.reshape(DEPTH, 1, HD)
    decay_f = ret_decay_fwd.reshape(DEPTH * HEADS, 1, 1)
    decay_b = ret_decay_bwd.reshape(DEPTH * HEADS, 1, 1)
    ret_nw = ret_norm_w.reshape(DEPTH * HEADS, 1, HD)
    cos, sin = _rope_tables()
    tmat = _index_table()

    y = None
    new_k = new_v = new_sf = new_sb = None
    for l in range(DEPTH):
        lam_init = 0.8 - 0.6 * math.exp(-0.3 * l)
        if y is None:
            zb, zq, new_k, new_v = _inproj(l, x, None, mod, norm1, w_in_bf, new_k, new_v)
        else:
            zb, zq, new_k, new_v, x = _inproj(l, x, y, mod, norm1, w_in_bf, new_k, new_v)
        ao, ao_lat = _attention(l, lam_init, zb, zq, cache_attn_k, cache_attn_v, cos, sin, attn_lambda, subln)
        ro, ro_lat, new_sf, new_sb = _retention(l, zb, state_ret_fwd, state_ret_bwd, decay_f, decay_b, ret_nw,
                                                new_sf, new_sb)
        x, h2s, aff_n = _merge(l, x, ao, ao_lat, ro, ro_lat, zb, mod, conv_w, w_br_attn_bf, w_br_conv_bf,
                               w_br_ret_bf, w_out_bf, norm2, w_router_pad)
        idx, gates = _topk(aff_n, tmat)
        idx_flat = idx.reshape(2 * N_EXPERTS * CAP)
        ye = _expert_ffn(l, idx_flat, h2s, gates, w_exp_gate, w_exp_up, w_exp_down)
        y = _combine(idx_flat, ye)

    y_prompt, y_sample = _final(x, y, mod, final_norm_w)
    return (y_prompt.reshape(N_CTX_SEQ, T_CTX, D), y_sample.reshape(N_LAT_SEQ, T_LAT, D), new_k, new_v,
            new_sf, new_sb)
```

```python
import functools
import math

import jax
import jax.numpy as jnp
import numpy as np
from jax import lax
from jax.experimental import pallas as pl
from jax.experimental.pallas import tpu as pltpu

F32 = jnp.float32
BF16 = jnp.bfloat16

D = 1024
DEPTH = 2
N_CTX_SEQ = 16
T_CTX = 256
N_LAT_SEQ = 4
T_LAT = 1024
PAST = 256
N_CTX = N_CTX_SEQ * T_CTX
N_LAT = N_LAT_SEQ * T_LAT
N_TOK = N_CTX + N_LAT
TM = 256
N_TILES = N_TOK // TM
CTX_TILES = N_CTX // TM
LAT_TILES_PER_SEQ = T_LAT // TM
HEADS = 4
HD = 128
GRID_W = 64
N_ROPE_FREQ = 16
ROPE_BASE = 10000.0
IN_COLS = 8192
N_MOD = 6
N_EXPERTS = 16
CAP = 512
FF = 2048
TF = 512
EPS = 1e-6
RET_SCALE = HD ** -0.5
ATTN_SCALE = 64 ** -0.5
SLAB = D // 128
VMEM_LIMIT = 56 * 1024 * 1024

QB, KB, VB = 0, 4, 8
RQB, RKB, RVB, RGB = 24, 28, 32, 36
CONV_B_COL, CONV_U_COL = 3, 4
MERGE_GATE_COL = 5
MERGE_SUB = 2


def _sigmoid(x):
    return 0.5 * jnp.tanh(0.5 * x) + 0.5


def _log_sigmoid(x):
    return jnp.minimum(x, 0.0) - jnp.log(1.0 + jnp.exp(-jnp.abs(x)))


def _rms(x):
    return x * lax.rsqrt(jnp.mean(x * x, axis=-1, keepdims=True) + EPS)


def _dot(a, b):
    return jnp.dot(a, b, preferred_element_type=F32)


def _dot_nt(a, b):
    return lax.dot_general(a, b, (((1,), (1,)), ((), ())), preferred_element_type=F32)


def _dot_tn(a, b):
    return lax.dot_general(a, b, (((0,), (0,)), ((), ())), preferred_element_type=F32)


def _to_slabs(ref, x):
    n = x.shape[0]
    for s in range(SLAB):
        ref[pl.ds(s, n, stride=SLAB), :] = x[:, s * 128:(s + 1) * 128]


def _from_slabs(ref):
    n = ref.shape[0] // SLAB
    return jnp.concatenate([ref[pl.ds(s, n, stride=SLAB), :] for s in range(SLAB)], axis=-1)


def _mod_row(i):
    return jnp.where(i < CTX_TILES, 0, 1 + (i - CTX_TILES) // LAT_TILES_PER_SEQ)


def _params(n_axes):
    return pltpu.CompilerParams(
        dimension_semantics=("arbitrary",) * n_axes, vmem_limit_bytes=VMEM_LIMIT)


def _mod_kernel(c_ref, w_ref, b_ref, o_ref):
    c = c_ref[...]
    s = (c * _sigmoid(c)).astype(BF16)
    o_ref[...] = _dot(s, w_ref[...].astype(BF16)) + b_ref[...]


def _modulation(cvec, w_ada, b_ada):
    tn = 1024
    return pl.pallas_call(
        _mod_kernel,
        grid=(DEPTH, N_MOD * D // tn),
        in_specs=[
            pl.BlockSpec((8, D), lambda l, n: (0, 0)),
            pl.BlockSpec((None, D, tn), lambda l, n: (l, 0, n)),
            pl.BlockSpec((None, 1, tn), lambda l, n: (l, 0, n)),
        ],
        out_specs=pl.BlockSpec((None, 8, tn), lambda l, n: (l, 0, n)),
        out_shape=jax.ShapeDtypeStruct((DEPTH, 8, N_MOD * D), F32),
        compiler_params=_params(2),
        name="modulation",
    )(cvec, w_ada, b_ada.reshape(DEPTH, 1, N_MOD * D))


def _inproj_kernel(has_y, x_pair, n_prev, *refs):
    refs = list(refs)
    x_ref = refs.pop(0)
    if x_pair:
        xs_ref = refs.pop(0)
    if has_y:
        y_ref, modp_ref = refs.pop(0), refs.pop(0)
    mod_ref, n1_ref, w_ref = refs.pop(0), refs.pop(0), refs.pop(0)
    if n_prev:
        kp_ref, vp_ref = refs.pop(0), refs.pop(0)
    zb_ref, zq_ref, kn_ref, vn_ref = refs[0:4]
    x = x_ref[...]
    if x_pair:
        x = jnp.where(pl.program_id(0) < N_TILES - CTX_TILES, xs_ref[...], x)
    if has_y:
        x = x + modp_ref[0][:, 5 * D:6 * D] * _from_slabs(y_ref)
        refs[4][...] = x
    m = mod_ref[0]
    shift1 = m[:, 0:D]
    scale1 = m[:, D:2 * D]
    h = ((_rms(x) * n1_ref[...]) * (1.0 + scale1) + shift1).astype(BF16)
    cw = 1024
    for c in range(IN_COLS // cw):
        z = _dot(h, w_ref[:, c * cw:(c + 1) * cw])
        if c == CONV_U_COL // 2:
            z = jnp.concatenate([z[:, 0:512] * z[:, 512:cw], z[:, 512:cw]], axis=-1)
        elif c == RGB // 8:
            g = z[:, 512:cw]
            z = jnp.concatenate([z[:, 0:512], g * _sigmoid(g)], axis=-1)
        elif c >= MERGE_GATE_COL:
            z = _sigmoid(z)
        zb_ref[:, c * cw:(c + 1) * cw] = z.astype(BF16)
        if c == 0:
            zq_ref[...] = z
            for hh in range(HEADS):
                kn_ref[n_prev, hh] = z[:, 512 + hh * HD:512 + (hh + 1) * HD]
        if c == 1:
            for hh in range(HEADS):
                vn_ref[n_prev, hh] = z[:, hh * HD:(hh + 1) * HD]
    if n_prev:
        kn_ref[0:n_prev] = kp_ref[...]
        vn_ref[0:n_prev] = vp_ref[...]


def _inproj(l, x, y, mod, norm1_w, w_in_bf, k_prev, v_prev):
    has_y = y is not None
    t = lambda i: (i + CTX_TILES) % N_TILES
    tile = pl.BlockSpec((TM, D), lambda i: (t(i), 0))
    ctx_i = lambda i: jnp.maximum(i - (N_TILES - CTX_TILES), 0)
    kv_spec = lambda n: pl.BlockSpec((None, n, HEADS, T_CTX, HD), lambda i: (ctx_i(i), 0, 0, 0, 0))
    kv_shape = jax.ShapeDtypeStruct((N_CTX_SEQ, l + 1, HEADS, T_CTX, HD), F32)
    x_pair = isinstance(x, tuple)
    if x_pair:
        n_lat = N_TILES - CTX_TILES
        in_specs = [pl.BlockSpec((TM, D), lambda i: (jnp.maximum(i - n_lat, 0), 0)),
                    pl.BlockSpec((TM, D), lambda i: (jnp.minimum(i, n_lat - 1), 0))]
        args = list(x)
    else:
        in_specs = [tile]
        args = [x]
    if has_y:
        in_specs += [pl.BlockSpec((TM * SLAB, 128), lambda i: (t(i), 0)),
                     pl.BlockSpec((1, 1, N_MOD * D), lambda i: ((l - 1) * 8 + _mod_row(t(i)), 0, 0))]
        args += [y, mod]
    in_specs += [
        pl.BlockSpec((1, 1, N_MOD * D), lambda i: (l * 8 + _mod_row(t(i)), 0, 0)),
        pl.BlockSpec((None, 1, D), lambda i: (l, 0, 0)),
        pl.BlockSpec((None, D, IN_COLS), lambda i: (l, 0, 0), pipeline_mode=pl.Buffered(1)),
    ]
    args += [mod, norm1_w, w_in_bf]
    if l:
        in_specs += [kv_spec(l), kv_spec(l)]
        args += [k_prev, v_prev]
    out_specs = [pl.BlockSpec((TM, IN_COLS), lambda i: (t(i), 0)),
                 pl.BlockSpec((TM, 1024), lambda i: (t(i), 0)),
                 kv_spec(l + 1), kv_spec(l + 1)]
    out_shape = [jax.ShapeDtypeStruct((N_TOK, IN_COLS), BF16),
                 jax.ShapeDtypeStruct((N_TOK, 1024), F32),
                 kv_shape, kv_shape]
    if has_y:
        out_specs.append(tile)
        out_shape.append(jax.ShapeDtypeStruct((N_TOK, D), F32))
    return pl.pallas_call(
        functools.partial(_inproj_kernel, has_y, x_pair, l),
        grid=(N_TILES,),
        in_specs=in_specs,
        out_specs=out_specs,
        out_shape=out_shape,
        compiler_params=_params(1),
        name="inproj",
    )(*args)


def _lambda(lam_ref, lam_init):
    lv = lam_ref[...]
    a = jnp.sum(lv[0:1] * lv[1:2], axis=-1, keepdims=True)
    b = jnp.sum(lv[2:3] * lv[3:4], axis=-1, keepdims=True)
    return jnp.exp(a) - jnp.exp(b) + lam_init


def _diff_attention(lam_init, q, keys, v_all, lam, sw, keys_transposed):
    lane = lax.broadcasted_iota(jnp.int32, q.shape, 1)
    zero = jnp.zeros_like(q)

    def exp_map(qm):
        s = _dot(qm, keys) if keys_transposed else _dot_nt(qm, keys)
        e = jnp.exp(s - jnp.max(s, axis=-1, keepdims=True))
        return e, 1.0 / jnp.sum(e, axis=-1, keepdims=True)

    e0, r0 = exp_map(jnp.where(lane < 64, q, zero))
    e1, r1 = exp_map(jnp.where(lane >= 64, q, zero))
    a = e0 * r0 - e1 * (lam * r1)
    o = _dot(a.astype(BF16), v_all)
    return (_rms(o) * sw) * (1.0 - lam_init)


def _scaled_q(q):
    return (q.astype(F32) * ATTN_SCALE).astype(BF16)


def _attn_ctx_kernel(lam_init, q_ref, k_ref, v_ref, lam_ref, sw_ref, o_ref):
    lam = _lambda(lam_ref, lam_init)
    for h in range(HEADS):
        sl = slice(h * HD, (h + 1) * HD)
        o = _diff_attention(lam_init, _scaled_q(q_ref[:, sl]), k_ref[:, sl], v_ref[:, sl], lam, sw_ref[...], False)
        o_ref[:, sl] = o.astype(BF16)


def _rope(x, cos, sin_signed):
    lane = lax.broadcasted_iota(jnp.int32, x.shape, 1)
    partner = jnp.where(lane % 32 < 16, pltpu.roll(x, 112, 1), pltpu.roll(x, 16, 1))
    return x * cos + partner * sin_signed


def _attn_lat_kernel(lam_init, q_ref, k_ref, v_ref, ck_ref, cv_ref, cos_ref, sin_ref,
                     cosq_ref, sinq_ref, lam_ref, sw_ref, o_ref, k_t, vall):
    @pl.when(pl.program_id(1) == 0)
    def _():
        for h in range(HEADS):
            sl = slice(h * HD, (h + 1) * HD)
            k_t[h, :, 0:PAST] = ck_ref[h].T.astype(BF16)
            k_t[h, :, PAST:] = _rope(k_ref[:, sl], cos_ref[...], sin_ref[...]).T.astype(BF16)
            vall[h, 0:PAST, :] = cv_ref[h].astype(BF16)
            vall[h, PAST:, :] = v_ref[:, sl]

    lam = _lambda(lam_ref, lam_init)
    for h in range(HEADS):
        sl = slice(h * HD, (h + 1) * HD)
        q = _scaled_q(_rope(q_ref[:, sl], cosq_ref[...], sinq_ref[...]))
        o = _diff_attention(lam_init, q, k_t[h], vall[h], lam, sw_ref[...], True)
        o_ref[:, sl] = o.astype(BF16)


def _attention(l, lam_init, zb, zq, cache_k, cache_v, cos, sin, attn_lambda, subln_w):
    width = HEADS * HD
    ao = pl.pallas_call(
        functools.partial(_attn_ctx_kernel, lam_init),
        grid=(N_CTX_SEQ,),
        in_specs=[
            pl.BlockSpec((T_CTX, width), lambda b: (b, QB // HEADS)),
            pl.BlockSpec((T_CTX, width), lambda b: (b, KB // HEADS)),
            pl.BlockSpec((T_CTX, width), lambda b: (b, VB // HEADS)),
            pl.BlockSpec((None, 4, 64), lambda b: (l, 0, 0)),
            pl.BlockSpec((None, 1, HD), lambda b: (l, 0, 0)),
        ],
        out_specs=pl.BlockSpec((T_CTX, width), lambda b: (b, 0)),
        out_shape=jax.ShapeDtypeStruct((N_CTX, width), BF16),
        compiler_params=_params(1),
        name="attn_ctx",
    )(zb, zb, zb, attn_lambda, subln_w)

    lat_row = lambda b, j: CTX_TILES + LAT_TILES_PER_SEQ * b + j
    seq_row = lambda b: N_CTX // T_LAT + b
    ao_lat = pl.pallas_call(
        functools.partial(_attn_lat_kernel, lam_init),
        grid=(N_LAT_SEQ, LAT_TILES_PER_SEQ),
        in_specs=[
            pl.BlockSpec((TM, width), lambda b, j: (lat_row(b, j), QB // HEADS)),
            pl.BlockSpec((T_LAT, width), lambda b, j: (seq_row(b), KB // HEADS)),
            pl.BlockSpec((T_LAT, width), lambda b, j: (seq_row(b), VB // HEADS)),
            pl.BlockSpec((None, None, HEADS, PAST, HD), lambda b, j: (b, l, 0, 0, 0)),
            pl.BlockSpec((None, None, HEADS, PAST, HD), lambda b, j: (b, l, 0, 0, 0)),
            pl.BlockSpec((T_LAT, HD), lambda b, j: (0, 0)),
            pl.BlockSpec((T_LAT, HD), lambda b, j: (0, 0)),
            pl.BlockSpec((TM, HD), lambda b, j: (j, 0)),
            pl.BlockSpec((TM, HD), lambda b, j: (j, 0)),
            pl.BlockSpec((None, 4, 64), lambda b, j: (l, 0, 0)),
            pl.BlockSpec((None, 1, HD), lambda b, j: (l, 0, 0)),
        ],
        out_specs=pl.BlockSpec((TM, width), lambda b, j: (LAT_TILES_PER_SEQ * b + j, 0)),
        out_shape=jax.ShapeDtypeStruct((N_LAT, width), BF16),
        scratch_shapes=[pltpu.VMEM((HEADS, HD, PAST + T_LAT), BF16),
                        pltpu.VMEM((HEADS, PAST + T_LAT, HD), BF16)],
        compiler_params=_params(2),
        name="attn_lat",
    )(zq, zq, zb, cache_k, cache_v, cos, sin, cos, sin, attn_lambda, subln_w)
    return ao, ao_lat


def _decay_matrix(lgf, lgb, row0, tq, tk):
    i = row0 + lax.broadcasted_iota(jnp.int32, (tq, tk), 0)
    j = lax.broadcasted_iota(jnp.int32, (tq, tk), 1)
    d = (i - j).astype(F32)
    fwd = jnp.where(d >= 0.0, jnp.exp(lgf * jnp.maximum(d, 0.0)), 0.0)
    bwd = jnp.where(d <= 0.0, jnp.exp(lgb * jnp.maximum(-d, 0.0)), 0.0)
    return fwd + bwd


def _ret_finish(o, g, nw):
    return (g.astype(F32) * (_rms(o) * nw)).astype(BF16)


def _ret_ctx_kernel(n_prev, *refs):
    refs = list(refs)
    q_ref, k_ref, v_ref, g_ref, df_ref, db_ref, nw_ref = refs[0:7]
    refs = refs[7:]
    if n_prev:
        sfp_ref, sbp_ref = refs.pop(0), refs.pop(0)
    o_ref, sf_ref, sb_ref, dmat = refs
    @pl.when(pl.program_id(0) == 0)
    def _():
        for h in range(HEADS):
            dmat[h] = RET_SCALE * _decay_matrix(_log_sigmoid(df_ref[h]), _log_sigmoid(db_ref[h]), 0, T_CTX, T_CTX)

    if n_prev:
        sf_ref[0:n_prev] = sfp_ref[...]
        sb_ref[0:n_prev] = sbp_ref[...]
    j = lax.broadcasted_iota(jnp.int32, (T_CTX, 1), 0).astype(F32)
    for h in range(HEADS):
        sl = slice(h * HD, (h + 1) * HD)
        q = q_ref[:, sl]
        k = k_ref[:, sl]
        v = v_ref[:, sl]
        s = _dot_nt(q, k) * dmat[h]
        o = _dot(s.astype(BF16), v)
        o_ref[:, sl] = _ret_finish(o, g_ref[:, sl], nw_ref[h])
        lgf = _log_sigmoid(df_ref[h])
        lgb = _log_sigmoid(db_ref[h])
        kf = k.astype(F32) * RET_SCALE
        sf_ref[n_prev, h] = _dot_tn((kf * jnp.exp(lgf * (T_CTX - 1.0 - j))).astype(BF16), v)
        sb_ref[n_prev, h] = _dot_tn((kf * jnp.exp(lgb * j)).astype(BF16), v)


N_DCHUNK = 2 * LAT_TILES_PER_SEQ - 1


def _ret_lat_kernel(q_ref, k_ref, v_ref, g_ref, s0f_ref, s0b_ref, df_ref, db_ref, nw_ref, o_ref, strip, k_t):
    b = pl.program_id(0)
    j = pl.program_id(1)

    @pl.when(jnp.logical_and(b == 0, j == 0))
    def _():
        for h in range(HEADS):
            lgf = _log_sigmoid(df_ref[h])
            lgb = _log_sigmoid(db_ref[h])
            for c in range(N_DCHUNK):
                strip[h, c] = RET_SCALE * _decay_matrix(lgf, lgb, T_LAT - TM - c * TM, TM, TM)

    @pl.when(j == 0)
    def _():
        for h in range(HEADS):
            k_t[h] = k_ref[:, h * HD:(h + 1) * HD].astype(F32).T.astype(BF16)

    i = (j * TM + lax.broadcasted_iota(jnp.int32, (TM, 1), 0)).astype(F32)
    c0 = LAT_TILES_PER_SEQ - 1 - j
    for h in range(HEADS):
        sl = slice(h * HD, (h + 1) * HD)
        lgf = _log_sigmoid(df_ref[h])
        lgb = _log_sigmoid(db_ref[h])
        q = q_ref[:, sl]
        dmat = jnp.concatenate([strip[h, c0 + c] for c in range(LAT_TILES_PER_SEQ)], axis=-1)
        s = _dot(q, k_t[h]) * dmat
        o = _dot(s.astype(BF16), v_ref[:, sl])
        o = o + jnp.exp(lgf * (i + 1.0)) * _dot(q, s0f_ref[h].astype(BF16))
        o = o + jnp.exp(lgb * (T_LAT - i)) * _dot(q, s0b_ref[h].astype(BF16))
        o_ref[:, sl] = _ret_finish(o, g_ref[:, sl], nw_ref[h])


def _retention(l, zb, state_f, state_b, decay_f, decay_b, ret_norm_w, sf_prev, sb_prev):
    width = HEADS * HD
    dec1 = pl.BlockSpec((HEADS, 1, 1), lambda b: (l, 0, 0))
    nw1 = pl.BlockSpec((HEADS, 1, HD), lambda b: (l, 0, 0))
    st_spec = lambda n: pl.BlockSpec((None, n, HEADS, HD, HD), lambda b: (b, 0, 0, 0, 0))
    st_shape = jax.ShapeDtypeStruct((N_CTX_SEQ, l + 1, HEADS, HD, HD), F32)
    in_specs = [
        pl.BlockSpec((T_CTX, width), lambda b: (b, RQB // HEADS)),
        pl.BlockSpec((T_CTX, width), lambda b: (b, RKB // HEADS)),
        pl.BlockSpec((T_CTX, width), lambda b: (b, RVB // HEADS)),
        pl.BlockSpec((T_CTX, width), lambda b: (b, RGB // HEADS)),
        dec1, dec1, nw1,
    ]
    args = [zb, zb, zb, zb, decay_f, decay_b, ret_norm_w]
    if l:
        in_specs += [st_spec(l), st_spec(l)]
        args += [sf_prev, sb_prev]
    ro, sf, sb = pl.pallas_call(
        functools.partial(_ret_ctx_kernel, l),
        grid=(N_CTX_SEQ,),
        in_specs=in_specs,
        out_specs=[pl.BlockSpec((T_CTX, width), lambda b: (b, 0)), st_spec(l + 1), st_spec(l + 1)],
        out_shape=[jax.ShapeDtypeStruct((N_CTX, width), BF16), st_shape, st_shape],
        scratch_shapes=[pltpu.VMEM((HEADS, T_CTX, T_CTX), F32)],
        compiler_params=_params(1),
        name="ret_ctx",
    )(*args)

    lat_row = lambda b, j: CTX_TILES + LAT_TILES_PER_SEQ * b + j
    seq_row = lambda b: N_CTX // T_LAT + b
    dec2 = pl.BlockSpec((HEADS, 1, 1), lambda b, j: (l, 0, 0))
    nw2 = pl.BlockSpec((HEADS, 1, HD), lambda b, j: (l, 0, 0))
    s0_spec = pl.BlockSpec((None, None, HEADS, HD, HD), lambda b, j: (b, l, 0, 0, 0))
    ro_lat = pl.pallas_call(
        _ret_lat_kernel,
        grid=(N_LAT_SEQ, LAT_TILES_PER_SEQ),
        in_specs=[
            pl.BlockSpec((TM, width), lambda b, j: (lat_row(b, j), RQB // HEADS)),
            pl.BlockSpec((T_LAT, width), lambda b, j: (seq_row(b), RKB // HEADS)),
            pl.BlockSpec((T_LAT, width), lambda b, j: (seq_row(b), RVB // HEADS)),
            pl.BlockSpec((TM, width), lambda b, j: (lat_row(b, j), RGB // HEADS)),
            s0_spec, s0_spec, dec2, dec2, nw2,
        ],
        out_specs=pl.BlockSpec((TM, width), lambda b, j: (LAT_TILES_PER_SEQ * b + j, 0)),
        out_shape=jax.ShapeDtypeStruct((N_LAT, width), BF16),
        scratch_shapes=[pltpu.VMEM((HEADS, N_DCHUNK, TM, TM), F32), pltpu.VMEM((HEADS, HD, T_LAT), BF16)],
        compiler_params=_params(2),
        name="ret_lat",
    )(zb, zb, zb, zb, state_f, state_b, decay_f, decay_b, ret_norm_w)
    return ro, ro_lat, sf, sb


def _merge_kernel(x_pair, *refs):
    refs = list(refs)
    x_ref = refs.pop(0)
    xs_ref = refs.pop(0) if x_pair else None
    (aoc_ref, aol_ref, roc_ref, rol_ref, cb_ref, u_ref, up_ref, un_ref,
     mg0_ref, mg1_ref, mg2_ref, mod_ref, cw_ref, wa_ref, wc_ref, wr_ref, wo_ref, n2_ref, wrt_ref,
     x1_ref, h2_ref, aff_ref) = refs
    i = pl.program_id(0)
    is_ctx = i < CTX_TILES // MERGE_SUB
    m = mod_ref[0]
    gate1 = m[:, 2 * D:3 * D]
    shift2 = m[:, 3 * D:4 * D]
    scale2 = m[:, 4 * D:5 * D]
    cw = cw_ref[...]
    u_all = u_ref[...].astype(F32)
    r = lax.broadcasted_iota(jnp.int32, (TM, 512), 0)
    lane = lax.broadcasted_iota(jnp.int32, (TM, 128), 1)

    for sub in range(MERGE_SUB):
        rows = slice(sub * TM, (sub + 1) * TM)
        x_in = x_ref[rows, :]
        if x_pair:
            x_in = jnp.where(is_ctx, x_in, xs_ref[rows, :])
        ao = jnp.where(is_ctx, aoc_ref[rows, :], aol_ref[rows, :])
        ro = jnp.where(is_ctx, roc_ref[rows, :], rol_ref[rows, :])
        j = (i * MERGE_SUB + sub - CTX_TILES) % LAT_TILES_PER_SEQ
        seq_first = jnp.logical_or(is_ctx, j == 0)
        seq_last = jnp.logical_or(is_ctx, j == LAT_TILES_PER_SEQ - 1)

        u = u_all[rows, :]
        up = up_ref[...].astype(F32)[15:16, :] if sub == 0 else u_all[sub * TM - 1:sub * TM, :]
        dn = un_ref[...].astype(F32)[0:1, :] if sub == MERGE_SUB - 1 else u_all[(sub + 1) * TM:(sub + 1) * TM + 1, :]
        up = up * jnp.where(seq_first, 0.0, 1.0)
        dn = dn * jnp.where(seq_last, 0.0, 1.0)
        u_prev = jnp.where(r == 0, up, pltpu.roll(u, 1, 0))
        u_next = jnp.where(r == TM - 1, dn, pltpu.roll(u, TM - 1, 0))
        conv = u_prev * cw[0:1, :] + u * cw[1:2, :] + u_next * cw[2:3, :]
        conv_o = (cb_ref[rows, :].astype(F32) * conv).astype(BF16)

        merged = mg0_ref[rows, :].astype(F32) * _dot(ao, wa_ref[...])
        merged = merged + mg1_ref[rows, :].astype(F32) * _dot(conv_o, wc_ref[...])
        merged = merged + mg2_ref[rows, :].astype(F32) * _dot(ro, wr_ref[...])

        x1 = x_in + gate1 * _dot(merged.astype(BF16), wo_ref[...])
        x1_ref[rows, :] = x1
        h2 = (_rms(x1) * n2_ref[...]) * (1.0 + scale2) + shift2
        _to_slabs(h2_ref.at[pl.ds(sub * TM * SLAB, TM * SLAB), :], h2)

        logits = _dot(h2.astype(BF16), wrt_ref[...])
        valid = lane < N_EXPERTS
        lmax = jnp.max(jnp.where(valid, logits, -jnp.inf), axis=-1, keepdims=True)
        e = jnp.where(valid, jnp.exp(logits - lmax), 0.0)
        aff_ref[rows, :] = e * (1.0 / jnp.sum(e, axis=-1, keepdims=True))


def _merge(l, x, ao, ao_lat, ro, ro_lat, zb, mod, conv_w, w_br_attn, w_br_conv, w_br_ret, w_out, norm2_w,
           w_router_pad):
    mt = MERGE_SUB * TM
    n_steps = N_TOK // mt
    ctx_steps = N_CTX // mt
    n16 = N_TOK // 16
    ctx_br = pl.BlockSpec((mt, 512), lambda i: (jnp.minimum(i, ctx_steps - 1), 0))
    lat_br = pl.BlockSpec((mt, 512), lambda i: (jnp.maximum(i - ctx_steps, 0), 0))
    col = lambda c: pl.BlockSpec((mt, 512), lambda i: (i, c))
    halo_p = lambda c: pl.BlockSpec((16, 512), lambda i: (jnp.maximum(i * (mt // 16) - 1, 0), c))
    halo_n = lambda c: pl.BlockSpec((16, 512), lambda i: (jnp.minimum((i + 1) * (mt // 16), n16 - 1), c))
    mgs = lambda c: pl.BlockSpec((mt, D), lambda i: (i, c))
    wbr = pl.BlockSpec((None, 512, D), lambda i: (l, 0, 0))
    x_pair = isinstance(x, tuple)
    if x_pair:
        x_specs = [pl.BlockSpec((mt, D), lambda i: (jnp.minimum(i, ctx_steps - 1), 0)),
                   pl.BlockSpec((mt, D), lambda i: (jnp.maximum(i - ctx_steps, 0), 0))]
        x_args = list(x)
    else:
        x_specs = [pl.BlockSpec((mt, D), lambda i: (i, 0))]
        x_args = [x]
    return pl.pallas_call(
        functools.partial(_merge_kernel, x_pair),
        grid=(n_steps,),
        in_specs=x_specs + [
            ctx_br, lat_br, ctx_br, lat_br,
            col(CONV_B_COL), col(CONV_U_COL), halo_p(CONV_U_COL), halo_n(CONV_U_COL),
            mgs(MERGE_GATE_COL), mgs(MERGE_GATE_COL + 1), mgs(MERGE_GATE_COL + 2),
            pl.BlockSpec((1, 1, N_MOD * D), lambda i: (l * 8 + _mod_row(i * MERGE_SUB), 0, 0)),
            pl.BlockSpec((None, 3, 512), lambda i: (l, 0, 0)),
            wbr, wbr, wbr,
            pl.BlockSpec((None, D, D), lambda i: (l, 0, 0)),
            pl.BlockSpec((None, 1, D), lambda i: (l, 0, 0)),
            pl.BlockSpec((None, D, 128), lambda i: (l, 0, 0)),
        ],
        out_specs=[
            pl.BlockSpec((mt, D), lambda i: (i, 0)),
            pl.BlockSpec((mt * SLAB, 128), lambda i: (i, 0)),
            pl.BlockSpec((mt, 128), lambda i: (i, 0)),
        ],
        out_shape=[
            jax.ShapeDtypeStruct((N_TOK, D), F32),
            jax.ShapeDtypeStruct((N_TOK * SLAB, 128), F32),
            jax.ShapeDtypeStruct((N_TOK, 128), F32),
        ],
        compiler_params=_params(1),
        name="merge",
    )(*x_args, ao, ao_lat, ro, ro_lat, zb, zb, zb, zb, zb, zb, zb, mod, conv_w,
      w_br_attn, w_br_conv, w_br_ret, w_out, norm2_w, w_router_pad)


def _cumsum_lanes(x, tri):
    run = jnp.zeros((x.shape[0], 1), F32)
    outs = []
    for b in range(x.shape[1] // 128):
        cs = _dot(x[:, b * 128:(b + 1) * 128].astype(BF16), tri) + run
        run = cs[:, 127:128]
        outs.append(cs)
    return jnp.concatenate(outs, axis=-1)


GATE_LANE = (0, 16, 32)
IDX_LANE = 48
BISECT_GROUP = 4


def _topk_kernel(an_ref, tmat_ref, idx_ref, gate_ref):
    an = an_ref[...]
    a = an.T[0:N_EXPERTS, :]
    n = a.shape[1]
    kf = float(CAP)

    hi = an.astype(BF16).astype(F32)
    mid = (an - hi).astype(BF16).astype(F32)
    lo = ((an - hi) - mid).astype(BF16).astype(F32)
    table = (tmat_ref[...].astype(F32) + hi + pltpu.roll(mid, GATE_LANE[1], 1)
             + pltpu.roll(lo, GATE_LANE[2], 1)).astype(BF16)

    def count_gt(thr):
        return jnp.sum(jnp.where(a > thr, 1.0, 0.0), axis=-1, keepdims=True)

    def span(lo, hi):
        inside = jnp.logical_and(a > lo, a <= hi)
        cmax = jnp.max(jnp.where(inside, a, -jnp.inf), axis=-1, keepdims=True)
        cmin = jnp.min(jnp.where(inside, a, jnp.inf), axis=-1, keepdims=True)
        return cmax, cmin

    def cond(c):
        return jnp.logical_and(c[2] > 0, c[3] < 400)

    def body(c):
        lo, hi, _, it = c
        for _ in range(BISECT_GROUP):
            mid = 0.5 * (lo + hi)
            ge = count_gt(mid) >= kf
            lo = jnp.where(ge, mid, lo)
            hi = jnp.where(ge, hi, mid)
        cmax, cmin = span(lo, hi)
        open_rows = jnp.max(jnp.where(cmax != cmin, 1, 0))
        return lo, hi, open_rows, it + 1

    lo0 = jnp.full((N_EXPERTS, 1), -1.0, F32)
    hi0 = jnp.max(a, axis=-1, keepdims=True)
    cmax0, cmin0 = span(lo0, hi0)
    lo, hi, _, _ = lax.while_loop(
        cond, body, (lo0, hi0, jnp.max(jnp.where(cmax0 != cmin0, 1, 0)), jnp.int32(0)))
    thr, _ = span(lo, hi)

    r = lax.broadcasted_iota(jnp.int32, (128, 128), 0)
    c = lax.broadcasted_iota(jnp.int32, (128, 128), 1)
    tri = jnp.where(r <= c, 1.0, 0.0).astype(BF16)
    gt = a > thr
    eq = jnp.where(a == thr, 1.0, 0.0)
    need = kf - count_gt(thr)
    eq_before = _cumsum_lanes(eq, tri) - eq
    sel = jnp.where(jnp.logical_or(gt, jnp.logical_and(eq > 0.0, eq_before < need)), 1.0, 0.0)
    pos = _cumsum_lanes(sel, tri) - 1.0
    slot = jnp.where(sel > 0.0, pos, -1.0).astype(jnp.int32)

    p_iota = lax.broadcasted_iota(jnp.int32, (CAP, 1024), 0)
    lane = lax.broadcasted_iota(jnp.int32, (CAP, 128), 1)
    for e in range(N_EXPERTS):
        acc = jnp.zeros((CAP, 128), F32)
        for cb in range(n // 1024):
            onehot = jnp.where(p_iota == slot[e:e + 1, cb * 1024:(cb + 1) * 1024], 1.0, 0.0).astype(BF16)
            acc = acc + _dot(onehot, table[cb * 1024:(cb + 1) * 1024, :])
        idx_ref[0, e] = (acc[:, IDX_LANE:IDX_LANE + 1] * 64.0 + acc[:, IDX_LANE + 1:IDX_LANE + 2]).astype(jnp.int32)
        g = jnp.zeros((CAP, 1), F32)
        for off in GATE_LANE:
            g = g + jnp.sum(jnp.where(lane == off + e, acc, 0.0), axis=-1, keepdims=True)
        gate_ref[0, e] = g


def _topk(aff_n, tmat):
    out_spec = pl.BlockSpec((1, N_EXPERTS, CAP, 1), lambda s: (s, 0, 0, 0))
    return pl.pallas_call(
        _topk_kernel,
        grid=(2,),
        in_specs=[
            pl.BlockSpec((N_CTX, 128), lambda s: (s, 0)),
            pl.BlockSpec((N_CTX, 128), lambda s: (0, 0)),
        ],
        out_specs=[out_spec, out_spec],
        out_shape=[jax.ShapeDtypeStruct((2, N_EXPERTS, CAP, 1), jnp.int32),
                   jax.ShapeDtypeStruct((2, N_EXPERTS, CAP, 1), F32)],
        compiler_params=_params(1),
        name="topk",
    )(aff_n, tmat)


N_FT = FF // TF
GATHER_ROWS = 2 * CAP
ROWS_PER_STEP = GATHER_ROWS // N_FT


def _ffn_kernel(l, idx_ref, h2_hbm, g_ref, wg_ref, wu_ref, wd_ref, ye_ref, xg, xb, acc, sems):
    e = pl.program_id(0)
    f = pl.program_id(1)
    slot = e % 2

    def row_copy(expert, s, p, dst_slot):
        row = idx_ref[(s * N_EXPERTS + expert) * CAP + p] + s * N_CTX
        src = h2_hbm.at[pl.ds(pl.multiple_of(row * SLAB, SLAB), SLAB), :]
        dst = xg.at[dst_slot, pl.ds(pl.multiple_of((s * CAP + p) * SLAB, SLAB), SLAB), :]
        return pltpu.make_async_copy(src, dst, sems.at[dst_slot])

    def slot_wait(dst_slot):
        pltpu.make_async_copy(h2_hbm.at[pl.ds(0, GATHER_ROWS * SLAB), :], xg.at[dst_slot],
                              sems.at[dst_slot]).wait()

    @pl.when(jnp.logical_and(e == 0, f == 0))
    def _():
        def issue(r, carry):
            row_copy(0, r // CAP, r % CAP, 0).start()
            return carry
        lax.fori_loop(0, GATHER_ROWS, issue, 0)

    @pl.when(f == 0)
    def _():
        slot_wait(slot)
        xb[...] = _from_slabs(xg.at[slot]).astype(BF16)
        acc[...] = jnp.zeros_like(acc)

    nxt = jnp.minimum(e + 1, N_EXPERTS - 1)
    s_nxt = f // (N_FT // 2)
    p0 = (f % (N_FT // 2)) * ROWS_PER_STEP
    for u in range(ROWS_PER_STEP):
        row_copy(nxt, s_nxt, p0 + u, 1 - slot).start()

    x = xb[...]
    hg = _dot(x, wg_ref[...].astype(BF16))
    hu = _dot(x, wu_ref[...].astype(BF16))
    hdn = ((hg * _sigmoid(hg)) * hu).astype(BF16)
    acc[...] += _dot(hdn, wd_ref[...].astype(BF16))

    @pl.when(f == N_FT - 1)
    def _():
        for s in range(2):
            _to_slabs(ye_ref.at[s], acc[s * CAP:(s + 1) * CAP, :] * g_ref[s])

    @pl.when(jnp.logical_and(e == N_EXPERTS - 1, f == N_FT - 1))
    def _():
        slot_wait(1 - slot)


def _expert_ffn(l, idx_flat, h2s, gates, w_gate, w_up, w_down):
    grid_spec = pltpu.PrefetchScalarGridSpec(
        num_scalar_prefetch=1,
        grid=(N_EXPERTS, FF // TF),
        in_specs=[
            pl.BlockSpec(memory_space=pl.ANY),
            pl.BlockSpec((2, None, CAP, 1), lambda e, f, idx: (0, e, 0, 0)),
            pl.BlockSpec((None, None, D, TF), lambda e, f, idx: (l, e, 0, f)),
            pl.BlockSpec((None, None, D, TF), lambda e, f, idx: (l, e, 0, f)),
            pl.BlockSpec((None, None, TF, D), lambda e, f, idx: (l, e, f, 0)),
        ],
        out_specs=pl.BlockSpec((2, None, CAP * SLAB, 128), lambda e, f, idx: (0, e, 0, 0)),
        scratch_shapes=[
            pltpu.VMEM((2, GATHER_ROWS * SLAB, 128), F32),
            pltpu.VMEM((GATHER_ROWS, D), BF16),
            pltpu.VMEM((GATHER_ROWS, D), F32),
            pltpu.SemaphoreType.DMA((2,)),
        ],
    )
    return pl.pallas_call(
        functools.partial(_ffn_kernel, l),
        grid_spec=grid_spec,
        out_shape=jax.ShapeDtypeStruct((2, N_EXPERTS, CAP * SLAB, 128), F32),
        compiler_params=_params(2),
        name="expert_ffn",
    )(idx_flat, h2s, gates, w_gate, w_up, w_down)


SCATTER_UNROLL = 8


def _combine_kernel(idx_ref, ye_ref, y_ref):
    s = pl.program_id(0)
    e = pl.program_id(1)

    @pl.when(e == 0)
    def _():
        y_ref[...] = jnp.zeros_like(y_ref)

    base = (s * N_EXPERTS + e) * CAP

    def group(g, carry):
        p0 = g * SCATTER_UNROLL
        tile = lambda r: pl.ds(pl.multiple_of(r * SLAB, SLAB), SLAB)
        rows = [idx_ref[base + p0 + u] for u in range(SCATTER_UNROLL)]
        vals = [y_ref[tile(rows[u]), :] + ye_ref[tile(p0 + u), :] for u in range(SCATTER_UNROLL)]
        for u in range(SCATTER_UNROLL):
            y_ref[tile(rows[u]), :] = vals[u]
        return carry

    lax.fori_loop(0, CAP // SCATTER_UNROLL, group, 0)


def _combine(idx_flat, ye):
    grid_spec = pltpu.PrefetchScalarGridSpec(
        num_scalar_prefetch=1,
        grid=(2, N_EXPERTS),
        in_specs=[pl.BlockSpec((None, None, CAP * SLAB, 128), lambda s, e, idx: (s, e, 0, 0))],
        out_specs=pl.BlockSpec((N_CTX * SLAB, 128), lambda s, e, idx: (s, 0)),
    )
    return pl.pallas_call(
        _combine_kernel,
        grid_spec=grid_spec,
        out_shape=jax.ShapeDtypeStruct((N_TOK * SLAB, 128), F32),
        compiler_params=_params(2),
        name="combine",
    )(idx_flat, ye)


def _final_kernel(x_ref, y_ref, mod_ref, w_ref, op_ref, os_ref):
    x = x_ref[...] + mod_ref[0][:, 5 * D:6 * D] * _from_slabs(y_ref)
    out = _rms(x) * w_ref[...]
    is_ctx = pl.program_id(0) < CTX_TILES

    @pl.when(is_ctx)
    def _():
        op_ref[...] = out

    @pl.when(jnp.logical_not(is_ctx))
    def _():
        os_ref[...] = out


def _final(x1, y, mod, final_norm_w):
    tile = pl.BlockSpec((TM, D), lambda i: (i, 0))
    return pl.pallas_call(
        _final_kernel,
        grid=(N_TILES,),
        in_specs=[tile, pl.BlockSpec((TM * SLAB, 128), lambda i: (i, 0)),
                  pl.BlockSpec((1, 1, N_MOD * D), lambda i: ((DEPTH - 1) * 8 + _mod_row(i), 0, 0)),
                  pl.BlockSpec((1, D), lambda i: (0, 0))],
        out_specs=[pl.BlockSpec((TM, D), lambda i: (jnp.minimum(i, CTX_TILES - 1), 0)),
                   pl.BlockSpec((TM, D), lambda i: (jnp.maximum(i - CTX_TILES, 0), 0))],
        out_shape=[jax.ShapeDtypeStruct((N_CTX, D), F32), jax.ShapeDtypeStruct((N_LAT, D), F32)],
        compiler_params=_params(1),
        name="final_norm",
    )(x1, y, mod, final_norm_w.reshape(1, D))


def _rope_tables():
    t = np.arange(T_LAT)
    row = (t // GRID_W).astype(np.float32)
    col = (t % GRID_W).astype(np.float32)
    inv = jnp.asarray(ROPE_BASE, F32) ** (-jnp.arange(N_ROPE_FREQ, dtype=F32) / N_ROPE_FREQ)
    ang_r = jnp.asarray(row)[:, None] * inv
    ang_c = jnp.asarray(col)[:, None] * inv
    def group(ang):
        return jnp.concatenate([ang, ang], axis=-1)
    ang = jnp.concatenate([group(ang_r), group(ang_c), group(ang_r), group(ang_c)], axis=-1)
    sign = np.where(np.arange(HD) % 32 < 16, -1.0, 1.0).astype(np.float32)
    return jnp.cos(ang), jnp.sin(ang) * sign


def _index_table():
    t = np.arange(N_CTX)
    tm = np.zeros((N_CTX, 128), np.float32)
    tm[:, IDX_LANE] = t // 64
    tm[:, IDX_LANE + 1] = t % 64
    return jnp.asarray(tm, BF16)


def kernel(x_prompt, x_sample, c, cache_attn_k, cache_attn_v, state_ret_fwd, state_ret_bwd, c_ctx, w_ada, b_ada, norm1_w, norm2_w, w_in, attn_lambda, attn_subln_w, conv_w, ret_decay_fwd, ret_decay_bwd, ret_norm_w, w_br_attn, w_br_conv, w_br_ret, w_out, w_router, w_exp_gate, w_exp_up, w_exp_down, final_norm_w):
    x = (x_prompt.reshape(N_CTX, D), x_sample.reshape(N_LAT, D))
    cvec = jnp.concatenate([c_ctx[None, :], c, jnp.zeros((3, D), F32)], axis=0)
    mod = _modulation(cvec, w_ada, b_ada).reshape(DEPTH * 8, 1, N_MOD * D)

    w_in_bf = w_in.astype(BF16)
    w_br_attn_bf = w_br_attn.astype(BF16)
    w_br_conv_bf = w_br_conv.astype(BF16)
    w_br_ret_bf = w_br_ret.astype(BF16)
    w_out_bf = w_out.astype(BF16)
    w_router_pad = jnp.pad(w_router, ((0, 0), (0, 0), (0, 128 - N_EXPERTS))).astype(BF16)
    norm1 = norm1_w.reshape(DEPTH, 1, D)
    norm2 = norm2_w.reshape(DEPTH, 1, D)
    subln = attn_subln_w.reshape(DEPTH, 1, HD)
    decay_f = ret_decay_fwd.reshape(DEPTH * HEADS, 1, 1)
    decay_b = ret_decay_bwd.reshape(DEPTH * HEADS, 1, 1)
    ret_nw = ret_norm_w.reshape(DEPTH * HEADS, 1, HD)
    cos, sin = _rope_tables()
    tmat = _index_table()

    y = None
    new_k = new_v = new_sf = new_sb = None
    for l in range(DEPTH):
        lam_init = 0.8 - 0.6 * math.exp(-0.3 * l)
        if y is None:
            zb, zq, new_k, new_v = _inproj(l, x, None, mod, norm1, w_in_bf, new_k, new_v)
        else:
            zb, zq, new_k, new_v, x = _inproj(l, x, y, mod, norm1, w_in_bf, new_k, new_v)
        ao, ao_lat = _attention(l, lam_init, zb, zq, cache_attn_k, cache_attn_v, cos, sin, attn_lambda, subln)
        ro, ro_lat, new_sf, new_sb = _retention(l, zb, state_ret_fwd, state_ret_bwd, decay_f, decay_b, ret_nw,
                                                new_sf, new_sb)
        x, h2s, aff_n = _merge(l, x, ao, ao_lat, ro, ro_lat, zb, mod, conv_w, w_br_attn_bf, w_br_conv_bf,
                               w_br_ret_bf, w_out_bf, norm2, w_router_pad)
        idx, gates = _topk(aff_n, tmat)
        idx_flat = idx.reshape(2 * N_EXPERTS * CAP)
        ye = _expert_ffn(l, idx_flat, h2s, gates, w_exp_gate, w_exp_up, w_exp_down)
        y = _combine(idx_flat, ye)

    y_prompt, y_sample = _final(x, y, mod, final_norm_w)
    return (y_prompt.reshape(N_CTX_SEQ, T_CTX, D), y_sample.reshape(N_LAT_SEQ, T_LAT, D), new_k, new_v,
            new_sf, new_sb)
```

```python
import functools
import math

import jax
import jax.numpy as jnp
import numpy as np
from jax import lax
from jax.experimental import pallas as pl
from jax.experimental.pallas import tpu as pltpu

F32 = jnp.float32
BF16 = jnp.bfloat16

D = 1024
DEPTH = 2
N_CTX_SEQ = 16
T_CTX = 256
N_LAT_SEQ = 4
T_LAT = 1024
PAST = 256
N_CTX = N_CTX_SEQ * T_CTX
N_LAT = N_LAT_SEQ * T_LAT
N_TOK = N_CTX + N_LAT
TM = 256
N_TILES = N_TOK // TM
CTX_TILES = N_CTX // TM
LAT_TILES_PER_SEQ = T_LAT // TM
HEADS = 4
HD = 128
GRID_W = 64
N_ROPE_FREQ = 16
ROPE_BASE = 10000.0
IN_COLS = 8192
N_MOD = 6
N_EXPERTS = 16
CAP = 512
FF = 2048
TF = 512
EPS = 1e-6
RET_SCALE = HD ** -0.5
ATTN_SCALE = 64 ** -0.5
SLAB = D // 128
VMEM_LIMIT = 56 * 1024 * 1024

QB, KB, VB = 0, 4, 8
RQB, RKB, RVB, RGB = 24, 28, 32, 36
CONV_B_COL, CONV_U_COL = 3, 4
MERGE_GATE_COL = 5
MERGE_SUB = 2


def _sigmoid(x):
    return 0.5 * jnp.tanh(0.5 * x) + 0.5


def _log_sigmoid(x):
    return jnp.minimum(x, 0.0) - jnp.log(1.0 + jnp.exp(-jnp.abs(x)))


def _rms(x):
    return x * lax.rsqrt(jnp.mean(x * x, axis=-1, keepdims=True) + EPS)


def _dot(a, b):
    return jnp.dot(a, b, preferred_element_type=F32)


def _dot_nt(a, b):
    return lax.dot_general(a, b, (((1,), (1,)), ((), ())), preferred_element_type=F32)


def _dot_tn(a, b):
    return lax.dot_general(a, b, (((0,), (0,)), ((), ())), preferred_element_type=F32)


def _to_slabs(ref, x):
    n = x.shape[0]
    for s in range(SLAB):
        ref[pl.ds(s, n, stride=SLAB), :] = x[:, s * 128:(s + 1) * 128]


def _from_slabs(ref):
    n = ref.shape[0] // SLAB
    return jnp.concatenate([ref[pl.ds(s, n, stride=SLAB), :] for s in range(SLAB)], axis=-1)


def _mod_row(i):
    return jnp.where(i < CTX_TILES, 0, 1 + (i - CTX_TILES) // LAT_TILES_PER_SEQ)


def _params(n_axes):
    return pltpu.CompilerParams(
        dimension_semantics=("arbitrary",) * n_axes, vmem_limit_bytes=VMEM_LIMIT)


def _mod_kernel(c_ref, w_ref, b_ref, o_ref):
    c = c_ref[...]
    s = (c * _sigmoid(c)).astype(BF16)
    o_ref[...] = _dot(s, w_ref[...].astype(BF16)) + b_ref[...]


def _modulation(cvec, w_ada, b_ada):
    tn = 1024
    return pl.pallas_call(
        _mod_kernel,
        grid=(DEPTH, N_MOD * D // tn),
        in_specs=[
            pl.BlockSpec((8, D), lambda l, n: (0, 0)),
            pl.BlockSpec((None, D, tn), lambda l, n: (l, 0, n)),
            pl.BlockSpec((None, 1, tn), lambda l, n: (l, 0, n)),
        ],
        out_specs=pl.BlockSpec((None, 8, tn), lambda l, n: (l, 0, n)),
        out_shape=jax.ShapeDtypeStruct((DEPTH, 8, N_MOD * D), F32),
        compiler_params=_params(2),
        name="modulation",
    )(cvec, w_ada, b_ada.reshape(DEPTH, 1, N_MOD * D))


def _inproj_kernel(has_y, x_pair, n_prev, *refs):
    refs = list(refs)
    x_ref = refs.pop(0)
    if x_pair:
        xs_ref = refs.pop(0)
    if has_y:
        y_ref, modp_ref = refs.pop(0), refs.pop(0)
    mod_ref, n1_ref, w_ref = refs.pop(0), refs.pop(0), refs.pop(0)
    if n_prev:
        kp_ref, vp_ref = refs.pop(0), refs.pop(0)
    zb_ref, zq_ref, kn_ref, vn_ref = refs[0:4]
    i = pl.program_id(0)

    def normalize():
        x = x_ref[...]
        if x_pair:
            x = jnp.where(i < N_TILES - CTX_TILES, xs_ref[...], x)
        if has_y:
            x = x + modp_ref[0][:, 5 * D:6 * D] * _from_slabs(y_ref)
            refs[4][...] = x
        m = mod_ref[0]
        shift1 = m[:, 0:D]
        scale1 = m[:, D:2 * D]
        return ((_rms(x) * n1_ref[...]) * (1.0 + scale1) + shift1).astype(BF16)

    def project(h):
        cw = 1024
        for c in range(IN_COLS // cw):
            z = _dot(h, w_ref[:, c * cw:(c + 1) * cw])
            if c == CONV_U_COL // 2:
                z = jnp.concatenate([z[:, 0:512] * z[:, 512:cw], z[:, 512:cw]], axis=-1)
            elif c == RGB // 8:
                g = z[:, 512:cw]
                z = jnp.concatenate([z[:, 0:512], g * _sigmoid(g)], axis=-1)
            elif c >= MERGE_GATE_COL:
                z = _sigmoid(z)
            zb_ref[:, c * cw:(c + 1) * cw] = z.astype(BF16)
            if c == 0:
                zq_ref[...] = z
                for hh in range(HEADS):
                    kn_ref[n_prev, hh] = z[:, 512 + hh * HD:512 + (hh + 1) * HD]
            if c == 1:
                for hh in range(HEADS):
                    vn_ref[n_prev, hh] = z[:, hh * HD:(hh + 1) * HD]
        if n_prev:
            kn_ref[0:n_prev] = kp_ref[...]
            vn_ref[0:n_prev] = vp_ref[...]

    project(normalize())


def _inproj(l, x, y, mod, norm1_w, w_in_bf, k_prev, v_prev):
    has_y = y is not None
    t = lambda i: (i + CTX_TILES) % N_TILES
    n_lat = N_TILES - CTX_TILES
    norm_tile = pl.BlockSpec((TM, D), lambda i: (t(i), 0))
    ctx_i = lambda i: jnp.maximum(i - n_lat, 0)
    kv_spec = lambda m: pl.BlockSpec((None, m, HEADS, T_CTX, HD), lambda i: (ctx_i(i), 0, 0, 0, 0))
    kv_shape = jax.ShapeDtypeStruct((N_CTX_SEQ, l + 1, HEADS, T_CTX, HD), F32)
    x_pair = isinstance(x, tuple)
    if x_pair:
        in_specs = [pl.BlockSpec((TM, D), lambda i: (jnp.maximum(i - n_lat, 0), 0)),
                    pl.BlockSpec((TM, D), lambda i: (jnp.minimum(i, n_lat - 1), 0))]
        args = list(x)
    else:
        in_specs = [norm_tile]
        args = [x]
    if has_y:
        in_specs += [pl.BlockSpec((TM * SLAB, 128), lambda i: (t(i), 0)),
                     pl.BlockSpec((1, 1, N_MOD * D), lambda i: ((l - 1) * 8 + _mod_row(t(i)), 0, 0))]
        args += [y, mod]
    in_specs += [
        pl.BlockSpec((1, 1, N_MOD * D), lambda i: (l * 8 + _mod_row(t(i)), 0, 0)),
        pl.BlockSpec((None, 1, D), lambda i: (l, 0, 0)),
        pl.BlockSpec((None, D, IN_COLS), lambda i: (l, 0, 0), pipeline_mode=pl.Buffered(1)),
    ]
    args += [mod, norm1_w, w_in_bf]
    if l:
        in_specs += [kv_spec(l), kv_spec(l)]
        args += [k_prev, v_prev]
    out_specs = [pl.BlockSpec((TM, IN_COLS), lambda i: (t(i), 0)),
                 pl.BlockSpec((TM, 1024), lambda i: (t(i), 0)),
                 kv_spec(l + 1), kv_spec(l + 1)]
    out_shape = [jax.ShapeDtypeStruct((N_TOK, IN_COLS), BF16),
                 jax.ShapeDtypeStruct((N_TOK, 1024), F32),
                 kv_shape, kv_shape]
    if has_y:
        out_specs.append(norm_tile)
        out_shape.append(jax.ShapeDtypeStruct((N_TOK, D), F32))
    return pl.pallas_call(
        functools.partial(_inproj_kernel, has_y, x_pair, l),
        grid=(N_TILES,),
        in_specs=in_specs,
        out_specs=out_specs,
        out_shape=out_shape,
        compiler_params=_params(1),
        name="inproj",
    )(*args)


def _lambda(lam_ref, lam_init):
    lv = lam_ref[...]
    a = jnp.sum(lv[0:1] * lv[1:2], axis=-1, keepdims=True)
    b = jnp.sum(lv[2:3] * lv[3:4], axis=-1, keepdims=True)
    return jnp.exp(a) - jnp.exp(b) + lam_init


def _diff_attention(lam_init, qs, keys, vs, lam, sw, keys_transposed):
    lane = lax.broadcasted_iota(jnp.int32, qs[0].shape, 1)
    score = _dot if keys_transposed else _dot_nt
    maps = [(h, jnp.where(keep, q, jnp.zeros_like(q)))
            for h, q in enumerate(qs) for keep in (lane < 64, lane >= 64)]
    s = [score(qm, keys[h]) for h, qm in maps]
    e = [jnp.exp(x - jnp.max(x, axis=-1, keepdims=True)) for x in s]
    r = [1.0 / jnp.sum(x, axis=-1, keepdims=True) for x in e]
    a = [(e[2 * h] * r[2 * h] - e[2 * h + 1] * (lam * r[2 * h + 1])).astype(BF16) for h in range(len(qs))]
    o = [_dot(a[h], vs[h]) for h in range(len(qs))]
    return [(_rms(x) * sw) * (1.0 - lam_init) for x in o]


def _scaled_q(q):
    return (q.astype(F32) * ATTN_SCALE).astype(BF16)


def _attn_ctx_kernel(lam_init, q_ref, k_ref, v_ref, lam_ref, sw_ref, o_ref):
    lam = _lambda(lam_ref, lam_init)
    sls = [slice(h * HD, (h + 1) * HD) for h in range(HEADS)]
    outs = _diff_attention(lam_init, [_scaled_q(q_ref[:, sl]) for sl in sls], [k_ref[:, sl] for sl in sls],
                           [v_ref[:, sl] for sl in sls], lam, sw_ref[...], False)
    for sl, o in zip(sls, outs):
        o_ref[:, sl] = o.astype(BF16)


def _rope(x, cos, sin_signed):
    lane = lax.broadcasted_iota(jnp.int32, x.shape, 1)
    partner = jnp.where(lane % 32 < 16, pltpu.roll(x, 112, 1), pltpu.roll(x, 16, 1))
    return x * cos + partner * sin_signed


def _attn_lat_kernel(lam_init, q_ref, k_ref, v_ref, ck_ref, cv_ref, cos_ref, sin_ref,
                     cosq_ref, sinq_ref, lam_ref, sw_ref, o_ref, k_t, vall):
    @pl.when(pl.program_id(1) == 0)
    def _():
        for h in range(HEADS):
            sl = slice(h * HD, (h + 1) * HD)
            k_t[h, :, 0:PAST] = ck_ref[h].T.astype(BF16)
            k_t[h, :, PAST:] = _rope(k_ref[:, sl], cos_ref[...], sin_ref[...]).T.astype(BF16)
            vall[h, 0:PAST, :] = cv_ref[h].astype(BF16)
            vall[h, PAST:, :] = v_ref[:, sl]

    lam = _lambda(lam_ref, lam_init)
    sls = [slice(h * HD, (h + 1) * HD) for h in range(HEADS)]
    qs = [_scaled_q(_rope(q_ref[:, sl], cosq_ref[...], sinq_ref[...])) for sl in sls]
    outs = _diff_attention(lam_init, qs, [k_t[h] for h in range(HEADS)], [vall[h] for h in range(HEADS)],
                           lam, sw_ref[...], True)
    for sl, o in zip(sls, outs):
        o_ref[:, sl] = o.astype(BF16)


def _attention(l, lam_init, zb, zq, cache_k, cache_v, cos, sin, attn_lambda, subln_w):
    width = HEADS * HD
    ao = pl.pallas_call(
        functools.partial(_attn_ctx_kernel, lam_init),
        grid=(N_CTX_SEQ,),
        in_specs=[
            pl.BlockSpec((T_CTX, width), lambda b: (b, QB // HEADS)),
            pl.BlockSpec((T_CTX, width), lambda b: (b, KB // HEADS)),
            pl.BlockSpec((T_CTX, width), lambda b: (b, VB // HEADS)),
            pl.BlockSpec((None, 4, 64), lambda b: (l, 0, 0)),
            pl.BlockSpec((None, 1, HD), lambda b: (l, 0, 0)),
        ],
        out_specs=pl.BlockSpec((T_CTX, width), lambda b: (b, 0)),
        out_shape=jax.ShapeDtypeStruct((N_CTX, width), BF16),
        compiler_params=_params(1),
        name="attn_ctx",
    )(zb, zb, zb, attn_lambda, subln_w)

    lat_row = lambda b, j: CTX_TILES + LAT_TILES_PER_SEQ * b + j
    seq_row = lambda b: N_CTX // T_LAT + b
    ao_lat = pl.pallas_call(
        functools.partial(_attn_lat_kernel, lam_init),
        grid=(N_LAT_SEQ, LAT_TILES_PER_SEQ),
        in_specs=[
            pl.BlockSpec((TM, width), lambda b, j: (lat_row(b, j), QB // HEADS)),
            pl.BlockSpec((T_LAT, width), lambda b, j: (seq_row(b), KB // HEADS)),
            pl.BlockSpec((T_LAT, width), lambda b, j: (seq_row(b), VB // HEADS)),
            pl.BlockSpec((None, None, HEADS, PAST, HD), lambda b, j: (b, l, 0, 0, 0)),
            pl.BlockSpec((None, None, HEADS, PAST, HD), lambda b, j: (b, l, 0, 0, 0)),
            pl.BlockSpec((T_LAT, HD), lambda b, j: (0, 0)),
            pl.BlockSpec((T_LAT, HD), lambda b, j: (0, 0)),
            pl.BlockSpec((TM, HD), lambda b, j: (j, 0)),
            pl.BlockSpec((TM, HD), lambda b, j: (j, 0)),
            pl.BlockSpec((None, 4, 64), lambda b, j: (l, 0, 0)),
            pl.BlockSpec((None, 1, HD), lambda b, j: (l, 0, 0)),
        ],
        out_specs=pl.BlockSpec((TM, width), lambda b, j: (LAT_TILES_PER_SEQ * b + j, 0)),
        out_shape=jax.ShapeDtypeStruct((N_LAT, width), BF16),
        scratch_shapes=[pltpu.VMEM((HEADS, HD, PAST + T_LAT), BF16),
                        pltpu.VMEM((HEADS, PAST + T_LAT, HD), BF16)],
        compiler_params=_params(2),
        name="attn_lat",
    )(zq, zq, zb, cache_k, cache_v, cos, sin, cos, sin, attn_lambda, subln_w)
    return ao, ao_lat


def _decay_matrix(lgf, lgb, row0, tq, tk):
    i = row0 + lax.broadcasted_iota(jnp.int32, (tq, tk), 0)
    j = lax.broadcasted_iota(jnp.int32, (tq, tk), 1)
    d = (i - j).astype(F32)
    fwd = jnp.where(d >= 0.0, jnp.exp(lgf * jnp.maximum(d, 0.0)), 0.0)
    bwd = jnp.where(d <= 0.0, jnp.exp(lgb * jnp.maximum(-d, 0.0)), 0.0)
    return fwd + bwd


def _ret_finish(o, g, nw):
    return (g.astype(F32) * (_rms(o) * nw)).astype(BF16)


def _ret_ctx_kernel(n_prev, *refs):
    refs = list(refs)
    q_ref, k_ref, v_ref, g_ref, df_ref, db_ref, nw_ref = refs[0:7]
    refs = refs[7:]
    if n_prev:
        sfp_ref, sbp_ref = refs.pop(0), refs.pop(0)
    o_ref, sf_ref, sb_ref, dmat = refs
    @pl.when(pl.program_id(0) == 0)
    def _():
        for h in range(HEADS):
            dmat[h] = RET_SCALE * _decay_matrix(_log_sigmoid(df_ref[h]), _log_sigmoid(db_ref[h]), 0, T_CTX, T_CTX)

    if n_prev:
        sf_ref[0:n_prev] = sfp_ref[...]
        sb_ref[0:n_prev] = sbp_ref[...]
    j = lax.broadcasted_iota(jnp.int32, (T_CTX, 1), 0).astype(F32)
    heads = range(HEADS)
    sls = [slice(h * HD, (h + 1) * HD) for h in heads]
    q = [q_ref[:, sl] for sl in sls]
    k = [k_ref[:, sl] for sl in sls]
    v = [v_ref[:, sl] for sl in sls]
    s = [(_dot_nt(q[h], k[h]) * dmat[h]).astype(BF16) for h in heads]
    o = [_dot(s[h], v[h]) for h in heads]
    lgf = [_log_sigmoid(df_ref[h]) for h in heads]
    lgb = [_log_sigmoid(db_ref[h]) for h in heads]
    kf = [k[h].astype(F32) * RET_SCALE for h in heads]
    kfw = [(kf[h] * jnp.exp(lgf[h] * (T_CTX - 1.0 - j))).astype(BF16) for h in heads]
    kbw = [(kf[h] * jnp.exp(lgb[h] * j)).astype(BF16) for h in heads]
    for h in heads:
        sf_ref[n_prev, h] = _dot_tn(kfw[h], v[h])
        sb_ref[n_prev, h] = _dot_tn(kbw[h], v[h])
    for h in heads:
        o_ref[:, sls[h]] = _ret_finish(o[h], g_ref[:, sls[h]], nw_ref[h])


N_DCHUNK = 2 * LAT_TILES_PER_SEQ - 1


def _ret_lat_kernel(q_ref, k_ref, v_ref, g_ref, s0f_ref, s0b_ref, df_ref, db_ref, nw_ref, o_ref, strip, k_t):
    b = pl.program_id(0)
    j = pl.program_id(1)

    @pl.when(jnp.logical_and(b == 0, j == 0))
    def _():
        for h in range(HEADS):
            lgf = _log_sigmoid(df_ref[h])
            lgb = _log_sigmoid(db_ref[h])
            for c in range(N_DCHUNK):
                strip[h, c] = RET_SCALE * _decay_matrix(lgf, lgb, T_LAT - TM - c * TM, TM, TM)

    @pl.when(j == 0)
    def _():
        for h in range(HEADS):
            k_t[h] = k_ref[:, h * HD:(h + 1) * HD].astype(F32).T.astype(BF16)

    i = (j * TM + lax.broadcasted_iota(jnp.int32, (TM, 1), 0)).astype(F32)
    c0 = LAT_TILES_PER_SEQ - 1 - j
    heads = range(HEADS)
    sls = [slice(h * HD, (h + 1) * HD) for h in heads]
    q = [q_ref[:, sl] for sl in sls]
    dmat = [jnp.concatenate([strip[h, c0 + c] for c in range(LAT_TILES_PER_SEQ)], axis=-1) for h in heads]
    s = [(_dot(q[h], k_t[h]) * dmat[h]).astype(BF16) for h in heads]
    o = [_dot(s[h], v_ref[:, sls[h]]) for h in heads]
    of = [_dot(q[h], s0f_ref[h].astype(BF16)) for h in heads]
    ob = [_dot(q[h], s0b_ref[h].astype(BF16)) for h in heads]
    for h in heads:
        tot = o[h] + jnp.exp(_log_sigmoid(df_ref[h]) * (i + 1.0)) * of[h]
        tot = tot + jnp.exp(_log_sigmoid(db_ref[h]) * (T_LAT - i)) * ob[h]
        o_ref[:, sls[h]] = _ret_finish(tot, g_ref[:, sls[h]], nw_ref[h])


def _retention(l, zb, state_f, state_b, decay_f, decay_b, ret_norm_w, sf_prev, sb_prev):
    width = HEADS * HD
    dec1 = pl.BlockSpec((HEADS, 1, 1), lambda b: (l, 0, 0))
    nw1 = pl.BlockSpec((HEADS, 1, HD), lambda b: (l, 0, 0))
    st_spec = lambda n: pl.BlockSpec((None, n, HEADS, HD, HD), lambda b: (b, 0, 0, 0, 0))
    st_shape = jax.ShapeDtypeStruct((N_CTX_SEQ, l + 1, HEADS, HD, HD), F32)
    in_specs = [
        pl.BlockSpec((T_CTX, width), lambda b: (b, RQB // HEADS)),
        pl.BlockSpec((T_CTX, width), lambda b: (b, RKB // HEADS)),
        pl.BlockSpec((T_CTX, width), lambda b: (b, RVB // HEADS)),
        pl.BlockSpec((T_CTX, width), lambda b: (b, RGB // HEADS)),
        dec1, dec1, nw1,
    ]
    args = [zb, zb, zb, zb, decay_f, decay_b, ret_norm_w]
    if l:
        in_specs += [st_spec(l), st_spec(l)]
        args += [sf_prev, sb_prev]
    ro, sf, sb = pl.pallas_call(
        functools.partial(_ret_ctx_kernel, l),
        grid=(N_CTX_SEQ,),
        in_specs=in_specs,
        out_specs=[pl.BlockSpec((T_CTX, width), lambda b: (b, 0)), st_spec(l + 1), st_spec(l + 1)],
        out_shape=[jax.ShapeDtypeStruct((N_CTX, width), BF16), st_shape, st_shape],
        scratch_shapes=[pltpu.VMEM((HEADS, T_CTX, T_CTX), F32)],
        compiler_params=_params(1),
        name="ret_ctx",
    )(*args)

    lat_row = lambda b, j: CTX_TILES + LAT_TILES_PER_SEQ * b + j
    seq_row = lambda b: N_CTX // T_LAT + b
    dec2 = pl.BlockSpec((HEADS, 1, 1), lambda b, j: (l, 0, 0))
    nw2 = pl.BlockSpec((HEADS, 1, HD), lambda b, j: (l, 0, 0))
    s0_spec = pl.BlockSpec((None, None, HEADS, HD, HD), lambda b, j: (b, l, 0, 0, 0))
    ro_lat = pl.pallas_call(
        _ret_lat_kernel,
        grid=(N_LAT_SEQ, LAT_TILES_PER_SEQ),
        in_specs=[
            pl.BlockSpec((TM, width), lambda b, j: (lat_row(b, j), RQB // HEADS)),
            pl.BlockSpec((T_LAT, width), lambda b, j: (seq_row(b), RKB // HEADS)),
            pl.BlockSpec((T_LAT, width), lambda b, j: (seq_row(b), RVB // HEADS)),
            pl.BlockSpec((TM, width), lambda b, j: (lat_row(b, j), RGB // HEADS)),
            s0_spec, s0_spec, dec2, dec2, nw2,
        ],
        out_specs=pl.BlockSpec((TM, width), lambda b, j: (LAT_TILES_PER_SEQ * b + j, 0)),
        out_shape=jax.ShapeDtypeStruct((N_LAT, width), BF16),
        scratch_shapes=[pltpu.VMEM((HEADS, N_DCHUNK, TM, TM), F32), pltpu.VMEM((HEADS, HD, T_LAT), BF16)],
        compiler_params=_params(2),
        name="ret_lat",
    )(zb, zb, zb, zb, state_f, state_b, decay_f, decay_b, ret_norm_w)
    return ro, ro_lat, sf, sb


def _merge_kernel(x_pair, *refs):
    refs = list(refs)
    x_ref = refs.pop(0)
    xs_ref = refs.pop(0) if x_pair else None
    (aoc_ref, aol_ref, roc_ref, rol_ref, cb_ref, u_ref, up_ref, un_ref,
     mg0_ref, mg1_ref, mg2_ref, mod_ref, cw_ref, wa_ref, wc_ref, wr_ref, wo_ref, n2_ref, wrt_ref,
     x1_ref, h2_ref, aff_ref) = refs
    i = pl.program_id(0)
    is_ctx = i < CTX_TILES // MERGE_SUB
    m = mod_ref[0]
    gate1 = m[:, 2 * D:3 * D]
    shift2 = m[:, 3 * D:4 * D]
    scale2 = m[:, 4 * D:5 * D]
    cw = cw_ref[...]
    u_all = u_ref[...].astype(F32)
    r = lax.broadcasted_iota(jnp.int32, (TM, 512), 0)
    lane = lax.broadcasted_iota(jnp.int32, (TM, 128), 1)

    subs = range(MERGE_SUB)
    rows = [slice(sub * TM, (sub + 1) * TM) for sub in subs]

    def conv_out(sub):
        j = (i * MERGE_SUB + sub - CTX_TILES) % LAT_TILES_PER_SEQ
        seq_first = jnp.logical_or(is_ctx, j == 0)
        seq_last = jnp.logical_or(is_ctx, j == LAT_TILES_PER_SEQ - 1)
        u = u_all[rows[sub], :]
        up = up_ref[...].astype(F32)[15:16, :] if sub == 0 else u_all[sub * TM - 1:sub * TM, :]
        dn = un_ref[...].astype(F32)[0:1, :] if sub == MERGE_SUB - 1 else u_all[(sub + 1) * TM:(sub + 1) * TM + 1, :]
        up = up * jnp.where(seq_first, 0.0, 1.0)
        dn = dn * jnp.where(seq_last, 0.0, 1.0)
        u_prev = jnp.where(r == 0, up, pltpu.roll(u, 1, 0))
        u_next = jnp.where(r == TM - 1, dn, pltpu.roll(u, TM - 1, 0))
        conv = u_prev * cw[0:1, :] + u * cw[1:2, :] + u_next * cw[2:3, :]
        return (cb_ref[rows[sub], :].astype(F32) * conv).astype(BF16)

    ao = [jnp.where(is_ctx, aoc_ref[rw, :], aol_ref[rw, :]) for rw in rows]
    ro = [jnp.where(is_ctx, roc_ref[rw, :], rol_ref[rw, :]) for rw in rows]
    conv_o = [conv_out(sub) for sub in subs]
    b_attn = [_dot(ao[s], wa_ref[...]) for s in subs]
    b_conv = [_dot(conv_o[s], wc_ref[...]) for s in subs]
    b_ret = [_dot(ro[s], wr_ref[...]) for s in subs]
    merged = [(mg0_ref[rows[s], :].astype(F32) * b_attn[s] + mg1_ref[rows[s], :].astype(F32) * b_conv[s]
               + mg2_ref[rows[s], :].astype(F32) * b_ret[s]).astype(BF16) for s in subs]
    proj = [_dot(merged[s], wo_ref[...]) for s in subs]
    x1 = []
    for s in subs:
        x_in = x_ref[rows[s], :]
        if x_pair:
            x_in = jnp.where(is_ctx, x_in, xs_ref[rows[s], :])
        x1.append(x_in + gate1 * proj[s])
        x1_ref[rows[s], :] = x1[s]
    h2 = [(_rms(x1[s]) * n2_ref[...]) * (1.0 + scale2) + shift2 for s in subs]
    logits = [_dot(h2[s].astype(BF16), wrt_ref[...]) for s in subs]
    valid = lane < N_EXPERTS
    for s in subs:
        _to_slabs(h2_ref.at[pl.ds(s * TM * SLAB, TM * SLAB), :], h2[s])
        lmax = jnp.max(jnp.where(valid, logits[s], -jnp.inf), axis=-1, keepdims=True)
        e = jnp.where(valid, jnp.exp(logits[s] - lmax), 0.0)
        aff_ref[rows[s], :] = e * (1.0 / jnp.sum(e, axis=-1, keepdims=True))


def _merge(l, x, ao, ao_lat, ro, ro_lat, zb, mod, conv_w, w_br_attn, w_br_conv, w_br_ret, w_out, norm2_w,
           w_router_pad):
    mt = MERGE_SUB * TM
    n_steps = N_TOK // mt
    ctx_steps = N_CTX // mt
    n16 = N_TOK // 16
    ctx_br = pl.BlockSpec((mt, 512), lambda i: (jnp.minimum(i, ctx_steps - 1), 0))
    lat_br = pl.BlockSpec((mt, 512), lambda i: (jnp.maximum(i - ctx_steps, 0), 0))
    col = lambda c: pl.BlockSpec((mt, 512), lambda i: (i, c))
    halo_p = lambda c: pl.BlockSpec((16, 512), lambda i: (jnp.maximum(i * (mt // 16) - 1, 0), c))
    halo_n = lambda c: pl.BlockSpec((16, 512), lambda i: (jnp.minimum((i + 1) * (mt // 16), n16 - 1), c))
    mgs = lambda c: pl.BlockSpec((mt, D), lambda i: (i, c))
    wbr = pl.BlockSpec((None, 512, D), lambda i: (l, 0, 0))
    x_pair = isinstance(x, tuple)
    if x_pair:
        x_specs = [pl.BlockSpec((mt, D), lambda i: (jnp.minimum(i, ctx_steps - 1), 0)),
                   pl.BlockSpec((mt, D), lambda i: (jnp.maximum(i - ctx_steps, 0), 0))]
        x_args = list(x)
    else:
        x_specs = [pl.BlockSpec((mt, D), lambda i: (i, 0))]
        x_args = [x]
    return pl.pallas_call(
        functools.partial(_merge_kernel, x_pair),
        grid=(n_steps,),
        in_specs=x_specs + [
            ctx_br, lat_br, ctx_br, lat_br,
            col(CONV_B_COL), col(CONV_U_COL), halo_p(CONV_U_COL), halo_n(CONV_U_COL),
            mgs(MERGE_GATE_COL), mgs(MERGE_GATE_COL + 1), mgs(MERGE_GATE_COL + 2),
            pl.BlockSpec((1, 1, N_MOD * D), lambda i: (l * 8 + _mod_row(i * MERGE_SUB), 0, 0)),
            pl.BlockSpec((None, 3, 512), lambda i: (l, 0, 0)),
            wbr, wbr, wbr,
            pl.BlockSpec((None, D, D), lambda i: (l, 0, 0)),
            pl.BlockSpec((None, 1, D), lambda i: (l, 0, 0)),
            pl.BlockSpec((None, D, 128), lambda i: (l, 0, 0)),
        ],
        out_specs=[
            pl.BlockSpec((mt, D), lambda i: (i, 0)),
            pl.BlockSpec((mt * SLAB, 128), lambda i: (i, 0)),
            pl.BlockSpec((mt, 128), lambda i: (i, 0)),
        ],
        out_shape=[
            jax.ShapeDtypeStruct((N_TOK, D), F32),
            jax.ShapeDtypeStruct((N_TOK * SLAB, 128), F32),
            jax.ShapeDtypeStruct((N_TOK, 128), F32),
        ],
        compiler_params=_params(1),
        name="merge",
    )(*x_args, ao, ao_lat, ro, ro_lat, zb, zb, zb, zb, zb, zb, zb, mod, conv_w,
      w_br_attn, w_br_conv, w_br_ret, w_out, norm2_w, w_router_pad)


def _cumsum_lanes(x, tri):
    run = jnp.zeros((x.shape[0], 1), F32)
    outs = []
    for b in range(x.shape[1] // 128):
        cs = _dot(x[:, b * 128:(b + 1) * 128].astype(BF16), tri) + run
        run = cs[:, 127:128]
        outs.append(cs)
    return jnp.concatenate(outs, axis=-1)


GATE_LANE = (0, 16, 32)
IDX_LANE = 48
BISECT_GROUP = 4


def _topk_kernel(an_ref, tmat_ref, idx_ref, gate_ref):
    an = an_ref[...]
    a = an.T[0:N_EXPERTS, :]
    n = a.shape[1]
    kf = float(CAP)

    hi = an.astype(BF16).astype(F32)
    mid = (an - hi).astype(BF16).astype(F32)
    lo = ((an - hi) - mid).astype(BF16).astype(F32)
    table = (tmat_ref[...].astype(F32) + hi + pltpu.roll(mid, GATE_LANE[1], 1)
             + pltpu.roll(lo, GATE_LANE[2], 1)).astype(BF16)

    def count_gt(thr):
        return jnp.sum(jnp.where(a > thr, 1.0, 0.0), axis=-1, keepdims=True)

    def span(lo, hi):
        inside = jnp.logical_and(a > lo, a <= hi)
        cmax = jnp.max(jnp.where(inside, a, -jnp.inf), axis=-1, keepdims=True)
        cmin = jnp.min(jnp.where(inside, a, jnp.inf), axis=-1, keepdims=True)
        return cmax, cmin

    def cond(c):
        return jnp.logical_and(c[2] > 0, c[3] < 400)

    def body(c):
        lo, hi, _, it = c
        for _ in range(BISECT_GROUP):
            mid = 0.5 * (lo + hi)
            ge = count_gt(mid) >= kf
            lo = jnp.where(ge, mid, lo)
            hi = jnp.where(ge, hi, mid)
        cmax, cmin = span(lo, hi)
        open_rows = jnp.max(jnp.where(cmax != cmin, 1, 0))
        return lo, hi, open_rows, it + 1

    lo0 = jnp.full((N_EXPERTS, 1), -1.0, F32)
    hi0 = jnp.max(a, axis=-1, keepdims=True)
    cmax0, cmin0 = span(lo0, hi0)
    lo, hi, _, _ = lax.while_loop(
        cond, body, (lo0, hi0, jnp.max(jnp.where(cmax0 != cmin0, 1, 0)), jnp.int32(0)))
    thr, _ = span(lo, hi)

    r = lax.broadcasted_iota(jnp.int32, (128, 128), 0)
    c = lax.broadcasted_iota(jnp.int32, (128, 128), 1)
    tri = jnp.where(r <= c, 1.0, 0.0).astype(BF16)
    gt = a > thr
    eq = jnp.where(a == thr, 1.0, 0.0)
    need = kf - count_gt(thr)
    eq_before = _cumsum_lanes(eq, tri) - eq
    sel = jnp.where(jnp.logical_or(gt, jnp.logical_and(eq > 0.0, eq_before < need)), 1.0, 0.0)
    pos = _cumsum_lanes(sel, tri) - 1.0
    slot = jnp.where(sel > 0.0, pos, -1.0).astype(jnp.int32)

    p_iota = lax.broadcasted_iota(jnp.int32, (CAP, 1024), 0)
    lane = lax.broadcasted_iota(jnp.int32, (CAP, 128), 1)
    for e in range(N_EXPERTS):
        acc = jnp.zeros((CAP, 128), F32)
        for cb in range(n // 1024):
            onehot = jnp.where(p_iota == slot[e:e + 1, cb * 1024:(cb + 1) * 1024], 1.0, 0.0).astype(BF16)
            acc = acc + _dot(onehot, table[cb * 1024:(cb + 1) * 1024, :])
        idx_ref[0, e] = (acc[:, IDX_LANE:IDX_LANE + 1] * 64.0 + acc[:, IDX_LANE + 1:IDX_LANE + 2]).astype(jnp.int32)
        g = jnp.zeros((CAP, 1), F32)
        for off in GATE_LANE:
            g = g + jnp.sum(jnp.where(lane == off + e, acc, 0.0), axis=-1, keepdims=True)
        gate_ref[0, e] = g


def _topk(aff_n, tmat):
    out_spec = pl.BlockSpec((1, N_EXPERTS, CAP, 1), lambda s: (s, 0, 0, 0))
    return pl.pallas_call(
        _topk_kernel,
        grid=(2,),
        in_specs=[
            pl.BlockSpec((N_CTX, 128), lambda s: (s, 0)),
            pl.BlockSpec((N_CTX, 128), lambda s: (0, 0)),
        ],
        out_specs=[out_spec, out_spec],
        out_shape=[jax.ShapeDtypeStruct((2, N_EXPERTS, CAP, 1), jnp.int32),
                   jax.ShapeDtypeStruct((2, N_EXPERTS, CAP, 1), F32)],
        compiler_params=_params(1),
        name="topk",
    )(aff_n, tmat)


N_FT = FF // TF
GATHER_ROWS = 2 * CAP
ROWS_PER_STEP = GATHER_ROWS // N_FT


def _ffn_kernel(l, idx_ref, h2_hbm, g_ref, wg_ref, wu_ref, wd_ref, ye_ref, xg, xb, acc, sems):
    e = pl.program_id(0)
    f = pl.program_id(1)
    slot = e % 2

    def row_copy(expert, s, p, dst_slot):
        row = idx_ref[(s * N_EXPERTS + expert) * CAP + p] + s * N_CTX
        src = h2_hbm.at[pl.ds(pl.multiple_of(row * SLAB, SLAB), SLAB), :]
        dst = xg.at[dst_slot, pl.ds(pl.multiple_of((s * CAP + p) * SLAB, SLAB), SLAB), :]
        return pltpu.make_async_copy(src, dst, sems.at[dst_slot])

    def slot_wait(dst_slot):
        pltpu.make_async_copy(h2_hbm.at[pl.ds(0, GATHER_ROWS * SLAB), :], xg.at[dst_slot],
                              sems.at[dst_slot]).wait()

    @pl.when(jnp.logical_and(e == 0, f == 0))
    def _():
        def issue(r, carry):
            row_copy(0, r // CAP, r % CAP, 0).start()
            return carry
        lax.fori_loop(0, GATHER_ROWS, issue, 0)

    @pl.when(f == 0)
    def _():
        slot_wait(slot)
        xb[...] = _from_slabs(xg.at[slot]).astype(BF16)
        acc[...] = jnp.zeros_like(acc)

    nxt = jnp.minimum(e + 1, N_EXPERTS - 1)
    s_nxt = f // (N_FT // 2)
    p0 = (f % (N_FT // 2)) * ROWS_PER_STEP
    for u in range(ROWS_PER_STEP):
        row_copy(nxt, s_nxt, p0 + u, 1 - slot).start()

    x = xb[...]
    hg = _dot(x, wg_ref[...].astype(BF16))
    hu = _dot(x, wu_ref[...].astype(BF16))
    hdn = ((hg * _sigmoid(hg)) * hu).astype(BF16)
    acc[...] += _dot(hdn, wd_ref[...].astype(BF16))

    @pl.when(f == N_FT - 1)
    def _():
        for s in range(2):
            _to_slabs(ye_ref.at[s], acc[s * CAP:(s + 1) * CAP, :] * g_ref[s])

    @pl.when(jnp.logical_and(e == N_EXPERTS - 1, f == N_FT - 1))
    def _():
        slot_wait(1 - slot)


def _expert_ffn(l, idx_flat, h2s, gates, w_gate, w_up, w_down):
    grid_spec = pltpu.PrefetchScalarGridSpec(
        num_scalar_prefetch=1,
        grid=(N_EXPERTS, FF // TF),
        in_specs=[
            pl.BlockSpec(memory_space=pl.ANY),
            pl.BlockSpec((2, None, CAP, 1), lambda e, f, idx: (0, e, 0, 0)),
            pl.BlockSpec((None, None, D, TF), lambda e, f, idx: (l, e, 0, f)),
            pl.BlockSpec((None, None, D, TF), lambda e, f, idx: (l, e, 0, f)),
            pl.BlockSpec((None, None, TF, D), lambda e, f, idx: (l, e, f, 0)),
        ],
        out_specs=pl.BlockSpec((2, None, CAP * SLAB, 128), lambda e, f, idx: (0, e, 0, 0)),
        scratch_shapes=[
            pltpu.VMEM((2, GATHER_ROWS * SLAB, 128), F32),
            pltpu.VMEM((GATHER_ROWS, D), BF16),
            pltpu.VMEM((GATHER_ROWS, D), F32),
            pltpu.SemaphoreType.DMA((2,)),
        ],
    )
    return pl.pallas_call(
        functools.partial(_ffn_kernel, l),
        grid_spec=grid_spec,
        out_shape=jax.ShapeDtypeStruct((2, N_EXPERTS, CAP * SLAB, 128), F32),
        compiler_params=_params(2),
        name="expert_ffn",
    )(idx_flat, h2s, gates, w_gate, w_up, w_down)


SCATTER_UNROLL = 8


def _combine_kernel(idx_ref, ye_ref, y_ref):
    s = pl.program_id(0)
    e = pl.program_id(1)

    @pl.when(e == 0)
    def _():
        y_ref[...] = jnp.zeros_like(y_ref)

    base = (s * N_EXPERTS + e) * CAP

    def group(g, carry):
        p0 = g * SCATTER_UNROLL
        tile = lambda r: pl.ds(pl.multiple_of(r * SLAB, SLAB), SLAB)
        rows = [idx_ref[base + p0 + u] for u in range(SCATTER_UNROLL)]
        vals = [y_ref[tile(rows[u]), :] + ye_ref[tile(p0 + u), :] for u in range(SCATTER_UNROLL)]
        for u in range(SCATTER_UNROLL):
            y_ref[tile(rows[u]), :] = vals[u]
        return carry

    lax.fori_loop(0, CAP // SCATTER_UNROLL, group, 0)


def _combine(idx_flat, ye):
    grid_spec = pltpu.PrefetchScalarGridSpec(
        num_scalar_prefetch=1,
        grid=(2, N_EXPERTS),
        in_specs=[pl.BlockSpec((None, None, CAP * SLAB, 128), lambda s, e, idx: (s, e, 0, 0))],
        out_specs=pl.BlockSpec((N_CTX * SLAB, 128), lambda s, e, idx: (s, 0)),
    )
    return pl.pallas_call(
        _combine_kernel,
        grid_spec=grid_spec,
        out_shape=jax.ShapeDtypeStruct((N_TOK * SLAB, 128), F32),
        compiler_params=_params(2),
        name="combine",
    )(idx_flat, ye)


def _final_kernel(x_ref, y_ref, mod_ref, w_ref, op_ref, os_ref):
    x = x_ref[...] + mod_ref[0][:, 5 * D:6 * D] * _from_slabs(y_ref)
    out = _rms(x) * w_ref[...]
    is_ctx = pl.program_id(0) < N_CTX // FINAL_ROWS

    @pl.when(is_ctx)
    def _():
        op_ref[...] = out

    @pl.when(jnp.logical_not(is_ctx))
    def _():
        os_ref[...] = out


FINAL_ROWS = 2 * TM


def _final(x1, y, mod, final_norm_w):
    ft = FINAL_ROWS
    ctx_steps = N_CTX // ft
    tile = pl.BlockSpec((ft, D), lambda i: (i, 0))
    return pl.pallas_call(
        _final_kernel,
        grid=(N_TOK // ft,),
        in_specs=[tile, pl.BlockSpec((ft * SLAB, 128), lambda i: (i, 0)),
                  pl.BlockSpec((1, 1, N_MOD * D), lambda i: ((DEPTH - 1) * 8 + _mod_row(i * (ft // TM)), 0, 0)),
                  pl.BlockSpec((1, D), lambda i: (0, 0))],
        out_specs=[pl.BlockSpec((ft, D), lambda i: (jnp.minimum(i, ctx_steps - 1), 0)),
                   pl.BlockSpec((ft, D), lambda i: (jnp.maximum(i - ctx_steps, 0), 0))],
        out_shape=[jax.ShapeDtypeStruct((N_CTX, D), F32), jax.ShapeDtypeStruct((N_LAT, D), F32)],
        compiler_params=_params(1),
        name="final_norm",
    )(x1, y, mod, final_norm_w.reshape(1, D))


def _rope_tables():
    t = np.arange(T_LAT)
    row = (t // GRID_W).astype(np.float32)
    col = (t % GRID_W).astype(np.float32)
    inv = jnp.asarray(ROPE_BASE, F32) ** (-jnp.arange(N_ROPE_FREQ, dtype=F32) / N_ROPE_FREQ)
    ang_r = jnp.asarray(row)[:, None] * inv
    ang_c = jnp.asarray(col)[:, None] * inv
    def group(ang):
        return jnp.concatenate([ang, ang], axis=-1)
    ang = jnp.concatenate([group(ang_r), group(ang_c), group(ang_r), group(ang_c)], axis=-1)
    sign = np.where(np.arange(HD) % 32 < 16, -1.0, 1.0).astype(np.float32)
    return jnp.cos(ang), jnp.sin(ang) * sign


def _index_table():
    t = np.arange(N_CTX)
    tm = np.zeros((N_CTX, 128), np.float32)
    tm[:, IDX_LANE] = t // 64
    tm[:, IDX_LANE + 1] = t % 64
    return jnp.asarray(tm, BF16)


def kernel(x_prompt, x_sample, c, cache_attn_k, cache_attn_v, state_ret_fwd, state_ret_bwd, c_ctx, w_ada, b_ada, norm1_w, norm2_w, w_in, attn_lambda, attn_subln_w, conv_w, ret_decay_fwd, ret_decay_bwd, ret_norm_w, w_br_attn, w_br_conv, w_br_ret, w_out, w_router, w_exp_gate, w_exp_up, w_exp_down, final_norm_w):
    x = (x_prompt.reshape(N_CTX, D), x_sample.reshape(N_LAT, D))
    cvec = jnp.concatenate([c_ctx[None, :], c, jnp.zeros((3, D), F32)], axis=0)
    mod = _modulation(cvec, w_ada, b_ada).reshape(DEPTH * 8, 1, N_MOD * D)

    w_in_bf = w_in.astype(BF16)
    w_br_attn_bf = w_br_attn.astype(BF16)
    w_br_conv_bf = w_br_conv.astype(BF16)
    w_br_ret_bf = w_br_ret.astype(BF16)
    w_out_bf = w_out.astype(BF16)
    w_router_pad = jnp.pad(w_router, ((0, 0), (0, 0), (0, 128 - N_EXPERTS))).astype(BF16)
    norm1 = norm1_w.reshape(DEPTH, 1, D)
    norm2 = norm2_w.reshape(DEPTH, 1, D)
    subln = attn_subln_w.reshape(DEPTH, 1, HD)
    decay_f = ret_decay_fwd.reshape(DEPTH * HEADS, 1, 1)
    decay_b = ret_decay_bwd.reshape(DEPTH * HEADS, 1, 1)
    ret_nw = ret_norm_w.reshape(DEPTH * HEADS, 1, HD)
    cos, sin = _rope_tables()
    tmat = _index_table()

    y = None
    new_k = new_v = new_sf = new_sb = None
    for l in range(DEPTH):
        lam_init = 0.8 - 0.6 * math.exp(-0.3 * l)
        if y is None:
            zb, zq, new_k, new_v = _inproj(l, x, None, mod, norm1, w_in_bf, new_k, new_v)
        else:
            zb, zq, new_k, new_v, x = _inproj(l, x, y, mod, norm1, w_in_bf, new_k, new_v)
        ao, ao_lat = _attention(l, lam_init, zb, zq, cache_attn_k, cache_attn_v, cos, sin, attn_lambda, subln)
        ro, ro_lat, new_sf, new_sb = _retention(l, zb, state_ret_fwd, state_ret_bwd, decay_f, decay_b, ret_nw,
                                                new_sf, new_sb)
        x, h2s, aff_n = _merge(l, x, ao, ao_lat, ro, ro_lat, zb, mod, conv_w, w_br_attn_bf, w_br_conv_bf,
                               w_br_ret_bf, w_out_bf, norm2, w_router_pad)
        idx, gates = _topk(aff_n, tmat)
        idx_flat = idx.reshape(2 * N_EXPERTS * CAP)
        ye = _expert_ffn(l, idx_flat, h2s, gates, w_exp_gate, w_exp_up, w_exp_down)
        y = _combine(idx_flat, ye)

    y_prompt, y_sample = _final(x, y, mod, final_norm_w)
    return (y_prompt.reshape(N_CTX_SEQ, T_CTX, D), y_sample.reshape(N_LAT_SEQ, T_LAT, D), new_k, new_v,
            new_sf, new_sb)
```

```python
import functools
import math

import jax
import jax.numpy as jnp
import numpy as np
from jax import lax
from jax.experimental import pallas as pl
from jax.experimental.pallas import tpu as pltpu

F32 = jnp.float32
BF16 = jnp.bfloat16

D = 1024
DEPTH = 2
N_CTX_SEQ = 16
T_CTX = 256
N_LAT_SEQ = 4
T_LAT = 1024
PAST = 256
N_CTX = N_CTX_SEQ * T_CTX
N_LAT = N_LAT_SEQ * T_LAT
N_TOK = N_CTX + N_LAT
TM = 256
N_TILES = N_TOK // TM
CTX_TILES = N_CTX // TM
LAT_TILES_PER_SEQ = T_LAT // TM
HEADS = 4
HD = 128
GRID_W = 64
N_ROPE_FREQ = 16
ROPE_BASE = 10000.0
IN_COLS = 8192
N_MOD = 6
N_EXPERTS = 16
CAP = 512
FF = 2048
TF = 512
EPS = 1e-6
RET_SCALE = HD ** -0.5
ATTN_SCALE = 64 ** -0.5
SLAB = D // 128
VMEM_LIMIT = 56 * 1024 * 1024

QB, KB, VB = 0, 4, 8
RQB, RKB, RVB, RGB = 24, 28, 32, 36
CONV_B_COL, CONV_U_COL = 3, 4
MERGE_GATE_COL = 5
MERGE_SUB = 2


def _sigmoid(x):
    return 0.5 * jnp.tanh(0.5 * x) + 0.5


def _log_sigmoid(x):
    return jnp.minimum(x, 0.0) - jnp.log(1.0 + jnp.exp(-jnp.abs(x)))


def _rms(x):
    return x * lax.rsqrt(jnp.mean(x * x, axis=-1, keepdims=True) + EPS)


def _dot(a, b):
    return jnp.dot(a, b, preferred_element_type=F32)


def _dot_nt(a, b):
    return lax.dot_general(a, b, (((1,), (1,)), ((), ())), preferred_element_type=F32)


def _dot_tn(a, b):
    return lax.dot_general(a, b, (((0,), (0,)), ((), ())), preferred_element_type=F32)


def _to_slabs(ref, x):
    n = x.shape[0]
    for s in range(SLAB):
        ref[pl.ds(s, n, stride=SLAB), :] = x[:, s * 128:(s + 1) * 128]


def _from_slabs(ref):
    n = ref.shape[0] // SLAB
    return jnp.concatenate([ref[pl.ds(s, n, stride=SLAB), :] for s in range(SLAB)], axis=-1)


def _mod_row(i):
    return jnp.where(i < CTX_TILES, 0, 1 + (i - CTX_TILES) // LAT_TILES_PER_SEQ)


def _params(n_axes):
    return pltpu.CompilerParams(
        dimension_semantics=("arbitrary",) * n_axes, vmem_limit_bytes=VMEM_LIMIT)


def _mod_kernel(c_ref, w_ref, b_ref, o_ref):
    c = c_ref[...]
    s = (c * _sigmoid(c)).astype(BF16)
    o_ref[...] = _dot(s, w_ref[...].astype(BF16)) + b_ref[...]


def _modulation(cvec, w_ada, b_ada):
    tn = 1024
    return pl.pallas_call(
        _mod_kernel,
        grid=(DEPTH, N_MOD * D // tn),
        in_specs=[
            pl.BlockSpec((8, D), lambda l, n: (0, 0)),
            pl.BlockSpec((None, D, tn), lambda l, n: (l, 0, n)),
            pl.BlockSpec((None, 1, tn), lambda l, n: (l, 0, n)),
        ],
        out_specs=pl.BlockSpec((None, 8, tn), lambda l, n: (l, 0, n)),
        out_shape=jax.ShapeDtypeStruct((DEPTH, 8, N_MOD * D), F32),
        compiler_params=_params(2),
        name="modulation",
    )(cvec, w_ada, b_ada.reshape(DEPTH, 1, N_MOD * D))


def _inproj_kernel(has_y, x_pair, n_prev, *refs):
    refs = list(refs)
    x_ref = refs.pop(0)
    if x_pair:
        xs_ref = refs.pop(0)
    if has_y:
        y_ref, modp_ref = refs.pop(0), refs.pop(0)
    mod_ref, n1_ref, w_ref = refs.pop(0), refs.pop(0), refs.pop(0)
    if n_prev:
        kp_ref, vp_ref = refs.pop(0), refs.pop(0)
    zb_ref, zq_ref, kn_ref, vn_ref = refs[0:4]
    i = pl.program_id(0)

    def normalize():
        x = x_ref[...]
        if x_pair:
            x = jnp.where(i < N_TILES - CTX_TILES, xs_ref[...], x)
        if has_y:
            x = x + modp_ref[0][:, 5 * D:6 * D] * _from_slabs(y_ref)
            refs[4][...] = x
        m = mod_ref[0]
        shift1 = m[:, 0:D]
        scale1 = m[:, D:2 * D]
        return ((_rms(x) * n1_ref[...]) * (1.0 + scale1) + shift1).astype(BF16)

    def project(h):
        cw = 1024
        for c in range(IN_COLS // cw):
            z = _dot(h, w_ref[:, c * cw:(c + 1) * cw])
            if c == CONV_U_COL // 2:
                z = jnp.concatenate([z[:, 0:512] * z[:, 512:cw], z[:, 512:cw]], axis=-1)
            elif c == RGB // 8:
                g = z[:, 512:cw]
                z = jnp.concatenate([z[:, 0:512], g * _sigmoid(g)], axis=-1)
            elif c >= MERGE_GATE_COL:
                z = _sigmoid(z)
            zb_ref[:, c * cw:(c + 1) * cw] = z.astype(BF16)
            if c == 0:
                zq_ref[...] = z
                for hh in range(HEADS):
                    kn_ref[n_prev, hh] = z[:, 512 + hh * HD:512 + (hh + 1) * HD]
            if c == 1:
                for hh in range(HEADS):
                    vn_ref[n_prev, hh] = z[:, hh * HD:(hh + 1) * HD]
        if n_prev:
            kn_ref[0:n_prev] = kp_ref[...]
            vn_ref[0:n_prev] = vp_ref[...]

    project(normalize())


def _inproj(l, x, y, mod, norm1_w, w_in_bf, k_prev, v_prev):
    has_y = y is not None
    t = lambda i: (i + CTX_TILES) % N_TILES
    n_lat = N_TILES - CTX_TILES
    norm_tile = pl.BlockSpec((TM, D), lambda i: (t(i), 0))
    ctx_i = lambda i: jnp.maximum(i - n_lat, 0)
    kv_spec = lambda m: pl.BlockSpec((None, m, HEADS, T_CTX, HD), lambda i: (ctx_i(i), 0, 0, 0, 0))
    kv_shape = jax.ShapeDtypeStruct((N_CTX_SEQ, l + 1, HEADS, T_CTX, HD), F32)
    x_pair = isinstance(x, tuple)
    if x_pair:
        in_specs = [pl.BlockSpec((TM, D), lambda i: (jnp.maximum(i - n_lat, 0), 0)),
                    pl.BlockSpec((TM, D), lambda i: (jnp.minimum(i, n_lat - 1), 0))]
        args = list(x)
    else:
        in_specs = [norm_tile]
        args = [x]
    if has_y:
        in_specs += [pl.BlockSpec((TM * SLAB, 128), lambda i: (t(i), 0)),
                     pl.BlockSpec((1, 1, N_MOD * D), lambda i: ((l - 1) * 8 + _mod_row(t(i)), 0, 0))]
        args += [y, mod]
    in_specs += [
        pl.BlockSpec((1, 1, N_MOD * D), lambda i: (l * 8 + _mod_row(t(i)), 0, 0)),
        pl.BlockSpec((None, 1, D), lambda i: (l, 0, 0)),
        pl.BlockSpec((D, IN_COLS), lambda i: (0, 0), pipeline_mode=pl.Buffered(1)),
    ]
    args += [mod, norm1_w, w_in_bf]
    if l:
        in_specs += [kv_spec(l), kv_spec(l)]
        args += [k_prev, v_prev]
    out_specs = [pl.BlockSpec((TM, IN_COLS), lambda i: (t(i), 0)),
                 pl.BlockSpec((TM, 1024), lambda i: (t(i), 0)),
                 kv_spec(l + 1), kv_spec(l + 1)]
    out_shape = [jax.ShapeDtypeStruct((N_TOK, IN_COLS), BF16),
                 jax.ShapeDtypeStruct((N_TOK, 1024), F32),
                 kv_shape, kv_shape]
    if has_y:
        out_specs.append(norm_tile)
        out_shape.append(jax.ShapeDtypeStruct((N_TOK, D), F32))
    return pl.pallas_call(
        functools.partial(_inproj_kernel, has_y, x_pair, l),
        grid=(N_TILES,),
        in_specs=in_specs,
        out_specs=out_specs,
        out_shape=out_shape,
        compiler_params=_params(1),
        name="inproj",
    )(*args)


def _lambda(lam_ref, lam_init):
    lv = lam_ref[...]
    a = jnp.sum(lv[0:1] * lv[1:2], axis=-1, keepdims=True)
    b = jnp.sum(lv[2:3] * lv[3:4], axis=-1, keepdims=True)
    return jnp.exp(a) - jnp.exp(b) + lam_init


def _diff_attention(lam_init, qs, keys, vs, lam, sw, keys_transposed):
    lane = lax.broadcasted_iota(jnp.int32, qs[0].shape, 1)
    score = _dot if keys_transposed else _dot_nt
    maps = [(h, jnp.where(keep, q, jnp.zeros_like(q)))
            for h, q in enumerate(qs) for keep in (lane < 64, lane >= 64)]
    s = [score(qm, keys[h]) for h, qm in maps]
    e = [jnp.exp(x - jnp.max(x, axis=-1, keepdims=True)) for x in s]
    r = [1.0 / jnp.sum(x, axis=-1, keepdims=True) for x in e]
    a = [(e[2 * h] * r[2 * h] - e[2 * h + 1] * (lam * r[2 * h + 1])).astype(BF16) for h in range(len(qs))]
    o = [_dot(a[h], vs[h]) for h in range(len(qs))]
    return [(_rms(x) * sw) * (1.0 - lam_init) for x in o]


def _scaled_q(q):
    return (q.astype(F32) * ATTN_SCALE).astype(BF16)


def _attn_ctx_kernel(lam_init, q_ref, k_ref, v_ref, lam_ref, sw_ref, o_ref):
    lam = _lambda(lam_ref, lam_init)
    sls = [slice(h * HD, (h + 1) * HD) for h in range(HEADS)]
    outs = _diff_attention(lam_init, [_scaled_q(q_ref[:, sl]) for sl in sls], [k_ref[:, sl] for sl in sls],
                           [v_ref[:, sl] for sl in sls], lam, sw_ref[...], False)
    for sl, o in zip(sls, outs):
        o_ref[:, sl] = o.astype(BF16)


def _rope(x, cos, sin_signed):
    lane = lax.broadcasted_iota(jnp.int32, x.shape, 1)
    partner = jnp.where(lane % 32 < 16, pltpu.roll(x, 112, 1), pltpu.roll(x, 16, 1))
    return x * cos + partner * sin_signed


def _attn_lat_kernel(lam_init, q_ref, k_ref, v_ref, ck_ref, cv_ref, cos_ref, sin_ref,
                     cosq_ref, sinq_ref, lam_ref, sw_ref, o_ref, k_t, vall):
    @pl.when(pl.program_id(1) == 0)
    def _():
        for h in range(HEADS):
            sl = slice(h * HD, (h + 1) * HD)
            k_t[h, :, 0:PAST] = ck_ref[h].T.astype(BF16)
            k_t[h, :, PAST:] = _rope(k_ref[:, sl], cos_ref[...], sin_ref[...]).T.astype(BF16)
            vall[h, 0:PAST, :] = cv_ref[h].astype(BF16)
            vall[h, PAST:, :] = v_ref[:, sl]

    lam = _lambda(lam_ref, lam_init)
    sls = [slice(h * HD, (h + 1) * HD) for h in range(HEADS)]
    qs = [_scaled_q(_rope(q_ref[:, sl], cosq_ref[...], sinq_ref[...])) for sl in sls]
    outs = _diff_attention(lam_init, qs, [k_t[h] for h in range(HEADS)], [vall[h] for h in range(HEADS)],
                           lam, sw_ref[...], True)
    for sl, o in zip(sls, outs):
        o_ref[:, sl] = o.astype(BF16)


def _attention(l, lam_init, zb, zq, cache_k, cache_v, cos, sin, attn_lambda, subln_w):
    width = HEADS * HD
    ao = pl.pallas_call(
        functools.partial(_attn_ctx_kernel, lam_init),
        grid=(N_CTX_SEQ,),
        in_specs=[
            pl.BlockSpec((T_CTX, width), lambda b: (b, QB // HEADS)),
            pl.BlockSpec((T_CTX, width), lambda b: (b, KB // HEADS)),
            pl.BlockSpec((T_CTX, width), lambda b: (b, VB // HEADS)),
            pl.BlockSpec((None, 4, 64), lambda b: (l, 0, 0)),
            pl.BlockSpec((None, 1, HD), lambda b: (l, 0, 0)),
        ],
        out_specs=pl.BlockSpec((T_CTX, width), lambda b: (b, 0)),
        out_shape=jax.ShapeDtypeStruct((N_CTX, width), BF16),
        compiler_params=_params(1),
        name="attn_ctx",
    )(zb, zb, zb, attn_lambda, subln_w)

    lat_row = lambda b, j: CTX_TILES + LAT_TILES_PER_SEQ * b + j
    seq_row = lambda b: N_CTX // T_LAT + b
    ao_lat = pl.pallas_call(
        functools.partial(_attn_lat_kernel, lam_init),
        grid=(N_LAT_SEQ, LAT_TILES_PER_SEQ),
        in_specs=[
            pl.BlockSpec((TM, width), lambda b, j: (lat_row(b, j), QB // HEADS)),
            pl.BlockSpec((T_LAT, width), lambda b, j: (seq_row(b), KB // HEADS)),
            pl.BlockSpec((T_LAT, width), lambda b, j: (seq_row(b), VB // HEADS)),
            pl.BlockSpec((None, None, HEADS, PAST, HD), lambda b, j: (b, l, 0, 0, 0)),
            pl.BlockSpec((None, None, HEADS, PAST, HD), lambda b, j: (b, l, 0, 0, 0)),
            pl.BlockSpec((T_LAT, HD), lambda b, j: (0, 0)),
            pl.BlockSpec((T_LAT, HD), lambda b, j: (0, 0)),
            pl.BlockSpec((TM, HD), lambda b, j: (j, 0)),
            pl.BlockSpec((TM, HD), lambda b, j: (j, 0)),
            pl.BlockSpec((None, 4, 64), lambda b, j: (l, 0, 0)),
            pl.BlockSpec((None, 1, HD), lambda b, j: (l, 0, 0)),
        ],
        out_specs=pl.BlockSpec((TM, width), lambda b, j: (LAT_TILES_PER_SEQ * b + j, 0)),
        out_shape=jax.ShapeDtypeStruct((N_LAT, width), BF16),
        scratch_shapes=[pltpu.VMEM((HEADS, HD, PAST + T_LAT), BF16),
                        pltpu.VMEM((HEADS, PAST + T_LAT, HD), BF16)],
        compiler_params=_params(2),
        name="attn_lat",
    )(zq, zq, zb, cache_k, cache_v, cos, sin, cos, sin, attn_lambda, subln_w)
    return ao, ao_lat


def _decay_matrix(lgf, lgb, row0, tq, tk):
    i = row0 + lax.broadcasted_iota(jnp.int32, (tq, tk), 0)
    j = lax.broadcasted_iota(jnp.int32, (tq, tk), 1)
    d = (i - j).astype(F32)
    fwd = jnp.where(d >= 0.0, jnp.exp(lgf * jnp.maximum(d, 0.0)), 0.0)
    bwd = jnp.where(d <= 0.0, jnp.exp(lgb * jnp.maximum(-d, 0.0)), 0.0)
    return fwd + bwd


def _ret_finish(o, g, nw):
    return (g.astype(F32) * (_rms(o) * nw)).astype(BF16)


def _ret_ctx_kernel(n_prev, *refs):
    refs = list(refs)
    q_ref, k_ref, v_ref, g_ref, df_ref, db_ref, nw_ref = refs[0:7]
    refs = refs[7:]
    if n_prev:
        sfp_ref, sbp_ref = refs.pop(0), refs.pop(0)
    o_ref, sf_ref, sb_ref, dmat = refs
    @pl.when(pl.program_id(0) == 0)
    def _():
        for h in range(HEADS):
            dmat[h] = RET_SCALE * _decay_matrix(_log_sigmoid(df_ref[h]), _log_sigmoid(db_ref[h]), 0, T_CTX, T_CTX)

    if n_prev:
        sf_ref[0:n_prev] = sfp_ref[...]
        sb_ref[0:n_prev] = sbp_ref[...]
    j = lax.broadcasted_iota(jnp.int32, (T_CTX, 1), 0).astype(F32)
    heads = range(HEADS)
    sls = [slice(h * HD, (h + 1) * HD) for h in heads]
    q = [q_ref[:, sl] for sl in sls]
    k = [k_ref[:, sl] for sl in sls]
    v = [v_ref[:, sl] for sl in sls]
    s = [(_dot_nt(q[h], k[h]) * dmat[h]).astype(BF16) for h in heads]
    o = [_dot(s[h], v[h]) for h in heads]
    lgf = [_log_sigmoid(df_ref[h]) for h in heads]
    lgb = [_log_sigmoid(db_ref[h]) for h in heads]
    kf = [k[h].astype(F32) * RET_SCALE for h in heads]
    kfw = [(kf[h] * jnp.exp(lgf[h] * (T_CTX - 1.0 - j))).astype(BF16) for h in heads]
    kbw = [(kf[h] * jnp.exp(lgb[h] * j)).astype(BF16) for h in heads]
    for h in heads:
        sf_ref[n_prev, h] = _dot_tn(kfw[h], v[h])
        sb_ref[n_prev, h] = _dot_tn(kbw[h], v[h])
    for h in heads:
        o_ref[:, sls[h]] = _ret_finish(o[h], g_ref[:, sls[h]], nw_ref[h])


N_DCHUNK = 2 * LAT_TILES_PER_SEQ - 1


def _ret_lat_kernel(q_ref, k_ref, v_ref, g_ref, s0f_ref, s0b_ref, df_ref, db_ref, nw_ref, o_ref, strip, k_t):
    b = pl.program_id(0)
    j = pl.program_id(1)

    @pl.when(jnp.logical_and(b == 0, j == 0))
    def _():
        for h in range(HEADS):
            lgf = _log_sigmoid(df_ref[h])
            lgb = _log_sigmoid(db_ref[h])
            for c in range(N_DCHUNK):
                strip[h, c] = RET_SCALE * _decay_matrix(lgf, lgb, T_LAT - TM - c * TM, TM, TM)

    @pl.when(j == 0)
    def _():
        for h in range(HEADS):
            k_t[h] = k_ref[:, h * HD:(h + 1) * HD].astype(F32).T.astype(BF16)

    i = (j * TM + lax.broadcasted_iota(jnp.int32, (TM, 1), 0)).astype(F32)
    c0 = LAT_TILES_PER_SEQ - 1 - j
    heads = range(HEADS)
    sls = [slice(h * HD, (h + 1) * HD) for h in heads]
    q = [q_ref[:, sl] for sl in sls]
    dmat = [jnp.concatenate([strip[h, c0 + c] for c in range(LAT_TILES_PER_SEQ)], axis=-1) for h in heads]
    s = [(_dot(q[h], k_t[h]) * dmat[h]).astype(BF16) for h in heads]
    o = [_dot(s[h], v_ref[:, sls[h]]) for h in heads]
    of = [_dot(q[h], s0f_ref[h].astype(BF16)) for h in heads]
    ob = [_dot(q[h], s0b_ref[h].astype(BF16)) for h in heads]
    for h in heads:
        tot = o[h] + jnp.exp(_log_sigmoid(df_ref[h]) * (i + 1.0)) * of[h]
        tot = tot + jnp.exp(_log_sigmoid(db_ref[h]) * (T_LAT - i)) * ob[h]
        o_ref[:, sls[h]] = _ret_finish(tot, g_ref[:, sls[h]], nw_ref[h])


def _retention(l, zb, state_f, state_b, decay_f, decay_b, ret_norm_w, sf_prev, sb_prev):
    width = HEADS * HD
    dec1 = pl.BlockSpec((HEADS, 1, 1), lambda b: (l, 0, 0))
    nw1 = pl.BlockSpec((HEADS, 1, HD), lambda b: (l, 0, 0))
    st_spec = lambda n: pl.BlockSpec((None, n, HEADS, HD, HD), lambda b: (b, 0, 0, 0, 0))
    st_shape = jax.ShapeDtypeStruct((N_CTX_SEQ, l + 1, HEADS, HD, HD), F32)
    in_specs = [
        pl.BlockSpec((T_CTX, width), lambda b: (b, RQB // HEADS)),
        pl.BlockSpec((T_CTX, width), lambda b: (b, RKB // HEADS)),
        pl.BlockSpec((T_CTX, width), lambda b: (b, RVB // HEADS)),
        pl.BlockSpec((T_CTX, width), lambda b: (b, RGB // HEADS)),
        dec1, dec1, nw1,
    ]
    args = [zb, zb, zb, zb, decay_f, decay_b, ret_norm_w]
    if l:
        in_specs += [st_spec(l), st_spec(l)]
        args += [sf_prev, sb_prev]
    ro, sf, sb = pl.pallas_call(
        functools.partial(_ret_ctx_kernel, l),
        grid=(N_CTX_SEQ,),
        in_specs=in_specs,
        out_specs=[pl.BlockSpec((T_CTX, width), lambda b: (b, 0)), st_spec(l + 1), st_spec(l + 1)],
        out_shape=[jax.ShapeDtypeStruct((N_CTX, width), BF16), st_shape, st_shape],
        scratch_shapes=[pltpu.VMEM((HEADS, T_CTX, T_CTX), F32)],
        compiler_params=_params(1),
        name="ret_ctx",
    )(*args)

    lat_row = lambda b, j: CTX_TILES + LAT_TILES_PER_SEQ * b + j
    seq_row = lambda b: N_CTX // T_LAT + b
    dec2 = pl.BlockSpec((HEADS, 1, 1), lambda b, j: (l, 0, 0))
    nw2 = pl.BlockSpec((HEADS, 1, HD), lambda b, j: (l, 0, 0))
    s0_spec = pl.BlockSpec((None, None, HEADS, HD, HD), lambda b, j: (b, l, 0, 0, 0))
    ro_lat = pl.pallas_call(
        _ret_lat_kernel,
        grid=(N_LAT_SEQ, LAT_TILES_PER_SEQ),
        in_specs=[
            pl.BlockSpec((TM, width), lambda b, j: (lat_row(b, j), RQB // HEADS)),
            pl.BlockSpec((T_LAT, width), lambda b, j: (seq_row(b), RKB // HEADS)),
            pl.BlockSpec((T_LAT, width), lambda b, j: (seq_row(b), RVB // HEADS)),
            pl.BlockSpec((TM, width), lambda b, j: (lat_row(b, j), RGB // HEADS)),
            s0_spec, s0_spec, dec2, dec2, nw2,
        ],
        out_specs=pl.BlockSpec((TM, width), lambda b, j: (LAT_TILES_PER_SEQ * b + j, 0)),
        out_shape=jax.ShapeDtypeStruct((N_LAT, width), BF16),
        scratch_shapes=[pltpu.VMEM((HEADS, N_DCHUNK, TM, TM), F32), pltpu.VMEM((HEADS, HD, T_LAT), BF16)],
        compiler_params=_params(2),
        name="ret_lat",
    )(zb, zb, zb, zb, state_f, state_b, decay_f, decay_b, ret_norm_w)
    return ro, ro_lat, sf, sb


def _merge_kernel(x_pair, *refs):
    refs = list(refs)
    x_ref = refs.pop(0)
    xs_ref = refs.pop(0) if x_pair else None
    (aoc_ref, aol_ref, roc_ref, rol_ref, cb_ref, u_ref, up_ref, un_ref,
     mg0_ref, mg1_ref, mg2_ref, mod_ref, cw_ref, wa_ref, wc_ref, wr_ref, wo_ref, n2_ref, wrt_ref,
     x1_ref, h2_ref, aff_ref) = refs
    i = pl.program_id(0)
    is_ctx = i < CTX_TILES // MERGE_SUB
    m = mod_ref[0]
    gate1 = m[:, 2 * D:3 * D]
    shift2 = m[:, 3 * D:4 * D]
    scale2 = m[:, 4 * D:5 * D]
    cw = cw_ref[...]
    u_all = u_ref[...].astype(F32)
    r = lax.broadcasted_iota(jnp.int32, (TM, 512), 0)
    lane = lax.broadcasted_iota(jnp.int32, (TM, 128), 1)

    subs = range(MERGE_SUB)
    rows = [slice(sub * TM, (sub + 1) * TM) for sub in subs]

    def conv_out(sub):
        j = (i * MERGE_SUB + sub - CTX_TILES) % LAT_TILES_PER_SEQ
        seq_first = jnp.logical_or(is_ctx, j == 0)
        seq_last = jnp.logical_or(is_ctx, j == LAT_TILES_PER_SEQ - 1)
        u = u_all[rows[sub], :]
        up = up_ref[...].astype(F32)[15:16, :] if sub == 0 else u_all[sub * TM - 1:sub * TM, :]
        dn = un_ref[...].astype(F32)[0:1, :] if sub == MERGE_SUB - 1 else u_all[(sub + 1) * TM:(sub + 1) * TM + 1, :]
        up = up * jnp.where(seq_first, 0.0, 1.0)
        dn = dn * jnp.where(seq_last, 0.0, 1.0)
        u_prev = jnp.where(r == 0, up, pltpu.roll(u, 1, 0))
        u_next = jnp.where(r == TM - 1, dn, pltpu.roll(u, TM - 1, 0))
        conv = u_prev * cw[0:1, :] + u * cw[1:2, :] + u_next * cw[2:3, :]
        return (cb_ref[rows[sub], :].astype(F32) * conv).astype(BF16)

    ao = [jnp.where(is_ctx, aoc_ref[rw, :], aol_ref[rw, :]) for rw in rows]
    ro = [jnp.where(is_ctx, roc_ref[rw, :], rol_ref[rw, :]) for rw in rows]
    conv_o = [conv_out(sub) for sub in subs]
    b_attn = [_dot(ao[s], wa_ref[...]) for s in subs]
    b_conv = [_dot(conv_o[s], wc_ref[...]) for s in subs]
    b_ret = [_dot(ro[s], wr_ref[...]) for s in subs]
    merged = [(mg0_ref[rows[s], :].astype(F32) * b_attn[s] + mg1_ref[rows[s], :].astype(F32) * b_conv[s]
               + mg2_ref[rows[s], :].astype(F32) * b_ret[s]).astype(BF16) for s in subs]
    proj = [_dot(merged[s], wo_ref[...]) for s in subs]
    x1 = []
    for s in subs:
        x_in = x_ref[rows[s], :]
        if x_pair:
            x_in = jnp.where(is_ctx, x_in, xs_ref[rows[s], :])
        x1.append(x_in + gate1 * proj[s])
        x1_ref[rows[s], :] = x1[s]
    h2 = [(_rms(x1[s]) * n2_ref[...]) * (1.0 + scale2) + shift2 for s in subs]
    logits = [_dot(h2[s].astype(BF16), wrt_ref[...]) for s in subs]
    valid = lane < N_EXPERTS
    for s in subs:
        _to_slabs(h2_ref.at[pl.ds(s * TM * SLAB, TM * SLAB), :], h2[s])
        lmax = jnp.max(jnp.where(valid, logits[s], -jnp.inf), axis=-1, keepdims=True)
        e = jnp.where(valid, jnp.exp(logits[s] - lmax), 0.0)
        aff_ref[rows[s], :] = e * (1.0 / jnp.sum(e, axis=-1, keepdims=True))


def _merge(l, x, ao, ao_lat, ro, ro_lat, zb, mod, conv_w, w_br_attn, w_br_conv, w_br_ret, w_out, norm2_w,
           w_router_pad):
    mt = MERGE_SUB * TM
    n_steps = N_TOK // mt
    ctx_steps = N_CTX // mt
    n16 = N_TOK // 16
    ctx_br = pl.BlockSpec((mt, 512), lambda i: (jnp.minimum(i, ctx_steps - 1), 0))
    lat_br = pl.BlockSpec((mt, 512), lambda i: (jnp.maximum(i - ctx_steps, 0), 0))
    col = lambda c: pl.BlockSpec((mt, 512), lambda i: (i, c))
    halo_p = lambda c: pl.BlockSpec((16, 512), lambda i: (jnp.maximum(i * (mt // 16) - 1, 0), c))
    halo_n = lambda c: pl.BlockSpec((16, 512), lambda i: (jnp.minimum((i + 1) * (mt // 16), n16 - 1), c))
    mgs = lambda c: pl.BlockSpec((mt, D), lambda i: (i, c))
    wbr = pl.BlockSpec((None, 512, D), lambda i: (l, 0, 0))
    x_pair = isinstance(x, tuple)
    if x_pair:
        x_specs = [pl.BlockSpec((mt, D), lambda i: (jnp.minimum(i, ctx_steps - 1), 0)),
                   pl.BlockSpec((mt, D), lambda i: (jnp.maximum(i - ctx_steps, 0), 0))]
        x_args = list(x)
    else:
        x_specs = [pl.BlockSpec((mt, D), lambda i: (i, 0))]
        x_args = [x]
    return pl.pallas_call(
        functools.partial(_merge_kernel, x_pair),
        grid=(n_steps,),
        in_specs=x_specs + [
            ctx_br, lat_br, ctx_br, lat_br,
            col(CONV_B_COL), col(CONV_U_COL), halo_p(CONV_U_COL), halo_n(CONV_U_COL),
            mgs(MERGE_GATE_COL), mgs(MERGE_GATE_COL + 1), mgs(MERGE_GATE_COL + 2),
            pl.BlockSpec((1, 1, N_MOD * D), lambda i: (l * 8 + _mod_row(i * MERGE_SUB), 0, 0)),
            pl.BlockSpec((None, 3, 512), lambda i: (l, 0, 0)),
            wbr, wbr, wbr,
            pl.BlockSpec((None, D, D), lambda i: (l, 0, 0)),
            pl.BlockSpec((None, 1, D), lambda i: (l, 0, 0)),
            pl.BlockSpec((None, D, 128), lambda i: (l, 0, 0)),
        ],
        out_specs=[
            pl.BlockSpec((mt, D), lambda i: (i, 0)),
            pl.BlockSpec((mt * SLAB, 128), lambda i: (i, 0)),
            pl.BlockSpec((mt, 128), lambda i: (i, 0)),
        ],
        out_shape=[
            jax.ShapeDtypeStruct((N_TOK, D), F32),
            jax.ShapeDtypeStruct((N_TOK * SLAB, 128), F32),
            jax.ShapeDtypeStruct((N_TOK, 128), F32),
        ],
        compiler_params=_params(1),
        name="merge",
    )(*x_args, ao, ao_lat, ro, ro_lat, zb, zb, zb, zb, zb, zb, zb, mod, conv_w,
      w_br_attn, w_br_conv, w_br_ret, w_out, norm2_w, w_router_pad)


def _cumsum_lanes(x, tri):
    run = jnp.zeros((x.shape[0], 1), F32)
    outs = []
    for b in range(x.shape[1] // 128):
        cs = _dot(x[:, b * 128:(b + 1) * 128].astype(BF16), tri) + run
        run = cs[:, 127:128]
        outs.append(cs)
    return jnp.concatenate(outs, axis=-1)


GATE_LANE = (0, 16, 32)
IDX_LANE = 48
BISECT_GROUP = 4


def _topk_kernel(an_ref, tmat_ref, idx_ref, gate_ref):
    an = an_ref[...]
    a = an.T[0:N_EXPERTS, :]
    n = a.shape[1]
    kf = float(CAP)

    hi = an.astype(BF16).astype(F32)
    mid = (an - hi).astype(BF16).astype(F32)
    lo = ((an - hi) - mid).astype(BF16).astype(F32)
    table = (tmat_ref[...].astype(F32) + hi + pltpu.roll(mid, GATE_LANE[1], 1)
             + pltpu.roll(lo, GATE_LANE[2], 1)).astype(BF16)

    def count_gt(thr):
        return jnp.sum(jnp.where(a > thr, 1.0, 0.0), axis=-1, keepdims=True)

    def span(lo, hi):
        inside = jnp.logical_and(a > lo, a <= hi)
        cmax = jnp.max(jnp.where(inside, a, -jnp.inf), axis=-1, keepdims=True)
        cmin = jnp.min(jnp.where(inside, a, jnp.inf), axis=-1, keepdims=True)
        return cmax, cmin

    def cond(c):
        return jnp.logical_and(c[2] > 0, c[3] < 400)

    def body(c):
        lo, hi, _, it = c
        for _ in range(BISECT_GROUP):
            mid = 0.5 * (lo + hi)
            ge = count_gt(mid) >= kf
            lo = jnp.where(ge, mid, lo)
            hi = jnp.where(ge, hi, mid)
        cmax, cmin = span(lo, hi)
        open_rows = jnp.max(jnp.where(cmax != cmin, 1, 0))
        return lo, hi, open_rows, it + 1

    lo0 = jnp.full((N_EXPERTS, 1), -1.0, F32)
    hi0 = jnp.max(a, axis=-1, keepdims=True)
    cmax0, cmin0 = span(lo0, hi0)
    lo, hi, _, _ = lax.while_loop(
        cond, body, (lo0, hi0, jnp.max(jnp.where(cmax0 != cmin0, 1, 0)), jnp.int32(0)))
    thr, _ = span(lo, hi)

    r = lax.broadcasted_iota(jnp.int32, (128, 128), 0)
    c = lax.broadcasted_iota(jnp.int32, (128, 128), 1)
    tri = jnp.where(r <= c, 1.0, 0.0).astype(BF16)
    gt = a > thr
    eq = jnp.where(a == thr, 1.0, 0.0)
    need = kf - count_gt(thr)
    eq_before = _cumsum_lanes(eq, tri) - eq
    sel = jnp.where(jnp.logical_or(gt, jnp.logical_and(eq > 0.0, eq_before < need)), 1.0, 0.0)
    pos = _cumsum_lanes(sel, tri) - 1.0
    slot = jnp.where(sel > 0.0, pos, -1.0).astype(jnp.int32)

    p_iota = lax.broadcasted_iota(jnp.int32, (CAP, 1024), 0)
    lane = lax.broadcasted_iota(jnp.int32, (CAP, 128), 1)
    for e in range(N_EXPERTS):
        acc = jnp.zeros((CAP, 128), F32)
        for cb in range(n // 1024):
            onehot = jnp.where(p_iota == slot[e:e + 1, cb * 1024:(cb + 1) * 1024], 1.0, 0.0).astype(BF16)
            acc = acc + _dot(onehot, table[cb * 1024:(cb + 1) * 1024, :])
        idx_ref[0, e] = (acc[:, IDX_LANE:IDX_LANE + 1] * 64.0 + acc[:, IDX_LANE + 1:IDX_LANE + 2]).astype(jnp.int32)
        g = jnp.zeros((CAP, 1), F32)
        for off in GATE_LANE:
            g = g + jnp.sum(jnp.where(lane == off + e, acc, 0.0), axis=-1, keepdims=True)
        gate_ref[0, e] = g


def _topk(aff_n, tmat):
    out_spec = pl.BlockSpec((1, N_EXPERTS, CAP, 1), lambda s: (s, 0, 0, 0))
    return pl.pallas_call(
        _topk_kernel,
        grid=(2,),
        in_specs=[
            pl.BlockSpec((N_CTX, 128), lambda s: (s, 0)),
            pl.BlockSpec((N_CTX, 128), lambda s: (0, 0)),
        ],
        out_specs=[out_spec, out_spec],
        out_shape=[jax.ShapeDtypeStruct((2, N_EXPERTS, CAP, 1), jnp.int32),
                   jax.ShapeDtypeStruct((2, N_EXPERTS, CAP, 1), F32)],
        compiler_params=_params(1),
        name="topk",
    )(aff_n, tmat)


N_FT = FF // TF
GATHER_ROWS = 2 * CAP
ROWS_PER_STEP = GATHER_ROWS // N_FT


def _ffn_kernel(cast_next, idx_ref, h2_hbm, g_ref, wg_ref, wu_ref, wd_ref, *refs):
    if cast_next:
        win_ref, ye_ref, wbf_ref, xg, xb, acc, sems = refs
        wbf_ref[...] = win_ref[...].astype(BF16)
    else:
        ye_ref, xg, xb, acc, sems = refs
    e = pl.program_id(0)
    f = pl.program_id(1)
    slot = e % 2

    def row_copy(expert, s, p, dst_slot):
        row = idx_ref[(s * N_EXPERTS + expert) * CAP + p] + s * N_CTX
        src = h2_hbm.at[pl.ds(pl.multiple_of(row * SLAB, SLAB), SLAB), :]
        dst = xg.at[dst_slot, pl.ds(pl.multiple_of((s * CAP + p) * SLAB, SLAB), SLAB), :]
        return pltpu.make_async_copy(src, dst, sems.at[dst_slot])

    def slot_wait(dst_slot):
        pltpu.make_async_copy(h2_hbm.at[pl.ds(0, GATHER_ROWS * SLAB), :], xg.at[dst_slot],
                              sems.at[dst_slot]).wait()

    @pl.when(jnp.logical_and(e == 0, f == 0))
    def _():
        def issue(r, carry):
            row_copy(0, r // CAP, r % CAP, 0).start()
            return carry
        lax.fori_loop(0, GATHER_ROWS, issue, 0)

    @pl.when(f == 0)
    def _():
        slot_wait(slot)
        xb[...] = _from_slabs(xg.at[slot]).astype(BF16)
        acc[...] = jnp.zeros_like(acc)

    nxt = jnp.minimum(e + 1, N_EXPERTS - 1)
    s_nxt = f // (N_FT // 2)
    p0 = (f % (N_FT // 2)) * ROWS_PER_STEP
    for u in range(ROWS_PER_STEP):
        row_copy(nxt, s_nxt, p0 + u, 1 - slot).start()

    x = xb[...]
    hg = _dot(x, wg_ref[...].astype(BF16))
    hu = _dot(x, wu_ref[...].astype(BF16))
    hdn = ((hg * _sigmoid(hg)) * hu).astype(BF16)
    acc[...] += _dot(hdn, wd_ref[...].astype(BF16))

    @pl.when(f == N_FT - 1)
    def _():
        for s in range(2):
            _to_slabs(ye_ref.at[s], acc[s * CAP:(s + 1) * CAP, :] * g_ref[s])

    @pl.when(jnp.logical_and(e == N_EXPERTS - 1, f == N_FT - 1))
    def _():
        slot_wait(1 - slot)


def _expert_ffn(l, idx_flat, h2s, gates, w_gate, w_up, w_down, w_in):
    cast_next = l + 1 < DEPTH
    n_steps = N_EXPERTS * N_FT
    in_specs = [
        pl.BlockSpec(memory_space=pl.ANY),
        pl.BlockSpec((2, None, CAP, 1), lambda e, f, idx: (0, e, 0, 0)),
        pl.BlockSpec((None, None, D, TF), lambda e, f, idx: (l, e, 0, f)),
        pl.BlockSpec((None, None, D, TF), lambda e, f, idx: (l, e, 0, f)),
        pl.BlockSpec((None, None, TF, D), lambda e, f, idx: (l, e, f, 0)),
    ]
    args = [idx_flat, h2s, gates, w_gate, w_up, w_down]
    out_specs = [pl.BlockSpec((2, None, CAP * SLAB, 128), lambda e, f, idx: (0, e, 0, 0))]
    out_shape = [jax.ShapeDtypeStruct((2, N_EXPERTS, CAP * SLAB, 128), F32)]
    if cast_next:
        cols = IN_COLS // n_steps
        in_specs.append(pl.BlockSpec((None, D, cols), lambda e, f, idx: (l + 1, 0, e * N_FT + f)))
        args.append(w_in)
        out_specs.append(pl.BlockSpec((D, cols), lambda e, f, idx: (0, e * N_FT + f)))
        out_shape.append(jax.ShapeDtypeStruct((D, IN_COLS), BF16))
    grid_spec = pltpu.PrefetchScalarGridSpec(
        num_scalar_prefetch=1,
        grid=(N_EXPERTS, N_FT),
        in_specs=in_specs,
        out_specs=out_specs,
        scratch_shapes=[
            pltpu.VMEM((2, GATHER_ROWS * SLAB, 128), F32),
            pltpu.VMEM((GATHER_ROWS, D), BF16),
            pltpu.VMEM((GATHER_ROWS, D), F32),
            pltpu.SemaphoreType.DMA((2,)),
        ],
    )
    outs = pl.pallas_call(
        functools.partial(_ffn_kernel, cast_next),
        grid_spec=grid_spec,
        out_shape=out_shape,
        compiler_params=_params(2),
        name="expert_ffn",
    )(*args)
    return (outs[0], outs[1]) if cast_next else (outs[0], None)


SCATTER_UNROLL = 8


def _combine_kernel(idx_ref, ye_ref, y_ref):
    s = pl.program_id(0)
    e = pl.program_id(1)

    @pl.when(e == 0)
    def _():
        y_ref[...] = jnp.zeros_like(y_ref)

    base = (s * N_EXPERTS + e) * CAP

    for p0 in range(0, CAP, SCATTER_UNROLL):
        rows = [idx_ref[base + p0 + u] for u in range(SCATTER_UNROLL)]
        tiles = [pl.ds(pl.multiple_of(r * SLAB, SLAB), SLAB) for r in rows]
        vals = [y_ref[tiles[u], :] + ye_ref[pl.ds((p0 + u) * SLAB, SLAB), :] for u in range(SCATTER_UNROLL)]
        for u in range(SCATTER_UNROLL):
            y_ref[tiles[u], :] = vals[u]


def _combine(idx_flat, ye):
    grid_spec = pltpu.PrefetchScalarGridSpec(
        num_scalar_prefetch=1,
        grid=(2, N_EXPERTS),
        in_specs=[pl.BlockSpec((None, None, CAP * SLAB, 128), lambda s, e, idx: (s, e, 0, 0))],
        out_specs=pl.BlockSpec((N_CTX * SLAB, 128), lambda s, e, idx: (s, 0)),
    )
    return pl.pallas_call(
        _combine_kernel,
        grid_spec=grid_spec,
        out_shape=jax.ShapeDtypeStruct((N_TOK * SLAB, 128), F32),
        compiler_params=_params(2),
        name="combine",
    )(idx_flat, ye)


def _final_kernel(x_ref, y_ref, mod_ref, w_ref, op_ref, os_ref):
    x = x_ref[...] + mod_ref[0][:, 5 * D:6 * D] * _from_slabs(y_ref)
    out = _rms(x) * w_ref[...]
    is_ctx = pl.program_id(0) < N_CTX // FINAL_ROWS

    @pl.when(is_ctx)
    def _():
        op_ref[...] = out

    @pl.when(jnp.logical_not(is_ctx))
    def _():
        os_ref[...] = out


FINAL_ROWS = 2 * TM


def _final(x1, y, mod, final_norm_w):
    ft = FINAL_ROWS
    ctx_steps = N_CTX // ft
    tile = pl.BlockSpec((ft, D), lambda i: (i, 0))
    return pl.pallas_call(
        _final_kernel,
        grid=(N_TOK // ft,),
        in_specs=[tile, pl.BlockSpec((ft * SLAB, 128), lambda i: (i, 0)),
                  pl.BlockSpec((1, 1, N_MOD * D), lambda i: ((DEPTH - 1) * 8 + _mod_row(i * (ft // TM)), 0, 0)),
                  pl.BlockSpec((1, D), lambda i: (0, 0))],
        out_specs=[pl.BlockSpec((ft, D), lambda i: (jnp.minimum(i, ctx_steps - 1), 0)),
                   pl.BlockSpec((ft, D), lambda i: (jnp.maximum(i - ctx_steps, 0), 0))],
        out_shape=[jax.ShapeDtypeStruct((N_CTX, D), F32), jax.ShapeDtypeStruct((N_LAT, D), F32)],
        compiler_params=_params(1),
        name="final_norm",
    )(x1, y, mod, final_norm_w.reshape(1, D))


def _rope_tables():
    t = np.arange(T_LAT)
    row = (t // GRID_W).astype(np.float32)
    col = (t % GRID_W).astype(np.float32)
    inv = jnp.asarray(ROPE_BASE, F32) ** (-jnp.arange(N_ROPE_FREQ, dtype=F32) / N_ROPE_FREQ)
    ang_r = jnp.asarray(row)[:, None] * inv
    ang_c = jnp.asarray(col)[:, None] * inv
    def group(ang):
        return jnp.concatenate([ang, ang], axis=-1)
    ang = jnp.concatenate([group(ang_r), group(ang_c), group(ang_r), group(ang_c)], axis=-1)
    sign = np.where(np.arange(HD) % 32 < 16, -1.0, 1.0).astype(np.float32)
    return jnp.cos(ang), jnp.sin(ang) * sign


def _index_table():
    t = np.arange(N_CTX)
    tm = np.zeros((N_CTX, 128), np.float32)
    tm[:, IDX_LANE] = t // 64
    tm[:, IDX_LANE + 1] = t % 64
    return jnp.asarray(tm, BF16)


def kernel(x_prompt, x_sample, c, cache_attn_k, cache_attn_v, state_ret_fwd, state_ret_bwd, c_ctx, w_ada, b_ada, norm1_w, norm2_w, w_in, attn_lambda, attn_subln_w, conv_w, ret_decay_fwd, ret_decay_bwd, ret_norm_w, w_br_attn, w_br_conv, w_br_ret, w_out, w_router, w_exp_gate, w_exp_up, w_exp_down, final_norm_w):
    x = (x_prompt.reshape(N_CTX, D), x_sample.reshape(N_LAT, D))
    cvec = jnp.concatenate([c_ctx[None, :], c, jnp.zeros((3, D), F32)], axis=0)
    mod = _modulation(cvec, w_ada, b_ada).reshape(DEPTH * 8, 1, N_MOD * D)

    w_in_bf = w_in[0].astype(BF16)
    w_br_attn_bf = w_br_attn.astype(BF16)
    w_br_conv_bf = w_br_conv.astype(BF16)
    w_br_ret_bf = w_br_ret.astype(BF16)
    w_out_bf = w_out.astype(BF16)
    w_router_pad = jnp.pad(w_router, ((0, 0), (0, 0), (0, 128 - N_EXPERTS))).astype(BF16)
    norm1 = norm1_w.reshape(DEPTH, 1, D)
    norm2 = norm2_w.reshape(DEPTH, 1, D)
    subln = attn_subln_w.reshape(DEPTH, 1, HD)
    decay_f = ret_decay_fwd.reshape(DEPTH * HEADS, 1, 1)
    decay_b = ret_decay_bwd.reshape(DEPTH * HEADS, 1, 1)
    ret_nw = ret_norm_w.reshape(DEPTH * HEADS, 1, HD)
    cos, sin = _rope_tables()
    tmat = _index_table()

    y = None
    new_k = new_v = new_sf = new_sb = None
    for l in range(DEPTH):
        lam_init = 0.8 - 0.6 * math.exp(-0.3 * l)
        if y is None:
            zb, zq, new_k, new_v = _inproj(l, x, None, mod, norm1, w_in_bf, new_k, new_v)
        else:
            zb, zq, new_k, new_v, x = _inproj(l, x, y, mod, norm1, w_in_bf, new_k, new_v)
        ao, ao_lat = _attention(l, lam_init, zb, zq, cache_attn_k, cache_attn_v, cos, sin, attn_lambda, subln)
        ro, ro_lat, new_sf, new_sb = _retention(l, zb, state_ret_fwd, state_ret_bwd, decay_f, decay_b, ret_nw,
                                                new_sf, new_sb)
        x, h2s, aff_n = _merge(l, x, ao, ao_lat, ro, ro_lat, zb, mod, conv_w, w_br_attn_bf, w_br_conv_bf,
                               w_br_ret_bf, w_out_bf, norm2, w_router_pad)
        idx, gates = _topk(aff_n, tmat)
        idx_flat = idx.reshape(2 * N_EXPERTS * CAP)
        ye, w_in_bf = _expert_ffn(l, idx_flat, h2s, gates, w_exp_gate, w_exp_up, w_exp_down, w_in)
        y = _combine(idx_flat, ye)

    y_prompt, y_sample = _final(x, y, mod, final_norm_w)
    return (y_prompt.reshape(N_CTX_SEQ, T_CTX, D), y_sample.reshape(N_LAT_SEQ, T_LAT, D), new_k, new_v,
            new_sf, new_sb)
```

```python
import functools
import math

import jax
import jax.numpy as jnp
import numpy as np
from jax import lax
from jax.experimental import pallas as pl
from jax.experimental.pallas import tpu as pltpu

F32 = jnp.float32
BF16 = jnp.bfloat16

D = 1024
DEPTH = 2
N_CTX_SEQ = 16
T_CTX = 256
N_LAT_SEQ = 4
T_LAT = 1024
PAST = 256
N_CTX = N_CTX_SEQ * T_CTX
N_LAT = N_LAT_SEQ * T_LAT
N_TOK = N_CTX + N_LAT
TM = 256
N_TILES = N_TOK // TM
CTX_TILES = N_CTX // TM
LAT_TILES_PER_SEQ = T_LAT // TM
HEADS = 4
HD = 128
GRID_W = 64
N_ROPE_FREQ = 16
ROPE_BASE = 10000.0
IN_COLS = 8192
N_MOD = 6
N_EXPERTS = 16
CAP = 512
FF = 2048
TF = 512
EPS = 1e-6
RET_SCALE = HD ** -0.5
ATTN_SCALE = 64 ** -0.5
SLAB = D // 128
VMEM_LIMIT = 56 * 1024 * 1024

QB, KB, VB = 0, 4, 8
RQB, RKB, RVB, RGB = 24, 28, 32, 36
CONV_B_COL, CONV_U_COL = 3, 4
MERGE_GATE_COL = 5
MERGE_SUB = 2


def _sigmoid(x):
    return 0.5 * jnp.tanh(0.5 * x) + 0.5


def _log_sigmoid(x):
    return jnp.minimum(x, 0.0) - jnp.log(1.0 + jnp.exp(-jnp.abs(x)))


def _rms(x):
    return x * lax.rsqrt(jnp.mean(x * x, axis=-1, keepdims=True) + EPS)


def _dot(a, b):
    return jnp.dot(a, b, preferred_element_type=F32)


def _dot_nt(a, b):
    return lax.dot_general(a, b, (((1,), (1,)), ((), ())), preferred_element_type=F32)


def _dot_tn(a, b):
    return lax.dot_general(a, b, (((0,), (0,)), ((), ())), preferred_element_type=F32)


def _to_slabs(ref, x):
    n = x.shape[0]
    for s in range(SLAB):
        ref[pl.ds(s, n, stride=SLAB), :] = x[:, s * 128:(s + 1) * 128]


def _from_slabs(ref):
    n = ref.shape[0] // SLAB
    return jnp.concatenate([ref[pl.ds(s, n, stride=SLAB), :] for s in range(SLAB)], axis=-1)


def _mod_row(i):
    return jnp.where(i < CTX_TILES, 0, 1 + (i - CTX_TILES) // LAT_TILES_PER_SEQ)


def _params(n_axes):
    return pltpu.CompilerParams(
        dimension_semantics=("arbitrary",) * n_axes, vmem_limit_bytes=VMEM_LIMIT)


def _mod_kernel(c_ref, w_ref, b_ref, o_ref):
    c = c_ref[...]
    s = (c * _sigmoid(c)).astype(BF16)
    o_ref[...] = _dot(s, w_ref[...].astype(BF16)) + b_ref[...]


def _modulation(cvec, w_ada, b_ada):
    tn = 1024
    return pl.pallas_call(
        _mod_kernel,
        grid=(DEPTH, N_MOD * D // tn),
        in_specs=[
            pl.BlockSpec((8, D), lambda l, n: (0, 0)),
            pl.BlockSpec((None, D, tn), lambda l, n: (l, 0, n)),
            pl.BlockSpec((None, 1, tn), lambda l, n: (l, 0, n)),
        ],
        out_specs=pl.BlockSpec((None, 8, tn), lambda l, n: (l, 0, n)),
        out_shape=jax.ShapeDtypeStruct((DEPTH, 8, N_MOD * D), F32),
        compiler_params=_params(2),
        name="modulation",
    )(cvec, w_ada, b_ada.reshape(DEPTH, 1, N_MOD * D))


def _inproj_kernel(has_y, x_pair, n_prev, *refs):
    refs = list(refs)
    x_ref = refs.pop(0)
    if x_pair:
        xs_ref = refs.pop(0)
    if has_y:
        y_ref, modp_ref = refs.pop(0), refs.pop(0)
    mod_ref, n1_ref, w_ref = refs.pop(0), refs.pop(0), refs.pop(0)
    if n_prev:
        kp_ref, vp_ref = refs.pop(0), refs.pop(0)
    zb_ref, zq_ref, kn_ref, vn_ref = refs[0:4]
    i = pl.program_id(0)

    def normalize():
        x = x_ref[...]
        if x_pair:
            x = jnp.where(i < N_TILES - CTX_TILES, xs_ref[...], x)
        if has_y:
            x = x + modp_ref[0][:, 5 * D:6 * D] * _from_slabs(y_ref)
            refs[4][...] = x
        m = mod_ref[0]
        shift1 = m[:, 0:D]
        scale1 = m[:, D:2 * D]
        return ((_rms(x) * n1_ref[...]) * (1.0 + scale1) + shift1).astype(BF16)

    def project(h):
        cw = 1024
        for c in range(IN_COLS // cw):
            z = _dot(h, w_ref[:, c * cw:(c + 1) * cw])
            if c == CONV_U_COL // 2:
                z = jnp.concatenate([z[:, 0:512] * z[:, 512:cw], z[:, 512:cw]], axis=-1)
            elif c == RGB // 8:
                g = z[:, 512:cw]
                z = jnp.concatenate([z[:, 0:512], g * _sigmoid(g)], axis=-1)
            elif c >= MERGE_GATE_COL:
                z = _sigmoid(z)
            zb_ref[:, c * cw:(c + 1) * cw] = z.astype(BF16)
            if c == 0:
                zq_ref[...] = z
                for hh in range(HEADS):
                    kn_ref[n_prev, hh] = z[:, 512 + hh * HD:512 + (hh + 1) * HD]
            if c == 1:
                for hh in range(HEADS):
                    vn_ref[n_prev, hh] = z[:, hh * HD:(hh + 1) * HD]
        if n_prev:
            kn_ref[0:n_prev] = kp_ref[...]
            vn_ref[0:n_prev] = vp_ref[...]

    project(normalize())


def _inproj(l, x, y, mod, norm1_w, w_in_bf, k_prev, v_prev):
    has_y = y is not None
    t = lambda i: (i + CTX_TILES) % N_TILES
    n_lat = N_TILES - CTX_TILES
    norm_tile = pl.BlockSpec((TM, D), lambda i: (t(i), 0))
    ctx_i = lambda i: jnp.maximum(i - n_lat, 0)
    kv_spec = lambda m: pl.BlockSpec((None, m, HEADS, T_CTX, HD), lambda i: (ctx_i(i), 0, 0, 0, 0))
    kv_shape = jax.ShapeDtypeStruct((N_CTX_SEQ, l + 1, HEADS, T_CTX, HD), F32)
    x_pair = isinstance(x, tuple)
    if x_pair:
        in_specs = [pl.BlockSpec((TM, D), lambda i: (jnp.maximum(i - n_lat, 0), 0)),
                    pl.BlockSpec((TM, D), lambda i: (jnp.minimum(i, n_lat - 1), 0))]
        args = list(x)
    else:
        in_specs = [norm_tile]
        args = [x]
    if has_y:
        in_specs += [pl.BlockSpec((TM * SLAB, 128), lambda i: (t(i), 0)),
                     pl.BlockSpec((1, 1, N_MOD * D), lambda i: ((l - 1) * 8 + _mod_row(t(i)), 0, 0))]
        args += [y, mod]
    in_specs += [
        pl.BlockSpec((1, 1, N_MOD * D), lambda i: (l * 8 + _mod_row(t(i)), 0, 0)),
        pl.BlockSpec((None, 1, D), lambda i: (l, 0, 0)),
        pl.BlockSpec((D, IN_COLS), lambda i: (0, 0), pipeline_mode=pl.Buffered(1)),
    ]
    args += [mod, norm1_w, w_in_bf]
    if l:
        in_specs += [kv_spec(l), kv_spec(l)]
        args += [k_prev, v_prev]
    out_specs = [pl.BlockSpec((TM, IN_COLS), lambda i: (t(i), 0)),
                 pl.BlockSpec((TM, 1024), lambda i: (t(i), 0)),
                 kv_spec(l + 1), kv_spec(l + 1)]
    out_shape = [jax.ShapeDtypeStruct((N_TOK, IN_COLS), BF16),
                 jax.ShapeDtypeStruct((N_TOK, 1024), F32),
                 kv_shape, kv_shape]
    if has_y:
        out_specs.append(norm_tile)
        out_shape.append(jax.ShapeDtypeStruct((N_TOK, D), F32))
    return pl.pallas_call(
        functools.partial(_inproj_kernel, has_y, x_pair, l),
        grid=(N_TILES,),
        in_specs=in_specs,
        out_specs=out_specs,
        out_shape=out_shape,
        compiler_params=_params(1),
        name="inproj",
    )(*args)


def _lambda(lam_ref, lam_init):
    lv = lam_ref[...]
    a = jnp.sum(lv[0:1] * lv[1:2], axis=-1, keepdims=True)
    b = jnp.sum(lv[2:3] * lv[3:4], axis=-1, keepdims=True)
    return jnp.exp(a) - jnp.exp(b) + lam_init


def _diff_attention(lam_init, qs, keys, vs, lam, sw, keys_transposed):
    lane = lax.broadcasted_iota(jnp.int32, qs[0].shape, 1)
    score = _dot if keys_transposed else _dot_nt
    maps = [(h, jnp.where(keep, q, jnp.zeros_like(q)))
            for h, q in enumerate(qs) for keep in (lane < 64, lane >= 64)]
    s = [score(qm, keys[h]) for h, qm in maps]
    e = [jnp.exp(x - jnp.max(x, axis=-1, keepdims=True)) for x in s]
    r = [1.0 / jnp.sum(x, axis=-1, keepdims=True) for x in e]
    a = [(e[2 * h] * r[2 * h] - e[2 * h + 1] * (lam * r[2 * h + 1])).astype(BF16) for h in range(len(qs))]
    o = [_dot(a[h], vs[h]) for h in range(len(qs))]
    return [(_rms(x) * sw) * (1.0 - lam_init) for x in o]


def _scaled_q(q):
    return (q.astype(F32) * ATTN_SCALE).astype(BF16)


def _attn_ctx_kernel(lam_init, q_ref, k_ref, v_ref, lam_ref, sw_ref, o_ref):
    lam = _lambda(lam_ref, lam_init)
    sls = [slice(h * HD, (h + 1) * HD) for h in range(HEADS)]
    outs = _diff_attention(lam_init, [_scaled_q(q_ref[:, sl]) for sl in sls], [k_ref[:, sl] for sl in sls],
                           [v_ref[:, sl] for sl in sls], lam, sw_ref[...], False)
    for sl, o in zip(sls, outs):
        o_ref[:, sl] = o.astype(BF16)


def _rope(x, cos, sin_signed):
    lane = lax.broadcasted_iota(jnp.int32, x.shape, 1)
    partner = jnp.where(lane % 32 < 16, pltpu.roll(x, 112, 1), pltpu.roll(x, 16, 1))
    return x * cos + partner * sin_signed


def _attn_lat_kernel(lam_init, q_ref, k_ref, v_ref, ck_ref, cv_ref, cos_ref, sin_ref,
                     cosq_ref, sinq_ref, lam_ref, sw_ref, o_ref, k_t, vall):
    @pl.when(pl.program_id(1) == 0)
    def _():
        for h in range(HEADS):
            sl = slice(h * HD, (h + 1) * HD)
            k_t[h, :, 0:PAST] = ck_ref[h].T.astype(BF16)
            k_t[h, :, PAST:] = _rope(k_ref[:, sl], cos_ref[...], sin_ref[...]).T.astype(BF16)
            vall[h, 0:PAST, :] = cv_ref[h].astype(BF16)
            vall[h, PAST:, :] = v_ref[:, sl]

    lam = _lambda(lam_ref, lam_init)
    sls = [slice(h * HD, (h + 1) * HD) for h in range(HEADS)]
    qs = [_scaled_q(_rope(q_ref[:, sl], cosq_ref[...], sinq_ref[...])) for sl in sls]
    outs = _diff_attention(lam_init, qs, [k_t[h] for h in range(HEADS)], [vall[h] for h in range(HEADS)],
                           lam, sw_ref[...], True)
    for sl, o in zip(sls, outs):
        o_ref[:, sl] = o.astype(BF16)


def _attention(l, lam_init, zb, zq, cache_k, cache_v, cos, sin, attn_lambda, subln_w):
    width = HEADS * HD
    ao = pl.pallas_call(
        functools.partial(_attn_ctx_kernel, lam_init),
        grid=(N_CTX_SEQ,),
        in_specs=[
            pl.BlockSpec((T_CTX, width), lambda b: (b, QB // HEADS)),
            pl.BlockSpec((T_CTX, width), lambda b: (b, KB // HEADS)),
            pl.BlockSpec((T_CTX, width), lambda b: (b, VB // HEADS)),
            pl.BlockSpec((None, 4, 64), lambda b: (l, 0, 0)),
            pl.BlockSpec((None, 1, HD), lambda b: (l, 0, 0)),
        ],
        out_specs=pl.BlockSpec((T_CTX, width), lambda b: (b, 0)),
        out_shape=jax.ShapeDtypeStruct((N_CTX, width), BF16),
        compiler_params=_params(1),
        name="attn_ctx",
    )(zb, zb, zb, attn_lambda, subln_w)

    lat_row = lambda b, j: CTX_TILES + LAT_TILES_PER_SEQ * b + j
    seq_row = lambda b: N_CTX // T_LAT + b
    ao_lat = pl.pallas_call(
        functools.partial(_attn_lat_kernel, lam_init),
        grid=(N_LAT_SEQ, LAT_TILES_PER_SEQ),
        in_specs=[
            pl.BlockSpec((TM, width), lambda b, j: (lat_row(b, j), QB // HEADS)),
            pl.BlockSpec((T_LAT, width), lambda b, j: (seq_row(b), KB // HEADS)),
            pl.BlockSpec((T_LAT, width), lambda b, j: (seq_row(b), VB // HEADS)),
            pl.BlockSpec((None, None, HEADS, PAST, HD), lambda b, j: (b, l, 0, 0, 0)),
            pl.BlockSpec((None, None, HEADS, PAST, HD), lambda b, j: (b, l, 0, 0, 0)),
            pl.BlockSpec((T_LAT, HD), lambda b, j: (0, 0)),
            pl.BlockSpec((T_LAT, HD), lambda b, j: (0, 0)),
            pl.BlockSpec((TM, HD), lambda b, j: (j, 0)),
            pl.BlockSpec((TM, HD), lambda b, j: (j, 0)),
            pl.BlockSpec((None, 4, 64), lambda b, j: (l, 0, 0)),
            pl.BlockSpec((None, 1, HD), lambda b, j: (l, 0, 0)),
        ],
        out_specs=pl.BlockSpec((TM, width), lambda b, j: (LAT_TILES_PER_SEQ * b + j, 0)),
        out_shape=jax.ShapeDtypeStruct((N_LAT, width), BF16),
        scratch_shapes=[pltpu.VMEM((HEADS, HD, PAST + T_LAT), BF16),
                        pltpu.VMEM((HEADS, PAST + T_LAT, HD), BF16)],
        compiler_params=_params(2),
        name="attn_lat",
    )(zq, zq, zb, cache_k, cache_v, cos, sin, cos, sin, attn_lambda, subln_w)
    return ao, ao_lat


def _decay_matrix(lgf, lgb, row0, tq, tk):
    i = row0 + lax.broadcasted_iota(jnp.int32, (tq, tk), 0)
    j = lax.broadcasted_iota(jnp.int32, (tq, tk), 1)
    d = (i - j).astype(F32)
    fwd = jnp.where(d >= 0.0, jnp.exp(lgf * jnp.maximum(d, 0.0)), 0.0)
    bwd = jnp.where(d <= 0.0, jnp.exp(lgb * jnp.maximum(-d, 0.0)), 0.0)
    return fwd + bwd


def _ret_finish(o, g, nw):
    return (g.astype(F32) * (_rms(o) * nw)).astype(BF16)


def _ret_ctx_kernel(n_prev, *refs):
    refs = list(refs)
    q_ref, k_ref, v_ref, g_ref, df_ref, db_ref, nw_ref = refs[0:7]
    refs = refs[7:]
    if n_prev:
        sfp_ref, sbp_ref = refs.pop(0), refs.pop(0)
    o_ref, sf_ref, sb_ref, dmat = refs
    @pl.when(pl.program_id(0) == 0)
    def _():
        for h in range(HEADS):
            dmat[h] = RET_SCALE * _decay_matrix(_log_sigmoid(df_ref[h]), _log_sigmoid(db_ref[h]), 0, T_CTX, T_CTX)

    if n_prev:
        sf_ref[0:n_prev] = sfp_ref[...]
        sb_ref[0:n_prev] = sbp_ref[...]
    j = lax.broadcasted_iota(jnp.int32, (T_CTX, 1), 0).astype(F32)
    heads = range(HEADS)
    sls = [slice(h * HD, (h + 1) * HD) for h in heads]
    q = [q_ref[:, sl] for sl in sls]
    k = [k_ref[:, sl] for sl in sls]
    v = [v_ref[:, sl] for sl in sls]
    s = [(_dot_nt(q[h], k[h]) * dmat[h]).astype(BF16) for h in heads]
    o = [_dot(s[h], v[h]) for h in heads]
    lgf = [_log_sigmoid(df_ref[h]) for h in heads]
    lgb = [_log_sigmoid(db_ref[h]) for h in heads]
    kf = [k[h].astype(F32) * RET_SCALE for h in heads]
    kfw = [(kf[h] * jnp.exp(lgf[h] * (T_CTX - 1.0 - j))).astype(BF16) for h in heads]
    kbw = [(kf[h] * jnp.exp(lgb[h] * j)).astype(BF16) for h in heads]
    for h in heads:
        sf_ref[n_prev, h] = _dot_tn(kfw[h], v[h])
        sb_ref[n_prev, h] = _dot_tn(kbw[h], v[h])
    for h in heads:
        o_ref[:, sls[h]] = _ret_finish(o[h], g_ref[:, sls[h]], nw_ref[h])


N_DCHUNK = 2 * LAT_TILES_PER_SEQ - 1


def _ret_lat_kernel(q_ref, k_ref, v_ref, g_ref, s0f_ref, s0b_ref, df_ref, db_ref, nw_ref, o_ref, strip, k_t):
    b = pl.program_id(0)
    j = pl.program_id(1)

    @pl.when(jnp.logical_and(b == 0, j == 0))
    def _():
        for h in range(HEADS):
            lgf = _log_sigmoid(df_ref[h])
            lgb = _log_sigmoid(db_ref[h])
            for c in range(N_DCHUNK):
                strip[h, c] = RET_SCALE * _decay_matrix(lgf, lgb, T_LAT - TM - c * TM, TM, TM)

    @pl.when(j == 0)
    def _():
        for h in range(HEADS):
            k_t[h] = k_ref[:, h * HD:(h + 1) * HD].astype(F32).T.astype(BF16)

    i = (j * TM + lax.broadcasted_iota(jnp.int32, (TM, 1), 0)).astype(F32)
    c0 = LAT_TILES_PER_SEQ - 1 - j
    heads = range(HEADS)
    sls = [slice(h * HD, (h + 1) * HD) for h in heads]
    q = [q_ref[:, sl] for sl in sls]
    dmat = [jnp.concatenate([strip[h, c0 + c] for c in range(LAT_TILES_PER_SEQ)], axis=-1) for h in heads]
    s = [(_dot(q[h], k_t[h]) * dmat[h]).astype(BF16) for h in heads]
    o = [_dot(s[h], v_ref[:, sls[h]]) for h in heads]
    of = [_dot(q[h], s0f_ref[h].astype(BF16)) for h in heads]
    ob = [_dot(q[h], s0b_ref[h].astype(BF16)) for h in heads]
    for h in heads:
        tot = o[h] + jnp.exp(_log_sigmoid(df_ref[h]) * (i + 1.0)) * of[h]
        tot = tot + jnp.exp(_log_sigmoid(db_ref[h]) * (T_LAT - i)) * ob[h]
        o_ref[:, sls[h]] = _ret_finish(tot, g_ref[:, sls[h]], nw_ref[h])


def _retention(l, zb, state_f, state_b, decay_f, decay_b, ret_norm_w, sf_prev, sb_prev):
    width = HEADS * HD
    dec1 = pl.BlockSpec((HEADS, 1, 1), lambda b: (l, 0, 0))
    nw1 = pl.BlockSpec((HEADS, 1, HD), lambda b: (l, 0, 0))
    st_spec = lambda n: pl.BlockSpec((None, n, HEADS, HD, HD), lambda b: (b, 0, 0, 0, 0))
    st_shape = jax.ShapeDtypeStruct((N_CTX_SEQ, l + 1, HEADS, HD, HD), F32)
    in_specs = [
        pl.BlockSpec((T_CTX, width), lambda b: (b, RQB // HEADS)),
        pl.BlockSpec((T_CTX, width), lambda b: (b, RKB // HEADS)),
        pl.BlockSpec((T_CTX, width), lambda b: (b, RVB // HEADS)),
        pl.BlockSpec((T_CTX, width), lambda b: (b, RGB // HEADS)),
        dec1, dec1, nw1,
    ]
    args = [zb, zb, zb, zb, decay_f, decay_b, ret_norm_w]
    if l:
        in_specs += [st_spec(l), st_spec(l)]
        args += [sf_prev, sb_prev]
    ro, sf, sb = pl.pallas_call(
        functools.partial(_ret_ctx_kernel, l),
        grid=(N_CTX_SEQ,),
        in_specs=in_specs,
        out_specs=[pl.BlockSpec((T_CTX, width), lambda b: (b, 0)), st_spec(l + 1), st_spec(l + 1)],
        out_shape=[jax.ShapeDtypeStruct((N_CTX, width), BF16), st_shape, st_shape],
        scratch_shapes=[pltpu.VMEM((HEADS, T_CTX, T_CTX), F32)],
        compiler_params=_params(1),
        name="ret_ctx",
    )(*args)

    lat_row = lambda b, j: CTX_TILES + LAT_TILES_PER_SEQ * b + j
    seq_row = lambda b: N_CTX // T_LAT + b
    dec2 = pl.BlockSpec((HEADS, 1, 1), lambda b, j: (l, 0, 0))
    nw2 = pl.BlockSpec((HEADS, 1, HD), lambda b, j: (l, 0, 0))
    s0_spec = pl.BlockSpec((None, None, HEADS, HD, HD), lambda b, j: (b, l, 0, 0, 0))
    ro_lat = pl.pallas_call(
        _ret_lat_kernel,
        grid=(N_LAT_SEQ, LAT_TILES_PER_SEQ),
        in_specs=[
            pl.BlockSpec((TM, width), lambda b, j: (lat_row(b, j), RQB // HEADS)),
            pl.BlockSpec((T_LAT, width), lambda b, j: (seq_row(b), RKB // HEADS)),
            pl.BlockSpec((T_LAT, width), lambda b, j: (seq_row(b), RVB // HEADS)),
            pl.BlockSpec((TM, width), lambda b, j: (lat_row(b, j), RGB // HEADS)),
            s0_spec, s0_spec, dec2, dec2, nw2,
        ],
        out_specs=pl.BlockSpec((TM, width), lambda b, j: (LAT_TILES_PER_SEQ * b + j, 0)),
        out_shape=jax.ShapeDtypeStruct((N_LAT, width), BF16),
        scratch_shapes=[pltpu.VMEM((HEADS, N_DCHUNK, TM, TM), F32), pltpu.VMEM((HEADS, HD, T_LAT), BF16)],
        compiler_params=_params(2),
        name="ret_lat",
    )(zb, zb, zb, zb, state_f, state_b, decay_f, decay_b, ret_norm_w)
    return ro, ro_lat, sf, sb


def _merge_kernel(x_pair, *refs):
    refs = list(refs)
    x_ref = refs.pop(0)
    xs_ref = refs.pop(0) if x_pair else None
    (aoc_ref, aol_ref, roc_ref, rol_ref, cb_ref, u_ref, up_ref, un_ref,
     mg0_ref, mg1_ref, mg2_ref, mod_ref, cw_ref, wa_ref, wc_ref, wr_ref, wo_ref, n2_ref, wrt_ref,
     x1_ref, h2_ref, aff_ref) = refs
    i = pl.program_id(0)
    is_ctx = i < CTX_TILES // MERGE_SUB
    m = mod_ref[0]
    gate1 = m[:, 2 * D:3 * D]
    shift2 = m[:, 3 * D:4 * D]
    scale2 = m[:, 4 * D:5 * D]
    cw = cw_ref[...]
    u_all = u_ref[...].astype(F32)
    r = lax.broadcasted_iota(jnp.int32, (TM, 512), 0)
    lane = lax.broadcasted_iota(jnp.int32, (TM, 128), 1)

    subs = range(MERGE_SUB)
    rows = [slice(sub * TM, (sub + 1) * TM) for sub in subs]

    def conv_out(sub):
        j = (i * MERGE_SUB + sub - CTX_TILES) % LAT_TILES_PER_SEQ
        seq_first = jnp.logical_or(is_ctx, j == 0)
        seq_last = jnp.logical_or(is_ctx, j == LAT_TILES_PER_SEQ - 1)
        u = u_all[rows[sub], :]
        up = up_ref[...].astype(F32)[15:16, :] if sub == 0 else u_all[sub * TM - 1:sub * TM, :]
        dn = un_ref[...].astype(F32)[0:1, :] if sub == MERGE_SUB - 1 else u_all[(sub + 1) * TM:(sub + 1) * TM + 1, :]
        up = up * jnp.where(seq_first, 0.0, 1.0)
        dn = dn * jnp.where(seq_last, 0.0, 1.0)
        u_prev = jnp.where(r == 0, up, pltpu.roll(u, 1, 0))
        u_next = jnp.where(r == TM - 1, dn, pltpu.roll(u, TM - 1, 0))
        conv = u_prev * cw[0:1, :] + u * cw[1:2, :] + u_next * cw[2:3, :]
        return (cb_ref[rows[sub], :].astype(F32) * conv).astype(BF16)

    ao = [jnp.where(is_ctx, aoc_ref[rw, :], aol_ref[rw, :]) for rw in rows]
    ro = [jnp.where(is_ctx, roc_ref[rw, :], rol_ref[rw, :]) for rw in rows]
    conv_o = [conv_out(sub) for sub in subs]
    b_attn = [_dot(ao[s], wa_ref[...]) for s in subs]
    b_conv = [_dot(conv_o[s], wc_ref[...]) for s in subs]
    b_ret = [_dot(ro[s], wr_ref[...]) for s in subs]
    merged = [(mg0_ref[rows[s], :].astype(F32) * b_attn[s] + mg1_ref[rows[s], :].astype(F32) * b_conv[s]
               + mg2_ref[rows[s], :].astype(F32) * b_ret[s]).astype(BF16) for s in subs]
    proj = [_dot(merged[s], wo_ref[...]) for s in subs]
    x1 = []
    for s in subs:
        x_in = x_ref[rows[s], :]
        if x_pair:
            x_in = jnp.where(is_ctx, x_in, xs_ref[rows[s], :])
        x1.append(x_in + gate1 * proj[s])
        x1_ref[rows[s], :] = x1[s]
    h2 = [(_rms(x1[s]) * n2_ref[...]) * (1.0 + scale2) + shift2 for s in subs]
    logits = [_dot(h2[s].astype(BF16), wrt_ref[...]) for s in subs]
    valid = lane < N_EXPERTS
    for s in subs:
        _to_slabs(h2_ref.at[pl.ds(s * TM * SLAB, TM * SLAB), :], h2[s])
        lmax = jnp.max(jnp.where(valid, logits[s], -jnp.inf), axis=-1, keepdims=True)
        e = jnp.where(valid, jnp.exp(logits[s] - lmax), 0.0)
        aff_ref[rows[s], :] = e * (1.0 / jnp.sum(e, axis=-1, keepdims=True))


def _merge(l, x, ao, ao_lat, ro, ro_lat, zb, mod, conv_w, w_br_attn, w_br_conv, w_br_ret, w_out, norm2_w,
           w_router_pad):
    mt = MERGE_SUB * TM
    n_steps = N_TOK // mt
    ctx_steps = N_CTX // mt
    n16 = N_TOK // 16
    ctx_br = pl.BlockSpec((mt, 512), lambda i: (jnp.minimum(i, ctx_steps - 1), 0))
    lat_br = pl.BlockSpec((mt, 512), lambda i: (jnp.maximum(i - ctx_steps, 0), 0))
    col = lambda c: pl.BlockSpec((mt, 512), lambda i: (i, c))
    halo_p = lambda c: pl.BlockSpec((16, 512), lambda i: (jnp.maximum(i * (mt // 16) - 1, 0), c))
    halo_n = lambda c: pl.BlockSpec((16, 512), lambda i: (jnp.minimum((i + 1) * (mt // 16), n16 - 1), c))
    mgs = lambda c: pl.BlockSpec((mt, D), lambda i: (i, c))
    wbr = pl.BlockSpec((None, 512, D), lambda i: (l, 0, 0))
    x_pair = isinstance(x, tuple)
    if x_pair:
        x_specs = [pl.BlockSpec((mt, D), lambda i: (jnp.minimum(i, ctx_steps - 1), 0)),
                   pl.BlockSpec((mt, D), lambda i: (jnp.maximum(i - ctx_steps, 0), 0))]
        x_args = list(x)
    else:
        x_specs = [pl.BlockSpec((mt, D), lambda i: (i, 0))]
        x_args = [x]
    return pl.pallas_call(
        functools.partial(_merge_kernel, x_pair),
        grid=(n_steps,),
        in_specs=x_specs + [
            ctx_br, lat_br, ctx_br, lat_br,
            col(CONV_B_COL), col(CONV_U_COL), halo_p(CONV_U_COL), halo_n(CONV_U_COL),
            mgs(MERGE_GATE_COL), mgs(MERGE_GATE_COL + 1), mgs(MERGE_GATE_COL + 2),
            pl.BlockSpec((1, 1, N_MOD * D), lambda i: (l * 8 + _mod_row(i * MERGE_SUB), 0, 0)),
            pl.BlockSpec((None, 3, 512), lambda i: (l, 0, 0)),
            wbr, wbr, wbr,
            pl.BlockSpec((None, D, D), lambda i: (l, 0, 0)),
            pl.BlockSpec((None, 1, D), lambda i: (l, 0, 0)),
            pl.BlockSpec((None, D, 128), lambda i: (l, 0, 0)),
        ],
        out_specs=[
            pl.BlockSpec((mt, D), lambda i: (i, 0)),
            pl.BlockSpec((mt * SLAB, 128), lambda i: (i, 0)),
            pl.BlockSpec((mt, 128), lambda i: (i, 0)),
        ],
        out_shape=[
            jax.ShapeDtypeStruct((N_TOK, D), F32),
            jax.ShapeDtypeStruct((N_TOK * SLAB, 128), F32),
            jax.ShapeDtypeStruct((N_TOK, 128), F32),
        ],
        compiler_params=_params(1),
        name="merge",
    )(*x_args, ao, ao_lat, ro, ro_lat, zb, zb, zb, zb, zb, zb, zb, mod, conv_w,
      w_br_attn, w_br_conv, w_br_ret, w_out, norm2_w, w_router_pad)


def _cumsum_lanes(x, tri):
    run = jnp.zeros((x.shape[0], 1), F32)
    outs = []
    for b in range(x.shape[1] // 128):
        cs = _dot(x[:, b * 128:(b + 1) * 128].astype(BF16), tri) + run
        run = cs[:, 127:128]
        outs.append(cs)
    return jnp.concatenate(outs, axis=-1)


GATE_LANE = (0, 16, 32)
IDX_LANE = 48
BISECT_GROUP = 4


def _topk_kernel(an_ref, tmat_ref, idx_ref, gate_ref):
    an = an_ref[...]
    a = an.T[0:N_EXPERTS, :]
    n = a.shape[1]
    kf = float(CAP)

    hi = an.astype(BF16).astype(F32)
    mid = (an - hi).astype(BF16).astype(F32)
    lo = ((an - hi) - mid).astype(BF16).astype(F32)
    table = (tmat_ref[...].astype(F32) + hi + pltpu.roll(mid, GATE_LANE[1], 1)
             + pltpu.roll(lo, GATE_LANE[2], 1)).astype(BF16)

    def count_gt(thr):
        return jnp.sum(jnp.where(a > thr, 1.0, 0.0), axis=-1, keepdims=True)

    def span(lo, hi):
        inside = jnp.logical_and(a > lo, a <= hi)
        cmax = jnp.max(jnp.where(inside, a, -jnp.inf), axis=-1, keepdims=True)
        cmin = jnp.min(jnp.where(inside, a, jnp.inf), axis=-1, keepdims=True)
        return cmax, cmin

    def cond(c):
        return jnp.logical_and(c[2] > 0, c[3] < 400)

    def body(c):
        lo, hi, _, it = c
        for _ in range(BISECT_GROUP):
            mid = 0.5 * (lo + hi)
            ge = count_gt(mid) >= kf
            lo = jnp.where(ge, mid, lo)
            hi = jnp.where(ge, hi, mid)
        cmax, cmin = span(lo, hi)
        open_rows = jnp.max(jnp.where(cmax != cmin, 1, 0))
        return lo, hi, open_rows, it + 1

    lo0 = jnp.full((N_EXPERTS, 1), -1.0, F32)
    hi0 = jnp.max(a, axis=-1, keepdims=True)
    cmax0, cmin0 = span(lo0, hi0)
    lo, hi, _, _ = lax.while_loop(
        cond, body, (lo0, hi0, jnp.max(jnp.where(cmax0 != cmin0, 1, 0)), jnp.int32(0)))
    thr, _ = span(lo, hi)

    r = lax.broadcasted_iota(jnp.int32, (128, 128), 0)
    c = lax.broadcasted_iota(jnp.int32, (128, 128), 1)
    tri = jnp.where(r <= c, 1.0, 0.0).astype(BF16)
    gt = a > thr
    eq = jnp.where(a == thr, 1.0, 0.0)
    need = kf - count_gt(thr)
    eq_before = _cumsum_lanes(eq, tri) - eq
    sel = jnp.where(jnp.logical_or(gt, jnp.logical_and(eq > 0.0, eq_before < need)), 1.0, 0.0)
    pos = _cumsum_lanes(sel, tri) - 1.0
    slot = jnp.where(sel > 0.0, pos, -1.0).astype(jnp.int32)

    p_iota = lax.broadcasted_iota(jnp.int32, (CAP, 1024), 0)
    lane = lax.broadcasted_iota(jnp.int32, (CAP, 128), 1)
    for e in range(N_EXPERTS):
        acc = jnp.zeros((CAP, 128), F32)
        for cb in range(n // 1024):
            onehot = jnp.where(p_iota == slot[e:e + 1, cb * 1024:(cb + 1) * 1024], 1.0, 0.0).astype(BF16)
            acc = acc + _dot(onehot, table[cb * 1024:(cb + 1) * 1024, :])
        acc_t = acc.T
        idx_ref[0, e] = (acc_t[IDX_LANE:IDX_LANE + 1, :] * 64.0 + acc_t[IDX_LANE + 1:IDX_LANE + 2, :]).astype(jnp.int32)
        g = jnp.zeros((CAP, 1), F32)
        for off in GATE_LANE:
            g = g + jnp.sum(jnp.where(lane == off + e, acc, 0.0), axis=-1, keepdims=True)
        gate_ref[0, e] = g


def _topk(aff_n, tmat):
    out_spec = pl.BlockSpec((1, N_EXPERTS, CAP, 1), lambda s: (s, 0, 0, 0))
    return pl.pallas_call(
        _topk_kernel,
        grid=(2,),
        in_specs=[
            pl.BlockSpec((N_CTX, 128), lambda s: (s, 0)),
            pl.BlockSpec((N_CTX, 128), lambda s: (0, 0)),
        ],
        out_specs=[pl.BlockSpec((1, N_EXPERTS, 1, CAP), lambda s: (s, 0, 0, 0)), out_spec],
        out_shape=[jax.ShapeDtypeStruct((2, N_EXPERTS, 1, CAP), jnp.int32),
                   jax.ShapeDtypeStruct((2, N_EXPERTS, CAP, 1), F32)],
        compiler_params=_params(1),
        name="topk",
    )(aff_n, tmat)


N_FT = FF // TF
GATHER_ROWS = 2 * CAP
ROWS_PER_STEP = GATHER_ROWS // N_FT


def _ffn_kernel(cast_next, idx_ref, h2_hbm, g_ref, wg_ref, wu_ref, wd_ref, *refs):
    if cast_next:
        win_ref, ye_ref, wbf_ref, xg, xb, acc, sems = refs
        wbf_ref[...] = win_ref[...].astype(BF16)
    else:
        ye_ref, xg, xb, acc, sems = refs
    e = pl.program_id(0)
    f = pl.program_id(1)
    slot = e % 2

    def row_copy(expert, s, p, dst_slot):
        row = idx_ref[(s * N_EXPERTS + expert) * CAP + p] + s * N_CTX
        src = h2_hbm.at[pl.ds(pl.multiple_of(row * SLAB, SLAB), SLAB), :]
        dst = xg.at[dst_slot, pl.ds(pl.multiple_of((s * CAP + p) * SLAB, SLAB), SLAB), :]
        return pltpu.make_async_copy(src, dst, sems.at[dst_slot])

    def slot_wait(dst_slot):
        pltpu.make_async_copy(h2_hbm.at[pl.ds(0, GATHER_ROWS * SLAB), :], xg.at[dst_slot],
                              sems.at[dst_slot]).wait()

    @pl.when(jnp.logical_and(e == 0, f == 0))
    def _():
        def issue(r, carry):
            row_copy(0, r // CAP, r % CAP, 0).start()
            return carry
        lax.fori_loop(0, GATHER_ROWS, issue, 0)

    @pl.when(f == 0)
    def _():
        slot_wait(slot)
        xb[...] = _from_slabs(xg.at[slot]).astype(BF16)
        acc[...] = jnp.zeros_like(acc)

    nxt = jnp.minimum(e + 1, N_EXPERTS - 1)
    s_nxt = f // (N_FT // 2)
    p0 = (f % (N_FT // 2)) * ROWS_PER_STEP
    for u in range(ROWS_PER_STEP):
        row_copy(nxt, s_nxt, p0 + u, 1 - slot).start()

    x = xb[...]
    hg = _dot(x, wg_ref[...].astype(BF16))
    hu = _dot(x, wu_ref[...].astype(BF16))
    hdn = ((hg * _sigmoid(hg)) * hu).astype(BF16)
    acc[...] += _dot(hdn, wd_ref[...].astype(BF16))

    @pl.when(f == N_FT - 1)
    def _():
        for s in range(2):
            _to_slabs(ye_ref.at[s], acc[s * CAP:(s + 1) * CAP, :] * g_ref[s])

    @pl.when(jnp.logical_and(e == N_EXPERTS - 1, f == N_FT - 1))
    def _():
        slot_wait(1 - slot)


def _expert_ffn(l, idx_flat, h2s, gates, w_gate, w_up, w_down, w_in):
    cast_next = l + 1 < DEPTH
    n_steps = N_EXPERTS * N_FT
    in_specs = [
        pl.BlockSpec(memory_space=pl.ANY),
        pl.BlockSpec((2, None, CAP, 1), lambda e, f, idx: (0, e, 0, 0)),
        pl.BlockSpec((None, None, D, TF), lambda e, f, idx: (l, e, 0, f)),
        pl.BlockSpec((None, None, D, TF), lambda e, f, idx: (l, e, 0, f)),
        pl.BlockSpec((None, None, TF, D), lambda e, f, idx: (l, e, f, 0)),
    ]
    args = [idx_flat, h2s, gates, w_gate, w_up, w_down]
    out_specs = [pl.BlockSpec((2, None, CAP * SLAB, 128), lambda e, f, idx: (0, e, 0, 0))]
    out_shape = [jax.ShapeDtypeStruct((2, N_EXPERTS, CAP * SLAB, 128), F32)]
    if cast_next:
        cols = IN_COLS // n_steps
        in_specs.append(pl.BlockSpec((None, D, cols), lambda e, f, idx: (l + 1, 0, e * N_FT + f)))
        args.append(w_in)
        out_specs.append(pl.BlockSpec((D, cols), lambda e, f, idx: (0, e * N_FT + f)))
        out_shape.append(jax.ShapeDtypeStruct((D, IN_COLS), BF16))
    grid_spec = pltpu.PrefetchScalarGridSpec(
        num_scalar_prefetch=1,
        grid=(N_EXPERTS, N_FT),
        in_specs=in_specs,
        out_specs=out_specs,
        scratch_shapes=[
            pltpu.VMEM((2, GATHER_ROWS * SLAB, 128), F32),
            pltpu.VMEM((GATHER_ROWS, D), BF16),
            pltpu.VMEM((GATHER_ROWS, D), F32),
            pltpu.SemaphoreType.DMA((2,)),
        ],
    )
    outs = pl.pallas_call(
        functools.partial(_ffn_kernel, cast_next),
        grid_spec=grid_spec,
        out_shape=out_shape,
        compiler_params=_params(2),
        name="expert_ffn",
    )(*args)
    return (outs[0], outs[1]) if cast_next else (outs[0], None)


SCATTER_UNROLL = 8


def _combine_kernel(idx_ref, ye_ref, y_ref):
    s = pl.program_id(0)
    e = pl.program_id(1)

    @pl.when(e == 0)
    def _():
        y_ref[...] = jnp.zeros_like(y_ref)

    base = (s * N_EXPERTS + e) * CAP

    for p0 in range(0, CAP, SCATTER_UNROLL):
        rows = [idx_ref[base + p0 + u] for u in range(SCATTER_UNROLL)]
        tiles = [pl.ds(pl.multiple_of(r * SLAB, SLAB), SLAB) for r in rows]
        vals = [y_ref[tiles[u], :] + ye_ref[pl.ds((p0 + u) * SLAB, SLAB), :] for u in range(SCATTER_UNROLL)]
        for u in range(SCATTER_UNROLL):
            y_ref[tiles[u], :] = vals[u]


def _combine(idx_flat, ye):
    grid_spec = pltpu.PrefetchScalarGridSpec(
        num_scalar_prefetch=1,
        grid=(2, N_EXPERTS),
        in_specs=[pl.BlockSpec((None, None, CAP * SLAB, 128), lambda s, e, idx: (s, e, 0, 0))],
        out_specs=pl.BlockSpec((N_CTX * SLAB, 128), lambda s, e, idx: (s, 0)),
    )
    return pl.pallas_call(
        _combine_kernel,
        grid_spec=grid_spec,
        out_shape=jax.ShapeDtypeStruct((N_TOK * SLAB, 128), F32),
        compiler_params=_params(2),
        name="combine",
    )(idx_flat, ye)


def _final_kernel(x_ref, y_ref, mod_ref, w_ref, op_ref, os_ref):
    x = x_ref[...] + mod_ref[0][:, 5 * D:6 * D] * _from_slabs(y_ref)
    out = _rms(x) * w_ref[...]
    is_ctx = pl.program_id(0) < N_CTX // FINAL_ROWS

    @pl.when(is_ctx)
    def _():
        op_ref[...] = out

    @pl.when(jnp.logical_not(is_ctx))
    def _():
        os_ref[...] = out


FINAL_ROWS = 2 * TM


def _final(x1, y, mod, final_norm_w):
    ft = FINAL_ROWS
    ctx_steps = N_CTX // ft
    tile = pl.BlockSpec((ft, D), lambda i: (i, 0))
    return pl.pallas_call(
        _final_kernel,
        grid=(N_TOK // ft,),
        in_specs=[tile, pl.BlockSpec((ft * SLAB, 128), lambda i: (i, 0)),
                  pl.BlockSpec((1, 1, N_MOD * D), lambda i: ((DEPTH - 1) * 8 + _mod_row(i * (ft // TM)), 0, 0)),
                  pl.BlockSpec((1, D), lambda i: (0, 0))],
        out_specs=[pl.BlockSpec((ft, D), lambda i: (jnp.minimum(i, ctx_steps - 1), 0)),
                   pl.BlockSpec((ft, D), lambda i: (jnp.maximum(i - ctx_steps, 0), 0))],
        out_shape=[jax.ShapeDtypeStruct((N_CTX, D), F32), jax.ShapeDtypeStruct((N_LAT, D), F32)],
        compiler_params=_params(1),
        name="final_norm",
    )(x1, y, mod, final_norm_w.reshape(1, D))


def _rope_tables():
    t = np.arange(T_LAT)
    row = (t // GRID_W).astype(np.float32)
    col = (t % GRID_W).astype(np.float32)
    inv = jnp.asarray(ROPE_BASE, F32) ** (-jnp.arange(N_ROPE_FREQ, dtype=F32) / N_ROPE_FREQ)
    ang_r = jnp.asarray(row)[:, None] * inv
    ang_c = jnp.asarray(col)[:, None] * inv
    def group(ang):
        return jnp.concatenate([ang, ang], axis=-1)
    ang = jnp.concatenate([group(ang_r), group(ang_c), group(ang_r), group(ang_c)], axis=-1)
    sign = np.where(np.arange(HD) % 32 < 16, -1.0, 1.0).astype(np.float32)
    return jnp.cos(ang), jnp.sin(ang) * sign


def _index_table():
    t = np.arange(N_CTX)
    tm = np.zeros((N_CTX, 128), np.float32)
    tm[:, IDX_LANE] = t // 64
    tm[:, IDX_LANE + 1] = t % 64
    return jnp.asarray(tm, BF16)


def kernel(x_prompt, x_sample, c, cache_attn_k, cache_attn_v, state_ret_fwd, state_ret_bwd, c_ctx, w_ada, b_ada, norm1_w, norm2_w, w_in, attn_lambda, attn_subln_w, conv_w, ret_decay_fwd, ret_decay_bwd, ret_norm_w, w_br_attn, w_br_conv, w_br_ret, w_out, w_router, w_exp_gate, w_exp_up, w_exp_down, final_norm_w):
    x = (x_prompt.reshape(N_CTX, D), x_sample.reshape(N_LAT, D))
    cvec = jnp.concatenate([c_ctx[None, :], c, jnp.zeros((3, D), F32)], axis=0)
    mod = _modulation(cvec, w_ada, b_ada).reshape(DEPTH * 8, 1, N_MOD * D)

    w_in_bf = w_in[0].astype(BF16)
    w_br_attn_bf = w_br_attn.astype(BF16)
    w_br_conv_bf = w_br_conv.astype(BF16)
    w_br_ret_bf = w_br_ret.astype(BF16)
    w_out_bf = w_out.astype(BF16)
    w_router_pad = jnp.pad(w_router, ((0, 0), (0, 0), (0, 128 - N_EXPERTS))).astype(BF16)
    norm1 = norm1_w.reshape(DEPTH, 1, D)
    norm2 = norm2_w.reshape(DEPTH, 1, D)
    subln = attn_subln_w.reshape(DEPTH, 1, HD)
    decay_f = ret_decay_fwd.reshape(DEPTH * HEADS, 1, 1)
    decay_b = ret_decay_bwd.reshape(DEPTH * HEADS, 1, 1)
    ret_nw = ret_norm_w.reshape(DEPTH * HEADS, 1, HD)
    cos, sin = _rope_tables()
    tmat = _index_table()

    y = None
    new_k = new_v = new_sf = new_sb = None
    for l in range(DEPTH):
        lam_init = 0.8 - 0.6 * math.exp(-0.3 * l)
        if y is None:
            zb, zq, new_k, new_v = _inproj(l, x, None, mod, norm1, w_in_bf, new_k, new_v)
        else:
            zb, zq, new_k, new_v, x = _inproj(l, x, y, mod, norm1, w_in_bf, new_k, new_v)
        ao, ao_lat = _attention(l, lam_init, zb, zq, cache_attn_k, cache_attn_v, cos, sin, attn_lambda, subln)
        ro, ro_lat, new_sf, new_sb = _retention(l, zb, state_ret_fwd, state_ret_bwd, decay_f, decay_b, ret_nw,
                                                new_sf, new_sb)
        x, h2s, aff_n = _merge(l, x, ao, ao_lat, ro, ro_lat, zb, mod, conv_w, w_br_attn_bf, w_br_conv_bf,
                               w_br_ret_bf, w_out_bf, norm2, w_router_pad)
        idx, gates = _topk(aff_n, tmat)
        idx_flat = idx.reshape(2 * N_EXPERTS * CAP)
        ye, w_in_bf = _expert_ffn(l, idx_flat, h2s, gates, w_exp_gate, w_exp_up, w_exp_down, w_in)
        y = _combine(idx_flat, ye)

    y_prompt, y_sample = _final(x, y, mod, final_norm_w)
    return (y_prompt.reshape(N_CTX_SEQ, T_CTX, D), y_sample.reshape(N_LAT_SEQ, T_LAT, D), new_k, new_v,
            new_sf, new_sb)
```

```python
import functools
import math

import jax
import jax.numpy as jnp
import numpy as np
from jax import lax
from jax.experimental import pallas as pl
from jax.experimental.pallas import tpu as pltpu

F32 = jnp.float32
BF16 = jnp.bfloat16

D = 1024
DEPTH = 2
N_CTX_SEQ = 16
T_CTX = 256
N_LAT_SEQ = 4
T_LAT = 1024
PAST = 256
N_CTX = N_CTX_SEQ * T_CTX
N_LAT = N_LAT_SEQ * T_LAT
N_TOK = N_CTX + N_LAT
TM = 256
N_TILES = N_TOK // TM
CTX_TILES = N_CTX // TM
LAT_TILES_PER_SEQ = T_LAT // TM
HEADS = 4
HD = 128
GRID_W = 64
N_ROPE_FREQ = 16
ROPE_BASE = 10000.0
IN_COLS = 8192
N_MOD = 6
N_EXPERTS = 16
CAP = 512
FF = 2048
TF = 512
EPS = 1e-6
RET_SCALE = HD ** -0.5
ATTN_SCALE = 64 ** -0.5
SLAB = D // 128
VMEM_LIMIT = 56 * 1024 * 1024

QB, KB, VB = 0, 4, 8
RQB, RKB, RVB, RGB = 24, 28, 32, 36
CONV_B_COL, CONV_U_COL = 3, 4
MERGE_GATE_COL = 5
MERGE_SUB = 2


def _sigmoid(x):
    return 0.5 * jnp.tanh(0.5 * x) + 0.5


def _log_sigmoid(x):
    return jnp.minimum(x, 0.0) - jnp.log(1.0 + jnp.exp(-jnp.abs(x)))


def _rms(x):
    return x * lax.rsqrt(jnp.mean(x * x, axis=-1, keepdims=True) + EPS)


def _dot(a, b):
    return jnp.dot(a, b, preferred_element_type=F32)


def _dot_nt(a, b):
    return lax.dot_general(a, b, (((1,), (1,)), ((), ())), preferred_element_type=F32)


def _dot_tn(a, b):
    return lax.dot_general(a, b, (((0,), (0,)), ((), ())), preferred_element_type=F32)


def _to_slabs(ref, x):
    n = x.shape[0]
    for s in range(SLAB):
        ref[pl.ds(s, n, stride=SLAB), :] = x[:, s * 128:(s + 1) * 128]


def _from_slabs(ref):
    n = ref.shape[0] // SLAB
    return jnp.concatenate([ref[pl.ds(s, n, stride=SLAB), :] for s in range(SLAB)], axis=-1)


def _mod_row(i):
    return jnp.where(i < CTX_TILES, 0, 1 + (i - CTX_TILES) // LAT_TILES_PER_SEQ)


def _params(n_axes):
    return pltpu.CompilerParams(
        dimension_semantics=("arbitrary",) * n_axes, vmem_limit_bytes=VMEM_LIMIT)


def _mod_kernel(c_ref, w_ref, b_ref, o_ref):
    c = c_ref[...]
    s = (c * _sigmoid(c)).astype(BF16)
    o_ref[...] = _dot(s, w_ref[...].astype(BF16)) + b_ref[...]


def _modulation(cvec, w_ada, b_ada):
    tn = 1024
    return pl.pallas_call(
        _mod_kernel,
        grid=(DEPTH, N_MOD * D // tn),
        in_specs=[
            pl.BlockSpec((8, D), lambda l, n: (0, 0)),
            pl.BlockSpec((None, D, tn), lambda l, n: (l, 0, n)),
            pl.BlockSpec((None, 1, tn), lambda l, n: (l, 0, n)),
        ],
        out_specs=pl.BlockSpec((None, 8, tn), lambda l, n: (l, 0, n)),
        out_shape=jax.ShapeDtypeStruct((DEPTH, 8, N_MOD * D), F32),
        compiler_params=_params(2),
        name="modulation",
    )(cvec, w_ada, b_ada.reshape(DEPTH, 1, N_MOD * D))


def _inproj_kernel(has_y, x_pair, n_prev, *refs):
    refs = list(refs)
    x_ref = refs.pop(0)
    if x_pair:
        xs_ref = refs.pop(0)
    if has_y:
        y_ref, modp_ref = refs.pop(0), refs.pop(0)
    mod_ref, n1_ref, w_ref = refs.pop(0), refs.pop(0), refs.pop(0)
    if n_prev:
        kp_ref, vp_ref = refs.pop(0), refs.pop(0)
    zb_ref, zq_ref, kn_ref, vn_ref = refs[0:4]
    i = pl.program_id(0)

    def normalize():
        x = x_ref[...]
        if x_pair:
            x = jnp.where(i < N_TILES - CTX_TILES, xs_ref[...], x)
        if has_y:
            x = x + modp_ref[0][:, 5 * D:6 * D] * _from_slabs(y_ref)
            refs[4][...] = x
        m = mod_ref[0]
        shift1 = m[:, 0:D]
        scale1 = m[:, D:2 * D]
        return ((_rms(x) * n1_ref[...]) * (1.0 + scale1) + shift1).astype(BF16)

    def project(h):
        cw = 1024
        for c in range(IN_COLS // cw):
            z = _dot(h, w_ref[:, c * cw:(c + 1) * cw])
            if c == CONV_U_COL // 2:
                z = jnp.concatenate([z[:, 0:512] * z[:, 512:cw], z[:, 512:cw]], axis=-1)
            elif c == RGB // 8:
                g = z[:, 512:cw]
                z = jnp.concatenate([z[:, 0:512], g * _sigmoid(g)], axis=-1)
            elif c >= MERGE_GATE_COL:
                z = _sigmoid(z)
            zb_ref[:, c * cw:(c + 1) * cw] = z.astype(BF16)
            if c == 0:
                zq_ref[...] = z
                for hh in range(HEADS):
                    kn_ref[n_prev, hh] = z[:, 512 + hh * HD:512 + (hh + 1) * HD]
            if c == 1:
                for hh in range(HEADS):
                    vn_ref[n_prev, hh] = z[:, hh * HD:(hh + 1) * HD]
        if n_prev:
            kn_ref[0:n_prev] = kp_ref[...]
            vn_ref[0:n_prev] = vp_ref[...]

    project(normalize())


def _inproj(l, x, y, mod, norm1_w, w_in_bf, k_prev, v_prev):
    has_y = y is not None
    t = lambda i: (i + CTX_TILES) % N_TILES
    n_lat = N_TILES - CTX_TILES
    norm_tile = pl.BlockSpec((TM, D), lambda i: (t(i), 0))
    ctx_i = lambda i: jnp.maximum(i - n_lat, 0)
    kv_spec = lambda m: pl.BlockSpec((None, m, HEADS, T_CTX, HD), lambda i: (ctx_i(i), 0, 0, 0, 0))
    kv_shape = jax.ShapeDtypeStruct((N_CTX_SEQ, l + 1, HEADS, T_CTX, HD), F32)
    x_pair = isinstance(x, tuple)
    if x_pair:
        in_specs = [pl.BlockSpec((TM, D), lambda i: (jnp.maximum(i - n_lat, 0), 0)),
                    pl.BlockSpec((TM, D), lambda i: (jnp.minimum(i, n_lat - 1), 0))]
        args = list(x)
    else:
        in_specs = [norm_tile]
        args = [x]
    if has_y:
        in_specs += [pl.BlockSpec((TM * SLAB, 128), lambda i: (t(i), 0)),
                     pl.BlockSpec((1, 1, N_MOD * D), lambda i: ((l - 1) * 8 + _mod_row(t(i)), 0, 0))]
        args += [y, mod]
    in_specs += [
        pl.BlockSpec((1, 1, N_MOD * D), lambda i: (l * 8 + _mod_row(t(i)), 0, 0)),
        pl.BlockSpec((None, 1, D), lambda i: (l, 0, 0)),
        pl.BlockSpec((D, IN_COLS), lambda i: (0, 0), pipeline_mode=pl.Buffered(1)),
    ]
    args += [mod, norm1_w, w_in_bf]
    if l:
        in_specs += [kv_spec(l), kv_spec(l)]
        args += [k_prev, v_prev]
    out_specs = [pl.BlockSpec((TM, IN_COLS), lambda i: (t(i), 0)),
                 pl.BlockSpec((TM, 1024), lambda i: (t(i), 0)),
                 kv_spec(l + 1), kv_spec(l + 1)]
    out_shape = [jax.ShapeDtypeStruct((N_TOK, IN_COLS), BF16),
                 jax.ShapeDtypeStruct((N_TOK, 1024), F32),
                 kv_shape, kv_shape]
    if has_y:
        out_specs.append(norm_tile)
        out_shape.append(jax.ShapeDtypeStruct((N_TOK, D), F32))
    return pl.pallas_call(
        functools.partial(_inproj_kernel, has_y, x_pair, l),
        grid=(N_TILES,),
        in_specs=in_specs,
        out_specs=out_specs,
        out_shape=out_shape,
        compiler_params=_params(1),
        name="inproj",
    )(*args)


def _lambda(lam_ref, lam_init):
    lv = lam_ref[...]
    a = jnp.sum(lv[0:1] * lv[1:2], axis=-1, keepdims=True)
    b = jnp.sum(lv[2:3] * lv[3:4], axis=-1, keepdims=True)
    return jnp.exp(a) - jnp.exp(b) + lam_init


def _diff_attention(lam_init, qs, keys, vs, lam, sw, keys_transposed):
    lane = lax.broadcasted_iota(jnp.int32, qs[0].shape, 1)
    score = _dot if keys_transposed else _dot_nt
    maps = [(h, jnp.where(keep, q, jnp.zeros_like(q)))
            for h, q in enumerate(qs) for keep in (lane < 64, lane >= 64)]
    s = [score(qm, keys[h]) for h, qm in maps]
    e = [jnp.exp(x - jnp.max(x, axis=-1, keepdims=True)) for x in s]
    r = [1.0 / jnp.sum(x, axis=-1, keepdims=True) for x in e]
    a = [(e[2 * h] * r[2 * h] - e[2 * h + 1] * (lam * r[2 * h + 1])).astype(BF16) for h in range(len(qs))]
    o = [_dot(a[h], vs[h]) for h in range(len(qs))]
    return [(_rms(x) * sw) * (1.0 - lam_init) for x in o]


def _scaled_q(q):
    return (q.astype(F32) * ATTN_SCALE).astype(BF16)


def _attn_ctx_kernel(lam_init, q_ref, k_ref, v_ref, lam_ref, sw_ref, o_ref):
    lam = _lambda(lam_ref, lam_init)
    sls = [slice(h * HD, (h + 1) * HD) for h in range(HEADS)]
    outs = _diff_attention(lam_init, [_scaled_q(q_ref[:, sl]) for sl in sls], [k_ref[:, sl] for sl in sls],
                           [v_ref[:, sl] for sl in sls], lam, sw_ref[...], False)
    for sl, o in zip(sls, outs):
        o_ref[:, sl] = o.astype(BF16)


def _rope(x, cos, sin_signed):
    lane = lax.broadcasted_iota(jnp.int32, x.shape, 1)
    partner = jnp.where(lane % 32 < 16, pltpu.roll(x, 112, 1), pltpu.roll(x, 16, 1))
    return x * cos + partner * sin_signed


def _attn_lat_kernel(lam_init, q_ref, k_ref, v_ref, ck_ref, cv_ref, cos_ref, sin_ref,
                     cosq_ref, sinq_ref, lam_ref, sw_ref, o_ref, k_t, vall):
    @pl.when(pl.program_id(1) == 0)
    def _():
        for h in range(HEADS):
            sl = slice(h * HD, (h + 1) * HD)
            k_t[h, :, 0:PAST] = ck_ref[h].T.astype(BF16)
            k_t[h, :, PAST:] = _rope(k_ref[:, sl], cos_ref[...], sin_ref[...]).T.astype(BF16)
            vall[h, 0:PAST, :] = cv_ref[h].astype(BF16)
            vall[h, PAST:, :] = v_ref[:, sl]

    lam = _lambda(lam_ref, lam_init)
    sls = [slice(h * HD, (h + 1) * HD) for h in range(HEADS)]
    qs = [_scaled_q(_rope(q_ref[:, sl], cosq_ref[...], sinq_ref[...])) for sl in sls]
    outs = _diff_attention(lam_init, qs, [k_t[h] for h in range(HEADS)], [vall[h] for h in range(HEADS)],
                           lam, sw_ref[...], True)
    for sl, o in zip(sls, outs):
        o_ref[:, sl] = o.astype(BF16)


def _attention(l, lam_init, zb, zq, cache_k, cache_v, cos, sin, attn_lambda, subln_w):
    width = HEADS * HD
    ao = pl.pallas_call(
        functools.partial(_attn_ctx_kernel, lam_init),
        grid=(N_CTX_SEQ,),
        in_specs=[
            pl.BlockSpec((T_CTX, width), lambda b: (b, QB // HEADS)),
            pl.BlockSpec((T_CTX, width), lambda b: (b, KB // HEADS)),
            pl.BlockSpec((T_CTX, width), lambda b: (b, VB // HEADS)),
            pl.BlockSpec((None, 4, 64), lambda b: (l, 0, 0)),
            pl.BlockSpec((None, 1, HD), lambda b: (l, 0, 0)),
        ],
        out_specs=pl.BlockSpec((T_CTX, width), lambda b: (b, 0)),
        out_shape=jax.ShapeDtypeStruct((N_CTX, width), BF16),
        compiler_params=_params(1),
        name="attn_ctx",
    )(zb, zb, zb, attn_lambda, subln_w)

    lat_row = lambda b, j: CTX_TILES + LAT_TILES_PER_SEQ * b + j
    seq_row = lambda b: N_CTX // T_LAT + b
    ao_lat = pl.pallas_call(
        functools.partial(_attn_lat_kernel, lam_init),
        grid=(N_LAT_SEQ, LAT_TILES_PER_SEQ),
        in_specs=[
            pl.BlockSpec((TM, width), lambda b, j: (lat_row(b, j), QB // HEADS)),
            pl.BlockSpec((T_LAT, width), lambda b, j: (seq_row(b), KB // HEADS)),
            pl.BlockSpec((T_LAT, width), lambda b, j: (seq_row(b), VB // HEADS)),
            pl.BlockSpec((None, None, HEADS, PAST, HD), lambda b, j: (b, l, 0, 0, 0)),
            pl.BlockSpec((None, None, HEADS, PAST, HD), lambda b, j: (b, l, 0, 0, 0)),
            pl.BlockSpec((T_LAT, HD), lambda b, j: (0, 0)),
            pl.BlockSpec((T_LAT, HD), lambda b, j: (0, 0)),
            pl.BlockSpec((TM, HD), lambda b, j: (j, 0)),
            pl.BlockSpec((TM, HD), lambda b, j: (j, 0)),
            pl.BlockSpec((None, 4, 64), lambda b, j: (l, 0, 0)),
            pl.BlockSpec((None, 1, HD), lambda b, j: (l, 0, 0)),
        ],
        out_specs=pl.BlockSpec((TM, width), lambda b, j: (LAT_TILES_PER_SEQ * b + j, 0)),
        out_shape=jax.ShapeDtypeStruct((N_LAT, width), BF16),
        scratch_shapes=[pltpu.VMEM((HEADS, HD, PAST + T_LAT), BF16),
                        pltpu.VMEM((HEADS, PAST + T_LAT, HD), BF16)],
        compiler_params=_params(2),
        name="attn_lat",
    )(zq, zq, zb, cache_k, cache_v, cos, sin, cos, sin, attn_lambda, subln_w)
    return ao, ao_lat


def _decay_matrix(lgf, lgb, row0, tq, tk):
    i = row0 + lax.broadcasted_iota(jnp.int32, (tq, tk), 0)
    j = lax.broadcasted_iota(jnp.int32, (tq, tk), 1)
    d = (i - j).astype(F32)
    fwd = jnp.where(d >= 0.0, jnp.exp(lgf * jnp.maximum(d, 0.0)), 0.0)
    bwd = jnp.where(d <= 0.0, jnp.exp(lgb * jnp.maximum(-d, 0.0)), 0.0)
    return fwd + bwd


def _ret_finish(o, g, nw):
    return (g.astype(F32) * (_rms(o) * nw)).astype(BF16)


def _ret_ctx_kernel(n_prev, *refs):
    refs = list(refs)
    q_ref, k_ref, v_ref, g_ref, df_ref, db_ref, nw_ref = refs[0:7]
    refs = refs[7:]
    if n_prev:
        sfp_ref, sbp_ref = refs.pop(0), refs.pop(0)
    o_ref, sf_ref, sb_ref, dmat = refs
    @pl.when(pl.program_id(0) == 0)
    def _():
        for h in range(HEADS):
            dmat[h] = RET_SCALE * _decay_matrix(_log_sigmoid(df_ref[h]), _log_sigmoid(db_ref[h]), 0, T_CTX, T_CTX)

    if n_prev:
        sf_ref[0:n_prev] = sfp_ref[...]
        sb_ref[0:n_prev] = sbp_ref[...]
    j = lax.broadcasted_iota(jnp.int32, (T_CTX, 1), 0).astype(F32)
    heads = range(HEADS)
    sls = [slice(h * HD, (h + 1) * HD) for h in heads]
    q = [q_ref[:, sl] for sl in sls]
    k = [k_ref[:, sl] for sl in sls]
    v = [v_ref[:, sl] for sl in sls]
    s = [(_dot_nt(q[h], k[h]) * dmat[h]).astype(BF16) for h in heads]
    o = [_dot(s[h], v[h]) for h in heads]
    lgf = [_log_sigmoid(df_ref[h]) for h in heads]
    lgb = [_log_sigmoid(db_ref[h]) for h in heads]
    kf = [k[h].astype(F32) * RET_SCALE for h in heads]
    kfw = [(kf[h] * jnp.exp(lgf[h] * (T_CTX - 1.0 - j))).astype(BF16) for h in heads]
    kbw = [(kf[h] * jnp.exp(lgb[h] * j)).astype(BF16) for h in heads]
    for h in heads:
        sf_ref[n_prev, h] = _dot_tn(kfw[h], v[h])
        sb_ref[n_prev, h] = _dot_tn(kbw[h], v[h])
    for h in heads:
        o_ref[:, sls[h]] = _ret_finish(o[h], g_ref[:, sls[h]], nw_ref[h])


N_DCHUNK = 2 * LAT_TILES_PER_SEQ - 1


def _ret_lat_kernel(q_ref, k_ref, v_ref, g_ref, s0f_ref, s0b_ref, df_ref, db_ref, nw_ref, o_ref, strip, k_t):
    b = pl.program_id(0)
    j = pl.program_id(1)

    @pl.when(jnp.logical_and(b == 0, j == 0))
    def _():
        for h in range(HEADS):
            lgf = _log_sigmoid(df_ref[h])
            lgb = _log_sigmoid(db_ref[h])
            for c in range(N_DCHUNK):
                strip[h, c] = RET_SCALE * _decay_matrix(lgf, lgb, T_LAT - TM - c * TM, TM, TM)

    @pl.when(j == 0)
    def _():
        for h in range(HEADS):
            k_t[h] = k_ref[:, h * HD:(h + 1) * HD].astype(F32).T.astype(BF16)

    i = (j * TM + lax.broadcasted_iota(jnp.int32, (TM, 1), 0)).astype(F32)
    c0 = LAT_TILES_PER_SEQ - 1 - j
    heads = range(HEADS)
    sls = [slice(h * HD, (h + 1) * HD) for h in heads]
    q = [q_ref[:, sl] for sl in sls]
    dmat = [jnp.concatenate([strip[h, c0 + c] for c in range(LAT_TILES_PER_SEQ)], axis=-1) for h in heads]
    s = [(_dot(q[h], k_t[h]) * dmat[h]).astype(BF16) for h in heads]
    o = [_dot(s[h], v_ref[:, sls[h]]) for h in heads]
    of = [_dot(q[h], s0f_ref[h].astype(BF16)) for h in heads]
    ob = [_dot(q[h], s0b_ref[h].astype(BF16)) for h in heads]
    for h in heads:
        tot = o[h] + jnp.exp(_log_sigmoid(df_ref[h]) * (i + 1.0)) * of[h]
        tot = tot + jnp.exp(_log_sigmoid(db_ref[h]) * (T_LAT - i)) * ob[h]
        o_ref[:, sls[h]] = _ret_finish(tot, g_ref[:, sls[h]], nw_ref[h])


def _retention(l, zb, state_f, state_b, decay_f, decay_b, ret_norm_w, sf_prev, sb_prev):
    width = HEADS * HD
    dec1 = pl.BlockSpec((HEADS, 1, 1), lambda b: (l, 0, 0))
    nw1 = pl.BlockSpec((HEADS, 1, HD), lambda b: (l, 0, 0))
    st_spec = lambda n: pl.BlockSpec((None, n, HEADS, HD, HD), lambda b: (b, 0, 0, 0, 0))
    st_shape = jax.ShapeDtypeStruct((N_CTX_SEQ, l + 1, HEADS, HD, HD), F32)
    in_specs = [
        pl.BlockSpec((T_CTX, width), lambda b: (b, RQB // HEADS)),
        pl.BlockSpec((T_CTX, width), lambda b: (b, RKB // HEADS)),
        pl.BlockSpec((T_CTX, width), lambda b: (b, RVB // HEADS)),
        pl.BlockSpec((T_CTX, width), lambda b: (b, RGB // HEADS)),
        dec1, dec1, nw1,
    ]
    args = [zb, zb, zb, zb, decay_f, decay_b, ret_norm_w]
    if l:
        in_specs += [st_spec(l), st_spec(l)]
        args += [sf_prev, sb_prev]
    ro, sf, sb = pl.pallas_call(
        functools.partial(_ret_ctx_kernel, l),
        grid=(N_CTX_SEQ,),
        in_specs=in_specs,
        out_specs=[pl.BlockSpec((T_CTX, width), lambda b: (b, 0)), st_spec(l + 1), st_spec(l + 1)],
        out_shape=[jax.ShapeDtypeStruct((N_CTX, width), BF16), st_shape, st_shape],
        scratch_shapes=[pltpu.VMEM((HEADS, T_CTX, T_CTX), F32)],
        compiler_params=_params(1),
        name="ret_ctx",
    )(*args)

    lat_row = lambda b, j: CTX_TILES + LAT_TILES_PER_SEQ * b + j
    seq_row = lambda b: N_CTX // T_LAT + b
    dec2 = pl.BlockSpec((HEADS, 1, 1), lambda b, j: (l, 0, 0))
    nw2 = pl.BlockSpec((HEADS, 1, HD), lambda b, j: (l, 0, 0))
    s0_spec = pl.BlockSpec((None, None, HEADS, HD, HD), lambda b, j: (b, l, 0, 0, 0))
    ro_lat = pl.pallas_call(
        _ret_lat_kernel,
        grid=(N_LAT_SEQ, LAT_TILES_PER_SEQ),
        in_specs=[
            pl.BlockSpec((TM, width), lambda b, j: (lat_row(b, j), RQB // HEADS)),
            pl.BlockSpec((T_LAT, width), lambda b, j: (seq_row(b), RKB // HEADS)),
            pl.BlockSpec((T_LAT, width), lambda b, j: (seq_row(b), RVB // HEADS)),
            pl.BlockSpec((TM, width), lambda b, j: (lat_row(b, j), RGB // HEADS)),
            s0_spec, s0_spec, dec2, dec2, nw2,
        ],
        out_specs=pl.BlockSpec((TM, width), lambda b, j: (LAT_TILES_PER_SEQ * b + j, 0)),
        out_shape=jax.ShapeDtypeStruct((N_LAT, width), BF16),
        scratch_shapes=[pltpu.VMEM((HEADS, N_DCHUNK, TM, TM), F32), pltpu.VMEM((HEADS, HD, T_LAT), BF16)],
        compiler_params=_params(2),
        name="ret_lat",
    )(zb, zb, zb, zb, state_f, state_b, decay_f, decay_b, ret_norm_w)
    return ro, ro_lat, sf, sb


def _merge_kernel(x_pair, *refs):
    refs = list(refs)
    x_ref = refs.pop(0)
    xs_ref = refs.pop(0) if x_pair else None
    (aoc_ref, aol_ref, roc_ref, rol_ref, cb_ref, u_ref, up_ref, un_ref,
     mg0_ref, mg1_ref, mg2_ref, mod_ref, cw_ref, wa_ref, wc_ref, wr_ref, wo_ref, n2_ref, wrt_ref,
     x1_ref, h2_ref, aff_ref) = refs
    i = pl.program_id(0)
    is_ctx = i < CTX_TILES // MERGE_SUB
    m = mod_ref[0]
    gate1 = m[:, 2 * D:3 * D]
    shift2 = m[:, 3 * D:4 * D]
    scale2 = m[:, 4 * D:5 * D]
    cw = cw_ref[...]
    u_all = u_ref[...].astype(F32)
    r = lax.broadcasted_iota(jnp.int32, (TM, 512), 0)
    lane = lax.broadcasted_iota(jnp.int32, (TM, 128), 1)

    subs = range(MERGE_SUB)
    rows = [slice(sub * TM, (sub + 1) * TM) for sub in subs]

    def conv_out(sub):
        j = (i * MERGE_SUB + sub - CTX_TILES) % LAT_TILES_PER_SEQ
        seq_first = jnp.logical_or(is_ctx, j == 0)
        seq_last = jnp.logical_or(is_ctx, j == LAT_TILES_PER_SEQ - 1)
        u = u_all[rows[sub], :]
        up = up_ref[...].astype(F32)[15:16, :] if sub == 0 else u_all[sub * TM - 1:sub * TM, :]
        dn = un_ref[...].astype(F32)[0:1, :] if sub == MERGE_SUB - 1 else u_all[(sub + 1) * TM:(sub + 1) * TM + 1, :]
        up = up * jnp.where(seq_first, 0.0, 1.0)
        dn = dn * jnp.where(seq_last, 0.0, 1.0)
        u_prev = jnp.where(r == 0, up, pltpu.roll(u, 1, 0))
        u_next = jnp.where(r == TM - 1, dn, pltpu.roll(u, TM - 1, 0))
        conv = u_prev * cw[0:1, :] + u * cw[1:2, :] + u_next * cw[2:3, :]
        return (cb_ref[rows[sub], :].astype(F32) * conv).astype(BF16)

    ao = [jnp.where(is_ctx, aoc_ref[rw, :], aol_ref[rw, :]) for rw in rows]
    ro = [jnp.where(is_ctx, roc_ref[rw, :], rol_ref[rw, :]) for rw in rows]
    conv_o = [conv_out(sub) for sub in subs]
    b_attn = [_dot(ao[s], wa_ref[...]) for s in subs]
    b_conv = [_dot(conv_o[s], wc_ref[...]) for s in subs]
    b_ret = [_dot(ro[s], wr_ref[...]) for s in subs]
    merged = [(mg0_ref[rows[s], :].astype(F32) * b_attn[s] + mg1_ref[rows[s], :].astype(F32) * b_conv[s]
               + mg2_ref[rows[s], :].astype(F32) * b_ret[s]).astype(BF16) for s in subs]
    proj = [_dot(merged[s], wo_ref[...]) for s in subs]
    x1 = []
    for s in subs:
        x_in = x_ref[rows[s], :]
        if x_pair:
            x_in = jnp.where(is_ctx, x_in, xs_ref[rows[s], :])
        x1.append(x_in + gate1 * proj[s])
        x1_ref[rows[s], :] = x1[s]
    h2 = [(_rms(x1[s]) * n2_ref[...]) * (1.0 + scale2) + shift2 for s in subs]
    logits = [_dot(h2[s].astype(BF16), wrt_ref[...]) for s in subs]
    valid = lane < N_EXPERTS
    for s in subs:
        _to_slabs(h2_ref.at[pl.ds(s * TM * SLAB, TM * SLAB), :], h2[s])
        lmax = jnp.max(jnp.where(valid, logits[s], -jnp.inf), axis=-1, keepdims=True)
        e = jnp.where(valid, jnp.exp(logits[s] - lmax), 0.0)
        aff_ref[rows[s], :] = e * (1.0 / jnp.sum(e, axis=-1, keepdims=True))


def _merge(l, x, ao, ao_lat, ro, ro_lat, zb, mod, conv_w, w_br_attn, w_br_conv, w_br_ret, w_out, norm2_w,
           w_router_pad):
    mt = MERGE_SUB * TM
    n_steps = N_TOK // mt
    ctx_steps = N_CTX // mt
    n16 = N_TOK // 16
    ctx_br = pl.BlockSpec((mt, 512), lambda i: (jnp.minimum(i, ctx_steps - 1), 0))
    lat_br = pl.BlockSpec((mt, 512), lambda i: (jnp.maximum(i - ctx_steps, 0), 0))
    col = lambda c: pl.BlockSpec((mt, 512), lambda i: (i, c))
    halo_p = lambda c: pl.BlockSpec((16, 512), lambda i: (jnp.maximum(i * (mt // 16) - 1, 0), c))
    halo_n = lambda c: pl.BlockSpec((16, 512), lambda i: (jnp.minimum((i + 1) * (mt // 16), n16 - 1), c))
    mgs = lambda c: pl.BlockSpec((mt, D), lambda i: (i, c))
    wbr = pl.BlockSpec((None, 512, D), lambda i: (l, 0, 0))
    x_pair = isinstance(x, tuple)
    if x_pair:
        x_specs = [pl.BlockSpec((mt, D), lambda i: (jnp.minimum(i, ctx_steps - 1), 0)),
                   pl.BlockSpec((mt, D), lambda i: (jnp.maximum(i - ctx_steps, 0), 0))]
        x_args = list(x)
    else:
        x_specs = [pl.BlockSpec((mt, D), lambda i: (i, 0))]
        x_args = [x]
    return pl.pallas_call(
        functools.partial(_merge_kernel, x_pair),
        grid=(n_steps,),
        in_specs=x_specs + [
            ctx_br, lat_br, ctx_br, lat_br,
            col(CONV_B_COL), col(CONV_U_COL), halo_p(CONV_U_COL), halo_n(CONV_U_COL),
            mgs(MERGE_GATE_COL), mgs(MERGE_GATE_COL + 1), mgs(MERGE_GATE_COL + 2),
            pl.BlockSpec((1, 1, N_MOD * D), lambda i: (l * 8 + _mod_row(i * MERGE_SUB), 0, 0)),
            pl.BlockSpec((None, 3, 512), lambda i: (l, 0, 0)),
            wbr, wbr, wbr,
            pl.BlockSpec((None, D, D), lambda i: (l, 0, 0)),
            pl.BlockSpec((None, 1, D), lambda i: (l, 0, 0)),
            pl.BlockSpec((None, D, 128), lambda i: (l, 0, 0)),
        ],
        out_specs=[
            pl.BlockSpec((mt, D), lambda i: (i, 0)),
            pl.BlockSpec((mt * SLAB, 128), lambda i: (i, 0)),
            pl.BlockSpec((mt, 128), lambda i: (i, 0)),
        ],
        out_shape=[
            jax.ShapeDtypeStruct((N_TOK, D), F32),
            jax.ShapeDtypeStruct((N_TOK * SLAB, 128), F32),
            jax.ShapeDtypeStruct((N_TOK, 128), F32),
        ],
        compiler_params=_params(1),
        name="merge",
    )(*x_args, ao, ao_lat, ro, ro_lat, zb, zb, zb, zb, zb, zb, zb, mod, conv_w,
      w_br_attn, w_br_conv, w_br_ret, w_out, norm2_w, w_router_pad)


def _cumsum_lanes(x, tri):
    run = jnp.zeros((x.shape[0], 1), F32)
    outs = []
    for b in range(x.shape[1] // 128):
        cs = _dot(x[:, b * 128:(b + 1) * 128].astype(BF16), tri) + run
        run = cs[:, 127:128]
        outs.append(cs)
    return jnp.concatenate(outs, axis=-1)


GATE_LANE = (0, 16, 32)
IDX_LANE = 48
BISECT_GROUP = 4


def _topk_kernel(an_ref, tmat_ref, idx_ref, gate_ref):
    an = an_ref[...]
    a = an.T[0:N_EXPERTS, :]
    n = a.shape[1]
    kf = float(CAP)

    hi = an.astype(BF16).astype(F32)
    mid = (an - hi).astype(BF16).astype(F32)
    lo = ((an - hi) - mid).astype(BF16).astype(F32)
    table = (tmat_ref[...].astype(F32) + hi + pltpu.roll(mid, GATE_LANE[1], 1)
             + pltpu.roll(lo, GATE_LANE[2], 1)).astype(BF16)

    def count_gt(thr):
        return jnp.sum(jnp.where(a > thr, 1.0, 0.0), axis=-1, keepdims=True)

    def span(lo, hi):
        inside = jnp.logical_and(a > lo, a <= hi)
        cmax = jnp.max(jnp.where(inside, a, -jnp.inf), axis=-1, keepdims=True)
        cmin = jnp.min(jnp.where(inside, a, jnp.inf), axis=-1, keepdims=True)
        return cmax, cmin

    def cond(c):
        return jnp.logical_and(c[2] > 0, c[3] < 400)

    def body(c):
        lo, hi, _, it = c
        for _ in range(BISECT_GROUP):
            mid = 0.5 * (lo + hi)
            ge = count_gt(mid) >= kf
            lo = jnp.where(ge, mid, lo)
            hi = jnp.where(ge, hi, mid)
        cmax, cmin = span(lo, hi)
        open_rows = jnp.max(jnp.where(cmax != cmin, 1, 0))
        return lo, hi, open_rows, it + 1

    lo0 = jnp.full((N_EXPERTS, 1), -1.0, F32)
    hi0 = jnp.max(a, axis=-1, keepdims=True)
    cmax0, cmin0 = span(lo0, hi0)
    lo, hi, _, _ = lax.while_loop(
        cond, body, (lo0, hi0, jnp.max(jnp.where(cmax0 != cmin0, 1, 0)), jnp.int32(0)))
    thr, _ = span(lo, hi)

    r = lax.broadcasted_iota(jnp.int32, (128, 128), 0)
    c = lax.broadcasted_iota(jnp.int32, (128, 128), 1)
    tri = jnp.where(r <= c, 1.0, 0.0).astype(BF16)
    gt = a > thr
    eq = jnp.where(a == thr, 1.0, 0.0)
    need = kf - count_gt(thr)
    eq_before = _cumsum_lanes(eq, tri) - eq
    sel = jnp.where(jnp.logical_or(gt, jnp.logical_and(eq > 0.0, eq_before < need)), 1.0, 0.0)
    pos = _cumsum_lanes(sel, tri) - 1.0
    slot = jnp.where(sel > 0.0, pos, -1.0).astype(jnp.int32)

    p_iota = lax.broadcasted_iota(jnp.int32, (CAP, 1024), 0)
    lane = lax.broadcasted_iota(jnp.int32, (CAP, 128), 1)
    for e in range(N_EXPERTS):
        acc = jnp.zeros((CAP, 128), F32)
        for cb in range(n // 1024):
            onehot = jnp.where(p_iota == slot[e:e + 1, cb * 1024:(cb + 1) * 1024], 1.0, 0.0).astype(BF16)
            acc = acc + _dot(onehot, table[cb * 1024:(cb + 1) * 1024, :])
        acc_t = acc.T
        idx_ref[0, e] = (acc_t[IDX_LANE:IDX_LANE + 1, :] * 64.0 + acc_t[IDX_LANE + 1:IDX_LANE + 2, :]).astype(jnp.int32)
        g = jnp.zeros((CAP, 1), F32)
        for off in GATE_LANE:
            g = g + jnp.sum(jnp.where(lane == off + e, acc, 0.0), axis=-1, keepdims=True)
        gate_ref[0, e] = g


def _topk(aff_n, tmat):
    out_spec = pl.BlockSpec((1, N_EXPERTS, CAP, 1), lambda s: (s, 0, 0, 0))
    return pl.pallas_call(
        _topk_kernel,
        grid=(2,),
        in_specs=[
            pl.BlockSpec((N_CTX, 128), lambda s: (s, 0)),
            pl.BlockSpec((N_CTX, 128), lambda s: (0, 0)),
        ],
        out_specs=[pl.BlockSpec((1, N_EXPERTS, 1, CAP), lambda s: (s, 0, 0, 0)), out_spec],
        out_shape=[jax.ShapeDtypeStruct((2, N_EXPERTS, 1, CAP), jnp.int32),
                   jax.ShapeDtypeStruct((2, N_EXPERTS, CAP, 1), F32)],
        compiler_params=_params(1),
        name="topk",
    )(aff_n, tmat)


N_FT = FF // TF
GATHER_ROWS = 2 * CAP
ROWS_PER_STEP = GATHER_ROWS // N_FT


def _ffn_kernel(cast_next, idx_ref, h2_hbm, g_ref, wg_ref, wu_ref, wd_ref, *refs):
    if cast_next:
        win_ref, ye_ref, wbf_ref, xg, xb, acc, sems = refs
        wbf_ref[...] = win_ref[...].astype(BF16)
    else:
        ye_ref, xg, xb, acc, sems = refs
    e = pl.program_id(0)
    f = pl.program_id(1)
    slot = e % 2

    def row_copy(expert, s, p, dst_slot):
        row = idx_ref[(s * N_EXPERTS + expert) * CAP + p] + s * N_CTX
        src = h2_hbm.at[pl.ds(pl.multiple_of(row * SLAB, SLAB), SLAB), :]
        dst = xg.at[dst_slot, pl.ds(pl.multiple_of((s * CAP + p) * SLAB, SLAB), SLAB), :]
        return pltpu.make_async_copy(src, dst, sems.at[dst_slot])

    def slot_wait(dst_slot):
        pltpu.make_async_copy(h2_hbm.at[pl.ds(0, GATHER_ROWS * SLAB), :], xg.at[dst_slot],
                              sems.at[dst_slot]).wait()

    @pl.when(jnp.logical_and(e == 0, f == 0))
    def _():
        def issue(r, carry):
            row_copy(0, r // CAP, r % CAP, 0).start()
            return carry
        lax.fori_loop(0, GATHER_ROWS, issue, 0)
        acc[...] = jnp.zeros_like(acc)

    @pl.when(f == 0)
    def _():
        slot_wait(slot)
        xb[...] = _from_slabs(xg.at[slot]).astype(BF16)

    nxt = jnp.minimum(e + 1, N_EXPERTS - 1)
    s_nxt = f // (N_FT // 2)
    p0 = (f % (N_FT // 2)) * ROWS_PER_STEP
    for u in range(ROWS_PER_STEP):
        row_copy(nxt, s_nxt, p0 + u, 1 - slot).start()

    x = xb[...]
    hg = _dot(x, wg_ref[...].astype(BF16))
    hu = _dot(x, wu_ref[...].astype(BF16))
    hdn = ((hg * _sigmoid(hg)) * hu).astype(BF16)
    acc[...] = jnp.where(f == 0, 0.0, acc[...]) + _dot(hdn, wd_ref[...].astype(BF16))

    @pl.when(f == N_FT - 1)
    def _():
        for s in range(2):
            _to_slabs(ye_ref.at[s], acc[s * CAP:(s + 1) * CAP, :] * g_ref[s])

    @pl.when(jnp.logical_and(e == N_EXPERTS - 1, f == N_FT - 1))
    def _():
        slot_wait(1 - slot)


def _expert_ffn(l, idx_flat, h2s, gates, w_gate, w_up, w_down, w_in):
    cast_next = l + 1 < DEPTH
    n_steps = N_EXPERTS * N_FT
    in_specs = [
        pl.BlockSpec(memory_space=pl.ANY),
        pl.BlockSpec((2, None, CAP, 1), lambda e, f, idx: (0, e, 0, 0)),
        pl.BlockSpec((None, None, D, TF), lambda e, f, idx: (l, e, 0, f)),
        pl.BlockSpec((None, None, D, TF), lambda e, f, idx: (l, e, 0, f)),
        pl.BlockSpec((None, None, TF, D), lambda e, f, idx: (l, e, f, 0)),
    ]
    args = [idx_flat, h2s, gates, w_gate, w_up, w_down]
    out_specs = [pl.BlockSpec((2, None, CAP * SLAB, 128), lambda e, f, idx: (0, e, 0, 0))]
    out_shape = [jax.ShapeDtypeStruct((2, N_EXPERTS, CAP * SLAB, 128), F32)]
    if cast_next:
        cols = IN_COLS // n_steps
        in_specs.append(pl.BlockSpec((None, D, cols), lambda e, f, idx: (l + 1, 0, e * N_FT + f)))
        args.append(w_in)
        out_specs.append(pl.BlockSpec((D, cols), lambda e, f, idx: (0, e * N_FT + f)))
        out_shape.append(jax.ShapeDtypeStruct((D, IN_COLS), BF16))
    grid_spec = pltpu.PrefetchScalarGridSpec(
        num_scalar_prefetch=1,
        grid=(N_EXPERTS, N_FT),
        in_specs=in_specs,
        out_specs=out_specs,
        scratch_shapes=[
            pltpu.VMEM((2, GATHER_ROWS * SLAB, 128), F32),
            pltpu.VMEM((GATHER_ROWS, D), BF16),
            pltpu.VMEM((GATHER_ROWS, D), F32),
            pltpu.SemaphoreType.DMA((2,)),
        ],
    )
    outs = pl.pallas_call(
        functools.partial(_ffn_kernel, cast_next),
        grid_spec=grid_spec,
        out_shape=out_shape,
        compiler_params=_params(2),
        name="expert_ffn",
    )(*args)
    return (outs[0], outs[1]) if cast_next else (outs[0], None)


SCATTER_UNROLL = 8
SCATTER_EXPERTS = 4


def _combine_kernel(idx_ref, ye_ref, y_ref):
    s = pl.program_id(0)
    eg = pl.program_id(1)

    @pl.when(eg == 0)
    def _():
        y_ref[...] = jnp.zeros_like(y_ref)

    for k in range(SCATTER_EXPERTS):
        base = (s * N_EXPERTS + eg * SCATTER_EXPERTS + k) * CAP
        for p0 in range(0, CAP, SCATTER_UNROLL):
            rows = [idx_ref[base + p0 + u] for u in range(SCATTER_UNROLL)]
            tiles = [pl.ds(pl.multiple_of(r * SLAB, SLAB), SLAB) for r in rows]
            vals = [y_ref[tiles[u], :] + ye_ref[k, pl.ds((p0 + u) * SLAB, SLAB), :] for u in range(SCATTER_UNROLL)]
            for u in range(SCATTER_UNROLL):
                y_ref[tiles[u], :] = vals[u]


def _combine(idx_flat, ye):
    grid_spec = pltpu.PrefetchScalarGridSpec(
        num_scalar_prefetch=1,
        grid=(2, N_EXPERTS // SCATTER_EXPERTS),
        in_specs=[pl.BlockSpec((None, SCATTER_EXPERTS, CAP * SLAB, 128), lambda s, eg, idx: (s, eg, 0, 0))],
        out_specs=pl.BlockSpec((N_CTX * SLAB, 128), lambda s, eg, idx: (s, 0)),
    )
    return pl.pallas_call(
        _combine_kernel,
        grid_spec=grid_spec,
        out_shape=jax.ShapeDtypeStruct((N_TOK * SLAB, 128), F32),
        compiler_params=_params(2),
        name="combine",
    )(idx_flat, ye)


def _final_kernel(x_ref, y_ref, mod_ref, w_ref, op_ref, os_ref):
    x = x_ref[...] + mod_ref[0][:, 5 * D:6 * D] * _from_slabs(y_ref)
    out = _rms(x) * w_ref[...]
    is_ctx = pl.program_id(0) < N_CTX // FINAL_ROWS

    @pl.when(is_ctx)
    def _():
        op_ref[...] = out

    @pl.when(jnp.logical_not(is_ctx))
    def _():
        os_ref[...] = out


FINAL_ROWS = 2 * TM


def _final(x1, y, mod, final_norm_w):
    ft = FINAL_ROWS
    ctx_steps = N_CTX // ft
    tile = pl.BlockSpec((ft, D), lambda i: (i, 0))
    return pl.pallas_call(
        _final_kernel,
        grid=(N_TOK // ft,),
        in_specs=[tile, pl.BlockSpec((ft * SLAB, 128), lambda i: (i, 0)),
                  pl.BlockSpec((1, 1, N_MOD * D), lambda i: ((DEPTH - 1) * 8 + _mod_row(i * (ft // TM)), 0, 0)),
                  pl.BlockSpec((1, D), lambda i: (0, 0))],
        out_specs=[pl.BlockSpec((ft, D), lambda i: (jnp.minimum(i, ctx_steps - 1), 0)),
                   pl.BlockSpec((ft, D), lambda i: (jnp.maximum(i - ctx_steps, 0), 0))],
        out_shape=[jax.ShapeDtypeStruct((N_CTX, D), F32), jax.ShapeDtypeStruct((N_LAT, D), F32)],
        compiler_params=_params(1),
        name="final_norm",
    )(x1, y, mod, final_norm_w.reshape(1, D))


def _rope_tables():
    t = np.arange(T_LAT)
    row = (t // GRID_W).astype(np.float32)
    col = (t % GRID_W).astype(np.float32)
    inv = jnp.asarray(ROPE_BASE, F32) ** (-jnp.arange(N_ROPE_FREQ, dtype=F32) / N_ROPE_FREQ)
    ang_r = jnp.asarray(row)[:, None] * inv
    ang_c = jnp.asarray(col)[:, None] * inv
    def group(ang):
        return jnp.concatenate([ang, ang], axis=-1)
    ang = jnp.concatenate([group(ang_r), group(ang_c), group(ang_r), group(ang_c)], axis=-1)
    sign = np.where(np.arange(HD) % 32 < 16, -1.0, 1.0).astype(np.float32)
    return jnp.cos(ang), jnp.sin(ang) * sign


def _index_table():
    t = np.arange(N_CTX)
    tm = np.zeros((N_CTX, 128), np.float32)
    tm[:, IDX_LANE] = t // 64
    tm[:, IDX_LANE + 1] = t % 64
    return jnp.asarray(tm, BF16)


def kernel(x_prompt, x_sample, c, cache_attn_k, cache_attn_v, state_ret_fwd, state_ret_bwd, c_ctx, w_ada, b_ada, norm1_w, norm2_w, w_in, attn_lambda, attn_subln_w, conv_w, ret_decay_fwd, ret_decay_bwd, ret_norm_w, w_br_attn, w_br_conv, w_br_ret, w_out, w_router, w_exp_gate, w_exp_up, w_exp_down, final_norm_w):
    x = (x_prompt.reshape(N_CTX, D), x_sample.reshape(N_LAT, D))
    cvec = jnp.concatenate([c_ctx[None, :], c, jnp.zeros((3, D), F32)], axis=0)
    mod = _modulation(cvec, w_ada, b_ada).reshape(DEPTH * 8, 1, N_MOD * D)

    w_in_bf = w_in[0].astype(BF16)
    w_br_attn_bf = w_br_attn.astype(BF16)
    w_br_conv_bf = w_br_conv.astype(BF16)
    w_br_ret_bf = w_br_ret.astype(BF16)
    w_out_bf = w_out.astype(BF16)
    w_router_pad = jnp.pad(w_router, ((0, 0), (0, 0), (0, 128 - N_EXPERTS))).astype(BF16)
    norm1 = norm1_w.reshape(DEPTH, 1, D)
    norm2 = norm2_w.reshape(DEPTH, 1, D)
    subln = attn_subln_w.reshape(DEPTH, 1, HD)
    decay_f = ret_decay_fwd.reshape(DEPTH * HEADS, 1, 1)
    decay_b = ret_decay_bwd.reshape(DEPTH * HEADS, 1, 1)
    ret_nw = ret_norm_w.reshape(DEPTH * HEADS, 1, HD)
    cos, sin = _rope_tables()
    tmat = _index_table()

    y = None
    new_k = new_v = new_sf = new_sb = None
    for l in range(DEPTH):
        lam_init = 0.8 - 0.6 * math.exp(-0.3 * l)
        if y is None:
            zb, zq, new_k, new_v = _inproj(l, x, None, mod, norm1, w_in_bf, new_k, new_v)
        else:
            zb, zq, new_k, new_v, x = _inproj(l, x, y, mod, norm1, w_in_bf, new_k, new_v)
        ao, ao_lat = _attention(l, lam_init, zb, zq, cache_attn_k, cache_attn_v, cos, sin, attn_lambda, subln)
        ro, ro_lat, new_sf, new_sb = _retention(l, zb, state_ret_fwd, state_ret_bwd, decay_f, decay_b, ret_nw,
                                                new_sf, new_sb)
        x, h2s, aff_n = _merge(l, x, ao, ao_lat, ro, ro_lat, zb, mod, conv_w, w_br_attn_bf, w_br_conv_bf,
                               w_br_ret_bf, w_out_bf, norm2, w_router_pad)
        idx, gates = _topk(aff_n, tmat)
        idx_flat = idx.reshape(2 * N_EXPERTS * CAP)
        ye, w_in_bf = _expert_ffn(l, idx_flat, h2s, gates, w_exp_gate, w_exp_up, w_exp_down, w_in)
        y = _combine(idx_flat, ye)

    y_prompt, y_sample = _final(x, y, mod, final_norm_w)
    return (y_prompt.reshape(N_CTX_SEQ, T_CTX, D), y_sample.reshape(N_LAT_SEQ, T_LAT, D), new_k, new_v,
            new_sf, new_sb)
```

```python
import functools
import math

import jax
import jax.numpy as jnp
import numpy as np
from jax import lax
from jax.experimental import pallas as pl
from jax.experimental.pallas import tpu as pltpu

F32 = jnp.float32
BF16 = jnp.bfloat16

D = 1024
DEPTH = 2
N_CTX_SEQ = 16
T_CTX = 256
N_LAT_SEQ = 4
T_LAT = 1024
PAST = 256
N_CTX = N_CTX_SEQ * T_CTX
N_LAT = N_LAT_SEQ * T_LAT
N_TOK = N_CTX + N_LAT
TM = 256
N_TILES = N_TOK // TM
CTX_TILES = N_CTX // TM
LAT_TILES_PER_SEQ = T_LAT // TM
HEADS = 4
HD = 128
GRID_W = 64
N_ROPE_FREQ = 16
ROPE_BASE = 10000.0
IN_COLS = 8192
N_MOD = 6
N_EXPERTS = 16
CAP = 512
FF = 2048
TF = 512
EPS = 1e-6
RET_SCALE = HD ** -0.5
ATTN_SCALE = 64 ** -0.5
LOG2_E = math.log2(math.e)
SLAB = D // 128
VMEM_LIMIT = 56 * 1024 * 1024

QB, KB, VB = 0, 4, 8
RQB, RKB, RVB, RGB = 24, 28, 32, 36
CONV_B_COL, CONV_U_COL = 3, 4
MERGE_GATE_COL = 5
MERGE_SUB = 2


def _sigmoid(x):
    return 0.5 * jnp.tanh(0.5 * x) + 0.5


def _log_sigmoid(x):
    return jnp.minimum(x, 0.0) - jnp.log(1.0 + jnp.exp(-jnp.abs(x)))


def _rms(x):
    return x * lax.rsqrt(jnp.mean(x * x, axis=-1, keepdims=True) + EPS)


def _dot(a, b):
    return jnp.dot(a, b, preferred_element_type=F32)


def _dot_nt(a, b):
    return lax.dot_general(a, b, (((1,), (1,)), ((), ())), preferred_element_type=F32)


def _dot_tn(a, b):
    return lax.dot_general(a, b, (((0,), (0,)), ((), ())), preferred_element_type=F32)


def _to_slabs(ref, x):
    n = x.shape[0]
    for s in range(SLAB):
        ref[pl.ds(s, n, stride=SLAB), :] = x[:, s * 128:(s + 1) * 128]


def _from_slabs(ref):
    n = ref.shape[0] // SLAB
    return jnp.concatenate([ref[pl.ds(s, n, stride=SLAB), :] for s in range(SLAB)], axis=-1)


def _mod_row(i):
    return jnp.where(i < CTX_TILES, 0, 1 + (i - CTX_TILES) // LAT_TILES_PER_SEQ)


def _params(n_axes):
    return pltpu.CompilerParams(
        dimension_semantics=("arbitrary",) * n_axes, vmem_limit_bytes=VMEM_LIMIT)


def _mod_kernel(c_ref, w_ref, b_ref, o_ref):
    c = c_ref[...]
    s = (c * _sigmoid(c)).astype(BF16)
    o_ref[...] = _dot(s, w_ref[...].astype(BF16)) + b_ref[...]


def _modulation(cvec, w_ada, b_ada):
    tn = 1024
    return pl.pallas_call(
        _mod_kernel,
        grid=(DEPTH, N_MOD * D // tn),
        in_specs=[
            pl.BlockSpec((8, D), lambda l, n: (0, 0)),
            pl.BlockSpec((None, D, tn), lambda l, n: (l, 0, n)),
            pl.BlockSpec((None, 1, tn), lambda l, n: (l, 0, n)),
        ],
        out_specs=pl.BlockSpec((None, 8, tn), lambda l, n: (l, 0, n)),
        out_shape=jax.ShapeDtypeStruct((DEPTH, 8, N_MOD * D), F32),
        compiler_params=_params(2),
        name="modulation",
    )(cvec, w_ada, b_ada.reshape(DEPTH, 1, N_MOD * D))


def _inproj_kernel(has_y, x_pair, n_prev, *refs):
    refs = list(refs)
    x_ref = refs.pop(0)
    if x_pair:
        xs_ref = refs.pop(0)
    if has_y:
        y_ref, modp_ref = refs.pop(0), refs.pop(0)
    mod_ref, n1_ref, w_ref = refs.pop(0), refs.pop(0), refs.pop(0)
    if n_prev:
        kp_ref, vp_ref = refs.pop(0), refs.pop(0)
    zb_ref, zq_ref, kn_ref, vn_ref = refs[0:4]
    i = pl.program_id(0)

    def normalize():
        x = x_ref[...]
        if x_pair:
            x = jnp.where(i < N_TILES - CTX_TILES, xs_ref[...], x)
        if has_y:
            x = x + modp_ref[0][:, 5 * D:6 * D] * _from_slabs(y_ref)
            refs[4][...] = x
        m = mod_ref[0]
        shift1 = m[:, 0:D]
        scale1 = m[:, D:2 * D]
        return ((_rms(x) * n1_ref[...]) * (1.0 + scale1) + shift1).astype(BF16)

    def project(h):
        cw = 1024
        for c in range(IN_COLS // cw):
            z = _dot(h, w_ref[:, c * cw:(c + 1) * cw])
            if c == CONV_U_COL // 2:
                z = jnp.concatenate([z[:, 0:512] * z[:, 512:cw], z[:, 512:cw]], axis=-1)
            elif c == RGB // 8:
                g = z[:, 512:cw]
                z = jnp.concatenate([z[:, 0:512], g * _sigmoid(g)], axis=-1)
            elif c >= MERGE_GATE_COL:
                z = _sigmoid(z)
            if c == 0:
                zb_ref[:, 0:cw] = jnp.concatenate([z[:, 0:512] * (ATTN_SCALE * LOG2_E), z[:, 512:cw]],
                                                  axis=-1).astype(BF16)
            else:
                zb_ref[:, c * cw:(c + 1) * cw] = z.astype(BF16)
            if c == 0:
                zq_ref[...] = z
                for hh in range(HEADS):
                    kn_ref[n_prev, hh] = z[:, 512 + hh * HD:512 + (hh + 1) * HD]
            if c == 1:
                for hh in range(HEADS):
                    vn_ref[n_prev, hh] = z[:, hh * HD:(hh + 1) * HD]
        if n_prev:
            kn_ref[0:n_prev] = kp_ref[...]
            vn_ref[0:n_prev] = vp_ref[...]

    project(normalize())


def _inproj(l, x, y, mod, norm1_w, w_in_bf, k_prev, v_prev):
    has_y = y is not None
    t = lambda i: (i + CTX_TILES) % N_TILES
    n_lat = N_TILES - CTX_TILES
    norm_tile = pl.BlockSpec((TM, D), lambda i: (t(i), 0))
    ctx_i = lambda i: jnp.maximum(i - n_lat, 0)
    kv_spec = lambda m: pl.BlockSpec((None, m, HEADS, T_CTX, HD), lambda i: (ctx_i(i), 0, 0, 0, 0))
    kv_shape = jax.ShapeDtypeStruct((N_CTX_SEQ, l + 1, HEADS, T_CTX, HD), F32)
    x_pair = isinstance(x, tuple)
    if x_pair:
        in_specs = [pl.BlockSpec((TM, D), lambda i: (jnp.maximum(i - n_lat, 0), 0)),
                    pl.BlockSpec((TM, D), lambda i: (jnp.minimum(i, n_lat - 1), 0))]
        args = list(x)
    else:
        in_specs = [norm_tile]
        args = [x]
    if has_y:
        in_specs += [pl.BlockSpec((TM * SLAB, 128), lambda i: (t(i), 0)),
                     pl.BlockSpec((1, 1, N_MOD * D), lambda i: ((l - 1) * 8 + _mod_row(t(i)), 0, 0))]
        args += [y, mod]
    in_specs += [
        pl.BlockSpec((1, 1, N_MOD * D), lambda i: (l * 8 + _mod_row(t(i)), 0, 0)),
        pl.BlockSpec((None, 1, D), lambda i: (l, 0, 0)),
        pl.BlockSpec((D, IN_COLS), lambda i: (0, 0), pipeline_mode=pl.Buffered(1)),
    ]
    args += [mod, norm1_w, w_in_bf]
    if l:
        in_specs += [kv_spec(l), kv_spec(l)]
        args += [k_prev, v_prev]
    out_specs = [pl.BlockSpec((TM, IN_COLS), lambda i: (t(i), 0)),
                 pl.BlockSpec((TM, 1024), lambda i: (t(i), 0)),
                 kv_spec(l + 1), kv_spec(l + 1)]
    out_shape = [jax.ShapeDtypeStruct((N_TOK, IN_COLS), BF16),
                 jax.ShapeDtypeStruct((N_TOK, 1024), F32),
                 kv_shape, kv_shape]
    if has_y:
        out_specs.append(norm_tile)
        out_shape.append(jax.ShapeDtypeStruct((N_TOK, D), F32))
    return pl.pallas_call(
        functools.partial(_inproj_kernel, has_y, x_pair, l),
        grid=(N_TILES,),
        in_specs=in_specs,
        out_specs=out_specs,
        out_shape=out_shape,
        compiler_params=_params(1),
        name="inproj",
    )(*args)


def _lambda(lam_ref, lam_init):
    lv = lam_ref[...]
    a = jnp.sum(lv[0:1] * lv[1:2], axis=-1, keepdims=True)
    b = jnp.sum(lv[2:3] * lv[3:4], axis=-1, keepdims=True)
    return jnp.exp(a) - jnp.exp(b) + lam_init


def _interleave(*stage_generators):
    live = list(stage_generators)
    while live:
        for g in list(live):
            try:
                next(g)
            except StopIteration:
                live.remove(g)


def _diff_attention(lam_init, qs, keys, vs, lam, sw, keys_transposed, o_ref, sls):
    lane = lax.broadcasted_iota(jnp.int32, qs[0].shape, 1)
    score = _dot if keys_transposed else _dot_nt
    maps = [(h, jnp.where(keep, q, jnp.zeros_like(q)))
            for h, q in enumerate(qs) for keep in (lane < 64, lane >= 64)]
    s = [score(qm, keys[h]) for h, qm in maps]
    yield
    e = [jnp.exp2(x - jnp.max(x, axis=-1, keepdims=True)) for x in s]
    yield
    r = [1.0 / jnp.sum(x, axis=-1, keepdims=True) for x in e]
    a = [(e[2 * h] * r[2 * h] - e[2 * h + 1] * (lam * r[2 * h + 1])).astype(BF16) for h in range(len(qs))]
    yield
    o = [_dot(a[h], vs[h]) for h in range(len(qs))]
    yield
    for sl, x in zip(sls, o):
        o_ref[:, sl] = ((_rms(x) * sw) * (1.0 - lam_init)).astype(BF16)


def _scaled_q(q):
    return (q * (ATTN_SCALE * LOG2_E)).astype(BF16)


def _attn_ctx_stages(lam_init, q_ref, k_ref, v_ref, lam_ref, sw_ref, o_ref):
    lam = _lambda(lam_ref, lam_init)
    sls = [slice(h * HD, (h + 1) * HD) for h in range(HEADS)]
    yield from _diff_attention(lam_init, [q_ref[:, sl] for sl in sls], [k_ref[:, sl] for sl in sls],
                               [v_ref[:, sl] for sl in sls], lam, sw_ref[...], False, o_ref, sls)


def _rope(x, cos, sin_signed):
    lane = lax.broadcasted_iota(jnp.int32, x.shape, 1)
    partner = jnp.where(lane % 32 < 16, pltpu.roll(x, 112, 1), pltpu.roll(x, 16, 1))
    return x * cos + partner * sin_signed


def _attn_lat_kernel(lam_init, q_ref, k_ref, v_ref, ck_ref, cv_ref, cos_ref, sin_ref,
                     cosq_ref, sinq_ref, lam_ref, sw_ref, o_ref, k_t, vall):
    @pl.when(pl.program_id(1) == 0)
    def _():
        for h in range(HEADS):
            sl = slice(h * HD, (h + 1) * HD)
            k_t[h, :, 0:PAST] = ck_ref[h].T.astype(BF16)
            k_t[h, :, PAST:] = _rope(k_ref[:, sl], cos_ref[...], sin_ref[...]).T.astype(BF16)
            vall[h, 0:PAST, :] = cv_ref[h].astype(BF16)
            vall[h, PAST:, :] = v_ref[:, sl]

    lam = _lambda(lam_ref, lam_init)
    sls = [slice(h * HD, (h + 1) * HD) for h in range(HEADS)]
    qs = [_scaled_q(_rope(q_ref[:, sl], cosq_ref[...], sinq_ref[...])) for sl in sls]
    _interleave(_diff_attention(lam_init, qs, [k_t[h] for h in range(HEADS)], [vall[h] for h in range(HEADS)],
                                lam, sw_ref[...], True, o_ref, sls))


def _mix_ctx(l, lam_init, zb, attn_lambda, subln_w, decay_f, decay_b, ret_norm_w, sf_prev, sb_prev):
    width = HEADS * HD
    col = lambda c: pl.BlockSpec((T_CTX, width), lambda b: (b, c // HEADS))
    dec = pl.BlockSpec((HEADS, 1, 1), lambda b: (l, 0, 0))
    st_spec = lambda n: pl.BlockSpec((None, n, HEADS, HD, HD), lambda b: (b, 0, 0, 0, 0))
    st_shape = jax.ShapeDtypeStruct((N_CTX_SEQ, l + 1, HEADS, HD, HD), F32)
    out_tile = pl.BlockSpec((T_CTX, width), lambda b: (b, 0))
    out_shape = jax.ShapeDtypeStruct((N_CTX, width), BF16)
    in_specs = [
        col(QB), col(KB), col(VB),
        pl.BlockSpec((None, 4, 64), lambda b: (l, 0, 0)),
        pl.BlockSpec((None, 1, HD), lambda b: (l, 0, 0)),
        col(RQB), col(RKB), col(RVB), col(RGB),
        dec, dec, pl.BlockSpec((HEADS, 1, HD), lambda b: (l, 0, 0)),
    ]
    args = [zb, zb, zb, attn_lambda, subln_w, zb, zb, zb, zb, decay_f, decay_b, ret_norm_w]
    if l:
        in_specs += [st_spec(l), st_spec(l)]
        args += [sf_prev, sb_prev]
    return pl.pallas_call(
        functools.partial(_mix_ctx_kernel, lam_init, l),
        grid=(N_CTX_SEQ,),
        in_specs=in_specs,
        out_specs=[out_tile, out_tile, st_spec(l + 1), st_spec(l + 1)],
        out_shape=[out_shape, out_shape, st_shape, st_shape],
        scratch_shapes=[pltpu.VMEM((HEADS, T_CTX, T_CTX), F32)],
        compiler_params=_params(1),
        name="mix_ctx",
    )(*args)


def _attention_lat(l, lam_init, zb, zq, cache_k, cache_v, cos, sin, attn_lambda, subln_w):
    width = HEADS * HD
    lat_row = lambda b, j: CTX_TILES + LAT_TILES_PER_SEQ * b + j
    seq_row = lambda b: N_CTX // T_LAT + b
    ao_lat = pl.pallas_call(
        functools.partial(_attn_lat_kernel, lam_init),
        grid=(N_LAT_SEQ, LAT_TILES_PER_SEQ),
        in_specs=[
            pl.BlockSpec((TM, width), lambda b, j: (lat_row(b, j), QB // HEADS)),
            pl.BlockSpec((T_LAT, width), lambda b, j: (seq_row(b), KB // HEADS)),
            pl.BlockSpec((T_LAT, width), lambda b, j: (seq_row(b), VB // HEADS)),
            pl.BlockSpec((None, None, HEADS, PAST, HD), lambda b, j: (b, l, 0, 0, 0)),
            pl.BlockSpec((None, None, HEADS, PAST, HD), lambda b, j: (b, l, 0, 0, 0)),
            pl.BlockSpec((T_LAT, HD), lambda b, j: (0, 0)),
            pl.BlockSpec((T_LAT, HD), lambda b, j: (0, 0)),
            pl.BlockSpec((TM, HD), lambda b, j: (j, 0)),
            pl.BlockSpec((TM, HD), lambda b, j: (j, 0)),
            pl.BlockSpec((None, 4, 64), lambda b, j: (l, 0, 0)),
            pl.BlockSpec((None, 1, HD), lambda b, j: (l, 0, 0)),
        ],
        out_specs=pl.BlockSpec((TM, width), lambda b, j: (LAT_TILES_PER_SEQ * b + j, 0)),
        out_shape=jax.ShapeDtypeStruct((N_LAT, width), BF16),
        scratch_shapes=[pltpu.VMEM((HEADS, HD, PAST + T_LAT), BF16),
                        pltpu.VMEM((HEADS, PAST + T_LAT, HD), BF16)],
        compiler_params=_params(2),
        name="attn_lat",
    )(zq, zq, zb, cache_k, cache_v, cos, sin, cos, sin, attn_lambda, subln_w)
    return ao_lat


def _decay_matrix(lgf, lgb, row0, tq, tk):
    i = row0 + lax.broadcasted_iota(jnp.int32, (tq, tk), 0)
    j = lax.broadcasted_iota(jnp.int32, (tq, tk), 1)
    d = (i - j).astype(F32)
    fwd = jnp.where(d >= 0.0, jnp.exp(lgf * jnp.maximum(d, 0.0)), 0.0)
    bwd = jnp.where(d <= 0.0, jnp.exp(lgb * jnp.maximum(-d, 0.0)), 0.0)
    return fwd + bwd


def _ret_finish(o, g, nw):
    return (g.astype(F32) * (_rms(o) * nw)).astype(BF16)


def _ret_ctx_stages(n_prev, q_ref, k_ref, v_ref, g_ref, df_ref, db_ref, nw_ref, o_ref, sf_ref, sb_ref, dmat):
    j = lax.broadcasted_iota(jnp.int32, (T_CTX, 1), 0).astype(F32)
    heads = range(HEADS)
    sls = [slice(h * HD, (h + 1) * HD) for h in heads]
    q = [q_ref[:, sl] for sl in sls]
    k = [k_ref[:, sl] for sl in sls]
    v = [v_ref[:, sl] for sl in sls]
    s = [(_dot_nt(q[h], k[h]) * dmat[h]).astype(BF16) for h in heads]
    yield
    o = [_dot(s[h], v[h]) for h in heads]
    yield
    lgf = [_log_sigmoid(df_ref[h]) for h in heads]
    lgb = [_log_sigmoid(db_ref[h]) for h in heads]
    kf = [k[h].astype(F32) * RET_SCALE for h in heads]
    kfw = [(kf[h] * jnp.exp(lgf[h] * (T_CTX - 1.0 - j))).astype(BF16) for h in heads]
    kbw = [(kf[h] * jnp.exp(lgb[h] * j)).astype(BF16) for h in heads]
    yield
    for h in heads:
        sf_ref[n_prev, h] = _dot_tn(kfw[h], v[h])
        sb_ref[n_prev, h] = _dot_tn(kbw[h], v[h])
    yield
    for h in heads:
        o_ref[:, sls[h]] = _ret_finish(o[h], g_ref[:, sls[h]], nw_ref[h])


def _mix_ctx_kernel(lam_init, n_prev, *refs):
    refs = list(refs)
    aq_ref, ak_ref, av_ref, lam_ref, sw_ref = refs[0:5]
    rq_ref, rk_ref, rv_ref, rg_ref, df_ref, db_ref, nw_ref = refs[5:12]
    refs = refs[12:]
    if n_prev:
        sfp_ref, sbp_ref = refs.pop(0), refs.pop(0)
    ao_ref, ro_ref, sf_ref, sb_ref, dmat = refs

    @pl.when(pl.program_id(0) == 0)
    def _():
        for h in range(HEADS):
            dmat[h] = RET_SCALE * _decay_matrix(_log_sigmoid(df_ref[h]), _log_sigmoid(db_ref[h]), 0, T_CTX, T_CTX)

    if n_prev:
        sf_ref[0:n_prev] = sfp_ref[...]
        sb_ref[0:n_prev] = sbp_ref[...]
    _interleave(
        _attn_ctx_stages(lam_init, aq_ref, ak_ref, av_ref, lam_ref, sw_ref, ao_ref),
        _ret_ctx_stages(n_prev, rq_ref, rk_ref, rv_ref, rg_ref, df_ref, db_ref, nw_ref, ro_ref, sf_ref, sb_ref, dmat))


N_DCHUNK = 2 * LAT_TILES_PER_SEQ - 1


def _ret_lat_kernel(q_ref, k_ref, v_ref, g_ref, s0f_ref, s0b_ref, df_ref, db_ref, nw_ref, o_ref, strip, k_t):
    b = pl.program_id(0)
    j = pl.program_id(1)

    @pl.when(jnp.logical_and(b == 0, j == 0))
    def _():
        for h in range(HEADS):
            lgf = _log_sigmoid(df_ref[h])
            lgb = _log_sigmoid(db_ref[h])
            for c in range(N_DCHUNK):
                strip[h, c] = RET_SCALE * _decay_matrix(lgf, lgb, T_LAT - TM - c * TM, TM, TM)

    @pl.when(j == 0)
    def _():
        for h in range(HEADS):
            k_t[h] = k_ref[:, h * HD:(h + 1) * HD].astype(F32).T.astype(BF16)

    i = (j * TM + lax.broadcasted_iota(jnp.int32, (TM, 1), 0)).astype(F32)
    c0 = LAT_TILES_PER_SEQ - 1 - j
    heads = range(HEADS)
    sls = [slice(h * HD, (h + 1) * HD) for h in heads]
    q = [q_ref[:, sl] for sl in sls]
    dmat = [jnp.concatenate([strip[h, c0 + c] for c in range(LAT_TILES_PER_SEQ)], axis=-1) for h in heads]
    s = [(_dot(q[h], k_t[h]) * dmat[h]).astype(BF16) for h in heads]
    o = [_dot(s[h], v_ref[:, sls[h]]) for h in heads]
    of = [_dot(q[h], s0f_ref[h].astype(BF16)) for h in heads]
    ob = [_dot(q[h], s0b_ref[h].astype(BF16)) for h in heads]
    for h in heads:
        tot = o[h] + jnp.exp(_log_sigmoid(df_ref[h]) * (i + 1.0)) * of[h]
        tot = tot + jnp.exp(_log_sigmoid(db_ref[h]) * (T_LAT - i)) * ob[h]
        o_ref[:, sls[h]] = _ret_finish(tot, g_ref[:, sls[h]], nw_ref[h])


def _retention_lat(l, zb, state_f, state_b, decay_f, decay_b, ret_norm_w):
    width = HEADS * HD
    lat_row = lambda b, j: CTX_TILES + LAT_TILES_PER_SEQ * b + j
    seq_row = lambda b: N_CTX // T_LAT + b
    dec2 = pl.BlockSpec((HEADS, 1, 1), lambda b, j: (l, 0, 0))
    nw2 = pl.BlockSpec((HEADS, 1, HD), lambda b, j: (l, 0, 0))
    s0_spec = pl.BlockSpec((None, None, HEADS, HD, HD), lambda b, j: (b, l, 0, 0, 0))
    ro_lat = pl.pallas_call(
        _ret_lat_kernel,
        grid=(N_LAT_SEQ, LAT_TILES_PER_SEQ),
        in_specs=[
            pl.BlockSpec((TM, width), lambda b, j: (lat_row(b, j), RQB // HEADS)),
            pl.BlockSpec((T_LAT, width), lambda b, j: (seq_row(b), RKB // HEADS)),
            pl.BlockSpec((T_LAT, width), lambda b, j: (seq_row(b), RVB // HEADS)),
            pl.BlockSpec((TM, width), lambda b, j: (lat_row(b, j), RGB // HEADS)),
            s0_spec, s0_spec, dec2, dec2, nw2,
        ],
        out_specs=pl.BlockSpec((TM, width), lambda b, j: (LAT_TILES_PER_SEQ * b + j, 0)),
        out_shape=jax.ShapeDtypeStruct((N_LAT, width), BF16),
        scratch_shapes=[pltpu.VMEM((HEADS, N_DCHUNK, TM, TM), F32), pltpu.VMEM((HEADS, HD, T_LAT), BF16)],
        compiler_params=_params(2),
        name="ret_lat",
    )(zb, zb, zb, zb, state_f, state_b, decay_f, decay_b, ret_norm_w)
    return ro_lat


def _merge_kernel(x_pair, *refs):
    refs = list(refs)
    x_ref = refs.pop(0)
    xs_ref = refs.pop(0) if x_pair else None
    (aoc_ref, aol_ref, roc_ref, rol_ref, cb_ref, u_ref, up_ref, un_ref,
     mg0_ref, mg1_ref, mg2_ref, mod_ref, cw_ref, wa_ref, wc_ref, wr_ref, wo_ref, n2_ref, wrt_ref,
     x1_ref, h2_ref, aff_ref) = refs
    i = pl.program_id(0)
    is_ctx = i < CTX_TILES // MERGE_SUB
    m = mod_ref[0]
    gate1 = m[:, 2 * D:3 * D]
    shift2 = m[:, 3 * D:4 * D]
    scale2 = m[:, 4 * D:5 * D]
    cw = cw_ref[...]
    u_all = u_ref[...].astype(F32)
    r = lax.broadcasted_iota(jnp.int32, (TM, 512), 0)
    lane = lax.broadcasted_iota(jnp.int32, (TM, 128), 1)

    subs = range(MERGE_SUB)
    rows = [slice(sub * TM, (sub + 1) * TM) for sub in subs]

    def conv_out(sub):
        j = (i * MERGE_SUB + sub - CTX_TILES) % LAT_TILES_PER_SEQ
        seq_first = jnp.logical_or(is_ctx, j == 0)
        seq_last = jnp.logical_or(is_ctx, j == LAT_TILES_PER_SEQ - 1)
        u = u_all[rows[sub], :]
        up = up_ref[...].astype(F32)[15:16, :] if sub == 0 else u_all[sub * TM - 1:sub * TM, :]
        dn = un_ref[...].astype(F32)[0:1, :] if sub == MERGE_SUB - 1 else u_all[(sub + 1) * TM:(sub + 1) * TM + 1, :]
        up = up * jnp.where(seq_first, 0.0, 1.0)
        dn = dn * jnp.where(seq_last, 0.0, 1.0)
        u_prev = jnp.where(r == 0, up, pltpu.roll(u, 1, 0))
        u_next = jnp.where(r == TM - 1, dn, pltpu.roll(u, TM - 1, 0))
        conv = u_prev * cw[0:1, :] + u * cw[1:2, :] + u_next * cw[2:3, :]
        return (cb_ref[rows[sub], :].astype(F32) * conv).astype(BF16)

    ao = [jnp.where(is_ctx, aoc_ref[rw, :], aol_ref[rw, :]) for rw in rows]
    ro = [jnp.where(is_ctx, roc_ref[rw, :], rol_ref[rw, :]) for rw in rows]
    conv_o = [conv_out(sub) for sub in subs]
    b_attn = [_dot(ao[s], wa_ref[...]) for s in subs]
    b_conv = [_dot(conv_o[s], wc_ref[...]) for s in subs]
    b_ret = [_dot(ro[s], wr_ref[...]) for s in subs]
    merged = [(mg0_ref[rows[s], :].astype(F32) * b_attn[s] + mg1_ref[rows[s], :].astype(F32) * b_conv[s]
               + mg2_ref[rows[s], :].astype(F32) * b_ret[s]).astype(BF16) for s in subs]
    proj = [_dot(merged[s], wo_ref[...]) for s in subs]
    x1 = []
    for s in subs:
        x_in = x_ref[rows[s], :]
        if x_pair:
            x_in = jnp.where(is_ctx, x_in, xs_ref[rows[s], :])
        x1.append(x_in + gate1 * proj[s])
        x1_ref[rows[s], :] = x1[s]
    h2 = [(_rms(x1[s]) * n2_ref[...]) * (1.0 + scale2) + shift2 for s in subs]
    logits = [_dot(h2[s].astype(BF16), wrt_ref[...]) for s in subs]
    valid = lane < N_EXPERTS
    for s in subs:
        _to_slabs(h2_ref.at[pl.ds(s * TM * SLAB, TM * SLAB), :], h2[s])
        lmax = jnp.max(jnp.where(valid, logits[s], -jnp.inf), axis=-1, keepdims=True)
        e = jnp.where(valid, jnp.exp(logits[s] - lmax), 0.0)
        aff_ref[rows[s], :] = e * (1.0 / jnp.sum(e, axis=-1, keepdims=True))


def _merge(l, x, ao, ao_lat, ro, ro_lat, zb, mod, conv_w, w_br_attn, w_br_conv, w_br_ret, w_out, norm2_w,
           w_router_pad):
    mt = MERGE_SUB * TM
    n_steps = N_TOK // mt
    ctx_steps = N_CTX // mt
    n16 = N_TOK // 16
    ctx_br = pl.BlockSpec((mt, 512), lambda i: (jnp.minimum(i, ctx_steps - 1), 0))
    lat_br = pl.BlockSpec((mt, 512), lambda i: (jnp.maximum(i - ctx_steps, 0), 0))
    col = lambda c: pl.BlockSpec((mt, 512), lambda i: (i, c))
    halo_p = lambda c: pl.BlockSpec((16, 512), lambda i: (jnp.maximum(i * (mt // 16) - 1, 0), c))
    halo_n = lambda c: pl.BlockSpec((16, 512), lambda i: (jnp.minimum((i + 1) * (mt // 16), n16 - 1), c))
    mgs = lambda c: pl.BlockSpec((mt, D), lambda i: (i, c))
    wbr = pl.BlockSpec((None, 512, D), lambda i: (l, 0, 0))
    x_pair = isinstance(x, tuple)
    if x_pair:
        x_specs = [pl.BlockSpec((mt, D), lambda i: (jnp.minimum(i, ctx_steps - 1), 0)),
                   pl.BlockSpec((mt, D), lambda i: (jnp.maximum(i - ctx_steps, 0), 0))]
        x_args = list(x)
    else:
        x_specs = [pl.BlockSpec((mt, D), lambda i: (i, 0))]
        x_args = [x]
    return pl.pallas_call(
        functools.partial(_merge_kernel, x_pair),
        grid=(n_steps,),
        in_specs=x_specs + [
            ctx_br, lat_br, ctx_br, lat_br,
            col(CONV_B_COL), col(CONV_U_COL), halo_p(CONV_U_COL), halo_n(CONV_U_COL),
            mgs(MERGE_GATE_COL), mgs(MERGE_GATE_COL + 1), mgs(MERGE_GATE_COL + 2),
            pl.BlockSpec((1, 1, N_MOD * D), lambda i: (l * 8 + _mod_row(i * MERGE_SUB), 0, 0)),
            pl.BlockSpec((None, 3, 512), lambda i: (l, 0, 0)),
            wbr, wbr, wbr,
            pl.BlockSpec((None, D, D), lambda i: (l, 0, 0)),
            pl.BlockSpec((None, 1, D), lambda i: (l, 0, 0)),
            pl.BlockSpec((None, D, 128), lambda i: (l, 0, 0)),
        ],
        out_specs=[
            pl.BlockSpec((mt, D), lambda i: (i, 0)),
            pl.BlockSpec((mt * SLAB, 128), lambda i: (i, 0)),
            pl.BlockSpec((mt, 128), lambda i: (i, 0)),
        ],
        out_shape=[
            jax.ShapeDtypeStruct((N_TOK, D), F32),
            jax.ShapeDtypeStruct((N_TOK * SLAB, 128), F32),
            jax.ShapeDtypeStruct((N_TOK, 128), F32),
        ],
        compiler_params=_params(1),
        name="merge",
    )(*x_args, ao, ao_lat, ro, ro_lat, zb, zb, zb, zb, zb, zb, zb, mod, conv_w,
      w_br_attn, w_br_conv, w_br_ret, w_out, norm2_w, w_router_pad)


def _cumsum_lanes(x, tri):
    run = jnp.zeros((x.shape[0], 1), F32)
    outs = []
    for b in range(x.shape[1] // 128):
        cs = _dot(x[:, b * 128:(b + 1) * 128].astype(BF16), tri) + run
        run = cs[:, 127:128]
        outs.append(cs)
    return jnp.concatenate(outs, axis=-1)


GATE_LANE = (0, 16, 32)
IDX_LANE = 48
BISECT_GROUP = 4


def _topk_kernel(an_ref, tmat_ref, idx_ref, gate_ref):
    an = an_ref[...]
    a = an.T[0:N_EXPERTS, :]
    n = a.shape[1]
    kf = float(CAP)

    hi = an.astype(BF16).astype(F32)
    mid = (an - hi).astype(BF16).astype(F32)
    lo = ((an - hi) - mid).astype(BF16).astype(F32)
    table = (tmat_ref[...].astype(F32) + hi + pltpu.roll(mid, GATE_LANE[1], 1)
             + pltpu.roll(lo, GATE_LANE[2], 1)).astype(BF16)

    def count_gt(thr):
        return jnp.sum(jnp.where(a > thr, 1.0, 0.0), axis=-1, keepdims=True)

    def span(lo, hi):
        inside = jnp.logical_and(a > lo, a <= hi)
        cmax = jnp.max(jnp.where(inside, a, -jnp.inf), axis=-1, keepdims=True)
        cmin = jnp.min(jnp.where(inside, a, jnp.inf), axis=-1, keepdims=True)
        return cmax, cmin

    def cond(c):
        return jnp.logical_and(c[2] > 0, c[3] < 400)

    def body(c):
        lo, hi, _, it = c
        for _ in range(BISECT_GROUP):
            mid = 0.5 * (lo + hi)
            ge = count_gt(mid) >= kf
            lo = jnp.where(ge, mid, lo)
            hi = jnp.where(ge, hi, mid)
        cmax, cmin = span(lo, hi)
        open_rows = jnp.max(jnp.where(cmax != cmin, 1, 0))
        return lo, hi, open_rows, it + 1

    lo0 = jnp.full((N_EXPERTS, 1), -1.0, F32)
    hi0 = jnp.max(a, axis=-1, keepdims=True)
    cmax0, cmin0 = span(lo0, hi0)
    lo, hi, _, _ = lax.while_loop(
        cond, body, (lo0, hi0, jnp.max(jnp.where(cmax0 != cmin0, 1, 0)), jnp.int32(0)))
    thr, _ = span(lo, hi)

    r = lax.broadcasted_iota(jnp.int32, (128, 128), 0)
    c = lax.broadcasted_iota(jnp.int32, (128, 128), 1)
    tri = jnp.where(r <= c, 1.0, 0.0).astype(BF16)
    gt = a > thr
    eq = jnp.where(a == thr, 1.0, 0.0)
    need = kf - count_gt(thr)
    eq_before = _cumsum_lanes(eq, tri) - eq
    sel = jnp.where(jnp.logical_or(gt, jnp.logical_and(eq > 0.0, eq_before < need)), 1.0, 0.0)
    pos = _cumsum_lanes(sel, tri) - 1.0
    slot = jnp.where(sel > 0.0, pos, -1.0).astype(jnp.int32)

    p_iota = lax.broadcasted_iota(jnp.int32, (CAP, 1024), 0)
    lane = lax.broadcasted_iota(jnp.int32, (CAP, 128), 1)
    for e in range(N_EXPERTS):
        acc = jnp.zeros((CAP, 128), F32)
        for cb in range(n // 1024):
            onehot = jnp.where(p_iota == slot[e:e + 1, cb * 1024:(cb + 1) * 1024], 1.0, 0.0).astype(BF16)
            acc = acc + _dot(onehot, table[cb * 1024:(cb + 1) * 1024, :])
        acc_t = acc.T
        idx_ref[0, e] = (acc_t[IDX_LANE:IDX_LANE + 1, :] * 64.0 + acc_t[IDX_LANE + 1:IDX_LANE + 2, :]).astype(jnp.int32)
        g = jnp.zeros((CAP, 1), F32)
        for off in GATE_LANE:
            g = g + jnp.sum(jnp.where(lane == off + e, acc, 0.0), axis=-1, keepdims=True)
        gate_ref[0, e] = g


def _topk(aff_n, tmat):
    out_spec = pl.BlockSpec((1, N_EXPERTS, CAP, 1), lambda s: (s, 0, 0, 0))
    return pl.pallas_call(
        _topk_kernel,
        grid=(2,),
        in_specs=[
            pl.BlockSpec((N_CTX, 128), lambda s: (s, 0)),
            pl.BlockSpec((N_CTX, 128), lambda s: (0, 0)),
        ],
        out_specs=[pl.BlockSpec((1, N_EXPERTS, 1, CAP), lambda s: (s, 0, 0, 0)), out_spec],
        out_shape=[jax.ShapeDtypeStruct((2, N_EXPERTS, 1, CAP), jnp.int32),
                   jax.ShapeDtypeStruct((2, N_EXPERTS, CAP, 1), F32)],
        compiler_params=_params(1),
        name="topk",
    )(aff_n, tmat)


N_FT = FF // TF
GATHER_ROWS = 2 * CAP
ROWS_PER_STEP = GATHER_ROWS // N_FT


def _ffn_kernel(cast_next, idx_ref, h2_hbm, g_ref, wg_ref, wu_ref, wd_ref, *refs):
    if cast_next:
        win_ref, ye_ref, wbf_ref, xg, xb, acc, sems = refs
        wbf_ref[...] = win_ref[...].astype(BF16)
    else:
        ye_ref, xg, xb, acc, sems = refs
    e = pl.program_id(0)
    f = pl.program_id(1)
    slot = e % 2

    def row_copy(expert, s, p, dst_slot):
        row = idx_ref[(s * N_EXPERTS + expert) * CAP + p] + s * N_CTX
        src = h2_hbm.at[pl.ds(pl.multiple_of(row * SLAB, SLAB), SLAB), :]
        dst = xg.at[dst_slot, pl.ds(pl.multiple_of((s * CAP + p) * SLAB, SLAB), SLAB), :]
        return pltpu.make_async_copy(src, dst, sems.at[dst_slot])

    def slot_wait(dst_slot):
        pltpu.make_async_copy(h2_hbm.at[pl.ds(0, GATHER_ROWS * SLAB), :], xg.at[dst_slot],
                              sems.at[dst_slot]).wait()

    @pl.when(jnp.logical_and(e == 0, f == 0))
    def _():
        def issue(r, carry):
            row_copy(0, r // CAP, r % CAP, 0).start()
            return carry
        lax.fori_loop(0, GATHER_ROWS, issue, 0)
        acc[...] = jnp.zeros_like(acc)

    @pl.when(f == 0)
    def _():
        slot_wait(slot)
        xb[...] = _from_slabs(xg.at[slot]).astype(BF16)

    nxt = jnp.minimum(e + 1, N_EXPERTS - 1)
    s_nxt = f // (N_FT // 2)
    p0 = (f % (N_FT // 2)) * ROWS_PER_STEP
    for u in range(ROWS_PER_STEP):
        row_copy(nxt, s_nxt, p0 + u, 1 - slot).start()

    x = xb[...]
    hg = _dot(x, wg_ref[...].astype(BF16))
    hu = _dot(x, wu_ref[...].astype(BF16))
    hdn = ((hg * _sigmoid(hg)) * hu).astype(BF16)
    acc[...] = jnp.where(f == 0, 0.0, acc[...]) + _dot(hdn, wd_ref[...].astype(BF16))

    @pl.when(f == N_FT - 1)
    def _():
        for s in range(2):
            _to_slabs(ye_ref.at[s], acc[s * CAP:(s + 1) * CAP, :] * g_ref[s])

    @pl.when(jnp.logical_and(e == N_EXPERTS - 1, f == N_FT - 1))
    def _():
        slot_wait(1 - slot)


def _expert_ffn(l, idx_flat, h2s, gates, w_gate, w_up, w_down, w_in):
    cast_next = l + 1 < DEPTH
    n_steps = N_EXPERTS * N_FT
    in_specs = [
        pl.BlockSpec(memory_space=pl.ANY),
        pl.BlockSpec((2, None, CAP, 1), lambda e, f, idx: (0, e, 0, 0)),
        pl.BlockSpec((None, None, D, TF), lambda e, f, idx: (l, e, 0, f)),
        pl.BlockSpec((None, None, D, TF), lambda e, f, idx: (l, e, 0, f)),
        pl.BlockSpec((None, None, TF, D), lambda e, f, idx: (l, e, f, 0)),
    ]
    args = [idx_flat, h2s, gates, w_gate, w_up, w_down]
    out_specs = [pl.BlockSpec((2, None, CAP * SLAB, 128), lambda e, f, idx: (0, e, 0, 0))]
    out_shape = [jax.ShapeDtypeStruct((2, N_EXPERTS, CAP * SLAB, 128), F32)]
    if cast_next:
        cols = IN_COLS // n_steps
        in_specs.append(pl.BlockSpec((None, D, cols), lambda e, f, idx: (l + 1, 0, e * N_FT + f)))
        args.append(w_in)
        out_specs.append(pl.BlockSpec((D, cols), lambda e, f, idx: (0, e * N_FT + f)))
        out_shape.append(jax.ShapeDtypeStruct((D, IN_COLS), BF16))
    grid_spec = pltpu.PrefetchScalarGridSpec(
        num_scalar_prefetch=1,
        grid=(N_EXPERTS, N_FT),
        in_specs=in_specs,
        out_specs=out_specs,
        scratch_shapes=[
            pltpu.VMEM((2, GATHER_ROWS * SLAB, 128), F32),
            pltpu.VMEM((GATHER_ROWS, D), BF16),
            pltpu.VMEM((GATHER_ROWS, D), F32),
            pltpu.SemaphoreType.DMA((2,)),
        ],
    )
    outs = pl.pallas_call(
        functools.partial(_ffn_kernel, cast_next),
        grid_spec=grid_spec,
        out_shape=out_shape,
        compiler_params=_params(2),
        name="expert_ffn",
    )(*args)
    return (outs[0], outs[1]) if cast_next else (outs[0], None)


SCATTER_UNROLL = 8
SCATTER_EXPERTS = 4


def _combine_kernel(idx_ref, ye_ref, y_ref):
    s = pl.program_id(0)
    eg = pl.program_id(1)

    @pl.when(eg == 0)
    def _():
        y_ref[...] = jnp.zeros_like(y_ref)

    for k in range(SCATTER_EXPERTS):
        base = (s * N_EXPERTS + eg * SCATTER_EXPERTS + k) * CAP
        for p0 in range(0, CAP, SCATTER_UNROLL):
            rows = [idx_ref[base + p0 + u] for u in range(SCATTER_UNROLL)]
            tiles = [pl.ds(pl.multiple_of(r * SLAB, SLAB), SLAB) for r in rows]
            vals = [y_ref[tiles[u], :] + ye_ref[k, pl.ds((p0 + u) * SLAB, SLAB), :] for u in range(SCATTER_UNROLL)]
            for u in range(SCATTER_UNROLL):
                y_ref[tiles[u], :] = vals[u]


def _combine(idx_flat, ye):
    grid_spec = pltpu.PrefetchScalarGridSpec(
        num_scalar_prefetch=1,
        grid=(2, N_EXPERTS // SCATTER_EXPERTS),
        in_specs=[pl.BlockSpec((None, SCATTER_EXPERTS, CAP * SLAB, 128), lambda s, eg, idx: (s, eg, 0, 0))],
        out_specs=pl.BlockSpec((N_CTX * SLAB, 128), lambda s, eg, idx: (s, 0)),
    )
    return pl.pallas_call(
        _combine_kernel,
        grid_spec=grid_spec,
        out_shape=jax.ShapeDtypeStruct((N_TOK * SLAB, 128), F32),
        compiler_params=_params(2),
        name="combine",
    )(idx_flat, ye)


def _final_kernel(x_ref, y_ref, mod_ref, w_ref, op_ref, os_ref):
    x = x_ref[...] + mod_ref[0][:, 5 * D:6 * D] * _from_slabs(y_ref)
    out = _rms(x) * w_ref[...]
    is_ctx = pl.program_id(0) < N_CTX // FINAL_ROWS

    @pl.when(is_ctx)
    def _():
        op_ref[...] = out

    @pl.when(jnp.logical_not(is_ctx))
    def _():
        os_ref[...] = out


FINAL_ROWS = 2 * TM


def _final(x1, y, mod, final_norm_w):
    ft = FINAL_ROWS
    ctx_steps = N_CTX // ft
    tile = pl.BlockSpec((ft, D), lambda i: (i, 0))
    return pl.pallas_call(
        _final_kernel,
        grid=(N_TOK // ft,),
        in_specs=[tile, pl.BlockSpec((ft * SLAB, 128), lambda i: (i, 0)),
                  pl.BlockSpec((1, 1, N_MOD * D), lambda i: ((DEPTH - 1) * 8 + _mod_row(i * (ft // TM)), 0, 0)),
                  pl.BlockSpec((1, D), lambda i: (0, 0))],
        out_specs=[pl.BlockSpec((ft, D), lambda i: (jnp.minimum(i, ctx_steps - 1), 0)),
                   pl.BlockSpec((ft, D), lambda i: (jnp.maximum(i - ctx_steps, 0), 0))],
        out_shape=[jax.ShapeDtypeStruct((N_CTX, D), F32), jax.ShapeDtypeStruct((N_LAT, D), F32)],
        compiler_params=_params(1),
        name="final_norm",
    )(x1, y, mod, final_norm_w.reshape(1, D))


def _rope_tables():
    t = np.arange(T_LAT)
    row = (t // GRID_W).astype(np.float32)
    col = (t % GRID_W).astype(np.float32)
    inv = jnp.asarray(ROPE_BASE, F32) ** (-jnp.arange(N_ROPE_FREQ, dtype=F32) / N_ROPE_FREQ)
    ang_r = jnp.asarray(row)[:, None] * inv
    ang_c = jnp.asarray(col)[:, None] * inv
    def group(ang):
        return jnp.concatenate([ang, ang], axis=-1)
    ang = jnp.concatenate([group(ang_r), group(ang_c), group(ang_r), group(ang_c)], axis=-1)
    sign = np.where(np.arange(HD) % 32 < 16, -1.0, 1.0).astype(np.float32)
    return jnp.cos(ang), jnp.sin(ang) * sign


def _index_table():
    t = np.arange(N_CTX)
    tm = np.zeros((N_CTX, 128), np.float32)
    tm[:, IDX_LANE] = t // 64
    tm[:, IDX_LANE + 1] = t % 64
    return jnp.asarray(tm, BF16)


def kernel(x_prompt, x_sample, c, cache_attn_k, cache_attn_v, state_ret_fwd, state_ret_bwd, c_ctx, w_ada, b_ada, norm1_w, norm2_w, w_in, attn_lambda, attn_subln_w, conv_w, ret_decay_fwd, ret_decay_bwd, ret_norm_w, w_br_attn, w_br_conv, w_br_ret, w_out, w_router, w_exp_gate, w_exp_up, w_exp_down, final_norm_w):
    x = (x_prompt.reshape(N_CTX, D), x_sample.reshape(N_LAT, D))
    cvec = jnp.concatenate([c_ctx[None, :], c, jnp.zeros((3, D), F32)], axis=0)
    mod = _modulation(cvec, w_ada, b_ada).reshape(DEPTH * 8, 1, N_MOD * D)

    w_in_bf = w_in[0].astype(BF16)
    w_br_attn_bf = w_br_attn.astype(BF16)
    w_br_conv_bf = w_br_conv.astype(BF16)
    w_br_ret_bf = w_br_ret.astype(BF16)
    w_out_bf = w_out.astype(BF16)
    w_router_pad = jnp.pad(w_router, ((0, 0), (0, 0), (0, 128 - N_EXPERTS))).astype(BF16)
    norm1 = norm1_w.reshape(DEPTH, 1, D)
    norm2 = norm2_w.reshape(DEPTH, 1, D)
    subln = attn_subln_w.reshape(DEPTH, 1, HD)
    decay_f = ret_decay_fwd.reshape(DEPTH * HEADS, 1, 1)
    decay_b = ret_decay_bwd.reshape(DEPTH * HEADS, 1, 1)
    ret_nw = ret_norm_w.reshape(DEPTH * HEADS, 1, HD)
    cos, sin = _rope_tables()
    tmat = _index_table()

    y = None
    new_k = new_v = new_sf = new_sb = None
    for l in range(DEPTH):
        lam_init = 0.8 - 0.6 * math.exp(-0.3 * l)
        if y is None:
            zb, zq, new_k, new_v = _inproj(l, x, None, mod, norm1, w_in_bf, new_k, new_v)
        else:
            zb, zq, new_k, new_v, x = _inproj(l, x, y, mod, norm1, w_in_bf, new_k, new_v)
        ao, ro, new_sf, new_sb = _mix_ctx(l, lam_init, zb, attn_lambda, subln, decay_f, decay_b, ret_nw,
                                          new_sf, new_sb)
        ao_lat = _attention_lat(l, lam_init, zb, zq, cache_attn_k, cache_attn_v, cos, sin, attn_lambda, subln)
        ro_lat = _retention_lat(l, zb, state_ret_fwd, state_ret_bwd, decay_f, decay_b, ret_nw)
        x, h2s, aff_n = _merge(l, x, ao, ao_lat, ro, ro_lat, zb, mod, conv_w, w_br_attn_bf, w_br_conv_bf,
                               w_br_ret_bf, w_out_bf, norm2, w_router_pad)
        idx, gates = _topk(aff_n, tmat)
        idx_flat = idx.reshape(2 * N_EXPERTS * CAP)
        ye, w_in_bf = _expert_ffn(l, idx_flat, h2s, gates, w_exp_gate, w_exp_up, w_exp_down, w_in)
        y = _combine(idx_flat, ye)

    y_prompt, y_sample = _final(x, y, mod, final_norm_w)
    return (y_prompt.reshape(N_CTX_SEQ, T_CTX, D), y_sample.reshape(N_LAT_SEQ, T_LAT, D), new_k, new_v,
            new_sf, new_sb)
```

```python
import functools
import math

import jax
import jax.numpy as jnp
import numpy as np
from jax import lax
from jax.experimental import pallas as pl
from jax.experimental.pallas import tpu as pltpu

F32 = jnp.float32
BF16 = jnp.bfloat16

D = 1024
DEPTH = 2
N_CTX_SEQ = 16
T_CTX = 256
N_LAT_SEQ = 4
T_LAT = 1024
PAST = 256
N_CTX = N_CTX_SEQ * T_CTX
N_LAT = N_LAT_SEQ * T_LAT
N_TOK = N_CTX + N_LAT
TM = 256
N_TILES = N_TOK // TM
CTX_TILES = N_CTX // TM
LAT_TILES_PER_SEQ = T_LAT // TM
HEADS = 4
HD = 128
GRID_W = 64
N_ROPE_FREQ = 16
ROPE_BASE = 10000.0
IN_COLS = 8192
N_MOD = 6
N_EXPERTS = 16
CAP = 512
FF = 2048
TF = 512
EPS = 1e-6
RET_SCALE = HD ** -0.5
ATTN_SCALE = 64 ** -0.5
LOG2_E = math.log2(math.e)
SLAB = D // 128
VMEM_LIMIT = 56 * 1024 * 1024

QB, KB, VB = 0, 4, 8
RQB, RKB, RVB, RGB = 24, 28, 32, 36
CONV_B_COL, CONV_U_COL = 3, 4
MERGE_GATE_COL = 5
MERGE_SUB = 2


def _sigmoid(x):
    return 0.5 * jnp.tanh(0.5 * x) + 0.5


def _log_sigmoid(x):
    return jnp.minimum(x, 0.0) - jnp.log(1.0 + jnp.exp(-jnp.abs(x)))


def _rms(x):
    return x * lax.rsqrt(jnp.mean(x * x, axis=-1, keepdims=True) + EPS)


def _dot(a, b):
    return jnp.dot(a, b, preferred_element_type=F32)


def _dot_nt(a, b):
    return lax.dot_general(a, b, (((1,), (1,)), ((), ())), preferred_element_type=F32)


def _dot_tn(a, b):
    return lax.dot_general(a, b, (((0,), (0,)), ((), ())), preferred_element_type=F32)


def _to_slabs(ref, x):
    n = x.shape[0]
    for s in range(SLAB):
        ref[pl.ds(s, n, stride=SLAB), :] = x[:, s * 128:(s + 1) * 128]


def _from_slabs(ref):
    n = ref.shape[0] // SLAB
    return jnp.concatenate([ref[pl.ds(s, n, stride=SLAB), :] for s in range(SLAB)], axis=-1)


def _mod_row(i):
    return jnp.where(i < CTX_TILES, 0, 1 + (i - CTX_TILES) // LAT_TILES_PER_SEQ)


def _params(n_axes):
    return pltpu.CompilerParams(
        dimension_semantics=("arbitrary",) * n_axes, vmem_limit_bytes=VMEM_LIMIT)


def _mod_kernel(c_ref, w_ref, b_ref, o_ref):
    c = c_ref[...]
    s = (c * _sigmoid(c)).astype(BF16)
    o_ref[...] = _dot(s, w_ref[...].astype(BF16)) + b_ref[...]


def _modulation(cvec, w_ada, b_ada):
    tn = 1024
    return pl.pallas_call(
        _mod_kernel,
        grid=(DEPTH, N_MOD * D // tn),
        in_specs=[
            pl.BlockSpec((8, D), lambda l, n: (0, 0)),
            pl.BlockSpec((None, D, tn), lambda l, n: (l, 0, n)),
            pl.BlockSpec((None, 1, tn), lambda l, n: (l, 0, n)),
        ],
        out_specs=pl.BlockSpec((None, 8, tn), lambda l, n: (l, 0, n)),
        out_shape=jax.ShapeDtypeStruct((DEPTH, 8, N_MOD * D), F32),
        compiler_params=_params(2),
        name="modulation",
    )(cvec, w_ada, b_ada.reshape(DEPTH, 1, N_MOD * D))


def _inproj_kernel(has_y, x_pair, n_prev, *refs):
    refs = list(refs)
    x_ref = refs.pop(0)
    if x_pair:
        xs_ref = refs.pop(0)
    if has_y:
        y_ref, modp_ref = refs.pop(0), refs.pop(0)
    mod_ref, n1_ref, w_ref = refs.pop(0), refs.pop(0), refs.pop(0)
    if n_prev:
        kp_ref, vp_ref = refs.pop(0), refs.pop(0)
    zb_ref, zq_ref, kn_ref, vn_ref = refs[0:4]
    i = pl.program_id(0)

    def normalize():
        x = x_ref[...]
        if x_pair:
            x = jnp.where(i < N_TILES - CTX_TILES, xs_ref[...], x)
        if has_y:
            x = x + modp_ref[0][:, 5 * D:6 * D] * _from_slabs(y_ref)
            refs[4][...] = x
        m = mod_ref[0]
        shift1 = m[:, 0:D]
        scale1 = m[:, D:2 * D]
        return ((_rms(x) * n1_ref[...]) * (1.0 + scale1) + shift1).astype(BF16)

    def project(h):
        cw = 1024
        for c in range(IN_COLS // cw):
            z = _dot(h, w_ref[:, c * cw:(c + 1) * cw])
            if c == CONV_U_COL // 2:
                z = jnp.concatenate([z[:, 0:512] * z[:, 512:cw], z[:, 512:cw]], axis=-1)
            elif c == RGB // 8:
                g = z[:, 512:cw]
                z = jnp.concatenate([z[:, 0:512], g * _sigmoid(g)], axis=-1)
            elif c >= MERGE_GATE_COL:
                z = _sigmoid(z)
            if c == 0:
                zb_ref[:, 0:cw] = jnp.concatenate([z[:, 0:512] * (ATTN_SCALE * LOG2_E), z[:, 512:cw]],
                                                  axis=-1).astype(BF16)
            else:
                zb_ref[:, c * cw:(c + 1) * cw] = z.astype(BF16)
            if c == 0:
                zq_ref[...] = z
                for hh in range(HEADS):
                    kn_ref[n_prev, hh] = z[:, 512 + hh * HD:512 + (hh + 1) * HD]
            if c == 1:
                for hh in range(HEADS):
                    vn_ref[n_prev, hh] = z[:, hh * HD:(hh + 1) * HD]
        if n_prev:
            kn_ref[0:n_prev] = kp_ref[...]
            vn_ref[0:n_prev] = vp_ref[...]

    project(normalize())


def _inproj(l, x, y, mod, norm1_w, w_in_bf, k_prev, v_prev):
    has_y = y is not None
    t = lambda i: (i + CTX_TILES) % N_TILES
    n_lat = N_TILES - CTX_TILES
    norm_tile = pl.BlockSpec((TM, D), lambda i: (t(i), 0))
    ctx_i = lambda i: jnp.maximum(i - n_lat, 0)
    kv_spec = lambda m: pl.BlockSpec((None, m, HEADS, T_CTX, HD), lambda i: (ctx_i(i), 0, 0, 0, 0))
    kv_shape = jax.ShapeDtypeStruct((N_CTX_SEQ, l + 1, HEADS, T_CTX, HD), F32)
    x_pair = isinstance(x, tuple)
    if x_pair:
        in_specs = [pl.BlockSpec((TM, D), lambda i: (jnp.maximum(i - n_lat, 0), 0)),
                    pl.BlockSpec((TM, D), lambda i: (jnp.minimum(i, n_lat - 1), 0))]
        args = list(x)
    else:
        in_specs = [norm_tile]
        args = [x]
    if has_y:
        in_specs += [pl.BlockSpec((TM * SLAB, 128), lambda i: (t(i), 0)),
                     pl.BlockSpec((1, 1, N_MOD * D), lambda i: ((l - 1) * 8 + _mod_row(t(i)), 0, 0))]
        args += [y, mod]
    in_specs += [
        pl.BlockSpec((1, 1, N_MOD * D), lambda i: (l * 8 + _mod_row(t(i)), 0, 0)),
        pl.BlockSpec((None, 1, D), lambda i: (l, 0, 0)),
        pl.BlockSpec((D, IN_COLS), lambda i: (0, 0), pipeline_mode=pl.Buffered(1)),
    ]
    args += [mod, norm1_w, w_in_bf]
    if l:
        in_specs += [kv_spec(l), kv_spec(l)]
        args += [k_prev, v_prev]
    out_specs = [pl.BlockSpec((TM, IN_COLS), lambda i: (t(i), 0)),
                 pl.BlockSpec((TM, 1024), lambda i: (t(i), 0)),
                 kv_spec(l + 1), kv_spec(l + 1)]
    out_shape = [jax.ShapeDtypeStruct((N_TOK, IN_COLS), BF16),
                 jax.ShapeDtypeStruct((N_TOK, 1024), F32),
                 kv_shape, kv_shape]
    if has_y:
        out_specs.append(norm_tile)
        out_shape.append(jax.ShapeDtypeStruct((N_TOK, D), F32))
    return pl.pallas_call(
        functools.partial(_inproj_kernel, has_y, x_pair, l),
        grid=(N_TILES,),
        in_specs=in_specs,
        out_specs=out_specs,
        out_shape=out_shape,
        compiler_params=_params(1),
        name="inproj",
    )(*args)


def _lambda(lam_ref, lam_init):
    lv = lam_ref[...]
    a = jnp.sum(lv[0:1] * lv[1:2], axis=-1, keepdims=True)
    b = jnp.sum(lv[2:3] * lv[3:4], axis=-1, keepdims=True)
    return jnp.exp(a) - jnp.exp(b) + lam_init


def _interleave(*stage_generators):
    live = list(stage_generators)
    while live:
        for g in list(live):
            try:
                next(g)
            except StopIteration:
                live.remove(g)


def _diff_attention(lam_init, qs, keys, vs, lam, sw, keys_transposed, o_ref, sls):
    lane = lax.broadcasted_iota(jnp.int32, qs[0].shape, 1)
    score = _dot if keys_transposed else _dot_nt
    maps = [(h, jnp.where(keep, q, jnp.zeros_like(q)))
            for h, q in enumerate(qs) for keep in (lane < 64, lane >= 64)]
    s = [score(qm, keys[h]) for h, qm in maps]
    yield
    e = [jnp.exp2(x - jnp.max(x, axis=-1, keepdims=True)) for x in s]
    yield
    r = [1.0 / jnp.sum(x, axis=-1, keepdims=True) for x in e]
    a = [(e[2 * h] * r[2 * h] - e[2 * h + 1] * (lam * r[2 * h + 1])).astype(BF16) for h in range(len(qs))]
    yield
    o = [_dot(a[h], vs[h]) for h in range(len(qs))]
    yield
    for sl, x in zip(sls, o):
        o_ref[:, sl] = ((_rms(x) * sw) * (1.0 - lam_init)).astype(BF16)


def _scaled_q(q):
    return (q * (ATTN_SCALE * LOG2_E)).astype(BF16)


def _attn_ctx_stages(lam_init, q_ref, k_ref, v_ref, lam_ref, sw_ref, o_ref):
    lam = _lambda(lam_ref, lam_init)
    sls = [slice(h * HD, (h + 1) * HD) for h in range(HEADS)]
    yield from _diff_attention(lam_init, [q_ref[:, sl] for sl in sls], [k_ref[:, sl] for sl in sls],
                               [v_ref[:, sl] for sl in sls], lam, sw_ref[...], False, o_ref, sls)


def _rope(x, cos, sin_signed):
    lane = lax.broadcasted_iota(jnp.int32, x.shape, 1)
    partner = jnp.where(lane % 32 < 16, pltpu.roll(x, 112, 1), pltpu.roll(x, 16, 1))
    return x * cos + partner * sin_signed


def _attn_lat_stages(lam_init, q_ref, k_ref, v_ref, ck_ref, cv_ref, cos_ref, sin_ref,
                     cosq_ref, sinq_ref, lam_ref, sw_ref, o_ref, k_t, vall):
    @pl.when(pl.program_id(1) == 0)
    def _():
        for h in range(HEADS):
            sl = slice(h * HD, (h + 1) * HD)
            k_t[h, :, 0:PAST] = ck_ref[h].T.astype(BF16)
            k_t[h, :, PAST:] = _rope(k_ref[:, sl], cos_ref[...], sin_ref[...]).T.astype(BF16)
            vall[h, 0:PAST, :] = cv_ref[h].astype(BF16)
            vall[h, PAST:, :] = v_ref[:, sl]

    lam = _lambda(lam_ref, lam_init)
    sls = [slice(h * HD, (h + 1) * HD) for h in range(HEADS)]
    qs = [_scaled_q(_rope(q_ref[:, sl], cosq_ref[...], sinq_ref[...])) for sl in sls]
    yield from _diff_attention(lam_init, qs, [k_t[h] for h in range(HEADS)], [vall[h] for h in range(HEADS)],
                               lam, sw_ref[...], True, o_ref, sls)


def _mix_ctx(l, lam_init, zb, attn_lambda, subln_w, decay_f, decay_b, ret_norm_w, sf_prev, sb_prev):
    width = HEADS * HD
    col = lambda c: pl.BlockSpec((T_CTX, width), lambda b: (b, c // HEADS))
    dec = pl.BlockSpec((HEADS, 1, 1), lambda b: (l, 0, 0))
    st_spec = lambda n: pl.BlockSpec((None, n, HEADS, HD, HD), lambda b: (b, 0, 0, 0, 0))
    st_shape = jax.ShapeDtypeStruct((N_CTX_SEQ, l + 1, HEADS, HD, HD), F32)
    out_tile = pl.BlockSpec((T_CTX, width), lambda b: (b, 0))
    out_shape = jax.ShapeDtypeStruct((N_CTX, width), BF16)
    in_specs = [
        col(QB), col(KB), col(VB),
        pl.BlockSpec((None, 4, 64), lambda b: (l, 0, 0)),
        pl.BlockSpec((None, 1, HD), lambda b: (l, 0, 0)),
        col(RQB), col(RKB), col(RVB), col(RGB),
        dec, dec, pl.BlockSpec((HEADS, 1, HD), lambda b: (l, 0, 0)),
    ]
    args = [zb, zb, zb, attn_lambda, subln_w, zb, zb, zb, zb, decay_f, decay_b, ret_norm_w]
    if l:
        in_specs += [st_spec(l), st_spec(l)]
        args += [sf_prev, sb_prev]
    return pl.pallas_call(
        functools.partial(_mix_ctx_kernel, lam_init, l),
        grid=(N_CTX_SEQ,),
        in_specs=in_specs,
        out_specs=[out_tile, out_tile, st_spec(l + 1), st_spec(l + 1)],
        out_shape=[out_shape, out_shape, st_shape, st_shape],
        scratch_shapes=[pltpu.VMEM((HEADS, T_CTX, T_CTX), F32)],
        compiler_params=_params(1),
        name="mix_ctx",
    )(*args)


def _decay_matrix(lgf, lgb, row0, tq, tk):
    i = row0 + lax.broadcasted_iota(jnp.int32, (tq, tk), 0)
    j = lax.broadcasted_iota(jnp.int32, (tq, tk), 1)
    d = (i - j).astype(F32)
    fwd = jnp.where(d >= 0.0, jnp.exp(lgf * jnp.maximum(d, 0.0)), 0.0)
    bwd = jnp.where(d <= 0.0, jnp.exp(lgb * jnp.maximum(-d, 0.0)), 0.0)
    return fwd + bwd


def _ret_finish(o, g, nw):
    return (g.astype(F32) * (_rms(o) * nw)).astype(BF16)


def _ret_ctx_stages(n_prev, q_ref, k_ref, v_ref, g_ref, df_ref, db_ref, nw_ref, o_ref, sf_ref, sb_ref, dmat):
    j = lax.broadcasted_iota(jnp.int32, (T_CTX, 1), 0).astype(F32)
    heads = range(HEADS)
    sls = [slice(h * HD, (h + 1) * HD) for h in heads]
    q = [q_ref[:, sl] for sl in sls]
    k = [k_ref[:, sl] for sl in sls]
    v = [v_ref[:, sl] for sl in sls]
    s = [(_dot_nt(q[h], k[h]) * dmat[h]).astype(BF16) for h in heads]
    yield
    o = [_dot(s[h], v[h]) for h in heads]
    yield
    lgf = [_log_sigmoid(df_ref[h]) for h in heads]
    lgb = [_log_sigmoid(db_ref[h]) for h in heads]
    kf = [k[h].astype(F32) * RET_SCALE for h in heads]
    kfw = [(kf[h] * jnp.exp(lgf[h] * (T_CTX - 1.0 - j))).astype(BF16) for h in heads]
    kbw = [(kf[h] * jnp.exp(lgb[h] * j)).astype(BF16) for h in heads]
    yield
    for h in heads:
        sf_ref[n_prev, h] = _dot_tn(kfw[h], v[h])
        sb_ref[n_prev, h] = _dot_tn(kbw[h], v[h])
    yield
    for h in heads:
        o_ref[:, sls[h]] = _ret_finish(o[h], g_ref[:, sls[h]], nw_ref[h])


def _mix_ctx_kernel(lam_init, n_prev, *refs):
    refs = list(refs)
    aq_ref, ak_ref, av_ref, lam_ref, sw_ref = refs[0:5]
    rq_ref, rk_ref, rv_ref, rg_ref, df_ref, db_ref, nw_ref = refs[5:12]
    refs = refs[12:]
    if n_prev:
        sfp_ref, sbp_ref = refs.pop(0), refs.pop(0)
    ao_ref, ro_ref, sf_ref, sb_ref, dmat = refs

    @pl.when(pl.program_id(0) == 0)
    def _():
        for h in range(HEADS):
            dmat[h] = RET_SCALE * _decay_matrix(_log_sigmoid(df_ref[h]), _log_sigmoid(db_ref[h]), 0, T_CTX, T_CTX)

    if n_prev:
        sf_ref[0:n_prev] = sfp_ref[...]
        sb_ref[0:n_prev] = sbp_ref[...]
    _interleave(
        _attn_ctx_stages(lam_init, aq_ref, ak_ref, av_ref, lam_ref, sw_ref, ao_ref),
        _ret_ctx_stages(n_prev, rq_ref, rk_ref, rv_ref, rg_ref, df_ref, db_ref, nw_ref, ro_ref, sf_ref, sb_ref, dmat))


N_DCHUNK = 2 * LAT_TILES_PER_SEQ - 1


def _ret_lat_stages(q_ref, k_ref, v_ref, g_ref, s0f_ref, s0b_ref, df_ref, db_ref, nw_ref, o_ref, strip, k_t):
    b = pl.program_id(0)
    j = pl.program_id(1)

    @pl.when(jnp.logical_and(b == 0, j == 0))
    def _():
        for h in range(HEADS):
            lgf = _log_sigmoid(df_ref[h])
            lgb = _log_sigmoid(db_ref[h])
            for c in range(N_DCHUNK):
                strip[h, c] = RET_SCALE * _decay_matrix(lgf, lgb, T_LAT - TM - c * TM, TM, TM)

    @pl.when(j == 0)
    def _():
        for h in range(HEADS):
            k_t[h] = k_ref[:, h * HD:(h + 1) * HD].astype(F32).T.astype(BF16)

    i = (j * TM + lax.broadcasted_iota(jnp.int32, (TM, 1), 0)).astype(F32)
    c0 = LAT_TILES_PER_SEQ - 1 - j
    heads = range(HEADS)
    sls = [slice(h * HD, (h + 1) * HD) for h in heads]
    q = [q_ref[:, sl] for sl in sls]
    dmat = [jnp.concatenate([strip[h, c0 + c] for c in range(LAT_TILES_PER_SEQ)], axis=-1) for h in heads]
    s = [(_dot(q[h], k_t[h]) * dmat[h]).astype(BF16) for h in heads]
    yield
    o = [_dot(s[h], v_ref[:, sls[h]]) for h in heads]
    yield
    of = [_dot(q[h], s0f_ref[h].astype(BF16)) for h in heads]
    ob = [_dot(q[h], s0b_ref[h].astype(BF16)) for h in heads]
    yield
    for h in heads:
        tot = o[h] + jnp.exp(_log_sigmoid(df_ref[h]) * (i + 1.0)) * of[h]
        tot = tot + jnp.exp(_log_sigmoid(db_ref[h]) * (T_LAT - i)) * ob[h]
        o_ref[:, sls[h]] = _ret_finish(tot, g_ref[:, sls[h]], nw_ref[h])


def _mix_lat_kernel(lam_init, *refs):
    attn_in, ret_in = refs[0:11], refs[11:20]
    ao_ref, ro_ref, ak_t, vall, strip, rk_t = refs[20:]
    _interleave(_attn_lat_stages(lam_init, *attn_in, ao_ref, ak_t, vall))
    _interleave(_ret_lat_stages(*ret_in, ro_ref, strip, rk_t))


def _mix_lat(l, lam_init, zb, zq, cache_k, cache_v, cos, sin, attn_lambda, subln_w,
             state_f, state_b, decay_f, decay_b, ret_norm_w):
    width = HEADS * HD
    lat_row = lambda b, j: CTX_TILES + LAT_TILES_PER_SEQ * b + j
    seq_row = lambda b: N_CTX // T_LAT + b
    q_tile = lambda c: pl.BlockSpec((TM, width), lambda b, j: (lat_row(b, j), c // HEADS))
    seq = lambda c: pl.BlockSpec((T_LAT, width), lambda b, j: (seq_row(b), c // HEADS))
    cache = pl.BlockSpec((None, None, HEADS, PAST, HD), lambda b, j: (b, l, 0, 0, 0))
    state = pl.BlockSpec((None, None, HEADS, HD, HD), lambda b, j: (b, l, 0, 0, 0))
    dec = pl.BlockSpec((HEADS, 1, 1), lambda b, j: (l, 0, 0))
    out_tile = pl.BlockSpec((TM, width), lambda b, j: (LAT_TILES_PER_SEQ * b + j, 0))
    out_shape = jax.ShapeDtypeStruct((N_LAT, width), BF16)
    return pl.pallas_call(
        functools.partial(_mix_lat_kernel, lam_init),
        grid=(N_LAT_SEQ, LAT_TILES_PER_SEQ),
        in_specs=[
            q_tile(QB), seq(KB), seq(VB), cache, cache,
            pl.BlockSpec((T_LAT, HD), lambda b, j: (0, 0)),
            pl.BlockSpec((T_LAT, HD), lambda b, j: (0, 0)),
            pl.BlockSpec((TM, HD), lambda b, j: (j, 0)),
            pl.BlockSpec((TM, HD), lambda b, j: (j, 0)),
            pl.BlockSpec((None, 4, 64), lambda b, j: (l, 0, 0)),
            pl.BlockSpec((None, 1, HD), lambda b, j: (l, 0, 0)),
            q_tile(RQB), seq(RKB), seq(RVB), q_tile(RGB), state, state, dec, dec,
            pl.BlockSpec((HEADS, 1, HD), lambda b, j: (l, 0, 0)),
        ],
        out_specs=[out_tile, out_tile],
        out_shape=[out_shape, out_shape],
        scratch_shapes=[pltpu.VMEM((HEADS, HD, PAST + T_LAT), BF16),
                        pltpu.VMEM((HEADS, PAST + T_LAT, HD), BF16),
                        pltpu.VMEM((HEADS, N_DCHUNK, TM, TM), F32),
                        pltpu.VMEM((HEADS, HD, T_LAT), BF16)],
        compiler_params=_params(2),
        name="mix_lat",
    )(zq, zq, zb, cache_k, cache_v, cos, sin, cos, sin, attn_lambda, subln_w,
      zb, zb, zb, zb, state_f, state_b, decay_f, decay_b, ret_norm_w)


def _merge_kernel(x_pair, *refs):
    refs = list(refs)
    x_ref = refs.pop(0)
    xs_ref = refs.pop(0) if x_pair else None
    (aoc_ref, aol_ref, roc_ref, rol_ref, cb_ref, u_ref, up_ref, un_ref,
     mg0_ref, mg1_ref, mg2_ref, mod_ref, cw_ref, wa_ref, wc_ref, wr_ref, wo_ref, n2_ref, wrt_ref,
     x1_ref, h2_ref, aff_ref) = refs
    i = pl.program_id(0)
    is_ctx = i < CTX_TILES // MERGE_SUB
    m = mod_ref[0]
    gate1 = m[:, 2 * D:3 * D]
    shift2 = m[:, 3 * D:4 * D]
    scale2 = m[:, 4 * D:5 * D]
    cw = cw_ref[...]
    u_all = u_ref[...].astype(F32)
    r = lax.broadcasted_iota(jnp.int32, (TM, 512), 0)
    lane = lax.broadcasted_iota(jnp.int32, (TM, 128), 1)

    subs = range(MERGE_SUB)
    rows = [slice(sub * TM, (sub + 1) * TM) for sub in subs]

    def conv_out(sub):
        j = (i * MERGE_SUB + sub - CTX_TILES) % LAT_TILES_PER_SEQ
        seq_first = jnp.logical_or(is_ctx, j == 0)
        seq_last = jnp.logical_or(is_ctx, j == LAT_TILES_PER_SEQ - 1)
        u = u_all[rows[sub], :]
        up = up_ref[...].astype(F32)[15:16, :] if sub == 0 else u_all[sub * TM - 1:sub * TM, :]
        dn = un_ref[...].astype(F32)[0:1, :] if sub == MERGE_SUB - 1 else u_all[(sub + 1) * TM:(sub + 1) * TM + 1, :]
        up = up * jnp.where(seq_first, 0.0, 1.0)
        dn = dn * jnp.where(seq_last, 0.0, 1.0)
        u_prev = jnp.where(r == 0, up, pltpu.roll(u, 1, 0))
        u_next = jnp.where(r == TM - 1, dn, pltpu.roll(u, TM - 1, 0))
        conv = u_prev * cw[0:1, :] + u * cw[1:2, :] + u_next * cw[2:3, :]
        return (cb_ref[rows[sub], :].astype(F32) * conv).astype(BF16)

    ao = [jnp.where(is_ctx, aoc_ref[rw, :], aol_ref[rw, :]) for rw in rows]
    ro = [jnp.where(is_ctx, roc_ref[rw, :], rol_ref[rw, :]) for rw in rows]
    conv_o = [conv_out(sub) for sub in subs]
    b_attn = [_dot(ao[s], wa_ref[...]) for s in subs]
    b_conv = [_dot(conv_o[s], wc_ref[...]) for s in subs]
    b_ret = [_dot(ro[s], wr_ref[...]) for s in subs]
    merged = [(mg0_ref[rows[s], :].astype(F32) * b_attn[s] + mg1_ref[rows[s], :].astype(F32) * b_conv[s]
               + mg2_ref[rows[s], :].astype(F32) * b_ret[s]).astype(BF16) for s in subs]
    proj = [_dot(merged[s], wo_ref[...]) for s in subs]
    x1 = []
    for s in subs:
        x_in = x_ref[rows[s], :]
        if x_pair:
            x_in = jnp.where(is_ctx, x_in, xs_ref[rows[s], :])
        x1.append(x_in + gate1 * proj[s])
        x1_ref[rows[s], :] = x1[s]
    h2 = [(_rms(x1[s]) * n2_ref[...]) * (1.0 + scale2) + shift2 for s in subs]
    logits = [_dot(h2[s].astype(BF16), wrt_ref[...]) for s in subs]
    valid = lane < N_EXPERTS
    for s in subs:
        _to_slabs(h2_ref.at[pl.ds(s * TM * SLAB, TM * SLAB), :], h2[s])
        lmax = jnp.max(jnp.where(valid, logits[s], -jnp.inf), axis=-1, keepdims=True)
        e = jnp.where(valid, jnp.exp(logits[s] - lmax), 0.0)
        aff_ref[rows[s], :] = e * (1.0 / jnp.sum(e, axis=-1, keepdims=True))


def _merge(l, x, ao, ao_lat, ro, ro_lat, zb, mod, conv_w, w_br_attn, w_br_conv, w_br_ret, w_out, norm2_w,
           w_router_pad):
    mt = MERGE_SUB * TM
    n_steps = N_TOK // mt
    ctx_steps = N_CTX // mt
    n16 = N_TOK // 16
    ctx_br = pl.BlockSpec((mt, 512), lambda i: (jnp.minimum(i, ctx_steps - 1), 0))
    lat_br = pl.BlockSpec((mt, 512), lambda i: (jnp.maximum(i - ctx_steps, 0), 0))
    col = lambda c: pl.BlockSpec((mt, 512), lambda i: (i, c))
    halo_p = lambda c: pl.BlockSpec((16, 512), lambda i: (jnp.maximum(i * (mt // 16) - 1, 0), c))
    halo_n = lambda c: pl.BlockSpec((16, 512), lambda i: (jnp.minimum((i + 1) * (mt // 16), n16 - 1), c))
    mgs = lambda c: pl.BlockSpec((mt, D), lambda i: (i, c))
    wbr = pl.BlockSpec((None, 512, D), lambda i: (l, 0, 0))
    x_pair = isinstance(x, tuple)
    if x_pair:
        x_specs = [pl.BlockSpec((mt, D), lambda i: (jnp.minimum(i, ctx_steps - 1), 0)),
                   pl.BlockSpec((mt, D), lambda i: (jnp.maximum(i - ctx_steps, 0), 0))]
        x_args = list(x)
    else:
        x_specs = [pl.BlockSpec((mt, D), lambda i: (i, 0))]
        x_args = [x]
    return pl.pallas_call(
        functools.partial(_merge_kernel, x_pair),
        grid=(n_steps,),
        in_specs=x_specs + [
            ctx_br, lat_br, ctx_br, lat_br,
            col(CONV_B_COL), col(CONV_U_COL), halo_p(CONV_U_COL), halo_n(CONV_U_COL),
            mgs(MERGE_GATE_COL), mgs(MERGE_GATE_COL + 1), mgs(MERGE_GATE_COL + 2),
            pl.BlockSpec((1, 1, N_MOD * D), lambda i: (l * 8 + _mod_row(i * MERGE_SUB), 0, 0)),
            pl.BlockSpec((None, 3, 512), lambda i: (l, 0, 0)),
            wbr, wbr, wbr,
            pl.BlockSpec((None, D, D), lambda i: (l, 0, 0)),
            pl.BlockSpec((None, 1, D), lambda i: (l, 0, 0)),
            pl.BlockSpec((None, D, 128), lambda i: (l, 0, 0)),
        ],
        out_specs=[
            pl.BlockSpec((mt, D), lambda i: (i, 0)),
            pl.BlockSpec((mt * SLAB, 128), lambda i: (i, 0)),
            pl.BlockSpec((mt, 128), lambda i: (i, 0)),
        ],
        out_shape=[
            jax.ShapeDtypeStruct((N_TOK, D), F32),
            jax.ShapeDtypeStruct((N_TOK * SLAB, 128), F32),
            jax.ShapeDtypeStruct((N_TOK, 128), F32),
        ],
        compiler_params=_params(1),
        name="merge",
    )(*x_args, ao, ao_lat, ro, ro_lat, zb, zb, zb, zb, zb, zb, zb, mod, conv_w,
      w_br_attn, w_br_conv, w_br_ret, w_out, norm2_w, w_router_pad)


def _cumsum_lanes(x, tri):
    run = jnp.zeros((x.shape[0], 1), F32)
    outs = []
    for b in range(x.shape[1] // 128):
        cs = _dot(x[:, b * 128:(b + 1) * 128].astype(BF16), tri) + run
        run = cs[:, 127:128]
        outs.append(cs)
    return jnp.concatenate(outs, axis=-1)


GATE_LANE = (0, 16, 32)
IDX_LANE = 48
BISECT_GROUP = 4


def _topk_kernel(an_ref, tmat_ref, idx_ref, gate_ref):
    an = an_ref[...]
    a = an.T[0:N_EXPERTS, :]
    n = a.shape[1]
    kf = float(CAP)

    hi = an.astype(BF16).astype(F32)
    mid = (an - hi).astype(BF16).astype(F32)
    lo = ((an - hi) - mid).astype(BF16).astype(F32)
    table = (tmat_ref[...].astype(F32) + hi + pltpu.roll(mid, GATE_LANE[1], 1)
             + pltpu.roll(lo, GATE_LANE[2], 1)).astype(BF16)

    def count_gt(thr):
        return jnp.sum(jnp.where(a > thr, 1.0, 0.0), axis=-1, keepdims=True)

    def span(lo, hi):
        inside = jnp.logical_and(a > lo, a <= hi)
        cmax = jnp.max(jnp.where(inside, a, -jnp.inf), axis=-1, keepdims=True)
        cmin = jnp.min(jnp.where(inside, a, jnp.inf), axis=-1, keepdims=True)
        return cmax, cmin

    def cond(c):
        return jnp.logical_and(c[2] > 0, c[3] < 400)

    def body(c):
        lo, hi, _, it = c
        for _ in range(BISECT_GROUP):
            mid = 0.5 * (lo + hi)
            ge = count_gt(mid) >= kf
            lo = jnp.where(ge, mid, lo)
            hi = jnp.where(ge, hi, mid)
        cmax, cmin = span(lo, hi)
        open_rows = jnp.max(jnp.where(cmax != cmin, 1, 0))
        return lo, hi, open_rows, it + 1

    lo0 = jnp.full((N_EXPERTS, 1), -1.0, F32)
    hi0 = jnp.max(a, axis=-1, keepdims=True)
    cmax0, cmin0 = span(lo0, hi0)
    lo, hi, _, _ = lax.while_loop(
        cond, body, (lo0, hi0, jnp.max(jnp.where(cmax0 != cmin0, 1, 0)), jnp.int32(0)))
    thr, _ = span(lo, hi)

    r = lax.broadcasted_iota(jnp.int32, (128, 128), 0)
    c = lax.broadcasted_iota(jnp.int32, (128, 128), 1)
    tri = jnp.where(r <= c, 1.0, 0.0).astype(BF16)
    gt = a > thr
    eq = jnp.where(a == thr, 1.0, 0.0)
    need = kf - count_gt(thr)
    eq_before = _cumsum_lanes(eq, tri) - eq
    sel = jnp.where(jnp.logical_or(gt, jnp.logical_and(eq > 0.0, eq_before < need)), 1.0, 0.0)
    pos = _cumsum_lanes(sel, tri) - 1.0
    slot = jnp.where(sel > 0.0, pos, -1.0).astype(jnp.int32)

    p_iota = lax.broadcasted_iota(jnp.int32, (CAP, 1024), 0)
    lane = lax.broadcasted_iota(jnp.int32, (CAP, 128), 1)
    for e in range(N_EXPERTS):
        acc = jnp.zeros((CAP, 128), F32)
        for cb in range(n // 1024):
            onehot = jnp.where(p_iota == slot[e:e + 1, cb * 1024:(cb + 1) * 1024], 1.0, 0.0).astype(BF16)
            acc = acc + _dot(onehot, table[cb * 1024:(cb + 1) * 1024, :])
        acc_t = acc.T
        idx_ref[0, e] = (acc_t[IDX_LANE:IDX_LANE + 1, :] * 64.0 + acc_t[IDX_LANE + 1:IDX_LANE + 2, :]).astype(jnp.int32)
        g = jnp.zeros((CAP, 1), F32)
        for off in GATE_LANE:
            g = g + jnp.sum(jnp.where(lane == off + e, acc, 0.0), axis=-1, keepdims=True)
        gate_ref[0, e] = g


def _topk(aff_n, tmat):
    out_spec = pl.BlockSpec((1, N_EXPERTS, CAP, 1), lambda s: (s, 0, 0, 0))
    return pl.pallas_call(
        _topk_kernel,
        grid=(2,),
        in_specs=[
            pl.BlockSpec((N_CTX, 128), lambda s: (s, 0)),
            pl.BlockSpec((N_CTX, 128), lambda s: (0, 0)),
        ],
        out_specs=[pl.BlockSpec((1, N_EXPERTS, 1, CAP), lambda s: (s, 0, 0, 0)), out_spec],
        out_shape=[jax.ShapeDtypeStruct((2, N_EXPERTS, 1, CAP), jnp.int32),
                   jax.ShapeDtypeStruct((2, N_EXPERTS, CAP, 1), F32)],
        compiler_params=_params(1),
        name="topk",
    )(aff_n, tmat)


N_FT = FF // TF
GATHER_ROWS = 2 * CAP
ROWS_PER_STEP = GATHER_ROWS // N_FT


def _ffn_kernel(cast_next, idx_ref, h2_hbm, g_ref, wg_ref, wu_ref, wd_ref, *refs):
    if cast_next:
        win_ref, ye_ref, wbf_ref, xg, xb, acc, sems = refs
        wbf_ref[...] = win_ref[...].astype(BF16)
    else:
        ye_ref, xg, xb, acc, sems = refs
    e = pl.program_id(0)
    f = pl.program_id(1)
    slot = e % 2

    def row_copy(expert, s, p, dst_slot):
        row = idx_ref[(s * N_EXPERTS + expert) * CAP + p] + s * N_CTX
        src = h2_hbm.at[pl.ds(pl.multiple_of(row * SLAB, SLAB), SLAB), :]
        dst = xg.at[dst_slot, pl.ds(pl.multiple_of((s * CAP + p) * SLAB, SLAB), SLAB), :]
        return pltpu.make_async_copy(src, dst, sems.at[dst_slot])

    def slot_wait(dst_slot):
        pltpu.make_async_copy(h2_hbm.at[pl.ds(0, GATHER_ROWS * SLAB), :], xg.at[dst_slot],
                              sems.at[dst_slot]).wait()

    @pl.when(jnp.logical_and(e == 0, f == 0))
    def _():
        def issue(r, carry):
            row_copy(0, r // CAP, r % CAP, 0).start()
            return carry
        lax.fori_loop(0, GATHER_ROWS, issue, 0)
        acc[...] = jnp.zeros_like(acc)

    @pl.when(f == 0)
    def _():
        slot_wait(slot)
        xb[...] = _from_slabs(xg.at[slot]).astype(BF16)

    nxt = jnp.minimum(e + 1, N_EXPERTS - 1)
    s_nxt = f // (N_FT // 2)
    p0 = (f % (N_FT // 2)) * ROWS_PER_STEP
    for u in range(ROWS_PER_STEP):
        row_copy(nxt, s_nxt, p0 + u, 1 - slot).start()

    x = xb[...]
    hg = _dot(x, wg_ref[...].astype(BF16))
    hu = _dot(x, wu_ref[...].astype(BF16))
    hdn = ((hg * _sigmoid(hg)) * hu).astype(BF16)
    acc[...] = jnp.where(f == 0, 0.0, acc[...]) + _dot(hdn, wd_ref[...].astype(BF16))

    @pl.when(f == N_FT - 1)
    def _():
        for s in range(2):
            _to_slabs(ye_ref.at[s], acc[s * CAP:(s + 1) * CAP, :] * g_ref[s])

    @pl.when(jnp.logical_and(e == N_EXPERTS - 1, f == N_FT - 1))
    def _():
        slot_wait(1 - slot)


def _expert_ffn(l, idx_flat, h2s, gates, w_gate, w_up, w_down, w_in):
    cast_next = l + 1 < DEPTH
    n_steps = N_EXPERTS * N_FT
    in_specs = [
        pl.BlockSpec(memory_space=pl.ANY),
        pl.BlockSpec((2, None, CAP, 1), lambda e, f, idx: (0, e, 0, 0)),
        pl.BlockSpec((None, None, D, TF), lambda e, f, idx: (l, e, 0, f)),
        pl.BlockSpec((None, None, D, TF), lambda e, f, idx: (l, e, 0, f)),
        pl.BlockSpec((None, None, TF, D), lambda e, f, idx: (l, e, f, 0)),
    ]
    args = [idx_flat, h2s, gates, w_gate, w_up, w_down]
    out_specs = [pl.BlockSpec((2, None, CAP * SLAB, 128), lambda e, f, idx: (0, e, 0, 0))]
    out_shape = [jax.ShapeDtypeStruct((2, N_EXPERTS, CAP * SLAB, 128), F32)]
    if cast_next:
        cols = IN_COLS // n_steps
        in_specs.append(pl.BlockSpec((None, D, cols), lambda e, f, idx: (l + 1, 0, e * N_FT + f)))
        args.append(w_in)
        out_specs.append(pl.BlockSpec((D, cols), lambda e, f, idx: (0, e * N_FT + f)))
        out_shape.append(jax.ShapeDtypeStruct((D, IN_COLS), BF16))
    grid_spec = pltpu.PrefetchScalarGridSpec(
        num_scalar_prefetch=1,
        grid=(N_EXPERTS, N_FT),
        in_specs=in_specs,
        out_specs=out_specs,
        scratch_shapes=[
            pltpu.VMEM((2, GATHER_ROWS * SLAB, 128), F32),
            pltpu.VMEM((GATHER_ROWS, D), BF16),
            pltpu.VMEM((GATHER_ROWS, D), F32),
            pltpu.SemaphoreType.DMA((2,)),
        ],
    )
    outs = pl.pallas_call(
        functools.partial(_ffn_kernel, cast_next),
        grid_spec=grid_spec,
        out_shape=out_shape,
        compiler_params=_params(2),
        name="expert_ffn",
    )(*args)
    return (outs[0], outs[1]) if cast_next else (outs[0], None)


SCATTER_UNROLL = 8
SCATTER_EXPERTS = 4


def _combine_kernel(idx_ref, ye_ref, y_ref):
    s = pl.program_id(0)
    eg = pl.program_id(1)

    @pl.when(eg == 0)
    def _():
        y_ref[...] = jnp.zeros_like(y_ref)

    for k in range(SCATTER_EXPERTS):
        base = (s * N_EXPERTS + eg * SCATTER_EXPERTS + k) * CAP
        for p0 in range(0, CAP, SCATTER_UNROLL):
            rows = [idx_ref[base + p0 + u] for u in range(SCATTER_UNROLL)]
            tiles = [pl.ds(pl.multiple_of(r * SLAB, SLAB), SLAB) for r in rows]
            vals = [y_ref[tiles[u], :] + ye_ref[k, pl.ds((p0 + u) * SLAB, SLAB), :] for u in range(SCATTER_UNROLL)]
            for u in range(SCATTER_UNROLL):
                y_ref[tiles[u], :] = vals[u]


def _combine(idx_flat, ye):
    grid_spec = pltpu.PrefetchScalarGridSpec(
        num_scalar_prefetch=1,
        grid=(2, N_EXPERTS // SCATTER_EXPERTS),
        in_specs=[pl.BlockSpec((None, SCATTER_EXPERTS, CAP * SLAB, 128), lambda s, eg, idx: (s, eg, 0, 0))],
        out_specs=pl.BlockSpec((N_CTX * SLAB, 128), lambda s, eg, idx: (s, 0)),
    )
    return pl.pallas_call(
        _combine_kernel,
        grid_spec=grid_spec,
        out_shape=jax.ShapeDtypeStruct((N_TOK * SLAB, 128), F32),
        compiler_params=_params(2),
        name="combine",
    )(idx_flat, ye)


def _final_kernel(x_ref, y_ref, mod_ref, w_ref, op_ref, os_ref):
    x = x_ref[...] + mod_ref[0][:, 5 * D:6 * D] * _from_slabs(y_ref)
    out = _rms(x) * w_ref[...]
    is_ctx = pl.program_id(0) < N_CTX // FINAL_ROWS

    @pl.when(is_ctx)
    def _():
        op_ref[...] = out

    @pl.when(jnp.logical_not(is_ctx))
    def _():
        os_ref[...] = out


FINAL_ROWS = 2 * TM


def _final(x1, y, mod, final_norm_w):
    ft = FINAL_ROWS
    ctx_steps = N_CTX // ft
    tile = pl.BlockSpec((ft, D), lambda i: (i, 0))
    return pl.pallas_call(
        _final_kernel,
        grid=(N_TOK // ft,),
        in_specs=[tile, pl.BlockSpec((ft * SLAB, 128), lambda i: (i, 0)),
                  pl.BlockSpec((1, 1, N_MOD * D), lambda i: ((DEPTH - 1) * 8 + _mod_row(i * (ft // TM)), 0, 0)),
                  pl.BlockSpec((1, D), lambda i: (0, 0))],
        out_specs=[pl.BlockSpec((ft, D), lambda i: (jnp.minimum(i, ctx_steps - 1), 0)),
                   pl.BlockSpec((ft, D), lambda i: (jnp.maximum(i - ctx_steps, 0), 0))],
        out_shape=[jax.ShapeDtypeStruct((N_CTX, D), F32), jax.ShapeDtypeStruct((N_LAT, D), F32)],
        compiler_params=_params(1),
        name="final_norm",
    )(x1, y, mod, final_norm_w.reshape(1, D))


def _rope_tables():
    t = np.arange(T_LAT)
    row = (t // GRID_W).astype(np.float32)
    col = (t % GRID_W).astype(np.float32)
    inv = jnp.asarray(ROPE_BASE, F32) ** (-jnp.arange(N_ROPE_FREQ, dtype=F32) / N_ROPE_FREQ)
    ang_r = jnp.asarray(row)[:, None] * inv
    ang_c = jnp.asarray(col)[:, None] * inv
    def group(ang):
        return jnp.concatenate([ang, ang], axis=-1)
    ang = jnp.concatenate([group(ang_r), group(ang_c), group(ang_r), group(ang_c)], axis=-1)
    sign = np.where(np.arange(HD) % 32 < 16, -1.0, 1.0).astype(np.float32)
    return jnp.cos(ang), jnp.sin(ang) * sign


def _index_table():
    t = np.arange(N_CTX)
    tm = np.zeros((N_CTX, 128), np.float32)
    tm[:, IDX_LANE] = t // 64
    tm[:, IDX_LANE + 1] = t % 64
    return jnp.asarray(tm, BF16)


def kernel(x_prompt, x_sample, c, cache_attn_k, cache_attn_v, state_ret_fwd, state_ret_bwd, c_ctx, w_ada, b_ada, norm1_w, norm2_w, w_in, attn_lambda, attn_subln_w, conv_w, ret_decay_fwd, ret_decay_bwd, ret_norm_w, w_br_attn, w_br_conv, w_br_ret, w_out, w_router, w_exp_gate, w_exp_up, w_exp_down, final_norm_w):
    x = (x_prompt.reshape(N_CTX, D), x_sample.reshape(N_LAT, D))
    cvec = jnp.concatenate([c_ctx[None, :], c, jnp.zeros((3, D), F32)], axis=0)
    mod = _modulation(cvec, w_ada, b_ada).reshape(DEPTH * 8, 1, N_MOD * D)

    w_in_bf = w_in[0].astype(BF16)
    w_br_attn_bf = w_br_attn.astype(BF16)
    w_br_conv_bf = w_br_conv.astype(BF16)
    w_br_ret_bf = w_br_ret.astype(BF16)
    w_out_bf = w_out.astype(BF16)
    w_router_pad = jnp.pad(w_router, ((0, 0), (0, 0), (0, 128 - N_EXPERTS))).astype(BF16)
    norm1 = norm1_w.reshape(DEPTH, 1, D)
    norm2 = norm2_w.reshape(DEPTH, 1, D)
    subln = attn_subln_w.reshape(DEPTH, 1, HD)
    decay_f = ret_decay_fwd.reshape(DEPTH * HEADS, 1, 1)
    decay_b = ret_decay_bwd.reshape(DEPTH * HEADS, 1, 1)
    ret_nw = ret_norm_w.reshape(DEPTH * HEADS, 1, HD)
    cos, sin = _rope_tables()
    tmat = _index_table()

    y = None
    new_k = new_v = new_sf = new_sb = None
    for l in range(DEPTH):
        lam_init = 0.8 - 0.6 * math.exp(-0.3 * l)
        if y is None:
            zb, zq, new_k, new_v = _inproj(l, x, None, mod, norm1, w_in_bf, new_k, new_v)
        else:
            zb, zq, new_k, new_v, x = _inproj(l, x, y, mod, norm1, w_in_bf, new_k, new_v)
        ao, ro, new_sf, new_sb = _mix_ctx(l, lam_init, zb, attn_lambda, subln, decay_f, decay_b, ret_nw,
                                          new_sf, new_sb)
        ao_lat, ro_lat = _mix_lat(l, lam_init, zb, zq, cache_attn_k, cache_attn_v, cos, sin, attn_lambda, subln,
                                  state_ret_fwd, state_ret_bwd, decay_f, decay_b, ret_nw)
        x, h2s, aff_n = _merge(l, x, ao, ao_lat, ro, ro_lat, zb, mod, conv_w, w_br_attn_bf, w_br_conv_bf,
                               w_br_ret_bf, w_out_bf, norm2, w_router_pad)
        idx, gates = _topk(aff_n, tmat)
        idx_flat = idx.reshape(2 * N_EXPERTS * CAP)
        ye, w_in_bf = _expert_ffn(l, idx_flat, h2s, gates, w_exp_gate, w_exp_up, w_exp_down, w_in)
        y = _combine(idx_flat, ye)

    y_prompt, y_sample = _final(x, y, mod, final_norm_w)
    return (y_prompt.reshape(N_CTX_SEQ, T_CTX, D), y_sample.reshape(N_LAT_SEQ, T_LAT, D), new_k, new_v,
            new_sf, new_sb)
```

```python
import functools
import math

import jax
import jax.numpy as jnp
import numpy as np
from jax import lax
from jax.experimental import pallas as pl
from jax.experimental.pallas import tpu as pltpu

F32 = jnp.float32
BF16 = jnp.bfloat16

D = 1024
DEPTH = 2
N_CTX_SEQ = 16
T_CTX = 256
N_LAT_SEQ = 4
T_LAT = 1024
PAST = 256
N_CTX = N_CTX_SEQ * T_CTX
N_LAT = N_LAT_SEQ * T_LAT
N_TOK = N_CTX + N_LAT
TM = 256
N_TILES = N_TOK // TM
CTX_TILES = N_CTX // TM
LAT_TILES_PER_SEQ = T_LAT // TM
HEADS = 4
HD = 128
GRID_W = 64
N_ROPE_FREQ = 16
ROPE_BASE = 10000.0
IN_COLS = 8192
N_MOD = 6
N_EXPERTS = 16
CAP = 512
FF = 2048
TF = 512
EPS = 1e-6
RET_SCALE = HD ** -0.5
ATTN_SCALE = 64 ** -0.5
LOG2_E = math.log2(math.e)
SLAB = D // 128
VMEM_LIMIT = 56 * 1024 * 1024

QB, KB, VB = 0, 4, 8
RQB, RKB, RVB, RGB = 24, 28, 32, 36
CONV_B_COL, CONV_U_COL = 3, 4
MERGE_GATE_COL = 5
MERGE_SUB = 2


def _sigmoid(x):
    return 0.5 * jnp.tanh(0.5 * x) + 0.5


def _log_sigmoid(x):
    return jnp.minimum(x, 0.0) - jnp.log(1.0 + jnp.exp(-jnp.abs(x)))


def _rms(x):
    return x * lax.rsqrt(jnp.mean(x * x, axis=-1, keepdims=True) + EPS)


def _dot(a, b):
    return jnp.dot(a, b, preferred_element_type=F32)


def _dot_nt(a, b):
    return lax.dot_general(a, b, (((1,), (1,)), ((), ())), preferred_element_type=F32)


def _dot_tn(a, b):
    return lax.dot_general(a, b, (((0,), (0,)), ((), ())), preferred_element_type=F32)


def _to_slabs(ref, x):
    n = x.shape[0]
    for s in range(SLAB):
        ref[pl.ds(s, n, stride=SLAB), :] = x[:, s * 128:(s + 1) * 128]


def _from_slabs(ref):
    n = ref.shape[0] // SLAB
    return jnp.concatenate([ref[pl.ds(s, n, stride=SLAB), :] for s in range(SLAB)], axis=-1)


def _mod_row(i):
    return jnp.where(i < CTX_TILES, 0, 1 + (i - CTX_TILES) // LAT_TILES_PER_SEQ)


def _params(n_axes):
    return pltpu.CompilerParams(
        dimension_semantics=("arbitrary",) * n_axes, vmem_limit_bytes=VMEM_LIMIT)


def _mod_kernel(c_ref, w_ref, b_ref, o_ref):
    c = c_ref[...]
    s = (c * _sigmoid(c)).astype(BF16)
    o_ref[...] = _dot(s, w_ref[...].astype(BF16)) + b_ref[...]


def _modulation(cvec, w_ada, b_ada):
    tn = 1024
    return pl.pallas_call(
        _mod_kernel,
        grid=(DEPTH, N_MOD * D // tn),
        in_specs=[
            pl.BlockSpec((8, D), lambda l, n: (0, 0)),
            pl.BlockSpec((None, D, tn), lambda l, n: (l, 0, n)),
            pl.BlockSpec((None, 1, tn), lambda l, n: (l, 0, n)),
        ],
        out_specs=pl.BlockSpec((None, 8, tn), lambda l, n: (l, 0, n)),
        out_shape=jax.ShapeDtypeStruct((DEPTH, 8, N_MOD * D), F32),
        compiler_params=_params(2),
        name="modulation",
    )(cvec, w_ada, b_ada.reshape(DEPTH, 1, N_MOD * D))


def _inproj_kernel(has_y, x_pair, n_prev, *refs):
    refs = list(refs)
    x_ref = refs.pop(0)
    if x_pair:
        xs_ref = refs.pop(0)
    if has_y:
        y_ref, modp_ref = refs.pop(0), refs.pop(0)
    mod_ref, n1_ref, w_ref = refs.pop(0), refs.pop(0), refs.pop(0)
    if n_prev:
        kp_ref, vp_ref = refs.pop(0), refs.pop(0)
    zb_ref, zq_ref, kn_ref, vn_ref = refs[0:4]
    i = pl.program_id(0)

    def normalize():
        x = x_ref[...]
        if x_pair:
            x = jnp.where(i < N_TILES - CTX_TILES, xs_ref[...], x)
        if has_y:
            x = x + modp_ref[0][:, 5 * D:6 * D] * _from_slabs(y_ref)
            refs[4][...] = x
        m = mod_ref[0]
        shift1 = m[:, 0:D]
        scale1 = m[:, D:2 * D]
        return ((_rms(x) * n1_ref[...]) * (1.0 + scale1) + shift1).astype(BF16)

    def project(h):
        cw = 1024
        for c in range(IN_COLS // cw):
            z = _dot(h, w_ref[:, c * cw:(c + 1) * cw])
            if c == CONV_U_COL // 2:
                z = jnp.concatenate([z[:, 0:512] * z[:, 512:cw], z[:, 512:cw]], axis=-1)
            elif c == RGB // 8:
                g = z[:, 512:cw]
                z = jnp.concatenate([z[:, 0:512], g * _sigmoid(g)], axis=-1)
            elif c >= MERGE_GATE_COL:
                z = _sigmoid(z)
            if c == 0:
                zb_ref[:, 0:cw] = jnp.concatenate([z[:, 0:512] * (ATTN_SCALE * LOG2_E), z[:, 512:cw]],
                                                  axis=-1).astype(BF16)
            else:
                zb_ref[:, c * cw:(c + 1) * cw] = z.astype(BF16)
            if c == 0:
                zq_ref[...] = z
                for hh in range(HEADS):
                    kn_ref[n_prev, hh] = z[:, 512 + hh * HD:512 + (hh + 1) * HD]
            if c == 1:
                for hh in range(HEADS):
                    vn_ref[n_prev, hh] = z[:, hh * HD:(hh + 1) * HD]
        if n_prev:
            kn_ref[0:n_prev] = kp_ref[...]
            vn_ref[0:n_prev] = vp_ref[...]

    project(normalize())


def _inproj(l, x, y, mod, norm1_w, w_in_bf, k_prev, v_prev):
    has_y = y is not None
    t = lambda i: (i + CTX_TILES) % N_TILES
    n_lat = N_TILES - CTX_TILES
    norm_tile = pl.BlockSpec((TM, D), lambda i: (t(i), 0))
    ctx_i = lambda i: jnp.maximum(i - n_lat, 0)
    kv_spec = lambda m: pl.BlockSpec((None, m, HEADS, T_CTX, HD), lambda i: (ctx_i(i), 0, 0, 0, 0))
    kv_shape = jax.ShapeDtypeStruct((N_CTX_SEQ, l + 1, HEADS, T_CTX, HD), F32)
    x_pair = isinstance(x, tuple)
    if x_pair:
        in_specs = [pl.BlockSpec((TM, D), lambda i: (jnp.maximum(i - n_lat, 0), 0)),
                    pl.BlockSpec((TM, D), lambda i: (jnp.minimum(i, n_lat - 1), 0))]
        args = list(x)
    else:
        in_specs = [norm_tile]
        args = [x]
    if has_y:
        in_specs += [pl.BlockSpec((TM * SLAB, 128), lambda i: (t(i), 0)),
                     pl.BlockSpec((1, 1, N_MOD * D), lambda i: ((l - 1) * 8 + _mod_row(t(i)), 0, 0))]
        args += [y, mod]
    in_specs += [
        pl.BlockSpec((1, 1, N_MOD * D), lambda i: (l * 8 + _mod_row(t(i)), 0, 0)),
        pl.BlockSpec((None, 1, D), lambda i: (l, 0, 0)),
        pl.BlockSpec((D, IN_COLS), lambda i: (0, 0), pipeline_mode=pl.Buffered(1)),
    ]
    args += [mod, norm1_w, w_in_bf]
    if l:
        in_specs += [kv_spec(l), kv_spec(l)]
        args += [k_prev, v_prev]
    out_specs = [pl.BlockSpec((TM, IN_COLS), lambda i: (t(i), 0)),
                 pl.BlockSpec((TM, 1024), lambda i: (t(i), 0)),
                 kv_spec(l + 1), kv_spec(l + 1)]
    out_shape = [jax.ShapeDtypeStruct((N_TOK, IN_COLS), BF16),
                 jax.ShapeDtypeStruct((N_TOK, 1024), F32),
                 kv_shape, kv_shape]
    if has_y:
        out_specs.append(norm_tile)
        out_shape.append(jax.ShapeDtypeStruct((N_TOK, D), F32))
    return pl.pallas_call(
        functools.partial(_inproj_kernel, has_y, x_pair, l),
        grid=(N_TILES,),
        in_specs=in_specs,
        out_specs=out_specs,
        out_shape=out_shape,
        compiler_params=_params(1),
        name="inproj",
    )(*args)


def _lambda(lam_ref, lam_init):
    lv = lam_ref[...]
    a = jnp.sum(lv[0:1] * lv[1:2], axis=-1, keepdims=True)
    b = jnp.sum(lv[2:3] * lv[3:4], axis=-1, keepdims=True)
    return jnp.exp(a) - jnp.exp(b) + lam_init


def _interleave(*stage_generators):
    live = list(stage_generators)
    while live:
        for g in list(live):
            try:
                next(g)
            except StopIteration:
                live.remove(g)


def _diff_attention(lam_init, qs, keys, vs, lam, sw, keys_transposed, o_ref, sls):
    lane = lax.broadcasted_iota(jnp.int32, qs[0].shape, 1)
    score = _dot if keys_transposed else _dot_nt
    maps = [(h, jnp.where(keep, q, jnp.zeros_like(q)))
            for h, q in enumerate(qs) for keep in (lane < 64, lane >= 64)]
    s = [score(qm, keys[h]) for h, qm in maps]
    yield
    e = [jnp.exp2(x - jnp.max(x, axis=-1, keepdims=True)) for x in s]
    yield
    r = [1.0 / jnp.sum(x, axis=-1, keepdims=True) for x in e]
    a = [(e[2 * h] * r[2 * h] - e[2 * h + 1] * (lam * r[2 * h + 1])).astype(BF16) for h in range(len(qs))]
    yield
    o = [_dot(a[h], vs[h]) for h in range(len(qs))]
    yield
    for sl, x in zip(sls, o):
        o_ref[:, sl] = ((_rms(x) * sw) * (1.0 - lam_init)).astype(BF16)


def _scaled_q(q):
    return (q * (ATTN_SCALE * LOG2_E)).astype(BF16)


def _attn_ctx_stages(lam_init, q_ref, k_ref, v_ref, lam_ref, sw_ref, o_ref):
    lam = _lambda(lam_ref, lam_init)
    sls = [slice(h * HD, (h + 1) * HD) for h in range(HEADS)]
    yield from _diff_attention(lam_init, [q_ref[:, sl] for sl in sls], [k_ref[:, sl] for sl in sls],
                               [v_ref[:, sl] for sl in sls], lam, sw_ref[...], False, o_ref, sls)


def _rope(x, cos, sin_signed):
    lane = lax.broadcasted_iota(jnp.int32, x.shape, 1)
    partner = jnp.where(lane % 32 < 16, pltpu.roll(x, 112, 1), pltpu.roll(x, 16, 1))
    return x * cos + partner * sin_signed


def _attn_lat_stages(lam_init, q_ref, k_ref, v_ref, ck_ref, cv_ref, cos_ref, sin_ref,
                     cosq_ref, sinq_ref, lam_ref, sw_ref, o_ref, k_t, vall):
    @pl.when(pl.program_id(1) == 0)
    def _():
        for h in range(HEADS):
            sl = slice(h * HD, (h + 1) * HD)
            k_t[h, :, 0:PAST] = ck_ref[h].T.astype(BF16)
            k_t[h, :, PAST:] = _rope(k_ref[:, sl], cos_ref[...], sin_ref[...]).T.astype(BF16)
            vall[h, 0:PAST, :] = cv_ref[h].astype(BF16)
            vall[h, PAST:, :] = v_ref[:, sl]

    lam = _lambda(lam_ref, lam_init)
    sls = [slice(h * HD, (h + 1) * HD) for h in range(HEADS)]
    qs = [_scaled_q(_rope(q_ref[:, sl], cosq_ref[...], sinq_ref[...])) for sl in sls]
    yield from _diff_attention(lam_init, qs, [k_t[h] for h in range(HEADS)], [vall[h] for h in range(HEADS)],
                               lam, sw_ref[...], True, o_ref, sls)


def _mix_ctx(l, lam_init, zb, attn_lambda, subln_w, decay_f, decay_b, ret_norm_w, sf_prev, sb_prev):
    width = HEADS * HD
    col = lambda c: pl.BlockSpec((T_CTX, width), lambda b: (b, c // HEADS))
    dec = pl.BlockSpec((HEADS, 1, 1), lambda b: (l, 0, 0))
    st_spec = lambda n: pl.BlockSpec((None, n, HEADS, HD, HD), lambda b: (b, 0, 0, 0, 0))
    st_shape = jax.ShapeDtypeStruct((N_CTX_SEQ, l + 1, HEADS, HD, HD), F32)
    out_tile = pl.BlockSpec((T_CTX, width), lambda b: (b, 0))
    out_shape = jax.ShapeDtypeStruct((N_CTX, width), BF16)
    in_specs = [
        col(QB), col(KB), col(VB),
        pl.BlockSpec((None, 4, 64), lambda b: (l, 0, 0)),
        pl.BlockSpec((None, 1, HD), lambda b: (l, 0, 0)),
        col(RQB), col(RKB), col(RVB), col(RGB),
        dec, dec, pl.BlockSpec((HEADS, 1, HD), lambda b: (l, 0, 0)),
    ]
    args = [zb, zb, zb, attn_lambda, subln_w, zb, zb, zb, zb, decay_f, decay_b, ret_norm_w]
    if l:
        in_specs += [st_spec(l), st_spec(l)]
        args += [sf_prev, sb_prev]
    return pl.pallas_call(
        functools.partial(_mix_ctx_kernel, lam_init, l),
        grid=(N_CTX_SEQ,),
        in_specs=in_specs,
        out_specs=[out_tile, out_tile, st_spec(l + 1), st_spec(l + 1)],
        out_shape=[out_shape, out_shape, st_shape, st_shape],
        scratch_shapes=[pltpu.VMEM((HEADS, T_CTX, T_CTX), F32)],
        compiler_params=_params(1),
        name="mix_ctx",
    )(*args)


def _decay_matrix(lgf, lgb, row0, tq, tk):
    i = row0 + lax.broadcasted_iota(jnp.int32, (tq, tk), 0)
    j = lax.broadcasted_iota(jnp.int32, (tq, tk), 1)
    d = (i - j).astype(F32)
    if row0 - (tk - 1) > 0:
        return jnp.exp(lgf * d)
    if row0 + (tq - 1) < 0:
        return jnp.exp(lgb * (-d))
    fwd = jnp.where(d >= 0.0, jnp.exp(lgf * jnp.maximum(d, 0.0)), 0.0)
    bwd = jnp.where(d <= 0.0, jnp.exp(lgb * jnp.maximum(-d, 0.0)), 0.0)
    return fwd + bwd


def _ret_finish(o, g, nw):
    return (g.astype(F32) * (_rms(o) * nw)).astype(BF16)


def _ret_ctx_stages(n_prev, q_ref, k_ref, v_ref, g_ref, df_ref, db_ref, nw_ref, o_ref, sf_ref, sb_ref, dmat):
    j = lax.broadcasted_iota(jnp.int32, (T_CTX, 1), 0).astype(F32)
    heads = range(HEADS)
    sls = [slice(h * HD, (h + 1) * HD) for h in heads]
    q = [q_ref[:, sl] for sl in sls]
    k = [k_ref[:, sl] for sl in sls]
    v = [v_ref[:, sl] for sl in sls]
    s = [(_dot_nt(q[h], k[h]) * dmat[h]).astype(BF16) for h in heads]
    yield
    o = [_dot(s[h], v[h]) for h in heads]
    yield
    lgf = [_log_sigmoid(df_ref[h]) for h in heads]
    lgb = [_log_sigmoid(db_ref[h]) for h in heads]
    kf = [k[h].astype(F32) * RET_SCALE for h in heads]
    kfw = [(kf[h] * jnp.exp(lgf[h] * (T_CTX - 1.0 - j))).astype(BF16) for h in heads]
    kbw = [(kf[h] * jnp.exp(lgb[h] * j)).astype(BF16) for h in heads]
    yield
    for h in heads:
        sf_ref[n_prev, h] = _dot_tn(kfw[h], v[h])
        sb_ref[n_prev, h] = _dot_tn(kbw[h], v[h])
    yield
    for h in heads:
        o_ref[:, sls[h]] = _ret_finish(o[h], g_ref[:, sls[h]], nw_ref[h])


def _mix_ctx_kernel(lam_init, n_prev, *refs):
    refs = list(refs)
    aq_ref, ak_ref, av_ref, lam_ref, sw_ref = refs[0:5]
    rq_ref, rk_ref, rv_ref, rg_ref, df_ref, db_ref, nw_ref = refs[5:12]
    refs = refs[12:]
    if n_prev:
        sfp_ref, sbp_ref = refs.pop(0), refs.pop(0)
    ao_ref, ro_ref, sf_ref, sb_ref, dmat = refs

    @pl.when(pl.program_id(0) == 0)
    def _():
        for h in range(HEADS):
            dmat[h] = RET_SCALE * _decay_matrix(_log_sigmoid(df_ref[h]), _log_sigmoid(db_ref[h]), 0, T_CTX, T_CTX)

    if n_prev:
        sf_ref[0:n_prev] = sfp_ref[...]
        sb_ref[0:n_prev] = sbp_ref[...]
    _interleave(
        _attn_ctx_stages(lam_init, aq_ref, ak_ref, av_ref, lam_ref, sw_ref, ao_ref),
        _ret_ctx_stages(n_prev, rq_ref, rk_ref, rv_ref, rg_ref, df_ref, db_ref, nw_ref, ro_ref, sf_ref, sb_ref, dmat))


N_DCHUNK = 2 * LAT_TILES_PER_SEQ - 1


def _ret_lat_stages(q_ref, k_ref, v_ref, g_ref, s0f_ref, s0b_ref, df_ref, db_ref, nw_ref, o_ref, strip, k_t):
    b = pl.program_id(0)
    j = pl.program_id(1)

    @pl.when(jnp.logical_and(b == 0, j == 0))
    def _():
        for h in range(HEADS):
            lgf = _log_sigmoid(df_ref[h])
            lgb = _log_sigmoid(db_ref[h])
            for c in range(N_DCHUNK):
                strip[h, c] = RET_SCALE * _decay_matrix(lgf, lgb, T_LAT - TM - c * TM, TM, TM)

    @pl.when(j == 0)
    def _():
        for h in range(HEADS):
            k_t[h] = k_ref[:, h * HD:(h + 1) * HD].astype(F32).T.astype(BF16)

    i = (j * TM + lax.broadcasted_iota(jnp.int32, (TM, 1), 0)).astype(F32)
    c0 = LAT_TILES_PER_SEQ - 1 - j
    heads = range(HEADS)
    sls = [slice(h * HD, (h + 1) * HD) for h in heads]
    q = [q_ref[:, sl] for sl in sls]
    dmat = [jnp.concatenate([strip[h, c0 + c] for c in range(LAT_TILES_PER_SEQ)], axis=-1) for h in heads]
    s = [(_dot(q[h], k_t[h]) * dmat[h]).astype(BF16) for h in heads]
    yield
    o = [_dot(s[h], v_ref[:, sls[h]]) for h in heads]
    yield
    of = [_dot(q[h], s0f_ref[h].astype(BF16)) for h in heads]
    ob = [_dot(q[h], s0b_ref[h].astype(BF16)) for h in heads]
    yield
    for h in heads:
        tot = o[h] + jnp.exp(_log_sigmoid(df_ref[h]) * (i + 1.0)) * of[h]
        tot = tot + jnp.exp(_log_sigmoid(db_ref[h]) * (T_LAT - i)) * ob[h]
        o_ref[:, sls[h]] = _ret_finish(tot, g_ref[:, sls[h]], nw_ref[h])


def _mix_lat_kernel(lam_init, *refs):
    attn_in, ret_in = refs[0:11], refs[11:20]
    ao_ref, ro_ref, ak_t, vall, strip, rk_t = refs[20:]
    _interleave(_attn_lat_stages(lam_init, *attn_in, ao_ref, ak_t, vall))
    _interleave(_ret_lat_stages(*ret_in, ro_ref, strip, rk_t))


def _mix_lat(l, lam_init, zb, zq, cache_k, cache_v, cos, sin, attn_lambda, subln_w,
             state_f, state_b, decay_f, decay_b, ret_norm_w):
    width = HEADS * HD
    lat_row = lambda b, j: CTX_TILES + LAT_TILES_PER_SEQ * b + j
    seq_row = lambda b: N_CTX // T_LAT + b
    q_tile = lambda c: pl.BlockSpec((TM, width), lambda b, j: (lat_row(b, j), c // HEADS))
    seq = lambda c: pl.BlockSpec((T_LAT, width), lambda b, j: (seq_row(b), c // HEADS))
    cache = pl.BlockSpec((None, None, HEADS, PAST, HD), lambda b, j: (b, l, 0, 0, 0))
    state = pl.BlockSpec((None, None, HEADS, HD, HD), lambda b, j: (b, l, 0, 0, 0))
    dec = pl.BlockSpec((HEADS, 1, 1), lambda b, j: (l, 0, 0))
    out_tile = pl.BlockSpec((TM, width), lambda b, j: (LAT_TILES_PER_SEQ * b + j, 0))
    out_shape = jax.ShapeDtypeStruct((N_LAT, width), BF16)
    return pl.pallas_call(
        functools.partial(_mix_lat_kernel, lam_init),
        grid=(N_LAT_SEQ, LAT_TILES_PER_SEQ),
        in_specs=[
            q_tile(QB), seq(KB), seq(VB), cache, cache,
            pl.BlockSpec((T_LAT, HD), lambda b, j: (0, 0)),
            pl.BlockSpec((T_LAT, HD), lambda b, j: (0, 0)),
            pl.BlockSpec((TM, HD), lambda b, j: (j, 0)),
            pl.BlockSpec((TM, HD), lambda b, j: (j, 0)),
            pl.BlockSpec((None, 4, 64), lambda b, j: (l, 0, 0)),
            pl.BlockSpec((None, 1, HD), lambda b, j: (l, 0, 0)),
            q_tile(RQB), seq(RKB), seq(RVB), q_tile(RGB), state, state, dec, dec,
            pl.BlockSpec((HEADS, 1, HD), lambda b, j: (l, 0, 0)),
        ],
        out_specs=[out_tile, out_tile],
        out_shape=[out_shape, out_shape],
        scratch_shapes=[pltpu.VMEM((HEADS, HD, PAST + T_LAT), BF16),
                        pltpu.VMEM((HEADS, PAST + T_LAT, HD), BF16),
                        pltpu.VMEM((HEADS, N_DCHUNK, TM, TM), F32),
                        pltpu.VMEM((HEADS, HD, T_LAT), BF16)],
        compiler_params=_params(2),
        name="mix_lat",
    )(zq, zq, zb, cache_k, cache_v, cos, sin, cos, sin, attn_lambda, subln_w,
      zb, zb, zb, zb, state_f, state_b, decay_f, decay_b, ret_norm_w)


def _merge_kernel(x_pair, *refs):
    refs = list(refs)
    x_ref = refs.pop(0)
    xs_ref = refs.pop(0) if x_pair else None
    (aoc_ref, aol_ref, roc_ref, rol_ref, cb_ref, u_ref, up_ref, un_ref,
     mg0_ref, mg1_ref, mg2_ref, mod_ref, cw_ref, wa_ref, wc_ref, wr_ref, wo_ref, n2_ref, wrt_ref,
     x1_ref, h2_ref, aff_ref) = refs
    i = pl.program_id(0)
    is_ctx = i < CTX_TILES // MERGE_SUB
    m = mod_ref[0]
    gate1 = m[:, 2 * D:3 * D]
    shift2 = m[:, 3 * D:4 * D]
    scale2 = m[:, 4 * D:5 * D]
    cw = cw_ref[...]
    u_all = u_ref[...].astype(F32)
    r = lax.broadcasted_iota(jnp.int32, (TM, 512), 0)
    lane = lax.broadcasted_iota(jnp.int32, (TM, 128), 1)

    subs = range(MERGE_SUB)
    rows = [slice(sub * TM, (sub + 1) * TM) for sub in subs]

    def conv_out(sub):
        j = (i * MERGE_SUB + sub - CTX_TILES) % LAT_TILES_PER_SEQ
        seq_first = jnp.logical_or(is_ctx, j == 0)
        seq_last = jnp.logical_or(is_ctx, j == LAT_TILES_PER_SEQ - 1)
        u = u_all[rows[sub], :]
        up = up_ref[...].astype(F32)[15:16, :] if sub == 0 else u_all[sub * TM - 1:sub * TM, :]
        dn = un_ref[...].astype(F32)[0:1, :] if sub == MERGE_SUB - 1 else u_all[(sub + 1) * TM:(sub + 1) * TM + 1, :]
        up = up * jnp.where(seq_first, 0.0, 1.0)
        dn = dn * jnp.where(seq_last, 0.0, 1.0)
        u_prev = jnp.where(r == 0, up, pltpu.roll(u, 1, 0))
        u_next = jnp.where(r == TM - 1, dn, pltpu.roll(u, TM - 1, 0))
        conv = u_prev * cw[0:1, :] + u * cw[1:2, :] + u_next * cw[2:3, :]
        return (cb_ref[rows[sub], :].astype(F32) * conv).astype(BF16)

    ao = [jnp.where(is_ctx, aoc_ref[rw, :], aol_ref[rw, :]) for rw in rows]
    ro = [jnp.where(is_ctx, roc_ref[rw, :], rol_ref[rw, :]) for rw in rows]
    conv_o = [conv_out(sub) for sub in subs]
    b_attn = [_dot(ao[s], wa_ref[...]) for s in subs]
    b_conv = [_dot(conv_o[s], wc_ref[...]) for s in subs]
    b_ret = [_dot(ro[s], wr_ref[...]) for s in subs]
    merged = [(mg0_ref[rows[s], :].astype(F32) * b_attn[s] + mg1_ref[rows[s], :].astype(F32) * b_conv[s]
               + mg2_ref[rows[s], :].astype(F32) * b_ret[s]).astype(BF16) for s in subs]
    proj = [_dot(merged[s], wo_ref[...]) for s in subs]
    x1 = []
    for s in subs:
        x_in = x_ref[rows[s], :]
        if x_pair:
            x_in = jnp.where(is_ctx, x_in, xs_ref[rows[s], :])
        x1.append(x_in + gate1 * proj[s])
        x1_ref[rows[s], :] = x1[s]
    h2 = [(_rms(x1[s]) * n2_ref[...]) * (1.0 + scale2) + shift2 for s in subs]
    logits = [_dot(h2[s].astype(BF16), wrt_ref[...]) for s in subs]
    valid = lane < N_EXPERTS
    for s in subs:
        _to_slabs(h2_ref.at[pl.ds(s * TM * SLAB, TM * SLAB), :], h2[s])
        lmax = jnp.max(jnp.where(valid, logits[s], -jnp.inf), axis=-1, keepdims=True)
        e = jnp.where(valid, jnp.exp(logits[s] - lmax), 0.0)
        aff_ref[rows[s], :] = e * (1.0 / jnp.sum(e, axis=-1, keepdims=True))


def _merge(l, x, ao, ao_lat, ro, ro_lat, zb, mod, conv_w, w_br_attn, w_br_conv, w_br_ret, w_out, norm2_w,
           w_router_pad):
    mt = MERGE_SUB * TM
    n_steps = N_TOK // mt
    ctx_steps = N_CTX // mt
    n16 = N_TOK // 16
    ctx_br = pl.BlockSpec((mt, 512), lambda i: (jnp.minimum(i, ctx_steps - 1), 0))
    lat_br = pl.BlockSpec((mt, 512), lambda i: (jnp.maximum(i - ctx_steps, 0), 0))
    col = lambda c: pl.BlockSpec((mt, 512), lambda i: (i, c))
    halo_p = lambda c: pl.BlockSpec((16, 512), lambda i: (jnp.maximum(i * (mt // 16) - 1, 0), c))
    halo_n = lambda c: pl.BlockSpec((16, 512), lambda i: (jnp.minimum((i + 1) * (mt // 16), n16 - 1), c))
    mgs = lambda c: pl.BlockSpec((mt, D), lambda i: (i, c))
    wbr = pl.BlockSpec((None, 512, D), lambda i: (l, 0, 0))
    x_pair = isinstance(x, tuple)
    if x_pair:
        x_specs = [pl.BlockSpec((mt, D), lambda i: (jnp.minimum(i, ctx_steps - 1), 0)),
                   pl.BlockSpec((mt, D), lambda i: (jnp.maximum(i - ctx_steps, 0), 0))]
        x_args = list(x)
    else:
        x_specs = [pl.BlockSpec((mt, D), lambda i: (i, 0))]
        x_args = [x]
    return pl.pallas_call(
        functools.partial(_merge_kernel, x_pair),
        grid=(n_steps,),
        in_specs=x_specs + [
            ctx_br, lat_br, ctx_br, lat_br,
            col(CONV_B_COL), col(CONV_U_COL), halo_p(CONV_U_COL), halo_n(CONV_U_COL),
            mgs(MERGE_GATE_COL), mgs(MERGE_GATE_COL + 1), mgs(MERGE_GATE_COL + 2),
            pl.BlockSpec((1, 1, N_MOD * D), lambda i: (l * 8 + _mod_row(i * MERGE_SUB), 0, 0)),
            pl.BlockSpec((None, 3, 512), lambda i: (l, 0, 0)),
            wbr, wbr, wbr,
            pl.BlockSpec((None, D, D), lambda i: (l, 0, 0)),
            pl.BlockSpec((None, 1, D), lambda i: (l, 0, 0)),
            pl.BlockSpec((None, D, 128), lambda i: (l, 0, 0)),
        ],
        out_specs=[
            pl.BlockSpec((mt, D), lambda i: (i, 0)),
            pl.BlockSpec((mt * SLAB, 128), lambda i: (i, 0)),
            pl.BlockSpec((mt, 128), lambda i: (i, 0)),
        ],
        out_shape=[
            jax.ShapeDtypeStruct((N_TOK, D), F32),
            jax.ShapeDtypeStruct((N_TOK * SLAB, 128), F32),
            jax.ShapeDtypeStruct((N_TOK, 128), F32),
        ],
        compiler_params=_params(1),
        name="merge",
    )(*x_args, ao, ao_lat, ro, ro_lat, zb, zb, zb, zb, zb, zb, zb, mod, conv_w,
      w_br_attn, w_br_conv, w_br_ret, w_out, norm2_w, w_router_pad)


def _cumsum_lanes(x, tri):
    run = jnp.zeros((x.shape[0], 1), F32)
    outs = []
    for b in range(x.shape[1] // 128):
        cs = _dot(x[:, b * 128:(b + 1) * 128].astype(BF16), tri) + run
        run = cs[:, 127:128]
        outs.append(cs)
    return jnp.concatenate(outs, axis=-1)


GATE_LANE = (0, 16, 32)
IDX_LANE = 48
BISECT_GROUP = 4


def _topk_kernel(an_ref, tmat_ref, idx_ref, gate_ref):
    an = an_ref[...]
    a = an.T[0:N_EXPERTS, :]
    n = a.shape[1]
    kf = float(CAP)

    hi = an.astype(BF16).astype(F32)
    mid = (an - hi).astype(BF16).astype(F32)
    lo = ((an - hi) - mid).astype(BF16).astype(F32)
    table = (tmat_ref[...].astype(F32) + hi + pltpu.roll(mid, GATE_LANE[1], 1)
             + pltpu.roll(lo, GATE_LANE[2], 1)).astype(BF16)

    def count_gt(thr):
        return jnp.sum(jnp.where(a > thr, 1.0, 0.0), axis=-1, keepdims=True)

    def span(lo, hi):
        inside = jnp.logical_and(a > lo, a <= hi)
        cmax = jnp.max(jnp.where(inside, a, -jnp.inf), axis=-1, keepdims=True)
        cmin = jnp.min(jnp.where(inside, a, jnp.inf), axis=-1, keepdims=True)
        return cmax, cmin

    def cond(c):
        return jnp.logical_and(c[2] > 0, c[3] < 400)

    def body(c):
        lo, hi, _, it = c
        for _ in range(BISECT_GROUP):
            mid = 0.5 * (lo + hi)
            ge = count_gt(mid) >= kf
            lo = jnp.where(ge, mid, lo)
            hi = jnp.where(ge, hi, mid)
        cmax, cmin = span(lo, hi)
        open_rows = jnp.max(jnp.where(cmax != cmin, 1, 0))
        return lo, hi, open_rows, it + 1

    lo0 = jnp.full((N_EXPERTS, 1), -1.0, F32)
    hi0 = jnp.max(a, axis=-1, keepdims=True)
    cmax0, cmin0 = span(lo0, hi0)
    lo, hi, _, _ = lax.while_loop(
        cond, body, (lo0, hi0, jnp.max(jnp.where(cmax0 != cmin0, 1, 0)), jnp.int32(0)))
    thr, _ = span(lo, hi)

    r = lax.broadcasted_iota(jnp.int32, (128, 128), 0)
    c = lax.broadcasted_iota(jnp.int32, (128, 128), 1)
    tri = jnp.where(r <= c, 1.0, 0.0).astype(BF16)
    gt = a > thr
    eq = jnp.where(a == thr, 1.0, 0.0)
    need = kf - count_gt(thr)
    eq_before = _cumsum_lanes(eq, tri) - eq
    sel = jnp.where(jnp.logical_or(gt, jnp.logical_and(eq > 0.0, eq_before < need)), 1.0, 0.0)
    pos = _cumsum_lanes(sel, tri) - 1.0
    slot = jnp.where(sel > 0.0, pos, -1.0).astype(jnp.int32)

    p_iota = lax.broadcasted_iota(jnp.int32, (CAP, 1024), 0)
    lane = lax.broadcasted_iota(jnp.int32, (CAP, 128), 1)
    for e in range(N_EXPERTS):
        acc = jnp.zeros((CAP, 128), F32)
        for cb in range(n // 1024):
            onehot = jnp.where(p_iota == slot[e:e + 1, cb * 1024:(cb + 1) * 1024], 1.0, 0.0).astype(BF16)
            acc = acc + _dot(onehot, table[cb * 1024:(cb + 1) * 1024, :])
        acc_t = acc.T
        idx_ref[0, e] = (acc_t[IDX_LANE:IDX_LANE + 1, :] * 64.0 + acc_t[IDX_LANE + 1:IDX_LANE + 2, :]).astype(jnp.int32)
        g = jnp.zeros((CAP, 1), F32)
        for off in GATE_LANE:
            g = g + jnp.sum(jnp.where(lane == off + e, acc, 0.0), axis=-1, keepdims=True)
        gate_ref[0, e] = g


def _topk(aff_n, tmat):
    out_spec = pl.BlockSpec((1, N_EXPERTS, CAP, 1), lambda s: (s, 0, 0, 0))
    return pl.pallas_call(
        _topk_kernel,
        grid=(2,),
        in_specs=[
            pl.BlockSpec((N_CTX, 128), lambda s: (s, 0)),
            pl.BlockSpec((N_CTX, 128), lambda s: (0, 0)),
        ],
        out_specs=[pl.BlockSpec((1, N_EXPERTS, 1, CAP), lambda s: (s, 0, 0, 0)), out_spec],
        out_shape=[jax.ShapeDtypeStruct((2, N_EXPERTS, 1, CAP), jnp.int32),
                   jax.ShapeDtypeStruct((2, N_EXPERTS, CAP, 1), F32)],
        compiler_params=_params(1),
        name="topk",
    )(aff_n, tmat)


N_FT = FF // TF
GATHER_ROWS = 2 * CAP
ROWS_PER_STEP = GATHER_ROWS // N_FT


def _ffn_kernel(cast_next, idx_ref, h2_hbm, g_ref, wg_ref, wu_ref, wd_ref, *refs):
    if cast_next:
        win_ref, ye_ref, wbf_ref, xg, xb, acc, sems = refs
        wbf_ref[...] = win_ref[...].astype(BF16)
    else:
        ye_ref, xg, xb, acc, sems = refs
    e = pl.program_id(0)
    f = pl.program_id(1)
    slot = e % 2

    def row_copy(expert, s, p, dst_slot):
        row = idx_ref[(s * N_EXPERTS + expert) * CAP + p] + s * N_CTX
        src = h2_hbm.at[pl.ds(pl.multiple_of(row * SLAB, SLAB), SLAB), :]
        dst = xg.at[dst_slot, pl.ds(pl.multiple_of((s * CAP + p) * SLAB, SLAB), SLAB), :]
        return pltpu.make_async_copy(src, dst, sems.at[dst_slot])

    def slot_wait(dst_slot):
        pltpu.make_async_copy(h2_hbm.at[pl.ds(0, GATHER_ROWS * SLAB), :], xg.at[dst_slot],
                              sems.at[dst_slot]).wait()

    @pl.when(jnp.logical_and(e == 0, f == 0))
    def _():
        for r in range(GATHER_ROWS):
            row_copy(0, r // CAP, r % CAP, 0).start()
        acc[...] = jnp.zeros_like(acc)

    @pl.when(f == 0)
    def _():
        slot_wait(slot)
        xb[...] = _from_slabs(xg.at[slot]).astype(BF16)

    nxt = jnp.minimum(e + 1, N_EXPERTS - 1)
    s_nxt = f // (N_FT // 2)
    p0 = (f % (N_FT // 2)) * ROWS_PER_STEP
    for u in range(ROWS_PER_STEP):
        row_copy(nxt, s_nxt, p0 + u, 1 - slot).start()

    x = xb[...]
    hg = _dot(x, wg_ref[...].astype(BF16))
    hu = _dot(x, wu_ref[...].astype(BF16))
    hdn = ((hg * _sigmoid(hg)) * hu).astype(BF16)
    acc[...] = jnp.where(f == 0, 0.0, acc[...]) + _dot(hdn, wd_ref[...].astype(BF16))

    @pl.when(f == N_FT - 1)
    def _():
        for s in range(2):
            _to_slabs(ye_ref.at[s], acc[s * CAP:(s + 1) * CAP, :] * g_ref[s])

    @pl.when(jnp.logical_and(e == N_EXPERTS - 1, f == N_FT - 1))
    def _():
        slot_wait(1 - slot)


def _expert_ffn(l, idx_flat, h2s, gates, w_gate, w_up, w_down, w_in):
    cast_next = l + 1 < DEPTH
    n_steps = N_EXPERTS * N_FT
    in_specs = [
        pl.BlockSpec(memory_space=pl.ANY),
        pl.BlockSpec((2, None, CAP, 1), lambda e, f, idx: (0, e, 0, 0)),
        pl.BlockSpec((None, None, D, TF), lambda e, f, idx: (l, e, 0, f)),
        pl.BlockSpec((None, None, D, TF), lambda e, f, idx: (l, e, 0, f)),
        pl.BlockSpec((None, None, TF, D), lambda e, f, idx: (l, e, f, 0)),
    ]
    args = [idx_flat, h2s, gates, w_gate, w_up, w_down]
    out_specs = [pl.BlockSpec((2, None, CAP * SLAB, 128), lambda e, f, idx: (0, e, 0, 0))]
    out_shape = [jax.ShapeDtypeStruct((2, N_EXPERTS, CAP * SLAB, 128), F32)]
    if cast_next:
        cols = IN_COLS // n_steps
        in_specs.append(pl.BlockSpec((None, D, cols), lambda e, f, idx: (l + 1, 0, e * N_FT + f)))
        args.append(w_in)
        out_specs.append(pl.BlockSpec((D, cols), lambda e, f, idx: (0, e * N_FT + f)))
        out_shape.append(jax.ShapeDtypeStruct((D, IN_COLS), BF16))
    grid_spec = pltpu.PrefetchScalarGridSpec(
        num_scalar_prefetch=1,
        grid=(N_EXPERTS, N_FT),
        in_specs=in_specs,
        out_specs=out_specs,
        scratch_shapes=[
            pltpu.VMEM((2, GATHER_ROWS * SLAB, 128), F32),
            pltpu.VMEM((GATHER_ROWS, D), BF16),
            pltpu.VMEM((GATHER_ROWS, D), F32),
            pltpu.SemaphoreType.DMA((2,)),
        ],
    )
    outs = pl.pallas_call(
        functools.partial(_ffn_kernel, cast_next),
        grid_spec=grid_spec,
        out_shape=out_shape,
        compiler_params=_params(2),
        name="expert_ffn",
    )(*args)
    return (outs[0], outs[1]) if cast_next else (outs[0], None)


SCATTER_UNROLL = 8
SCATTER_EXPERTS = 4


def _combine_kernel(idx_ref, ye_ref, y_ref):
    s = pl.program_id(0)
    eg = pl.program_id(1)

    @pl.when(eg == 0)
    def _():
        y_ref[...] = jnp.zeros_like(y_ref)

    for k in range(SCATTER_EXPERTS):
        base = (s * N_EXPERTS + eg * SCATTER_EXPERTS + k) * CAP
        for p0 in range(0, CAP, SCATTER_UNROLL):
            rows = [idx_ref[base + p0 + u] for u in range(SCATTER_UNROLL)]
            tiles = [pl.ds(pl.multiple_of(r * SLAB, SLAB), SLAB) for r in rows]
            vals = [y_ref[tiles[u], :] + ye_ref[k, pl.ds((p0 + u) * SLAB, SLAB), :] for u in range(SCATTER_UNROLL)]
            for u in range(SCATTER_UNROLL):
                y_ref[tiles[u], :] = vals[u]


def _combine(idx_flat, ye):
    grid_spec = pltpu.PrefetchScalarGridSpec(
        num_scalar_prefetch=1,
        grid=(2, N_EXPERTS // SCATTER_EXPERTS),
        in_specs=[pl.BlockSpec((None, SCATTER_EXPERTS, CAP * SLAB, 128), lambda s, eg, idx: (s, eg, 0, 0))],
        out_specs=pl.BlockSpec((N_CTX * SLAB, 128), lambda s, eg, idx: (s, 0)),
    )
    return pl.pallas_call(
        _combine_kernel,
        grid_spec=grid_spec,
        out_shape=jax.ShapeDtypeStruct((N_TOK * SLAB, 128), F32),
        compiler_params=_params(2),
        name="combine",
    )(idx_flat, ye)


def _final_kernel(x_ref, y_ref, mod_ref, w_ref, op_ref, os_ref):
    x = x_ref[...] + mod_ref[0][:, 5 * D:6 * D] * _from_slabs(y_ref)
    out = _rms(x) * w_ref[...]
    is_ctx = pl.program_id(0) < N_CTX // FINAL_ROWS

    @pl.when(is_ctx)
    def _():
        op_ref[...] = out

    @pl.when(jnp.logical_not(is_ctx))
    def _():
        os_ref[...] = out


FINAL_ROWS = 2 * TM


def _final(x1, y, mod, final_norm_w):
    ft = FINAL_ROWS
    ctx_steps = N_CTX // ft
    tile = pl.BlockSpec((ft, D), lambda i: (i, 0))
    return pl.pallas_call(
        _final_kernel,
        grid=(N_TOK // ft,),
        in_specs=[tile, pl.BlockSpec((ft * SLAB, 128), lambda i: (i, 0)),
                  pl.BlockSpec((1, 1, N_MOD * D), lambda i: ((DEPTH - 1) * 8 + _mod_row(i * (ft // TM)), 0, 0)),
                  pl.BlockSpec((1, D), lambda i: (0, 0))],
        out_specs=[pl.BlockSpec((ft, D), lambda i: (jnp.minimum(i, ctx_steps - 1), 0)),
                   pl.BlockSpec((ft, D), lambda i: (jnp.maximum(i - ctx_steps, 0), 0))],
        out_shape=[jax.ShapeDtypeStruct((N_CTX, D), F32), jax.ShapeDtypeStruct((N_LAT, D), F32)],
        compiler_params=_params(1),
        name="final_norm",
    )(x1, y, mod, final_norm_w.reshape(1, D))


def _rope_tables():
    t = np.arange(T_LAT)
    row = (t // GRID_W).astype(np.float32)
    col = (t % GRID_W).astype(np.float32)
    inv = jnp.asarray(ROPE_BASE, F32) ** (-jnp.arange(N_ROPE_FREQ, dtype=F32) / N_ROPE_FREQ)
    ang_r = jnp.asarray(row)[:, None] * inv
    ang_c = jnp.asarray(col)[:, None] * inv
    def group(ang):
        return jnp.concatenate([ang, ang], axis=-1)
    ang = jnp.concatenate([group(ang_r), group(ang_c), group(ang_r), group(ang_c)], axis=-1)
    sign = np.where(np.arange(HD) % 32 < 16, -1.0, 1.0).astype(np.float32)
    return jnp.cos(ang), jnp.sin(ang) * sign


def _index_table():
    t = np.arange(N_CTX)
    tm = np.zeros((N_CTX, 128), np.float32)
    tm[:, IDX_LANE] = t // 64
    tm[:, IDX_LANE + 1] = t % 64
    return jnp.asarray(tm, BF16)


def kernel(x_prompt, x_sample, c, cache_attn_k, cache_attn_v, state_ret_fwd, state_ret_bwd, c_ctx, w_ada, b_ada, norm1_w, norm2_w, w_in, attn_lambda, attn_subln_w, conv_w, ret_decay_fwd, ret_decay_bwd, ret_norm_w, w_br_attn, w_br_conv, w_br_ret, w_out, w_router, w_exp_gate, w_exp_up, w_exp_down, final_norm_w):
    x = (x_prompt.reshape(N_CTX, D), x_sample.reshape(N_LAT, D))
    cvec = jnp.concatenate([c_ctx[None, :], c, jnp.zeros((3, D), F32)], axis=0)
    mod = _modulation(cvec, w_ada, b_ada).reshape(DEPTH * 8, 1, N_MOD * D)

    w_in_bf = w_in[0].astype(BF16)
    w_br_attn_bf = w_br_attn.astype(BF16)
    w_br_conv_bf = w_br_conv.astype(BF16)
    w_br_ret_bf = w_br_ret.astype(BF16)
    w_out_bf = w_out.astype(BF16)
    w_router_pad = jnp.pad(w_router, ((0, 0), (0, 0), (0, 128 - N_EXPERTS))).astype(BF16)
    norm1 = norm1_w.reshape(DEPTH, 1, D)
    norm2 = norm2_w.reshape(DEPTH, 1, D)
    subln = attn_subln_w.reshape(DEPTH, 1, HD)
    decay_f = ret_decay_fwd.reshape(DEPTH * HEADS, 1, 1)
    decay_b = ret_decay_bwd.reshape(DEPTH * HEADS, 1, 1)
    ret_nw = ret_norm_w.reshape(DEPTH * HEADS, 1, HD)
    cos, sin = _rope_tables()
    tmat = _index_table()

    y = None
    new_k = new_v = new_sf = new_sb = None
    for l in range(DEPTH):
        lam_init = 0.8 - 0.6 * math.exp(-0.3 * l)
        if y is None:
            zb, zq, new_k, new_v = _inproj(l, x, None, mod, norm1, w_in_bf, new_k, new_v)
        else:
            zb, zq, new_k, new_v, x = _inproj(l, x, y, mod, norm1, w_in_bf, new_k, new_v)
        ao, ro, new_sf, new_sb = _mix_ctx(l, lam_init, zb, attn_lambda, subln, decay_f, decay_b, ret_nw,
                                          new_sf, new_sb)
        ao_lat, ro_lat = _mix_lat(l, lam_init, zb, zq, cache_attn_k, cache_attn_v, cos, sin, attn_lambda, subln,
                                  state_ret_fwd, state_ret_bwd, decay_f, decay_b, ret_nw)
        x, h2s, aff_n = _merge(l, x, ao, ao_lat, ro, ro_lat, zb, mod, conv_w, w_br_attn_bf, w_br_conv_bf,
                               w_br_ret_bf, w_out_bf, norm2, w_router_pad)
        idx, gates = _topk(aff_n, tmat)
        idx_flat = idx.reshape(2 * N_EXPERTS * CAP)
        ye, w_in_bf = _expert_ffn(l, idx_flat, h2s, gates, w_exp_gate, w_exp_up, w_exp_down, w_in)
        y = _combine(idx_flat, ye)

    y_prompt, y_sample = _final(x, y, mod, final_norm_w)
    return (y_prompt.reshape(N_CTX_SEQ, T_CTX, D), y_sample.reshape(N_LAT_SEQ, T_LAT, D), new_k, new_v,
            new_sf, new_sb)
```

```python
import functools
import math

import jax
import jax.numpy as jnp
import numpy as np
from jax import lax
from jax.experimental import pallas as pl
from jax.experimental.pallas import tpu as pltpu

F32 = jnp.float32
BF16 = jnp.bfloat16

D = 1024
DEPTH = 2
N_CTX_SEQ = 16
T_CTX = 256
N_LAT_SEQ = 4
T_LAT = 1024
PAST = 256
N_CTX = N_CTX_SEQ * T_CTX
N_LAT = N_LAT_SEQ * T_LAT
N_TOK = N_CTX + N_LAT
TM = 256
N_TILES = N_TOK // TM
CTX_TILES = N_CTX // TM
LAT_TILES_PER_SEQ = T_LAT // TM
HEADS = 4
HD = 128
GRID_W = 64
N_ROPE_FREQ = 16
ROPE_BASE = 10000.0
IN_COLS = 8192
N_MOD = 6
N_EXPERTS = 16
CAP = 512
FF = 2048
TF = 512
EPS = 1e-6
RET_SCALE = HD ** -0.5
ATTN_SCALE = 64 ** -0.5
LOG2_E = math.log2(math.e)
SLAB = D // 128
VMEM_LIMIT = 56 * 1024 * 1024

QB, KB, VB = 0, 4, 8
RQB, RKB, RVB, RGB = 24, 28, 32, 36
CONV_B_COL, CONV_U_COL = 3, 4
MERGE_GATE_COL = 5
MERGE_SUB = 2


def _sigmoid(x):
    return 0.5 * jnp.tanh(0.5 * x) + 0.5


def _log_sigmoid(x):
    return jnp.minimum(x, 0.0) - jnp.log(1.0 + jnp.exp(-jnp.abs(x)))


def _rms(x):
    return x * lax.rsqrt(jnp.mean(x * x, axis=-1, keepdims=True) + EPS)


def _dot(a, b):
    return jnp.dot(a, b, preferred_element_type=F32)


def _dot_nt(a, b):
    return lax.dot_general(a, b, (((1,), (1,)), ((), ())), preferred_element_type=F32)


def _dot_tn(a, b):
    return lax.dot_general(a, b, (((0,), (0,)), ((), ())), preferred_element_type=F32)


def _to_slabs(ref, x):
    n = x.shape[0]
    for s in range(SLAB):
        ref[pl.ds(s, n, stride=SLAB), :] = x[:, s * 128:(s + 1) * 128]


def _from_slabs(ref):
    n = ref.shape[0] // SLAB
    return jnp.concatenate([ref[pl.ds(s, n, stride=SLAB), :] for s in range(SLAB)], axis=-1)


def _mod_row(i):
    return jnp.where(i < CTX_TILES, 0, 1 + (i - CTX_TILES) // LAT_TILES_PER_SEQ)


def _params(n_axes):
    return pltpu.CompilerParams(
        dimension_semantics=("arbitrary",) * n_axes, vmem_limit_bytes=VMEM_LIMIT)


def _mod_kernel(c_ref, w_ref, b_ref, o_ref):
    c = c_ref[...]
    s = (c * _sigmoid(c)).astype(BF16)
    o_ref[...] = _dot(s, w_ref[...].astype(BF16)) + b_ref[...]


def _modulation(cvec, w_ada, b_ada):
    tn = 1024
    return pl.pallas_call(
        _mod_kernel,
        grid=(DEPTH, N_MOD * D // tn),
        in_specs=[
            pl.BlockSpec((8, D), lambda l, n: (0, 0)),
            pl.BlockSpec((None, D, tn), lambda l, n: (l, 0, n)),
            pl.BlockSpec((None, 1, tn), lambda l, n: (l, 0, n)),
        ],
        out_specs=pl.BlockSpec((None, 8, tn), lambda l, n: (l, 0, n)),
        out_shape=jax.ShapeDtypeStruct((DEPTH, 8, N_MOD * D), F32),
        compiler_params=_params(2),
        name="modulation",
    )(cvec, w_ada, b_ada.reshape(DEPTH, 1, N_MOD * D))


def _inproj_kernel(x_pair, n_prev, *refs):
    refs = list(refs)
    x_ref = refs.pop(0)
    if x_pair:
        xs_ref = refs.pop(0)
    mod_ref, n1_ref, w_ref = refs.pop(0), refs.pop(0), refs.pop(0)
    if n_prev:
        kp_ref, vp_ref = refs.pop(0), refs.pop(0)
    zb_ref, zq_ref, kn_ref, vn_ref = refs
    i = pl.program_id(0)

    def normalize():
        x = x_ref[...]
        if x_pair:
            x = jnp.where(i < N_TILES - CTX_TILES, xs_ref[...], x)
        m = mod_ref[0]
        shift1 = m[:, 0:D]
        scale1 = m[:, D:2 * D]
        return ((_rms(x) * n1_ref[...]) * (1.0 + scale1) + shift1).astype(BF16)

    def project(h):
        cw = 1024
        for c in range(IN_COLS // cw):
            z = _dot(h, w_ref[:, c * cw:(c + 1) * cw])
            if c == CONV_U_COL // 2:
                z = jnp.concatenate([z[:, 0:512] * z[:, 512:cw], z[:, 512:cw]], axis=-1)
            elif c == RGB // 8:
                g = z[:, 512:cw]
                z = jnp.concatenate([z[:, 0:512], g * _sigmoid(g)], axis=-1)
            elif c >= MERGE_GATE_COL:
                z = _sigmoid(z)
            if c == 0:
                zb_ref[:, 0:cw] = jnp.concatenate([z[:, 0:512] * (ATTN_SCALE * LOG2_E), z[:, 512:cw]],
                                                  axis=-1).astype(BF16)
            else:
                zb_ref[:, c * cw:(c + 1) * cw] = z.astype(BF16)
            if c == 0:
                zq_ref[...] = z
                for hh in range(HEADS):
                    kn_ref[n_prev, hh] = z[:, 512 + hh * HD:512 + (hh + 1) * HD]
            if c == 1:
                for hh in range(HEADS):
                    vn_ref[n_prev, hh] = z[:, hh * HD:(hh + 1) * HD]
        if n_prev:
            kn_ref[0:n_prev] = kp_ref[...]
            vn_ref[0:n_prev] = vp_ref[...]

    project(normalize())


def _inproj(l, x, mod, norm1_w, w_in_bf, k_prev, v_prev):
    t = lambda i: (i + CTX_TILES) % N_TILES
    n_lat = N_TILES - CTX_TILES
    norm_tile = pl.BlockSpec((TM, D), lambda i: (t(i), 0))
    ctx_i = lambda i: jnp.maximum(i - n_lat, 0)
    kv_spec = lambda m: pl.BlockSpec((None, m, HEADS, T_CTX, HD), lambda i: (ctx_i(i), 0, 0, 0, 0))
    kv_shape = jax.ShapeDtypeStruct((N_CTX_SEQ, l + 1, HEADS, T_CTX, HD), F32)
    x_pair = isinstance(x, tuple)
    if x_pair:
        in_specs = [pl.BlockSpec((TM, D), lambda i: (jnp.maximum(i - n_lat, 0), 0)),
                    pl.BlockSpec((TM, D), lambda i: (jnp.minimum(i, n_lat - 1), 0))]
        args = list(x)
    else:
        in_specs = [norm_tile]
        args = [x]
    in_specs += [
        pl.BlockSpec((1, 1, N_MOD * D), lambda i: (l * 8 + _mod_row(t(i)), 0, 0)),
        pl.BlockSpec((None, 1, D), lambda i: (l, 0, 0)),
        pl.BlockSpec((D, IN_COLS), lambda i: (0, 0), pipeline_mode=pl.Buffered(1)),
    ]
    args += [mod, norm1_w, w_in_bf]
    if l:
        in_specs += [kv_spec(l), kv_spec(l)]
        args += [k_prev, v_prev]
    out_specs = [pl.BlockSpec((TM, IN_COLS), lambda i: (t(i), 0)),
                 pl.BlockSpec((TM, 1024), lambda i: (t(i), 0)),
                 kv_spec(l + 1), kv_spec(l + 1)]
    out_shape = [jax.ShapeDtypeStruct((N_TOK, IN_COLS), BF16),
                 jax.ShapeDtypeStruct((N_TOK, 1024), F32),
                 kv_shape, kv_shape]
    return pl.pallas_call(
        functools.partial(_inproj_kernel, x_pair, l),
        grid=(N_TILES,),
        in_specs=in_specs,
        out_specs=out_specs,
        out_shape=out_shape,
        compiler_params=_params(1),
        name="inproj",
    )(*args)


def _lambda(lam_ref, lam_init):
    lv = lam_ref[...]
    a = jnp.sum(lv[0:1] * lv[1:2], axis=-1, keepdims=True)
    b = jnp.sum(lv[2:3] * lv[3:4], axis=-1, keepdims=True)
    return jnp.exp(a) - jnp.exp(b) + lam_init


def _interleave(*stage_generators):
    live = list(stage_generators)
    while live:
        for g in list(live):
            try:
                next(g)
            except StopIteration:
                live.remove(g)


def _diff_attention(lam_init, qs, keys, vs, lam, sw, keys_transposed, o_ref, sls):
    lane = lax.broadcasted_iota(jnp.int32, qs[0].shape, 1)
    score = _dot if keys_transposed else _dot_nt
    maps = [(h, jnp.where(keep, q, jnp.zeros_like(q)))
            for h, q in enumerate(qs) for keep in (lane < 64, lane >= 64)]
    s = [score(qm, keys[h]) for h, qm in maps]
    yield
    e = [jnp.exp2(x - jnp.max(x, axis=-1, keepdims=True)) for x in s]
    yield
    r = [1.0 / jnp.sum(x, axis=-1, keepdims=True) for x in e]
    a = [(e[2 * h] * r[2 * h] - e[2 * h + 1] * (lam * r[2 * h + 1])).astype(BF16) for h in range(len(qs))]
    yield
    o = [_dot(a[h], vs[h]) for h in range(len(qs))]
    yield
    for sl, x in zip(sls, o):
        o_ref[:, sl] = ((_rms(x) * sw) * (1.0 - lam_init)).astype(BF16)


def _scaled_q(q):
    return (q * (ATTN_SCALE * LOG2_E)).astype(BF16)


def _attn_ctx_stages(lam_init, q_ref, k_ref, v_ref, lam_ref, sw_ref, o_ref):
    lam = _lambda(lam_ref, lam_init)
    sls = [slice(h * HD, (h + 1) * HD) for h in range(HEADS)]
    yield from _diff_attention(lam_init, [q_ref[:, sl] for sl in sls], [k_ref[:, sl] for sl in sls],
                               [v_ref[:, sl] for sl in sls], lam, sw_ref[...], False, o_ref, sls)


def _rope(x, cos, sin_signed):
    lane = lax.broadcasted_iota(jnp.int32, x.shape, 1)
    partner = jnp.where(lane % 32 < 16, pltpu.roll(x, 112, 1), pltpu.roll(x, 16, 1))
    return x * cos + partner * sin_signed


def _attn_lat_stages(lam_init, q_ref, k_ref, v_ref, ck_ref, cv_ref, cos_ref, sin_ref,
                     cosq_ref, sinq_ref, lam_ref, sw_ref, o_ref, k_t, vall):
    @pl.when(pl.program_id(1) == 0)
    def _():
        for h in range(HEADS):
            sl = slice(h * HD, (h + 1) * HD)
            k_t[h, :, 0:PAST] = ck_ref[h].T.astype(BF16)
            k_t[h, :, PAST:] = _rope(k_ref[:, sl], cos_ref[...], sin_ref[...]).T.astype(BF16)
            vall[h, 0:PAST, :] = cv_ref[h].astype(BF16)
            vall[h, PAST:, :] = v_ref[:, sl]

    lam = _lambda(lam_ref, lam_init)
    sls = [slice(h * HD, (h + 1) * HD) for h in range(HEADS)]
    qs = [_scaled_q(_rope(q_ref[:, sl], cosq_ref[...], sinq_ref[...])) for sl in sls]
    yield from _diff_attention(lam_init, qs, [k_t[h] for h in range(HEADS)], [vall[h] for h in range(HEADS)],
                               lam, sw_ref[...], True, o_ref, sls)


def _mix_ctx(l, lam_init, zb, attn_lambda, subln_w, decay_f, decay_b, ret_norm_w, sf_prev, sb_prev):
    width = HEADS * HD
    col = lambda c: pl.BlockSpec((T_CTX, width), lambda b: (b, c // HEADS))
    dec = pl.BlockSpec((HEADS, 1, 1), lambda b: (l, 0, 0))
    st_spec = lambda n: pl.BlockSpec((None, n, HEADS, HD, HD), lambda b: (b, 0, 0, 0, 0))
    st_shape = jax.ShapeDtypeStruct((N_CTX_SEQ, l + 1, HEADS, HD, HD), F32)
    out_tile = pl.BlockSpec((T_CTX, width), lambda b: (b, 0))
    out_shape = jax.ShapeDtypeStruct((N_CTX, width), BF16)
    in_specs = [
        col(QB), col(KB), col(VB),
        pl.BlockSpec((None, 4, 64), lambda b: (l, 0, 0)),
        pl.BlockSpec((None, 1, HD), lambda b: (l, 0, 0)),
        col(RQB), col(RKB), col(RVB), col(RGB),
        dec, dec, pl.BlockSpec((HEADS, 1, HD), lambda b: (l, 0, 0)),
    ]
    args = [zb, zb, zb, attn_lambda, subln_w, zb, zb, zb, zb, decay_f, decay_b, ret_norm_w]
    if l:
        in_specs += [st_spec(l), st_spec(l)]
        args += [sf_prev, sb_prev]
    return pl.pallas_call(
        functools.partial(_mix_ctx_kernel, lam_init, l),
        grid=(N_CTX_SEQ,),
        in_specs=in_specs,
        out_specs=[out_tile, out_tile, st_spec(l + 1), st_spec(l + 1)],
        out_shape=[out_shape, out_shape, st_shape, st_shape],
        scratch_shapes=[pltpu.VMEM((HEADS, T_CTX, T_CTX), F32)],
        compiler_params=_params(1),
        name="mix_ctx",
    )(*args)


def _decay_matrix(lgf, lgb, row0, tq, tk):
    i = row0 + lax.broadcasted_iota(jnp.int32, (tq, tk), 0)
    j = lax.broadcasted_iota(jnp.int32, (tq, tk), 1)
    d = (i - j).astype(F32)
    if row0 - (tk - 1) > 0:
        return jnp.exp(lgf * d)
    if row0 + (tq - 1) < 0:
        return jnp.exp(lgb * (-d))
    fwd = jnp.where(d >= 0.0, jnp.exp(lgf * jnp.maximum(d, 0.0)), 0.0)
    bwd = jnp.where(d <= 0.0, jnp.exp(lgb * jnp.maximum(-d, 0.0)), 0.0)
    return fwd + bwd


def _ret_finish(o, g, nw):
    return (g.astype(F32) * (_rms(o) * nw)).astype(BF16)


def _ret_ctx_stages(n_prev, q_ref, k_ref, v_ref, g_ref, df_ref, db_ref, nw_ref, o_ref, sf_ref, sb_ref, dmat):
    j = lax.broadcasted_iota(jnp.int32, (T_CTX, 1), 0).astype(F32)
    heads = range(HEADS)
    sls = [slice(h * HD, (h + 1) * HD) for h in heads]
    q = [q_ref[:, sl] for sl in sls]
    k = [k_ref[:, sl] for sl in sls]
    v = [v_ref[:, sl] for sl in sls]
    s = [(_dot_nt(q[h], k[h]) * dmat[h]).astype(BF16) for h in heads]
    yield
    o = [_dot(s[h], v[h]) for h in heads]
    yield
    lgf = [_log_sigmoid(df_ref[h]) for h in heads]
    lgb = [_log_sigmoid(db_ref[h]) for h in heads]
    kf = [k[h].astype(F32) * RET_SCALE for h in heads]
    kfw = [(kf[h] * jnp.exp(lgf[h] * (T_CTX - 1.0 - j))).astype(BF16) for h in heads]
    kbw = [(kf[h] * jnp.exp(lgb[h] * j)).astype(BF16) for h in heads]
    yield
    for h in heads:
        sf_ref[n_prev, h] = _dot_tn(kfw[h], v[h])
        sb_ref[n_prev, h] = _dot_tn(kbw[h], v[h])
    yield
    for h in heads:
        o_ref[:, sls[h]] = _ret_finish(o[h], g_ref[:, sls[h]], nw_ref[h])


def _mix_ctx_kernel(lam_init, n_prev, *refs):
    refs = list(refs)
    aq_ref, ak_ref, av_ref, lam_ref, sw_ref = refs[0:5]
    rq_ref, rk_ref, rv_ref, rg_ref, df_ref, db_ref, nw_ref = refs[5:12]
    refs = refs[12:]
    if n_prev:
        sfp_ref, sbp_ref = refs.pop(0), refs.pop(0)
    ao_ref, ro_ref, sf_ref, sb_ref, dmat = refs

    @pl.when(pl.program_id(0) == 0)
    def _():
        for h in range(HEADS):
            dmat[h] = RET_SCALE * _decay_matrix(_log_sigmoid(df_ref[h]), _log_sigmoid(db_ref[h]), 0, T_CTX, T_CTX)

    if n_prev:
        sf_ref[0:n_prev] = sfp_ref[...]
        sb_ref[0:n_prev] = sbp_ref[...]
    _interleave(
        _attn_ctx_stages(lam_init, aq_ref, ak_ref, av_ref, lam_ref, sw_ref, ao_ref),
        _ret_ctx_stages(n_prev, rq_ref, rk_ref, rv_ref, rg_ref, df_ref, db_ref, nw_ref, ro_ref, sf_ref, sb_ref, dmat))


N_DCHUNK = 2 * LAT_TILES_PER_SEQ - 1


def _ret_lat_stages(q_ref, k_ref, v_ref, g_ref, s0f_ref, s0b_ref, df_ref, db_ref, nw_ref, o_ref, strip, k_t):
    b = pl.program_id(0)
    j = pl.program_id(1)

    @pl.when(jnp.logical_and(b == 0, j == 0))
    def _():
        for h in range(HEADS):
            lgf = _log_sigmoid(df_ref[h])
            lgb = _log_sigmoid(db_ref[h])
            for c in range(N_DCHUNK):
                strip[h, c] = RET_SCALE * _decay_matrix(lgf, lgb, T_LAT - TM - c * TM, TM, TM)

    @pl.when(j == 0)
    def _():
        for h in range(HEADS):
            k_t[h] = k_ref[:, h * HD:(h + 1) * HD].astype(F32).T.astype(BF16)

    i = (j * TM + lax.broadcasted_iota(jnp.int32, (TM, 1), 0)).astype(F32)
    c0 = LAT_TILES_PER_SEQ - 1 - j
    heads = range(HEADS)
    sls = [slice(h * HD, (h + 1) * HD) for h in heads]
    q = [q_ref[:, sl] for sl in sls]
    dmat = [jnp.concatenate([strip[h, c0 + c] for c in range(LAT_TILES_PER_SEQ)], axis=-1) for h in heads]
    s = [(_dot(q[h], k_t[h]) * dmat[h]).astype(BF16) for h in heads]
    yield
    o = [_dot(s[h], v_ref[:, sls[h]]) for h in heads]
    yield
    of = [_dot(q[h], s0f_ref[h].astype(BF16)) for h in heads]
    ob = [_dot(q[h], s0b_ref[h].astype(BF16)) for h in heads]
    yield
    for h in heads:
        tot = o[h] + jnp.exp(_log_sigmoid(df_ref[h]) * (i + 1.0)) * of[h]
        tot = tot + jnp.exp(_log_sigmoid(db_ref[h]) * (T_LAT - i)) * ob[h]
        o_ref[:, sls[h]] = _ret_finish(tot, g_ref[:, sls[h]], nw_ref[h])


def _mix_lat_kernel(lam_init, *refs):
    attn_in, ret_in = refs[0:11], refs[11:20]
    ao_ref, ro_ref, ak_t, vall, strip, rk_t = refs[20:]
    _interleave(_attn_lat_stages(lam_init, *attn_in, ao_ref, ak_t, vall))
    _interleave(_ret_lat_stages(*ret_in, ro_ref, strip, rk_t))


def _mix_lat(l, lam_init, zb, zq, cache_k, cache_v, cos, sin, attn_lambda, subln_w,
             state_f, state_b, decay_f, decay_b, ret_norm_w):
    width = HEADS * HD
    lat_row = lambda b, j: CTX_TILES + LAT_TILES_PER_SEQ * b + j
    seq_row = lambda b: N_CTX // T_LAT + b
    q_tile = lambda c: pl.BlockSpec((TM, width), lambda b, j: (lat_row(b, j), c // HEADS))
    seq = lambda c: pl.BlockSpec((T_LAT, width), lambda b, j: (seq_row(b), c // HEADS))
    cache = pl.BlockSpec((None, None, HEADS, PAST, HD), lambda b, j: (b, l, 0, 0, 0))
    state = pl.BlockSpec((None, None, HEADS, HD, HD), lambda b, j: (b, l, 0, 0, 0))
    dec = pl.BlockSpec((HEADS, 1, 1), lambda b, j: (l, 0, 0))
    out_tile = pl.BlockSpec((TM, width), lambda b, j: (LAT_TILES_PER_SEQ * b + j, 0))
    out_shape = jax.ShapeDtypeStruct((N_LAT, width), BF16)
    return pl.pallas_call(
        functools.partial(_mix_lat_kernel, lam_init),
        grid=(N_LAT_SEQ, LAT_TILES_PER_SEQ),
        in_specs=[
            q_tile(QB), seq(KB), seq(VB), cache, cache,
            pl.BlockSpec((T_LAT, HD), lambda b, j: (0, 0)),
            pl.BlockSpec((T_LAT, HD), lambda b, j: (0, 0)),
            pl.BlockSpec((TM, HD), lambda b, j: (j, 0)),
            pl.BlockSpec((TM, HD), lambda b, j: (j, 0)),
            pl.BlockSpec((None, 4, 64), lambda b, j: (l, 0, 0)),
            pl.BlockSpec((None, 1, HD), lambda b, j: (l, 0, 0)),
            q_tile(RQB), seq(RKB), seq(RVB), q_tile(RGB), state, state, dec, dec,
            pl.BlockSpec((HEADS, 1, HD), lambda b, j: (l, 0, 0)),
        ],
        out_specs=[out_tile, out_tile],
        out_shape=[out_shape, out_shape],
        scratch_shapes=[pltpu.VMEM((HEADS, HD, PAST + T_LAT), BF16),
                        pltpu.VMEM((HEADS, PAST + T_LAT, HD), BF16),
                        pltpu.VMEM((HEADS, N_DCHUNK, TM, TM), F32),
                        pltpu.VMEM((HEADS, HD, T_LAT), BF16)],
        compiler_params=_params(2),
        name="mix_lat",
    )(zq, zq, zb, cache_k, cache_v, cos, sin, cos, sin, attn_lambda, subln_w,
      zb, zb, zb, zb, state_f, state_b, decay_f, decay_b, ret_norm_w)


def _merge_kernel(x_pair, *refs):
    refs = list(refs)
    x_ref = refs.pop(0)
    xs_ref = refs.pop(0) if x_pair else None
    (aoc_ref, aol_ref, roc_ref, rol_ref, cb_ref, u_ref, up_ref, un_ref,
     mg0_ref, mg1_ref, mg2_ref, mod_ref, cw_ref, wa_ref, wc_ref, wr_ref, wo_ref, n2_ref, wrt_ref,
     x1_ref, h2_ref, aff_ref) = refs
    i = pl.program_id(0)
    is_ctx = i < CTX_TILES // MERGE_SUB
    m = mod_ref[0]
    gate1 = m[:, 2 * D:3 * D]
    shift2 = m[:, 3 * D:4 * D]
    scale2 = m[:, 4 * D:5 * D]
    cw = cw_ref[...]
    u_all = u_ref[...].astype(F32)
    r = lax.broadcasted_iota(jnp.int32, (TM, 512), 0)
    lane = lax.broadcasted_iota(jnp.int32, (TM, 128), 1)

    subs = range(MERGE_SUB)
    rows = [slice(sub * TM, (sub + 1) * TM) for sub in subs]

    def conv_out(sub):
        j = (i * MERGE_SUB + sub - CTX_TILES) % LAT_TILES_PER_SEQ
        seq_first = jnp.logical_or(is_ctx, j == 0)
        seq_last = jnp.logical_or(is_ctx, j == LAT_TILES_PER_SEQ - 1)
        u = u_all[rows[sub], :]
        up = up_ref[...].astype(F32)[15:16, :] if sub == 0 else u_all[sub * TM - 1:sub * TM, :]
        dn = un_ref[...].astype(F32)[0:1, :] if sub == MERGE_SUB - 1 else u_all[(sub + 1) * TM:(sub + 1) * TM + 1, :]
        up = up * jnp.where(seq_first, 0.0, 1.0)
        dn = dn * jnp.where(seq_last, 0.0, 1.0)
        u_prev = jnp.where(r == 0, up, pltpu.roll(u, 1, 0))
        u_next = jnp.where(r == TM - 1, dn, pltpu.roll(u, TM - 1, 0))
        conv = u_prev * cw[0:1, :] + u * cw[1:2, :] + u_next * cw[2:3, :]
        return (cb_ref[rows[sub], :].astype(F32) * conv).astype(BF16)

    ao = [jnp.where(is_ctx, aoc_ref[rw, :], aol_ref[rw, :]) for rw in rows]
    ro = [jnp.where(is_ctx, roc_ref[rw, :], rol_ref[rw, :]) for rw in rows]
    conv_o = [conv_out(sub) for sub in subs]
    b_attn = [_dot(ao[s], wa_ref[...]) for s in subs]
    b_conv = [_dot(conv_o[s], wc_ref[...]) for s in subs]
    b_ret = [_dot(ro[s], wr_ref[...]) for s in subs]
    merged = [(mg0_ref[rows[s], :].astype(F32) * b_attn[s] + mg1_ref[rows[s], :].astype(F32) * b_conv[s]
               + mg2_ref[rows[s], :].astype(F32) * b_ret[s]).astype(BF16) for s in subs]
    proj = [_dot(merged[s], wo_ref[...]) for s in subs]
    x1 = []
    for s in subs:
        x_in = x_ref[rows[s], :]
        if x_pair:
            x_in = jnp.where(is_ctx, x_in, xs_ref[rows[s], :])
        x1.append(x_in + gate1 * proj[s])
        x1_ref[rows[s], :] = x1[s]
    h2 = [(_rms(x1[s]) * n2_ref[...]) * (1.0 + scale2) + shift2 for s in subs]
    logits = [_dot(h2[s].astype(BF16), wrt_ref[...]) for s in subs]
    valid = lane < N_EXPERTS
    for s in subs:
        _to_slabs(h2_ref.at[pl.ds(s * TM * SLAB, TM * SLAB), :], h2[s])
        lmax = jnp.max(jnp.where(valid, logits[s], -jnp.inf), axis=-1, keepdims=True)
        e = jnp.where(valid, jnp.exp(logits[s] - lmax), 0.0)
        aff_ref[rows[s], :] = e * (1.0 / jnp.sum(e, axis=-1, keepdims=True))


def _merge(l, x, ao, ao_lat, ro, ro_lat, zb, mod, conv_w, w_br_attn, w_br_conv, w_br_ret, w_out, norm2_w,
           w_router_pad):
    mt = MERGE_SUB * TM
    n_steps = N_TOK // mt
    ctx_steps = N_CTX // mt
    n16 = N_TOK // 16
    ctx_br = pl.BlockSpec((mt, 512), lambda i: (jnp.minimum(i, ctx_steps - 1), 0))
    lat_br = pl.BlockSpec((mt, 512), lambda i: (jnp.maximum(i - ctx_steps, 0), 0))
    col = lambda c: pl.BlockSpec((mt, 512), lambda i: (i, c))
    halo_p = lambda c: pl.BlockSpec((16, 512), lambda i: (jnp.maximum(i * (mt // 16) - 1, 0), c))
    halo_n = lambda c: pl.BlockSpec((16, 512), lambda i: (jnp.minimum((i + 1) * (mt // 16), n16 - 1), c))
    mgs = lambda c: pl.BlockSpec((mt, D), lambda i: (i, c))
    wbr = pl.BlockSpec((None, 512, D), lambda i: (l, 0, 0))
    x_pair = isinstance(x, tuple)
    if x_pair:
        x_specs = [pl.BlockSpec((mt, D), lambda i: (jnp.minimum(i, ctx_steps - 1), 0)),
                   pl.BlockSpec((mt, D), lambda i: (jnp.maximum(i - ctx_steps, 0), 0))]
        x_args = list(x)
    else:
        x_specs = [pl.BlockSpec((mt, D), lambda i: (i, 0))]
        x_args = [x]
    return pl.pallas_call(
        functools.partial(_merge_kernel, x_pair),
        grid=(n_steps,),
        in_specs=x_specs + [
            ctx_br, lat_br, ctx_br, lat_br,
            col(CONV_B_COL), col(CONV_U_COL), halo_p(CONV_U_COL), halo_n(CONV_U_COL),
            mgs(MERGE_GATE_COL), mgs(MERGE_GATE_COL + 1), mgs(MERGE_GATE_COL + 2),
            pl.BlockSpec((1, 1, N_MOD * D), lambda i: (l * 8 + _mod_row(i * MERGE_SUB), 0, 0)),
            pl.BlockSpec((None, 3, 512), lambda i: (l, 0, 0)),
            wbr, wbr, wbr,
            pl.BlockSpec((None, D, D), lambda i: (l, 0, 0)),
            pl.BlockSpec((None, 1, D), lambda i: (l, 0, 0)),
            pl.BlockSpec((None, D, 128), lambda i: (l, 0, 0)),
        ],
        out_specs=[
            pl.BlockSpec((mt, D), lambda i: (i, 0)),
            pl.BlockSpec((mt * SLAB, 128), lambda i: (i, 0)),
            pl.BlockSpec((mt, 128), lambda i: (i, 0)),
        ],
        out_shape=[
            jax.ShapeDtypeStruct((N_TOK, D), F32),
            jax.ShapeDtypeStruct((N_TOK * SLAB, 128), F32),
            jax.ShapeDtypeStruct((N_TOK, 128), F32),
        ],
        compiler_params=_params(1),
        name="merge",
    )(*x_args, ao, ao_lat, ro, ro_lat, zb, zb, zb, zb, zb, zb, zb, mod, conv_w,
      w_br_attn, w_br_conv, w_br_ret, w_out, norm2_w, w_router_pad)


def _cumsum_lanes(x, tri):
    run = jnp.zeros((x.shape[0], 1), F32)
    outs = []
    for b in range(x.shape[1] // 128):
        cs = _dot(x[:, b * 128:(b + 1) * 128].astype(BF16), tri) + run
        run = cs[:, 127:128]
        outs.append(cs)
    return jnp.concatenate(outs, axis=-1)


GATE_LANE = (0, 16, 32)
IDX_LANE = 48
BISECT_GROUP = 4


def _topk_kernel(an_ref, tmat_ref, idx_ref, gate_ref):
    an = an_ref[...]
    a = an.T[0:N_EXPERTS, :]
    n = a.shape[1]
    kf = float(CAP)

    hi = an.astype(BF16).astype(F32)
    mid = (an - hi).astype(BF16).astype(F32)
    lo = ((an - hi) - mid).astype(BF16).astype(F32)
    table = (tmat_ref[...].astype(F32) + hi + pltpu.roll(mid, GATE_LANE[1], 1)
             + pltpu.roll(lo, GATE_LANE[2], 1)).astype(BF16)

    def count_gt(thr):
        return jnp.sum(jnp.where(a > thr, 1.0, 0.0), axis=-1, keepdims=True)

    def span(lo, hi):
        inside = jnp.logical_and(a > lo, a <= hi)
        cmax = jnp.max(jnp.where(inside, a, -jnp.inf), axis=-1, keepdims=True)
        cmin = jnp.min(jnp.where(inside, a, jnp.inf), axis=-1, keepdims=True)
        return cmax, cmin

    def cond(c):
        return jnp.logical_and(c[2] > 0, c[3] < 400)

    def body(c):
        lo, hi, _, it = c
        for _ in range(BISECT_GROUP):
            mid = 0.5 * (lo + hi)
            ge = count_gt(mid) >= kf
            lo = jnp.where(ge, mid, lo)
            hi = jnp.where(ge, hi, mid)
        cmax, cmin = span(lo, hi)
        open_rows = jnp.max(jnp.where(cmax != cmin, 1, 0))
        return lo, hi, open_rows, it + 1

    lo0 = jnp.full((N_EXPERTS, 1), -1.0, F32)
    hi0 = jnp.max(a, axis=-1, keepdims=True)
    cmax0, cmin0 = span(lo0, hi0)
    lo, hi, _, _ = lax.while_loop(
        cond, body, (lo0, hi0, jnp.max(jnp.where(cmax0 != cmin0, 1, 0)), jnp.int32(0)))
    thr, _ = span(lo, hi)

    r = lax.broadcasted_iota(jnp.int32, (128, 128), 0)
    c = lax.broadcasted_iota(jnp.int32, (128, 128), 1)
    tri = jnp.where(r <= c, 1.0, 0.0).astype(BF16)
    gt = a > thr
    eq = jnp.where(a == thr, 1.0, 0.0)
    need = kf - count_gt(thr)
    eq_before = _cumsum_lanes(eq, tri) - eq
    sel = jnp.where(jnp.logical_or(gt, jnp.logical_and(eq > 0.0, eq_before < need)), 1.0, 0.0)
    pos = _cumsum_lanes(sel, tri) - 1.0
    slot = jnp.where(sel > 0.0, pos, -1.0).astype(jnp.int32)

    p_iota = lax.broadcasted_iota(jnp.int32, (CAP, 1024), 0)
    lane = lax.broadcasted_iota(jnp.int32, (CAP, 128), 1)
    for e in range(N_EXPERTS):
        acc = jnp.zeros((CAP, 128), F32)
        for cb in range(n // 1024):
            onehot = jnp.where(p_iota == slot[e:e + 1, cb * 1024:(cb + 1) * 1024], 1.0, 0.0).astype(BF16)
            acc = acc + _dot(onehot, table[cb * 1024:(cb + 1) * 1024, :])
        acc_t = acc.T
        idx_ref[0, e] = (acc_t[IDX_LANE:IDX_LANE + 1, :] * 64.0 + acc_t[IDX_LANE + 1:IDX_LANE + 2, :]).astype(jnp.int32)
        g = jnp.zeros((CAP, 1), F32)
        for off in GATE_LANE:
            g = g + jnp.sum(jnp.where(lane == off + e, acc, 0.0), axis=-1, keepdims=True)
        gate_ref[0, e] = g


def _topk(aff_n, tmat):
    out_spec = pl.BlockSpec((1, N_EXPERTS, CAP, 1), lambda s: (s, 0, 0, 0))
    return pl.pallas_call(
        _topk_kernel,
        grid=(2,),
        in_specs=[
            pl.BlockSpec((N_CTX, 128), lambda s: (s, 0)),
            pl.BlockSpec((N_CTX, 128), lambda s: (0, 0)),
        ],
        out_specs=[pl.BlockSpec((1, N_EXPERTS, 1, CAP), lambda s: (s, 0, 0, 0)), out_spec],
        out_shape=[jax.ShapeDtypeStruct((2, N_EXPERTS, 1, CAP), jnp.int32),
                   jax.ShapeDtypeStruct((2, N_EXPERTS, CAP, 1), F32)],
        compiler_params=_params(1),
        name="topk",
    )(aff_n, tmat)


N_FT = FF // TF
GATHER_ROWS = 2 * CAP
ROWS_PER_STEP = GATHER_ROWS // N_FT


def _ffn_kernel(cast_next, idx_ref, h2_hbm, g_ref, wg_ref, wu_ref, wd_ref, *refs):
    if cast_next:
        win_ref, ye_ref, wbf_ref, xg, xb, acc, sems = refs
        wbf_ref[...] = win_ref[...].astype(BF16)
    else:
        ye_ref, xg, xb, acc, sems = refs
    e = pl.program_id(0)
    f = pl.program_id(1)
    slot = e % 2

    def row_copy(expert, s, p, dst_slot):
        row = idx_ref[(s * N_EXPERTS + expert) * CAP + p] + s * N_CTX
        src = h2_hbm.at[pl.ds(pl.multiple_of(row * SLAB, SLAB), SLAB), :]
        dst = xg.at[dst_slot, pl.ds(pl.multiple_of((s * CAP + p) * SLAB, SLAB), SLAB), :]
        return pltpu.make_async_copy(src, dst, sems.at[dst_slot])

    def slot_wait(dst_slot):
        pltpu.make_async_copy(h2_hbm.at[pl.ds(0, GATHER_ROWS * SLAB), :], xg.at[dst_slot],
                              sems.at[dst_slot]).wait()

    @pl.when(jnp.logical_and(e == 0, f == 0))
    def _():
        for r in range(GATHER_ROWS):
            row_copy(0, r // CAP, r % CAP, 0).start()
        acc[...] = jnp.zeros_like(acc)

    @pl.when(f == 0)
    def _():
        slot_wait(slot)
        xb[...] = _from_slabs(xg.at[slot]).astype(BF16)

    nxt = jnp.minimum(e + 1, N_EXPERTS - 1)
    s_nxt = f // (N_FT // 2)
    p0 = (f % (N_FT // 2)) * ROWS_PER_STEP
    for u in range(ROWS_PER_STEP):
        row_copy(nxt, s_nxt, p0 + u, 1 - slot).start()

    x = xb[...]
    hg = _dot(x, wg_ref[...].astype(BF16))
    hu = _dot(x, wu_ref[...].astype(BF16))
    hdn = ((hg * _sigmoid(hg)) * hu).astype(BF16)
    acc[...] = jnp.where(f == 0, 0.0, acc[...]) + _dot(hdn, wd_ref[...].astype(BF16))

    @pl.when(f == N_FT - 1)
    def _():
        for s in range(2):
            _to_slabs(ye_ref.at[s], acc[s * CAP:(s + 1) * CAP, :] * g_ref[s])

    @pl.when(jnp.logical_and(e == N_EXPERTS - 1, f == N_FT - 1))
    def _():
        slot_wait(1 - slot)


def _expert_ffn(l, idx_flat, h2s, gates, w_gate, w_up, w_down, w_in):
    cast_next = l + 1 < DEPTH
    n_steps = N_EXPERTS * N_FT
    in_specs = [
        pl.BlockSpec(memory_space=pl.ANY),
        pl.BlockSpec((2, None, CAP, 1), lambda e, f, idx: (0, e, 0, 0)),
        pl.BlockSpec((None, None, D, TF), lambda e, f, idx: (l, e, 0, f)),
        pl.BlockSpec((None, None, D, TF), lambda e, f, idx: (l, e, 0, f)),
        pl.BlockSpec((None, None, TF, D), lambda e, f, idx: (l, e, f, 0)),
    ]
    args = [idx_flat, h2s, gates, w_gate, w_up, w_down]
    out_specs = [pl.BlockSpec((2, None, CAP * SLAB, 128), lambda e, f, idx: (0, e, 0, 0))]
    out_shape = [jax.ShapeDtypeStruct((2, N_EXPERTS, CAP * SLAB, 128), F32)]
    if cast_next:
        cols = IN_COLS // n_steps
        in_specs.append(pl.BlockSpec((None, D, cols), lambda e, f, idx: (l + 1, 0, e * N_FT + f)))
        args.append(w_in)
        out_specs.append(pl.BlockSpec((D, cols), lambda e, f, idx: (0, e * N_FT + f)))
        out_shape.append(jax.ShapeDtypeStruct((D, IN_COLS), BF16))
    grid_spec = pltpu.PrefetchScalarGridSpec(
        num_scalar_prefetch=1,
        grid=(N_EXPERTS, N_FT),
        in_specs=in_specs,
        out_specs=out_specs,
        scratch_shapes=[
            pltpu.VMEM((2, GATHER_ROWS * SLAB, 128), F32),
            pltpu.VMEM((GATHER_ROWS, D), BF16),
            pltpu.VMEM((GATHER_ROWS, D), F32),
            pltpu.SemaphoreType.DMA((2,)),
        ],
    )
    outs = pl.pallas_call(
        functools.partial(_ffn_kernel, cast_next),
        grid_spec=grid_spec,
        out_shape=out_shape,
        compiler_params=_params(2),
        name="expert_ffn",
    )(*args)
    return (outs[0], outs[1]) if cast_next else (outs[0], None)


SCATTER_UNROLL = 8
SCATTER_EXPERTS = 4


N_SCATTER_STEPS = N_EXPERTS // SCATTER_EXPERTS
RESID_ROWS = 2 * TM
N_RESID_STEPS = N_CTX // RESID_ROWS


def _combine_kernel(l, last, idx_ref, ye_ref, x_ref, mod_ref, *refs):
    if last:
        w_ref, op_ref, os_ref, y_acc = refs
    else:
        o_ref, y_acc = refs
    s = pl.program_id(0)
    g = pl.program_id(1)

    @pl.when(g == 0)
    def _():
        y_acc[...] = jnp.zeros_like(y_acc)

    @pl.when(g < N_SCATTER_STEPS)
    def _():
        for k in range(SCATTER_EXPERTS):
            base = (s * N_EXPERTS + g * SCATTER_EXPERTS + k) * CAP
            for p0 in range(0, CAP, SCATTER_UNROLL):
                rows = [idx_ref[base + p0 + u] for u in range(SCATTER_UNROLL)]
                tiles = [pl.ds(pl.multiple_of(r * SLAB, SLAB), SLAB) for r in rows]
                vals = [y_acc[tiles[u], :] + ye_ref[k, pl.ds((p0 + u) * SLAB, SLAB), :]
                        for u in range(SCATTER_UNROLL)]
                for u in range(SCATTER_UNROLL):
                    y_acc[tiles[u], :] = vals[u]

    @pl.when(g >= N_SCATTER_STEPS)
    def _():
        c = g - N_SCATTER_STEPS
        row = jnp.where(s == 0, 0, 1 + c // (T_LAT // RESID_ROWS))
        gate2 = mod_ref[l * 8 + row][:, 5 * D:6 * D]
        y = _from_slabs(y_acc.at[pl.ds(pl.multiple_of(c * RESID_ROWS * SLAB, RESID_ROWS * SLAB), RESID_ROWS * SLAB), :])
        x = x_ref[...] + gate2 * y
        if last:
            out = _rms(x) * w_ref[...]

            @pl.when(s == 0)
            def _():
                op_ref[...] = out

            @pl.when(s == 1)
            def _():
                os_ref[...] = out
        else:
            o_ref[...] = x


def _combine(l, idx_flat, ye, x1, mod, final_norm_w):
    last = l == DEPTH - 1
    resid = lambda g: jnp.maximum(g - N_SCATTER_STEPS, 0)
    in_specs = [
        pl.BlockSpec((None, SCATTER_EXPERTS, CAP * SLAB, 128),
                     lambda s, g, idx: (s, jnp.minimum(g, N_SCATTER_STEPS - 1), 0, 0)),
        pl.BlockSpec((RESID_ROWS, D), lambda s, g, idx: (s * N_RESID_STEPS + resid(g), 0)),
        pl.BlockSpec((DEPTH * 8, 1, N_MOD * D), lambda s, g, idx: (0, 0, 0)),
    ]
    args = [idx_flat, ye, x1, mod]
    if last:
        in_specs.append(pl.BlockSpec((1, D), lambda s, g, idx: (0, 0)))
        args.append(final_norm_w.reshape(1, D))
        out_specs = [
            pl.BlockSpec((RESID_ROWS, D), lambda s, g, idx: (jnp.where(s == 0, resid(g), N_RESID_STEPS - 1), 0)),
            pl.BlockSpec((RESID_ROWS, D), lambda s, g, idx: (jnp.where(s == 1, resid(g), 0), 0)),
        ]
        out_shape = [jax.ShapeDtypeStruct((N_CTX, D), F32), jax.ShapeDtypeStruct((N_LAT, D), F32)]
    else:
        out_specs = pl.BlockSpec((RESID_ROWS, D), lambda s, g, idx: (s * N_RESID_STEPS + resid(g), 0))
        out_shape = jax.ShapeDtypeStruct((N_TOK, D), F32)
    grid_spec = pltpu.PrefetchScalarGridSpec(
        num_scalar_prefetch=1,
        grid=(2, N_SCATTER_STEPS + N_RESID_STEPS),
        in_specs=in_specs,
        out_specs=out_specs,
        scratch_shapes=[pltpu.VMEM((N_CTX * SLAB, 128), F32)],
    )
    return pl.pallas_call(
        functools.partial(_combine_kernel, l, last),
        grid_spec=grid_spec,
        out_shape=out_shape,
        compiler_params=_params(2),
        name="combine",
    )(*args)


def _rope_tables():
    t = np.arange(T_LAT)
    row = (t // GRID_W).astype(np.float32)
    col = (t % GRID_W).astype(np.float32)
    inv = jnp.asarray(ROPE_BASE, F32) ** (-jnp.arange(N_ROPE_FREQ, dtype=F32) / N_ROPE_FREQ)
    ang_r = jnp.asarray(row)[:, None] * inv
    ang_c = jnp.asarray(col)[:, None] * inv
    def group(ang):
        return jnp.concatenate([ang, ang], axis=-1)
    ang = jnp.concatenate([group(ang_r), group(ang_c), group(ang_r), group(ang_c)], axis=-1)
    sign = np.where(np.arange(HD) % 32 < 16, -1.0, 1.0).astype(np.float32)
    return jnp.cos(ang), jnp.sin(ang) * sign


def _index_table():
    t = np.arange(N_CTX)
    tm = np.zeros((N_CTX, 128), np.float32)
    tm[:, IDX_LANE] = t // 64
    tm[:, IDX_LANE + 1] = t % 64
    return jnp.asarray(tm, BF16)


def kernel(x_prompt, x_sample, c, cache_attn_k, cache_attn_v, state_ret_fwd, state_ret_bwd, c_ctx, w_ada, b_ada, norm1_w, norm2_w, w_in, attn_lambda, attn_subln_w, conv_w, ret_decay_fwd, ret_decay_bwd, ret_norm_w, w_br_attn, w_br_conv, w_br_ret, w_out, w_router, w_exp_gate, w_exp_up, w_exp_down, final_norm_w):
    x = (x_prompt.reshape(N_CTX, D), x_sample.reshape(N_LAT, D))
    cvec = jnp.concatenate([c_ctx[None, :], c, jnp.zeros((3, D), F32)], axis=0)
    mod = _modulation(cvec, w_ada, b_ada).reshape(DEPTH * 8, 1, N_MOD * D)

    w_in_bf = w_in[0].astype(BF16)
    w_br_attn_bf = w_br_attn.astype(BF16)
    w_br_conv_bf = w_br_conv.astype(BF16)
    w_br_ret_bf = w_br_ret.astype(BF16)
    w_out_bf = w_out.astype(BF16)
    w_router_pad = jnp.pad(w_router, ((0, 0), (0, 0), (0, 128 - N_EXPERTS))).astype(BF16)
    norm1 = norm1_w.reshape(DEPTH, 1, D)
    norm2 = norm2_w.reshape(DEPTH, 1, D)
    subln = attn_subln_w.reshape(DEPTH, 1, HD)
    decay_f = ret_decay_fwd.reshape(DEPTH * HEADS, 1, 1)
    decay_b = ret_decay_bwd.reshape(DEPTH * HEADS, 1, 1)
    ret_nw = ret_norm_w.reshape(DEPTH * HEADS, 1, HD)
    cos, sin = _rope_tables()
    tmat = _index_table()

    new_k = new_v = new_sf = new_sb = None
    for l in range(DEPTH):
        lam_init = 0.8 - 0.6 * math.exp(-0.3 * l)
        zb, zq, new_k, new_v = _inproj(l, x, mod, norm1, w_in_bf, new_k, new_v)
        ao, ro, new_sf, new_sb = _mix_ctx(l, lam_init, zb, attn_lambda, subln, decay_f, decay_b, ret_nw,
                                          new_sf, new_sb)
        ao_lat, ro_lat = _mix_lat(l, lam_init, zb, zq, cache_attn_k, cache_attn_v, cos, sin, attn_lambda, subln,
                                  state_ret_fwd, state_ret_bwd, decay_f, decay_b, ret_nw)
        x, h2s, aff_n = _merge(l, x, ao, ao_lat, ro, ro_lat, zb, mod, conv_w, w_br_attn_bf, w_br_conv_bf,
                               w_br_ret_bf, w_out_bf, norm2, w_router_pad)
        idx, gates = _topk(aff_n, tmat)
        idx_flat = idx.reshape(2 * N_EXPERTS * CAP)
        ye, w_in_bf = _expert_ffn(l, idx_flat, h2s, gates, w_exp_gate, w_exp_up, w_exp_down, w_in)
        x = _combine(l, idx_flat, ye, x, mod, final_norm_w)

    y_prompt, y_sample = x
    return (y_prompt.reshape(N_CTX_SEQ, T_CTX, D), y_sample.reshape(N_LAT_SEQ, T_LAT, D), new_k, new_v,
            new_sf, new_sb)
```

```python
import functools
import math

import jax
import jax.numpy as jnp
import numpy as np
from jax import lax
from jax.experimental import pallas as pl
from jax.experimental.pallas import tpu as pltpu

F32 = jnp.float32
BF16 = jnp.bfloat16

D = 1024
DEPTH = 2
N_CTX_SEQ = 16
T_CTX = 256
N_LAT_SEQ = 4
T_LAT = 1024
PAST = 256
N_CTX = N_CTX_SEQ * T_CTX
N_LAT = N_LAT_SEQ * T_LAT
N_TOK = N_CTX + N_LAT
TM = 256
N_TILES = N_TOK // TM
CTX_TILES = N_CTX // TM
LAT_TILES_PER_SEQ = T_LAT // TM
HEADS = 4
HD = 128
GRID_W = 64
N_ROPE_FREQ = 16
ROPE_BASE = 10000.0
IN_COLS = 8192
N_MOD = 6
N_EXPERTS = 16
CAP = 512
FF = 2048
TF = 512
EPS = 1e-6
RET_SCALE = HD ** -0.5
ATTN_SCALE = 64 ** -0.5
LOG2_E = math.log2(math.e)
SLAB = D // 128
VMEM_LIMIT = 56 * 1024 * 1024

QB, KB, VB = 0, 4, 8
RQB, RKB, RVB, RGB = 24, 28, 32, 36
CONV_B_COL, CONV_U_COL = 3, 4
MERGE_GATE_COL = 5
MERGE_SUB = 2


def _sigmoid(x):
    return 0.5 * jnp.tanh(0.5 * x) + 0.5


def _log_sigmoid(x):
    return jnp.minimum(x, 0.0) - jnp.log(1.0 + jnp.exp(-jnp.abs(x)))


def _rms(x):
    return x * lax.rsqrt(jnp.mean(x * x, axis=-1, keepdims=True) + EPS)


def _dot(a, b):
    return jnp.dot(a, b, preferred_element_type=F32)


def _dot_nt(a, b):
    return lax.dot_general(a, b, (((1,), (1,)), ((), ())), preferred_element_type=F32)


def _dot_tn(a, b):
    return lax.dot_general(a, b, (((0,), (0,)), ((), ())), preferred_element_type=F32)


def _to_slabs(ref, x):
    n = x.shape[0]
    for s in range(SLAB):
        ref[pl.ds(s, n, stride=SLAB), :] = x[:, s * 128:(s + 1) * 128]


def _from_slabs(ref):
    n = ref.shape[0] // SLAB
    return jnp.concatenate([ref[pl.ds(s, n, stride=SLAB), :] for s in range(SLAB)], axis=-1)


def _mod_row(i):
    return jnp.where(i < CTX_TILES, 0, 1 + (i - CTX_TILES) // LAT_TILES_PER_SEQ)


def _params(n_axes):
    return pltpu.CompilerParams(
        dimension_semantics=("arbitrary",) * n_axes, vmem_limit_bytes=VMEM_LIMIT)


def _mod_kernel(c_ref, w_ref, b_ref, o_ref):
    c = c_ref[...]
    s = (c * _sigmoid(c)).astype(BF16)
    o_ref[...] = _dot(s, w_ref[...].astype(BF16)) + b_ref[...]


def _modulation(cvec, w_ada, b_ada):
    tn = 1024
    return pl.pallas_call(
        _mod_kernel,
        grid=(DEPTH, N_MOD * D // tn),
        in_specs=[
            pl.BlockSpec((8, D), lambda l, n: (0, 0)),
            pl.BlockSpec((None, D, tn), lambda l, n: (l, 0, n)),
            pl.BlockSpec((None, 1, tn), lambda l, n: (l, 0, n)),
        ],
        out_specs=pl.BlockSpec((None, 8, tn), lambda l, n: (l, 0, n)),
        out_shape=jax.ShapeDtypeStruct((DEPTH, 8, N_MOD * D), F32),
        compiler_params=_params(2),
        name="modulation",
    )(cvec, w_ada, b_ada.reshape(DEPTH, 1, N_MOD * D))


def _inproj_kernel(x_pair, n_prev, *refs):
    refs = list(refs)
    x_ref = refs.pop(0)
    if x_pair:
        xs_ref = refs.pop(0)
    mod_ref, n1_ref, w_ref = refs.pop(0), refs.pop(0), refs.pop(0)
    if n_prev:
        kp_ref, vp_ref = refs.pop(0), refs.pop(0)
    zb_ref, zq_ref, kn_ref, vn_ref = refs
    i = pl.program_id(0)

    def normalize():
        x = x_ref[...]
        if x_pair:
            x = jnp.where(i < N_TILES - CTX_TILES, xs_ref[...], x)
        m = mod_ref[0]
        shift1 = m[:, 0:D]
        scale1 = m[:, D:2 * D]
        return ((_rms(x) * n1_ref[...]) * (1.0 + scale1) + shift1).astype(BF16)

    def project(h):
        cw = 1024
        for c in range(IN_COLS // cw):
            z = _dot(h, w_ref[:, c * cw:(c + 1) * cw])
            if c == CONV_U_COL // 2:
                z = jnp.concatenate([z[:, 0:512] * z[:, 512:cw], z[:, 512:cw]], axis=-1)
            elif c == RGB // 8:
                g = z[:, 512:cw]
                z = jnp.concatenate([z[:, 0:512], g * _sigmoid(g)], axis=-1)
            elif c >= MERGE_GATE_COL:
                z = _sigmoid(z)
            if c == 0:
                zb_ref[:, 0:cw] = jnp.concatenate([z[:, 0:512] * (ATTN_SCALE * LOG2_E), z[:, 512:cw]],
                                                  axis=-1).astype(BF16)
            else:
                zb_ref[:, c * cw:(c + 1) * cw] = z.astype(BF16)
            if c == 0:
                zq_ref[...] = z
                for hh in range(HEADS):
                    kn_ref[n_prev, hh] = z[:, 512 + hh * HD:512 + (hh + 1) * HD]
            if c == 1:
                for hh in range(HEADS):
                    vn_ref[n_prev, hh] = z[:, hh * HD:(hh + 1) * HD]
        if n_prev:
            kn_ref[0:n_prev] = kp_ref[...]
            vn_ref[0:n_prev] = vp_ref[...]

    project(normalize())


def _inproj(l, x, mod, norm1_w, w_in_bf, k_prev, v_prev):
    t = lambda i: (i + CTX_TILES) % N_TILES
    n_lat = N_TILES - CTX_TILES
    norm_tile = pl.BlockSpec((TM, D), lambda i: (t(i), 0))
    ctx_i = lambda i: jnp.maximum(i - n_lat, 0)
    kv_spec = lambda m: pl.BlockSpec((None, m, HEADS, T_CTX, HD), lambda i: (ctx_i(i), 0, 0, 0, 0))
    kv_shape = jax.ShapeDtypeStruct((N_CTX_SEQ, l + 1, HEADS, T_CTX, HD), F32)
    x_pair = isinstance(x, tuple)
    if x_pair:
        in_specs = [pl.BlockSpec((TM, D), lambda i: (jnp.maximum(i - n_lat, 0), 0)),
                    pl.BlockSpec((TM, D), lambda i: (jnp.minimum(i, n_lat - 1), 0))]
        args = list(x)
    else:
        in_specs = [norm_tile]
        args = [x]
    in_specs += [
        pl.BlockSpec((1, 1, N_MOD * D), lambda i: (l * 8 + _mod_row(t(i)), 0, 0)),
        pl.BlockSpec((None, 1, D), lambda i: (l, 0, 0)),
        pl.BlockSpec((D, IN_COLS), lambda i: (0, 0), pipeline_mode=pl.Buffered(1)),
    ]
    args += [mod, norm1_w, w_in_bf]
    if l:
        in_specs += [kv_spec(l), kv_spec(l)]
        args += [k_prev, v_prev]
    out_specs = [pl.BlockSpec((TM, IN_COLS), lambda i: (t(i), 0)),
                 pl.BlockSpec((TM, 1024), lambda i: (t(i), 0)),
                 kv_spec(l + 1), kv_spec(l + 1)]
    out_shape = [jax.ShapeDtypeStruct((N_TOK, IN_COLS), BF16),
                 jax.ShapeDtypeStruct((N_TOK, 1024), F32),
                 kv_shape, kv_shape]
    return pl.pallas_call(
        functools.partial(_inproj_kernel, x_pair, l),
        grid=(N_TILES,),
        in_specs=in_specs,
        out_specs=out_specs,
        out_shape=out_shape,
        compiler_params=_params(1),
        name="inproj",
    )(*args)


def _lambda(lam_ref, lam_init):
    lv = lam_ref[...]
    a = jnp.sum(lv[0:1] * lv[1:2], axis=-1, keepdims=True)
    b = jnp.sum(lv[2:3] * lv[3:4], axis=-1, keepdims=True)
    return jnp.exp(a) - jnp.exp(b) + lam_init


def _interleave(*stage_generators):
    live = list(stage_generators)
    while live:
        for g in list(live):
            try:
                next(g)
            except StopIteration:
                live.remove(g)


def _diff_attention(lam_init, qs, keys, vs, lam, sw, keys_transposed, o_ref, sls):
    lane = lax.broadcasted_iota(jnp.int32, qs[0].shape, 1)
    score = _dot if keys_transposed else _dot_nt
    maps = [(h, jnp.where(keep, q, jnp.zeros_like(q)))
            for h, q in enumerate(qs) for keep in (lane < 64, lane >= 64)]
    s = [score(qm, keys[h]) for h, qm in maps]
    yield
    e = [jnp.exp2(x - jnp.max(x, axis=-1, keepdims=True)) for x in s]
    yield
    r = [1.0 / jnp.sum(x, axis=-1, keepdims=True) for x in e]
    a = [(e[2 * h] * r[2 * h] - e[2 * h + 1] * (lam * r[2 * h + 1])).astype(BF16) for h in range(len(qs))]
    yield
    o = [_dot(a[h], vs[h]) for h in range(len(qs))]
    yield
    for sl, x in zip(sls, o):
        o_ref[:, sl] = ((_rms(x) * sw) * (1.0 - lam_init)).astype(BF16)


def _scaled_q(q):
    return (q * (ATTN_SCALE * LOG2_E)).astype(BF16)


def _attn_ctx_stages(lam_init, q_ref, k_ref, v_ref, lam_ref, sw_ref, o_ref):
    lam = _lambda(lam_ref, lam_init)
    sls = [slice(h * HD, (h + 1) * HD) for h in range(HEADS)]
    yield from _diff_attention(lam_init, [q_ref[:, sl] for sl in sls], [k_ref[:, sl] for sl in sls],
                               [v_ref[:, sl] for sl in sls], lam, sw_ref[...], False, o_ref, sls)


def _rope(x, cos, sin_signed):
    lane = lax.broadcasted_iota(jnp.int32, x.shape, 1)
    partner = jnp.where(lane % 32 < 16, pltpu.roll(x, 112, 1), pltpu.roll(x, 16, 1))
    return x * cos + partner * sin_signed


def _attn_lat_stages(lam_init, q_ref, k_ref, v_ref, ck_ref, cv_ref, cos_ref, sin_ref,
                     cosq_ref, sinq_ref, lam_ref, sw_ref, o_ref, k_t, vall):
    @pl.when(pl.program_id(1) == 0)
    def _():
        for h in range(HEADS):
            sl = slice(h * HD, (h + 1) * HD)
            k_t[h, :, 0:PAST] = ck_ref[h].T.astype(BF16)
            k_t[h, :, PAST:] = _rope(k_ref[:, sl], cos_ref[...], sin_ref[...]).T.astype(BF16)
            vall[h, 0:PAST, :] = cv_ref[h].astype(BF16)
            vall[h, PAST:, :] = v_ref[:, sl]

    lam = _lambda(lam_ref, lam_init)
    sls = [slice(h * HD, (h + 1) * HD) for h in range(HEADS)]
    qs = [_scaled_q(_rope(q_ref[:, sl], cosq_ref[...], sinq_ref[...])) for sl in sls]
    yield from _diff_attention(lam_init, qs, [k_t[h] for h in range(HEADS)], [vall[h] for h in range(HEADS)],
                               lam, sw_ref[...], True, o_ref, sls)


def _mix_ctx(l, lam_init, zb, attn_lambda, subln_w, decay_f, decay_b, ret_norm_w, sf_prev, sb_prev):
    width = HEADS * HD
    col = lambda c: pl.BlockSpec((T_CTX, width), lambda b: (b, c // HEADS))
    dec = pl.BlockSpec((HEADS, 1, 1), lambda b: (l, 0, 0))
    st_spec = lambda n: pl.BlockSpec((None, n, HEADS, HD, HD), lambda b: (b, 0, 0, 0, 0))
    st_shape = jax.ShapeDtypeStruct((N_CTX_SEQ, l + 1, HEADS, HD, HD), F32)
    out_tile = pl.BlockSpec((T_CTX, width), lambda b: (b, 0))
    out_shape = jax.ShapeDtypeStruct((N_CTX, width), BF16)
    in_specs = [
        col(QB), col(KB), col(VB),
        pl.BlockSpec((None, 4, 64), lambda b: (l, 0, 0)),
        pl.BlockSpec((None, 1, HD), lambda b: (l, 0, 0)),
        col(RQB), col(RKB), col(RVB), col(RGB),
        dec, dec, pl.BlockSpec((HEADS, 1, HD), lambda b: (l, 0, 0)),
    ]
    args = [zb, zb, zb, attn_lambda, subln_w, zb, zb, zb, zb, decay_f, decay_b, ret_norm_w]
    if l:
        in_specs += [st_spec(l), st_spec(l)]
        args += [sf_prev, sb_prev]
    return pl.pallas_call(
        functools.partial(_mix_ctx_kernel, lam_init, l),
        grid=(N_CTX_SEQ,),
        in_specs=in_specs,
        out_specs=[out_tile, out_tile, st_spec(l + 1), st_spec(l + 1)],
        out_shape=[out_shape, out_shape, st_shape, st_shape],
        scratch_shapes=[pltpu.VMEM((HEADS, T_CTX, T_CTX), F32)],
        compiler_params=_params(1),
        name="mix_ctx",
    )(*args)


def _decay_matrix(lgf, lgb, row0, tq, tk):
    i = row0 + lax.broadcasted_iota(jnp.int32, (tq, tk), 0)
    j = lax.broadcasted_iota(jnp.int32, (tq, tk), 1)
    d = (i - j).astype(F32)
    if row0 - (tk - 1) > 0:
        return jnp.exp(lgf * d)
    if row0 + (tq - 1) < 0:
        return jnp.exp(lgb * (-d))
    fwd = jnp.where(d >= 0.0, jnp.exp(lgf * jnp.maximum(d, 0.0)), 0.0)
    bwd = jnp.where(d <= 0.0, jnp.exp(lgb * jnp.maximum(-d, 0.0)), 0.0)
    return fwd + bwd


def _ret_finish(o, g, nw):
    return (g.astype(F32) * (_rms(o) * nw)).astype(BF16)


def _ret_ctx_stages(n_prev, q_ref, k_ref, v_ref, g_ref, df_ref, db_ref, nw_ref, o_ref, sf_ref, sb_ref, dmat):
    j = lax.broadcasted_iota(jnp.int32, (T_CTX, 1), 0).astype(F32)
    heads = range(HEADS)
    sls = [slice(h * HD, (h + 1) * HD) for h in heads]
    q = [q_ref[:, sl] for sl in sls]
    k = [k_ref[:, sl] for sl in sls]
    v = [v_ref[:, sl] for sl in sls]
    s = [(_dot_nt(q[h], k[h]) * dmat[h]).astype(BF16) for h in heads]
    yield
    o = [_dot(s[h], v[h]) for h in heads]
    yield
    lgf = [_log_sigmoid(df_ref[h]) for h in heads]
    lgb = [_log_sigmoid(db_ref[h]) for h in heads]
    kf = [k[h].astype(F32) * RET_SCALE for h in heads]
    kfw = [(kf[h] * jnp.exp(lgf[h] * (T_CTX - 1.0 - j))).astype(BF16) for h in heads]
    kbw = [(kf[h] * jnp.exp(lgb[h] * j)).astype(BF16) for h in heads]
    yield
    for h in heads:
        sf_ref[n_prev, h] = _dot_tn(kfw[h], v[h])
        sb_ref[n_prev, h] = _dot_tn(kbw[h], v[h])
    yield
    for h in heads:
        o_ref[:, sls[h]] = _ret_finish(o[h], g_ref[:, sls[h]], nw_ref[h])


def _mix_ctx_kernel(lam_init, n_prev, *refs):
    refs = list(refs)
    aq_ref, ak_ref, av_ref, lam_ref, sw_ref = refs[0:5]
    rq_ref, rk_ref, rv_ref, rg_ref, df_ref, db_ref, nw_ref = refs[5:12]
    refs = refs[12:]
    if n_prev:
        sfp_ref, sbp_ref = refs.pop(0), refs.pop(0)
    ao_ref, ro_ref, sf_ref, sb_ref, dmat = refs

    @pl.when(pl.program_id(0) == 0)
    def _():
        for h in range(HEADS):
            dmat[h] = RET_SCALE * _decay_matrix(_log_sigmoid(df_ref[h]), _log_sigmoid(db_ref[h]), 0, T_CTX, T_CTX)

    if n_prev:
        sf_ref[0:n_prev] = sfp_ref[...]
        sb_ref[0:n_prev] = sbp_ref[...]
    _interleave(
        _attn_ctx_stages(lam_init, aq_ref, ak_ref, av_ref, lam_ref, sw_ref, ao_ref),
        _ret_ctx_stages(n_prev, rq_ref, rk_ref, rv_ref, rg_ref, df_ref, db_ref, nw_ref, ro_ref, sf_ref, sb_ref, dmat))


N_DCHUNK = 2 * LAT_TILES_PER_SEQ - 1


def _ret_lat_stages(q_ref, k_ref, v_ref, g_ref, s0f_ref, s0b_ref, df_ref, db_ref, nw_ref, o_ref, strip, k_t):
    b = pl.program_id(0)
    j = pl.program_id(1)

    @pl.when(jnp.logical_and(b == 0, j == 0))
    def _():
        for h in range(HEADS):
            lgf = _log_sigmoid(df_ref[h])
            lgb = _log_sigmoid(db_ref[h])
            for c in range(N_DCHUNK):
                strip[h, c] = RET_SCALE * _decay_matrix(lgf, lgb, T_LAT - TM - c * TM, TM, TM)

    @pl.when(j == 0)
    def _():
        for h in range(HEADS):
            k_t[h] = k_ref[:, h * HD:(h + 1) * HD].astype(F32).T.astype(BF16)

    i = (j * TM + lax.broadcasted_iota(jnp.int32, (TM, 1), 0)).astype(F32)
    c0 = LAT_TILES_PER_SEQ - 1 - j
    heads = range(HEADS)
    sls = [slice(h * HD, (h + 1) * HD) for h in heads]
    q = [q_ref[:, sl] for sl in sls]
    dmat = [jnp.concatenate([strip[h, c0 + c] for c in range(LAT_TILES_PER_SEQ)], axis=-1) for h in heads]
    s = [(_dot(q[h], k_t[h]) * dmat[h]).astype(BF16) for h in heads]
    yield
    o = [_dot(s[h], v_ref[:, sls[h]]) for h in heads]
    yield
    of = [_dot(q[h], s0f_ref[h].astype(BF16)) for h in heads]
    ob = [_dot(q[h], s0b_ref[h].astype(BF16)) for h in heads]
    yield
    for h in heads:
        tot = o[h] + jnp.exp(_log_sigmoid(df_ref[h]) * (i + 1.0)) * of[h]
        tot = tot + jnp.exp(_log_sigmoid(db_ref[h]) * (T_LAT - i)) * ob[h]
        o_ref[:, sls[h]] = _ret_finish(tot, g_ref[:, sls[h]], nw_ref[h])


def _mix_lat_kernel(lam_init, *refs):
    attn_in, ret_in = refs[0:11], refs[11:20]
    ao_ref, ro_ref, ak_t, vall, strip, rk_t = refs[20:]
    _interleave(_attn_lat_stages(lam_init, *attn_in, ao_ref, ak_t, vall))
    _interleave(_ret_lat_stages(*ret_in, ro_ref, strip, rk_t))


def _mix_lat(l, lam_init, zb, zq, cache_k, cache_v, cos, sin, attn_lambda, subln_w,
             state_f, state_b, decay_f, decay_b, ret_norm_w):
    width = HEADS * HD
    lat_row = lambda b, j: CTX_TILES + LAT_TILES_PER_SEQ * b + j
    seq_row = lambda b: N_CTX // T_LAT + b
    q_tile = lambda c: pl.BlockSpec((TM, width), lambda b, j: (lat_row(b, j), c // HEADS))
    seq = lambda c: pl.BlockSpec((T_LAT, width), lambda b, j: (seq_row(b), c // HEADS))
    cache = pl.BlockSpec((None, None, HEADS, PAST, HD), lambda b, j: (b, l, 0, 0, 0))
    state = pl.BlockSpec((None, None, HEADS, HD, HD), lambda b, j: (b, l, 0, 0, 0))
    dec = pl.BlockSpec((HEADS, 1, 1), lambda b, j: (l, 0, 0))
    out_tile = pl.BlockSpec((TM, width), lambda b, j: (LAT_TILES_PER_SEQ * b + j, 0))
    out_shape = jax.ShapeDtypeStruct((N_LAT, width), BF16)
    return pl.pallas_call(
        functools.partial(_mix_lat_kernel, lam_init),
        grid=(N_LAT_SEQ, LAT_TILES_PER_SEQ),
        in_specs=[
            q_tile(QB), seq(KB), seq(VB), cache, cache,
            pl.BlockSpec((T_LAT, HD), lambda b, j: (0, 0)),
            pl.BlockSpec((T_LAT, HD), lambda b, j: (0, 0)),
            pl.BlockSpec((TM, HD), lambda b, j: (j, 0)),
            pl.BlockSpec((TM, HD), lambda b, j: (j, 0)),
            pl.BlockSpec((None, 4, 64), lambda b, j: (l, 0, 0)),
            pl.BlockSpec((None, 1, HD), lambda b, j: (l, 0, 0)),
            q_tile(RQB), seq(RKB), seq(RVB), q_tile(RGB), state, state, dec, dec,
            pl.BlockSpec((HEADS, 1, HD), lambda b, j: (l, 0, 0)),
        ],
        out_specs=[out_tile, out_tile],
        out_shape=[out_shape, out_shape],
        scratch_shapes=[pltpu.VMEM((HEADS, HD, PAST + T_LAT), BF16),
                        pltpu.VMEM((HEADS, PAST + T_LAT, HD), BF16),
                        pltpu.VMEM((HEADS, N_DCHUNK, TM, TM), F32),
                        pltpu.VMEM((HEADS, HD, T_LAT), BF16)],
        compiler_params=_params(2),
        name="mix_lat",
    )(zq, zq, zb, cache_k, cache_v, cos, sin, cos, sin, attn_lambda, subln_w,
      zb, zb, zb, zb, state_f, state_b, decay_f, decay_b, ret_norm_w)


def _merge_kernel(x_pair, *refs):
    refs = list(refs)
    x_ref = refs.pop(0)
    xs_ref = refs.pop(0) if x_pair else None
    (aoc_ref, aol_ref, roc_ref, rol_ref, cb_ref, u_ref, up_ref, un_ref,
     mg0_ref, mg1_ref, mg2_ref, mod_ref, cw_ref, wa_ref, wc_ref, wr_ref, wo_ref, n2_ref, wrt_ref,
     x1_ref, h2_ref, aff_ref) = refs
    i = pl.program_id(0)
    is_ctx = i < CTX_TILES // MERGE_SUB
    m = mod_ref[0]
    gate1 = m[:, 2 * D:3 * D]
    shift2 = m[:, 3 * D:4 * D]
    scale2 = m[:, 4 * D:5 * D]
    cw = cw_ref[...]
    u_all = u_ref[...].astype(F32)
    r = lax.broadcasted_iota(jnp.int32, (TM, 512), 0)
    lane = lax.broadcasted_iota(jnp.int32, (TM, 128), 1)

    subs = range(MERGE_SUB)
    rows = [slice(sub * TM, (sub + 1) * TM) for sub in subs]

    def conv_out(sub):
        j = (i * MERGE_SUB + sub - CTX_TILES) % LAT_TILES_PER_SEQ
        seq_first = jnp.logical_or(is_ctx, j == 0)
        seq_last = jnp.logical_or(is_ctx, j == LAT_TILES_PER_SEQ - 1)
        u = u_all[rows[sub], :]
        up = up_ref[...].astype(F32)[15:16, :] if sub == 0 else u_all[sub * TM - 1:sub * TM, :]
        dn = un_ref[...].astype(F32)[0:1, :] if sub == MERGE_SUB - 1 else u_all[(sub + 1) * TM:(sub + 1) * TM + 1, :]
        up = up * jnp.where(seq_first, 0.0, 1.0)
        dn = dn * jnp.where(seq_last, 0.0, 1.0)
        u_prev = jnp.where(r == 0, up, pltpu.roll(u, 1, 0))
        u_next = jnp.where(r == TM - 1, dn, pltpu.roll(u, TM - 1, 0))
        conv = u_prev * cw[0:1, :] + u * cw[1:2, :] + u_next * cw[2:3, :]
        return (cb_ref[rows[sub], :].astype(F32) * conv).astype(BF16)

    ao = [jnp.where(is_ctx, aoc_ref[rw, :], aol_ref[rw, :]) for rw in rows]
    ro = [jnp.where(is_ctx, roc_ref[rw, :], rol_ref[rw, :]) for rw in rows]
    conv_o = [conv_out(sub) for sub in subs]
    b_attn = [_dot(ao[s], wa_ref[...]) for s in subs]
    b_conv = [_dot(conv_o[s], wc_ref[...]) for s in subs]
    b_ret = [_dot(ro[s], wr_ref[...]) for s in subs]
    merged = [(mg0_ref[rows[s], :].astype(F32) * b_attn[s] + mg1_ref[rows[s], :].astype(F32) * b_conv[s]
               + mg2_ref[rows[s], :].astype(F32) * b_ret[s]).astype(BF16) for s in subs]
    proj = [_dot(merged[s], wo_ref[...]) for s in subs]
    x1 = []
    for s in subs:
        x_in = x_ref[rows[s], :]
        if x_pair:
            x_in = jnp.where(is_ctx, x_in, xs_ref[rows[s], :])
        x1.append(x_in + gate1 * proj[s])
        x1_ref[rows[s], :] = x1[s]
    h2 = [(_rms(x1[s]) * n2_ref[...]) * (1.0 + scale2) + shift2 for s in subs]
    logits = [_dot(h2[s].astype(BF16), wrt_ref[...]) for s in subs]
    valid = lane < N_EXPERTS
    for s in subs:
        _to_slabs(h2_ref.at[pl.ds(s * TM * SLAB, TM * SLAB), :], h2[s])
        lmax = jnp.max(jnp.where(valid, logits[s], -jnp.inf), axis=-1, keepdims=True)
        e = jnp.where(valid, jnp.exp(logits[s] - lmax), 0.0)
        aff_ref[rows[s], :] = e * (1.0 / jnp.sum(e, axis=-1, keepdims=True))


def _merge(l, x, ao, ao_lat, ro, ro_lat, zb, mod, conv_w, w_br_attn, w_br_conv, w_br_ret, w_out, norm2_w,
           w_router_pad):
    mt = MERGE_SUB * TM
    n_steps = N_TOK // mt
    ctx_steps = N_CTX // mt
    n16 = N_TOK // 16
    ctx_br = pl.BlockSpec((mt, 512), lambda i: (jnp.minimum(i, ctx_steps - 1), 0))
    lat_br = pl.BlockSpec((mt, 512), lambda i: (jnp.maximum(i - ctx_steps, 0), 0))
    col = lambda c: pl.BlockSpec((mt, 512), lambda i: (i, c))
    halo_p = lambda c: pl.BlockSpec((16, 512), lambda i: (jnp.maximum(i * (mt // 16) - 1, 0), c))
    halo_n = lambda c: pl.BlockSpec((16, 512), lambda i: (jnp.minimum((i + 1) * (mt // 16), n16 - 1), c))
    mgs = lambda c: pl.BlockSpec((mt, D), lambda i: (i, c))
    wbr = pl.BlockSpec((None, 512, D), lambda i: (l, 0, 0))
    x_pair = isinstance(x, tuple)
    if x_pair:
        x_specs = [pl.BlockSpec((mt, D), lambda i: (jnp.minimum(i, ctx_steps - 1), 0)),
                   pl.BlockSpec((mt, D), lambda i: (jnp.maximum(i - ctx_steps, 0), 0))]
        x_args = list(x)
    else:
        x_specs = [pl.BlockSpec((mt, D), lambda i: (i, 0))]
        x_args = [x]
    return pl.pallas_call(
        functools.partial(_merge_kernel, x_pair),
        grid=(n_steps,),
        in_specs=x_specs + [
            ctx_br, lat_br, ctx_br, lat_br,
            col(CONV_B_COL), col(CONV_U_COL), halo_p(CONV_U_COL), halo_n(CONV_U_COL),
            mgs(MERGE_GATE_COL), mgs(MERGE_GATE_COL + 1), mgs(MERGE_GATE_COL + 2),
            pl.BlockSpec((1, 1, N_MOD * D), lambda i: (l * 8 + _mod_row(i * MERGE_SUB), 0, 0)),
            pl.BlockSpec((None, 3, 512), lambda i: (l, 0, 0)),
            wbr, wbr, wbr,
            pl.BlockSpec((None, D, D), lambda i: (l, 0, 0)),
            pl.BlockSpec((None, 1, D), lambda i: (l, 0, 0)),
            pl.BlockSpec((None, D, 128), lambda i: (l, 0, 0)),
        ],
        out_specs=[
            pl.BlockSpec((mt, D), lambda i: (i, 0)),
            pl.BlockSpec((mt * SLAB, 128), lambda i: (i, 0)),
            pl.BlockSpec((mt, 128), lambda i: (i, 0)),
        ],
        out_shape=[
            jax.ShapeDtypeStruct((N_TOK, D), F32),
            jax.ShapeDtypeStruct((N_TOK * SLAB, 128), F32),
            jax.ShapeDtypeStruct((N_TOK, 128), F32),
        ],
        compiler_params=_params(1),
        name="merge",
    )(*x_args, ao, ao_lat, ro, ro_lat, zb, zb, zb, zb, zb, zb, zb, mod, conv_w,
      w_br_attn, w_br_conv, w_br_ret, w_out, norm2_w, w_router_pad)


def _cumsum_lanes(x, tri):
    run = jnp.zeros((x.shape[0], 1), F32)
    outs = []
    for b in range(x.shape[1] // 128):
        cs = _dot(x[:, b * 128:(b + 1) * 128].astype(BF16), tri) + run
        run = cs[:, 127:128]
        outs.append(cs)
    return jnp.concatenate(outs, axis=-1)


GATE_LANE = (0, 16, 32)
IDX_LANE = 48
BISECT_GROUP = 4


def _topk_kernel(an_ref, tmat_ref, idx_ref, gate_ref):
    an = an_ref[...]
    a = an.T[0:N_EXPERTS, :]
    n = a.shape[1]
    kf = float(CAP)

    hi = an.astype(BF16).astype(F32)
    mid = (an - hi).astype(BF16).astype(F32)
    lo = ((an - hi) - mid).astype(BF16).astype(F32)
    table = (tmat_ref[...].astype(F32) + hi + pltpu.roll(mid, GATE_LANE[1], 1)
             + pltpu.roll(lo, GATE_LANE[2], 1)).astype(BF16)

    def count_gt(thr):
        return jnp.sum(jnp.where(a > thr, 1.0, 0.0), axis=-1, keepdims=True)

    def span(lo, hi):
        inside = jnp.logical_and(a > lo, a <= hi)
        cmax = jnp.max(jnp.where(inside, a, -jnp.inf), axis=-1, keepdims=True)
        cmin = jnp.min(jnp.where(inside, a, jnp.inf), axis=-1, keepdims=True)
        return cmax, cmin

    def cond(c):
        return jnp.logical_and(c[2] > 0, c[3] < 400)

    def body(c):
        lo, hi, _, it = c
        for _ in range(BISECT_GROUP):
            mid = 0.5 * (lo + hi)
            ge = count_gt(mid) >= kf
            lo = jnp.where(ge, mid, lo)
            hi = jnp.where(ge, hi, mid)
        cmax, cmin = span(lo, hi)
        open_rows = jnp.max(jnp.where(cmax != cmin, 1, 0))
        return lo, hi, open_rows, it + 1

    lo0 = jnp.full((N_EXPERTS, 1), -1.0, F32)
    hi0 = jnp.max(a, axis=-1, keepdims=True)
    cmax0, cmin0 = span(lo0, hi0)
    lo, hi, _, _ = lax.while_loop(
        cond, body, (lo0, hi0, jnp.max(jnp.where(cmax0 != cmin0, 1, 0)), jnp.int32(0)))
    thr, _ = span(lo, hi)

    r = lax.broadcasted_iota(jnp.int32, (128, 128), 0)
    c = lax.broadcasted_iota(jnp.int32, (128, 128), 1)
    tri = jnp.where(r <= c, 1.0, 0.0).astype(BF16)
    gt = a > thr
    eq = jnp.where(a == thr, 1.0, 0.0)
    need = kf - count_gt(thr)
    eq_before = _cumsum_lanes(eq, tri) - eq
    sel = jnp.where(jnp.logical_or(gt, jnp.logical_and(eq > 0.0, eq_before < need)), 1.0, 0.0)
    pos = _cumsum_lanes(sel, tri) - 1.0
    slot = jnp.where(sel > 0.0, pos, -1.0).astype(jnp.int32)

    p_iota = lax.broadcasted_iota(jnp.int32, (CAP, 1024), 0)
    lane = lax.broadcasted_iota(jnp.int32, (CAP, 128), 1)
    for e in range(N_EXPERTS):
        acc = jnp.zeros((CAP, 128), F32)
        for cb in range(n // 1024):
            onehot = jnp.where(p_iota == slot[e:e + 1, cb * 1024:(cb + 1) * 1024], 1.0, 0.0).astype(BF16)
            acc = acc + _dot(onehot, table[cb * 1024:(cb + 1) * 1024, :])
        acc_t = acc.T
        idx_ref[0, e] = (acc_t[IDX_LANE:IDX_LANE + 1, :] * 64.0 + acc_t[IDX_LANE + 1:IDX_LANE + 2, :]).astype(jnp.int32)
        g = jnp.zeros((CAP, 1), F32)
        for off in GATE_LANE:
            g = g + jnp.sum(jnp.where(lane == off + e, acc, 0.0), axis=-1, keepdims=True)
        gate_ref[0, e] = g


def _topk(aff_n, tmat):
    out_spec = pl.BlockSpec((1, N_EXPERTS, CAP, 1), lambda s: (s, 0, 0, 0))
    return pl.pallas_call(
        _topk_kernel,
        grid=(2,),
        in_specs=[
            pl.BlockSpec((N_CTX, 128), lambda s: (s, 0)),
            pl.BlockSpec((N_CTX, 128), lambda s: (0, 0)),
        ],
        out_specs=[pl.BlockSpec((1, N_EXPERTS, 1, CAP), lambda s: (s, 0, 0, 0)), out_spec],
        out_shape=[jax.ShapeDtypeStruct((2, N_EXPERTS, 1, CAP), jnp.int32),
                   jax.ShapeDtypeStruct((2, N_EXPERTS, CAP, 1), F32)],
        compiler_params=_params(1),
        name="topk",
    )(aff_n, tmat)


N_FT = FF // TF
GATHER_ROWS = 2 * CAP
ROWS_PER_STEP = GATHER_ROWS // N_FT


def _ffn_kernel(cast_next, idx_ref, h2_hbm, g_ref, wg_ref, wu_ref, wd_ref, *refs):
    if cast_next:
        win_ref, ye_ref, wbf_ref, xg, xb, acc, sems = refs
        wbf_ref[...] = win_ref[...].astype(BF16)
    else:
        ye_ref, xg, xb, acc, sems = refs
    e = pl.program_id(0)
    f = pl.program_id(1)
    slot = e % 2

    def row_copy(expert, s, p, dst_slot):
        row = idx_ref[(s * N_EXPERTS + expert) * CAP + p] + s * N_CTX
        src = h2_hbm.at[pl.ds(pl.multiple_of(row * SLAB, SLAB), SLAB), :]
        dst = xg.at[dst_slot, pl.ds(pl.multiple_of((s * CAP + p) * SLAB, SLAB), SLAB), :]
        return pltpu.make_async_copy(src, dst, sems.at[dst_slot])

    def slot_wait(dst_slot):
        pltpu.make_async_copy(h2_hbm.at[pl.ds(0, GATHER_ROWS * SLAB), :], xg.at[dst_slot],
                              sems.at[dst_slot]).wait()

    @pl.when(jnp.logical_and(e == 0, f == 0))
    def _():
        for r in range(GATHER_ROWS):
            row_copy(0, r // CAP, r % CAP, 0).start()
        acc[...] = jnp.zeros_like(acc)

    @pl.when(f == 0)
    def _():
        slot_wait(slot)
        xb[...] = _from_slabs(xg.at[slot]).astype(BF16)

    nxt = jnp.minimum(e + 1, N_EXPERTS - 1)
    s_nxt = f // (N_FT // 2)
    p0 = (f % (N_FT // 2)) * ROWS_PER_STEP
    for u in range(ROWS_PER_STEP):
        row_copy(nxt, s_nxt, p0 + u, 1 - slot).start()

    x = xb[...]
    hg = _dot(x, wg_ref[...].astype(BF16))
    hu = _dot(x, wu_ref[...].astype(BF16))
    hdn = ((hg * _sigmoid(hg)) * hu).astype(BF16)
    acc[...] = jnp.where(f == 0, 0.0, acc[...]) + _dot(hdn, wd_ref[...].astype(BF16))

    @pl.when(f == N_FT - 1)
    def _():
        for s in range(2):
            _to_slabs(ye_ref.at[s], acc[s * CAP:(s + 1) * CAP, :] * g_ref[s])

    @pl.when(jnp.logical_and(e == N_EXPERTS - 1, f == N_FT - 1))
    def _():
        slot_wait(1 - slot)


def _expert_ffn(l, idx_flat, h2s, gates, w_gate, w_up, w_down, w_in):
    cast_next = l + 1 < DEPTH
    n_steps = N_EXPERTS * N_FT
    in_specs = [
        pl.BlockSpec(memory_space=pl.ANY),
        pl.BlockSpec((2, None, CAP, 1), lambda e, f, idx: (0, e, 0, 0)),
        pl.BlockSpec((None, None, D, TF), lambda e, f, idx: (l, e, 0, f)),
        pl.BlockSpec((None, None, D, TF), lambda e, f, idx: (l, e, 0, f)),
        pl.BlockSpec((None, None, TF, D), lambda e, f, idx: (l, e, f, 0)),
    ]
    args = [idx_flat, h2s, gates, w_gate, w_up, w_down]
    out_specs = [pl.BlockSpec((2, None, CAP * SLAB, 128), lambda e, f, idx: (0, e, 0, 0))]
    out_shape = [jax.ShapeDtypeStruct((2, N_EXPERTS, CAP * SLAB, 128), F32)]
    if cast_next:
        cols = IN_COLS // n_steps
        in_specs.append(pl.BlockSpec((None, D, cols), lambda e, f, idx: (l + 1, 0, e * N_FT + f)))
        args.append(w_in)
        out_specs.append(pl.BlockSpec((D, cols), lambda e, f, idx: (0, e * N_FT + f)))
        out_shape.append(jax.ShapeDtypeStruct((D, IN_COLS), BF16))
    grid_spec = pltpu.PrefetchScalarGridSpec(
        num_scalar_prefetch=1,
        grid=(N_EXPERTS, N_FT),
        in_specs=in_specs,
        out_specs=out_specs,
        scratch_shapes=[
            pltpu.VMEM((2, GATHER_ROWS * SLAB, 128), F32),
            pltpu.VMEM((GATHER_ROWS, D), BF16),
            pltpu.VMEM((GATHER_ROWS, D), F32),
            pltpu.SemaphoreType.DMA((2,)),
        ],
    )
    outs = pl.pallas_call(
        functools.partial(_ffn_kernel, cast_next),
        grid_spec=grid_spec,
        out_shape=out_shape,
        compiler_params=_params(2),
        name="expert_ffn",
    )(*args)
    return (outs[0], outs[1]) if cast_next else (outs[0], None)


SCATTER_UNROLL = 16
SCATTER_EXPERTS = 4


N_SCATTER_STEPS = N_EXPERTS // SCATTER_EXPERTS
RESID_ROWS = 2 * TM
N_RESID_STEPS = N_CTX // RESID_ROWS


def _combine_kernel(l, last, idx_ref, ye_ref, x_ref, mod_ref, *refs):
    if last:
        w_ref, op_ref, os_ref, y_acc = refs
    else:
        o_ref, y_acc = refs
    s = pl.program_id(0)
    g = pl.program_id(1)

    @pl.when(g == 0)
    def _():
        y_acc[...] = jnp.zeros_like(y_acc)

    @pl.when(g < N_SCATTER_STEPS)
    def _():
        for k in range(SCATTER_EXPERTS):
            base = (s * N_EXPERTS + g * SCATTER_EXPERTS + k) * CAP
            for p0 in range(0, CAP, SCATTER_UNROLL):
                rows = [idx_ref[base + p0 + u] for u in range(SCATTER_UNROLL)]
                tiles = [pl.ds(pl.multiple_of(r * SLAB, SLAB), SLAB) for r in rows]
                vals = [y_acc[tiles[u], :] + ye_ref[k, pl.ds((p0 + u) * SLAB, SLAB), :]
                        for u in range(SCATTER_UNROLL)]
                for u in range(SCATTER_UNROLL):
                    y_acc[tiles[u], :] = vals[u]

    @pl.when(g >= N_SCATTER_STEPS)
    def _():
        c = g - N_SCATTER_STEPS
        row = jnp.where(s == 0, 0, 1 + c // (T_LAT // RESID_ROWS))
        gate2 = mod_ref[l * 8 + row][:, 5 * D:6 * D]
        y = _from_slabs(y_acc.at[pl.ds(pl.multiple_of(c * RESID_ROWS * SLAB, RESID_ROWS * SLAB), RESID_ROWS * SLAB), :])
        x = x_ref[...] + gate2 * y
        if last:
            out = _rms(x) * w_ref[...]

            @pl.when(s == 0)
            def _():
                op_ref[...] = out

            @pl.when(s == 1)
            def _():
                os_ref[...] = out
        else:
            o_ref[...] = x


def _combine(l, idx_flat, ye, x1, mod, final_norm_w):
    last = l == DEPTH - 1
    resid = lambda g: jnp.maximum(g - N_SCATTER_STEPS, 0)
    in_specs = [
        pl.BlockSpec((None, SCATTER_EXPERTS, CAP * SLAB, 128),
                     lambda s, g, idx: (s, jnp.minimum(g, N_SCATTER_STEPS - 1), 0, 0)),
        pl.BlockSpec((RESID_ROWS, D), lambda s, g, idx: (s * N_RESID_STEPS + resid(g), 0)),
        pl.BlockSpec((DEPTH * 8, 1, N_MOD * D), lambda s, g, idx: (0, 0, 0)),
    ]
    args = [idx_flat, ye, x1, mod]
    if last:
        in_specs.append(pl.BlockSpec((1, D), lambda s, g, idx: (0, 0)))
        args.append(final_norm_w.reshape(1, D))
        out_specs = [
            pl.BlockSpec((RESID_ROWS, D), lambda s, g, idx: (jnp.where(s == 0, resid(g), N_RESID_STEPS - 1), 0)),
            pl.BlockSpec((RESID_ROWS, D), lambda s, g, idx: (jnp.where(s == 1, resid(g), 0), 0)),
        ]
        out_shape = [jax.ShapeDtypeStruct((N_CTX, D), F32), jax.ShapeDtypeStruct((N_LAT, D), F32)]
    else:
        out_specs = pl.BlockSpec((RESID_ROWS, D), lambda s, g, idx: (s * N_RESID_STEPS + resid(g), 0))
        out_shape = jax.ShapeDtypeStruct((N_TOK, D), F32)
    grid_spec = pltpu.PrefetchScalarGridSpec(
        num_scalar_prefetch=1,
        grid=(2, N_SCATTER_STEPS + N_RESID_STEPS),
        in_specs=in_specs,
        out_specs=out_specs,
        scratch_shapes=[pltpu.VMEM((N_CTX * SLAB, 128), F32)],
    )
    return pl.pallas_call(
        functools.partial(_combine_kernel, l, last),
        grid_spec=grid_spec,
        out_shape=out_shape,
        compiler_params=_params(2),
        name="combine",
    )(*args)


def _rope_tables():
    t = np.arange(T_LAT)
    row = (t // GRID_W).astype(np.float32)
    col = (t % GRID_W).astype(np.float32)
    inv = jnp.asarray(ROPE_BASE, F32) ** (-jnp.arange(N_ROPE_FREQ, dtype=F32) / N_ROPE_FREQ)
    ang_r = jnp.asarray(row)[:, None] * inv
    ang_c = jnp.asarray(col)[:, None] * inv
    def group(ang):
        return jnp.concatenate([ang, ang], axis=-1)
    ang = jnp.concatenate([group(ang_r), group(ang_c), group(ang_r), group(ang_c)], axis=-1)
    sign = np.where(np.arange(HD) % 32 < 16, -1.0, 1.0).astype(np.float32)
    return jnp.cos(ang), jnp.sin(ang) * sign


def _index_table():
    t = np.arange(N_CTX)
    tm = np.zeros((N_CTX, 128), np.float32)
    tm[:, IDX_LANE] = t // 64
    tm[:, IDX_LANE + 1] = t % 64
    return jnp.asarray(tm, BF16)


def kernel(x_prompt, x_sample, c, cache_attn_k, cache_attn_v, state_ret_fwd, state_ret_bwd, c_ctx, w_ada, b_ada, norm1_w, norm2_w, w_in, attn_lambda, attn_subln_w, conv_w, ret_decay_fwd, ret_decay_bwd, ret_norm_w, w_br_attn, w_br_conv, w_br_ret, w_out, w_router, w_exp_gate, w_exp_up, w_exp_down, final_norm_w):
    x = (x_prompt.reshape(N_CTX, D), x_sample.reshape(N_LAT, D))
    cvec = jnp.concatenate([c_ctx[None, :], c, jnp.zeros((3, D), F32)], axis=0)
    mod = _modulation(cvec, w_ada, b_ada).reshape(DEPTH * 8, 1, N_MOD * D)

    w_in_bf = w_in[0].astype(BF16)
    w_br_attn_bf = w_br_attn.astype(BF16)
    w_br_conv_bf = w_br_conv.astype(BF16)
    w_br_ret_bf = w_br_ret.astype(BF16)
    w_out_bf = w_out.astype(BF16)
    w_router_pad = jnp.pad(w_router, ((0, 0), (0, 0), (0, 128 - N_EXPERTS))).astype(BF16)
    norm1 = norm1_w.reshape(DEPTH, 1, D)
    norm2 = norm2_w.reshape(DEPTH, 1, D)
    subln = attn_subln_w.reshape(DEPTH, 1, HD)
    decay_f = ret_decay_fwd.reshape(DEPTH * HEADS, 1, 1)
    decay_b = ret_decay_bwd.reshape(DEPTH * HEADS, 1, 1)
    ret_nw = ret_norm_w.reshape(DEPTH * HEADS, 1, HD)
    cos, sin = _rope_tables()
    tmat = _index_table()

    new_k = new_v = new_sf = new_sb = None
    for l in range(DEPTH):
        lam_init = 0.8 - 0.6 * math.exp(-0.3 * l)
        zb, zq, new_k, new_v = _inproj(l, x, mod, norm1, w_in_bf, new_k, new_v)
        ao, ro, new_sf, new_sb = _mix_ctx(l, lam_init, zb, attn_lambda, subln, decay_f, decay_b, ret_nw,
                                          new_sf, new_sb)
        ao_lat, ro_lat = _mix_lat(l, lam_init, zb, zq, cache_attn_k, cache_attn_v, cos, sin, attn_lambda, subln,
                                  state_ret_fwd, state_ret_bwd, decay_f, decay_b, ret_nw)
        x, h2s, aff_n = _merge(l, x, ao, ao_lat, ro, ro_lat, zb, mod, conv_w, w_br_attn_bf, w_br_conv_bf,
                               w_br_ret_bf, w_out_bf, norm2, w_router_pad)
        idx, gates = _topk(aff_n, tmat)
        idx_flat = idx.reshape(2 * N_EXPERTS * CAP)
        ye, w_in_bf = _expert_ffn(l, idx_flat, h2s, gates, w_exp_gate, w_exp_up, w_exp_down, w_in)
        x = _combine(l, idx_flat, ye, x, mod, final_norm_w)

    y_prompt, y_sample = x
    return (y_prompt.reshape(N_CTX_SEQ, T_CTX, D), y_sample.reshape(N_LAT_SEQ, T_LAT, D), new_k, new_v,
            new_sf, new_sb)
```

```python
import functools
import math

import jax
import jax.numpy as jnp
import numpy as np
from jax import lax
from jax.experimental import pallas as pl
from jax.experimental.pallas import tpu as pltpu

F32 = jnp.float32
BF16 = jnp.bfloat16

D = 1024
DEPTH = 2
N_CTX_SEQ = 16
T_CTX = 256
N_LAT_SEQ = 4
T_LAT = 1024
PAST = 256
N_CTX = N_CTX_SEQ * T_CTX
N_LAT = N_LAT_SEQ * T_LAT
N_TOK = N_CTX + N_LAT
TM = 256
N_TILES = N_TOK // TM
CTX_TILES = N_CTX // TM
LAT_TILES_PER_SEQ = T_LAT // TM
HEADS = 4
HD = 128
GRID_W = 64
N_ROPE_FREQ = 16
ROPE_BASE = 10000.0
IN_COLS = 8192
N_MOD = 6
N_EXPERTS = 16
CAP = 512
FF = 2048
TF = 512
EPS = 1e-6
RET_SCALE = HD ** -0.5
ATTN_SCALE = 64 ** -0.5
LOG2_E = math.log2(math.e)
SLAB = D // 128
VMEM_LIMIT = 56 * 1024 * 1024

QB, KB, VB = 0, 4, 8
RQB, RKB, RVB, RGB = 24, 28, 32, 36
CONV_B_COL, CONV_U_COL = 3, 4
MERGE_GATE_COL = 5
MERGE_SUB = 2


def _sigmoid(x):
    return 0.5 * jnp.tanh(0.5 * x) + 0.5


def _log_sigmoid(x):
    return jnp.minimum(x, 0.0) - jnp.log(1.0 + jnp.exp(-jnp.abs(x)))


def _rms(x):
    return x * lax.rsqrt(jnp.mean(x * x, axis=-1, keepdims=True) + EPS)


def _dot(a, b):
    return jnp.dot(a, b, preferred_element_type=F32)


def _dot_nt(a, b):
    return lax.dot_general(a, b, (((1,), (1,)), ((), ())), preferred_element_type=F32)


def _dot_tn(a, b):
    return lax.dot_general(a, b, (((0,), (0,)), ((), ())), preferred_element_type=F32)


def _to_slabs(ref, x):
    n = x.shape[0]
    for s in range(SLAB):
        ref[pl.ds(s, n, stride=SLAB), :] = x[:, s * 128:(s + 1) * 128]


def _from_slabs(ref):
    n = ref.shape[0] // SLAB
    return jnp.concatenate([ref[pl.ds(s, n, stride=SLAB), :] for s in range(SLAB)], axis=-1)


def _mod_row(i):
    return jnp.where(i < CTX_TILES, 0, 1 + (i - CTX_TILES) // LAT_TILES_PER_SEQ)


def _params(n_axes):
    return pltpu.CompilerParams(
        dimension_semantics=("arbitrary",) * n_axes, vmem_limit_bytes=VMEM_LIMIT)


def _mod_kernel(c_ref, w_ref, b_ref, o_ref):
    c = c_ref[...]
    s = (c * _sigmoid(c)).astype(BF16)
    o_ref[...] = _dot(s, w_ref[...].astype(BF16)) + b_ref[...]


def _modulation(cvec, w_ada, b_ada):
    tn = 1024
    return pl.pallas_call(
        _mod_kernel,
        grid=(DEPTH, N_MOD * D // tn),
        in_specs=[
            pl.BlockSpec((8, D), lambda l, n: (0, 0)),
            pl.BlockSpec((None, D, tn), lambda l, n: (l, 0, n)),
            pl.BlockSpec((None, 1, tn), lambda l, n: (l, 0, n)),
        ],
        out_specs=pl.BlockSpec((None, 8, tn), lambda l, n: (l, 0, n)),
        out_shape=jax.ShapeDtypeStruct((DEPTH, 8, N_MOD * D), F32),
        compiler_params=_params(2),
        name="modulation",
    )(cvec, w_ada, b_ada.reshape(DEPTH, 1, N_MOD * D))


W_CHUNK = 1024


def _inproj_kernel(x_pair, n_prev, f32_w_layer, *refs):
    refs = list(refs)
    x_ref = refs.pop(0)
    if x_pair:
        xs_ref = refs.pop(0)
    mod_ref, n1_ref, w_ref = refs.pop(0), refs.pop(0), refs.pop(0)
    if n_prev:
        kp_ref, vp_ref = refs.pop(0), refs.pop(0)
    i = pl.program_id(0)
    if f32_w_layer is None:
        zb_ref, zq_ref, kn_ref, vn_ref = refs
    else:
        zb_ref, zq_ref, kn_ref, vn_ref, w_bf, stage, sems = refs

        def chunk_copy(c):
            return pltpu.make_async_copy(w_ref.at[f32_w_layer, :, pl.ds(c * W_CHUNK, W_CHUNK)],
                                         stage.at[c % 2], sems.at[c % 2])

        @pl.when(i == 0)
        def _():
            n_chunks = IN_COLS // W_CHUNK
            chunk_copy(0).start()
            for c in range(n_chunks):
                if c + 1 < n_chunks:
                    chunk_copy(c + 1).start()
                chunk_copy(c).wait()
                w_bf[:, c * W_CHUNK:(c + 1) * W_CHUNK] = stage[c % 2].astype(BF16)

        w_ref = w_bf

    def normalize():
        x = x_ref[...]
        if x_pair:
            x = jnp.where(i < N_TILES - CTX_TILES, xs_ref[...], x)
        m = mod_ref[0]
        shift1 = m[:, 0:D]
        scale1 = m[:, D:2 * D]
        return ((_rms(x) * n1_ref[...]) * (1.0 + scale1) + shift1).astype(BF16)

    def project(h):
        cw = 1024
        for c in range(IN_COLS // cw):
            z = _dot(h, w_ref[:, c * cw:(c + 1) * cw])
            if c == CONV_U_COL // 2:
                z = jnp.concatenate([z[:, 0:512] * z[:, 512:cw], z[:, 512:cw]], axis=-1)
            elif c == RGB // 8:
                g = z[:, 512:cw]
                z = jnp.concatenate([z[:, 0:512], g * _sigmoid(g)], axis=-1)
            elif c >= MERGE_GATE_COL:
                z = _sigmoid(z)
            if c == 0:
                zb_ref[:, 0:cw] = jnp.concatenate([z[:, 0:512] * (ATTN_SCALE * LOG2_E), z[:, 512:cw]],
                                                  axis=-1).astype(BF16)
            else:
                zb_ref[:, c * cw:(c + 1) * cw] = z.astype(BF16)
            if c == 0:
                zq_ref[...] = z
                for hh in range(HEADS):
                    kn_ref[n_prev, hh] = z[:, 512 + hh * HD:512 + (hh + 1) * HD]
            if c == 1:
                for hh in range(HEADS):
                    vn_ref[n_prev, hh] = z[:, hh * HD:(hh + 1) * HD]
        if n_prev:
            kn_ref[0:n_prev] = kp_ref[...]
            vn_ref[0:n_prev] = vp_ref[...]

    project(normalize())


def _inproj(l, x, mod, norm1_w, w, k_prev, v_prev):
    t = lambda i: (i + CTX_TILES) % N_TILES
    n_lat = N_TILES - CTX_TILES
    norm_tile = pl.BlockSpec((TM, D), lambda i: (t(i), 0))
    ctx_i = lambda i: jnp.maximum(i - n_lat, 0)
    kv_spec = lambda m: pl.BlockSpec((None, m, HEADS, T_CTX, HD), lambda i: (ctx_i(i), 0, 0, 0, 0))
    kv_shape = jax.ShapeDtypeStruct((N_CTX_SEQ, l + 1, HEADS, T_CTX, HD), F32)
    x_pair = isinstance(x, tuple)
    if x_pair:
        in_specs = [pl.BlockSpec((TM, D), lambda i: (jnp.maximum(i - n_lat, 0), 0)),
                    pl.BlockSpec((TM, D), lambda i: (jnp.minimum(i, n_lat - 1), 0))]
        args = list(x)
    else:
        in_specs = [norm_tile]
        args = [x]
    f32_w = w.dtype == F32
    in_specs += [
        pl.BlockSpec((1, 1, N_MOD * D), lambda i: (l * 8 + _mod_row(t(i)), 0, 0)),
        pl.BlockSpec((None, 1, D), lambda i: (l, 0, 0)),
        pl.BlockSpec(memory_space=pl.ANY) if f32_w else
        pl.BlockSpec((D, IN_COLS), lambda i: (0, 0), pipeline_mode=pl.Buffered(1)),
    ]
    args += [mod, norm1_w, w]
    scratch = [pltpu.VMEM((D, IN_COLS), BF16), pltpu.VMEM((2, D, W_CHUNK), F32),
               pltpu.SemaphoreType.DMA((2,))] if f32_w else []
    if l:
        in_specs += [kv_spec(l), kv_spec(l)]
        args += [k_prev, v_prev]
    out_specs = [pl.BlockSpec((TM, IN_COLS), lambda i: (t(i), 0)),
                 pl.BlockSpec((TM, 1024), lambda i: (t(i), 0)),
                 kv_spec(l + 1), kv_spec(l + 1)]
    out_shape = [jax.ShapeDtypeStruct((N_TOK, IN_COLS), BF16),
                 jax.ShapeDtypeStruct((N_TOK, 1024), F32),
                 kv_shape, kv_shape]
    return pl.pallas_call(
        functools.partial(_inproj_kernel, x_pair, l, l if f32_w else None),
        grid=(N_TILES,),
        in_specs=in_specs,
        out_specs=out_specs,
        out_shape=out_shape,
        scratch_shapes=scratch,
        compiler_params=_params(1),
        name="inproj",
    )(*args)


def _lambda(lam_ref, lam_init):
    lv = lam_ref[...]
    a = jnp.sum(lv[0:1] * lv[1:2], axis=-1, keepdims=True)
    b = jnp.sum(lv[2:3] * lv[3:4], axis=-1, keepdims=True)
    return jnp.exp(a) - jnp.exp(b) + lam_init


def _interleave(*stage_generators):
    live = list(stage_generators)
    while live:
        for g in list(live):
            try:
                next(g)
            except StopIteration:
                live.remove(g)


def _diff_attention(lam_init, qs, keys, vs, lam, sw, keys_transposed, o_ref, sls):
    lane = lax.broadcasted_iota(jnp.int32, qs[0].shape, 1)
    score = _dot if keys_transposed else _dot_nt
    maps = [(h, jnp.where(keep, q, jnp.zeros_like(q)))
            for h, q in enumerate(qs) for keep in (lane < 64, lane >= 64)]
    s = [score(qm, keys[h]) for h, qm in maps]
    yield
    e = [jnp.exp2(x - jnp.max(x, axis=-1, keepdims=True)) for x in s]
    yield
    r = [1.0 / jnp.sum(x, axis=-1, keepdims=True) for x in e]
    a = [(e[2 * h] * r[2 * h] - e[2 * h + 1] * (lam * r[2 * h + 1])).astype(BF16) for h in range(len(qs))]
    yield
    o = [_dot(a[h], vs[h]) for h in range(len(qs))]
    yield
    for sl, x in zip(sls, o):
        o_ref[:, sl] = ((_rms(x) * sw) * (1.0 - lam_init)).astype(BF16)


def _scaled_q(q):
    return (q * (ATTN_SCALE * LOG2_E)).astype(BF16)


def _attn_ctx_stages(lam_init, q_ref, k_ref, v_ref, lam_ref, sw_ref, o_ref):
    lam = _lambda(lam_ref, lam_init)
    sls = [slice(h * HD, (h + 1) * HD) for h in range(HEADS)]
    yield from _diff_attention(lam_init, [q_ref[:, sl] for sl in sls], [k_ref[:, sl] for sl in sls],
                               [v_ref[:, sl] for sl in sls], lam, sw_ref[...], False, o_ref, sls)


def _rope(x, cos, sin_signed):
    lane = lax.broadcasted_iota(jnp.int32, x.shape, 1)
    partner = jnp.where(lane % 32 < 16, pltpu.roll(x, 112, 1), pltpu.roll(x, 16, 1))
    return x * cos + partner * sin_signed


def _attn_lat_stages(lam_init, q_ref, k_ref, v_ref, ck_ref, cv_ref, cos_ref, sin_ref,
                     cosq_ref, sinq_ref, lam_ref, sw_ref, o_ref, k_t, vall):
    @pl.when(pl.program_id(1) == 0)
    def _():
        for h in range(HEADS):
            sl = slice(h * HD, (h + 1) * HD)
            k_t[h, :, 0:PAST] = ck_ref[h].T.astype(BF16)
            k_t[h, :, PAST:] = _rope(k_ref[:, sl], cos_ref[...], sin_ref[...]).T.astype(BF16)
            vall[h, 0:PAST, :] = cv_ref[h].astype(BF16)
            vall[h, PAST:, :] = v_ref[:, sl]

    lam = _lambda(lam_ref, lam_init)
    sls = [slice(h * HD, (h + 1) * HD) for h in range(HEADS)]
    qs = [_scaled_q(_rope(q_ref[:, sl], cosq_ref[...], sinq_ref[...])) for sl in sls]
    yield from _diff_attention(lam_init, qs, [k_t[h] for h in range(HEADS)], [vall[h] for h in range(HEADS)],
                               lam, sw_ref[...], True, o_ref, sls)


def _mix_ctx(l, lam_init, zb, attn_lambda, subln_w, decay_f, decay_b, ret_norm_w, sf_prev, sb_prev):
    width = HEADS * HD
    col = lambda c: pl.BlockSpec((T_CTX, width), lambda b: (b, c // HEADS))
    dec = pl.BlockSpec((HEADS, 1, 1), lambda b: (l, 0, 0))
    st_spec = lambda n: pl.BlockSpec((None, n, HEADS, HD, HD), lambda b: (b, 0, 0, 0, 0))
    st_shape = jax.ShapeDtypeStruct((N_CTX_SEQ, l + 1, HEADS, HD, HD), F32)
    out_tile = pl.BlockSpec((T_CTX, width), lambda b: (b, 0))
    out_shape = jax.ShapeDtypeStruct((N_CTX, width), BF16)
    in_specs = [
        col(QB), col(KB), col(VB),
        pl.BlockSpec((None, 4, 64), lambda b: (l, 0, 0)),
        pl.BlockSpec((None, 1, HD), lambda b: (l, 0, 0)),
        col(RQB), col(RKB), col(RVB), col(RGB),
        dec, dec, pl.BlockSpec((HEADS, 1, HD), lambda b: (l, 0, 0)),
    ]
    args = [zb, zb, zb, attn_lambda, subln_w, zb, zb, zb, zb, decay_f, decay_b, ret_norm_w]
    if l:
        in_specs += [st_spec(l), st_spec(l)]
        args += [sf_prev, sb_prev]
    return pl.pallas_call(
        functools.partial(_mix_ctx_kernel, lam_init, l),
        grid=(N_CTX_SEQ,),
        in_specs=in_specs,
        out_specs=[out_tile, out_tile, st_spec(l + 1), st_spec(l + 1)],
        out_shape=[out_shape, out_shape, st_shape, st_shape],
        scratch_shapes=[pltpu.VMEM((HEADS, T_CTX, T_CTX), F32)],
        compiler_params=_params(1),
        name="mix_ctx",
    )(*args)


def _decay_matrix(lgf, lgb, row0, tq, tk):
    i = row0 + lax.broadcasted_iota(jnp.int32, (tq, tk), 0)
    j = lax.broadcasted_iota(jnp.int32, (tq, tk), 1)
    d = (i - j).astype(F32)
    if row0 - (tk - 1) > 0:
        return jnp.exp(lgf * d)
    if row0 + (tq - 1) < 0:
        return jnp.exp(lgb * (-d))
    fwd = jnp.where(d >= 0.0, jnp.exp(lgf * jnp.maximum(d, 0.0)), 0.0)
    bwd = jnp.where(d <= 0.0, jnp.exp(lgb * jnp.maximum(-d, 0.0)), 0.0)
    return fwd + bwd


def _ret_finish(o, g, nw):
    return (g.astype(F32) * (_rms(o) * nw)).astype(BF16)


def _ret_ctx_stages(n_prev, q_ref, k_ref, v_ref, g_ref, df_ref, db_ref, nw_ref, o_ref, sf_ref, sb_ref, dmat):
    j = lax.broadcasted_iota(jnp.int32, (T_CTX, 1), 0).astype(F32)
    heads = range(HEADS)
    sls = [slice(h * HD, (h + 1) * HD) for h in heads]
    q = [q_ref[:, sl] for sl in sls]
    k = [k_ref[:, sl] for sl in sls]
    v = [v_ref[:, sl] for sl in sls]
    s = [(_dot_nt(q[h], k[h]) * dmat[h]).astype(BF16) for h in heads]
    yield
    o = [_dot(s[h], v[h]) for h in heads]
    yield
    lgf = [_log_sigmoid(df_ref[h]) for h in heads]
    lgb = [_log_sigmoid(db_ref[h]) for h in heads]
    kf = [k[h].astype(F32) * RET_SCALE for h in heads]
    kfw = [(kf[h] * jnp.exp(lgf[h] * (T_CTX - 1.0 - j))).astype(BF16) for h in heads]
    kbw = [(kf[h] * jnp.exp(lgb[h] * j)).astype(BF16) for h in heads]
    yield
    for h in heads:
        sf_ref[n_prev, h] = _dot_tn(kfw[h], v[h])
        sb_ref[n_prev, h] = _dot_tn(kbw[h], v[h])
    yield
    for h in heads:
        o_ref[:, sls[h]] = _ret_finish(o[h], g_ref[:, sls[h]], nw_ref[h])


def _mix_ctx_kernel(lam_init, n_prev, *refs):
    refs = list(refs)
    aq_ref, ak_ref, av_ref, lam_ref, sw_ref = refs[0:5]
    rq_ref, rk_ref, rv_ref, rg_ref, df_ref, db_ref, nw_ref = refs[5:12]
    refs = refs[12:]
    if n_prev:
        sfp_ref, sbp_ref = refs.pop(0), refs.pop(0)
    ao_ref, ro_ref, sf_ref, sb_ref, dmat = refs

    @pl.when(pl.program_id(0) == 0)
    def _():
        for h in range(HEADS):
            dmat[h] = RET_SCALE * _decay_matrix(_log_sigmoid(df_ref[h]), _log_sigmoid(db_ref[h]), 0, T_CTX, T_CTX)

    if n_prev:
        sf_ref[0:n_prev] = sfp_ref[...]
        sb_ref[0:n_prev] = sbp_ref[...]
    _interleave(
        _attn_ctx_stages(lam_init, aq_ref, ak_ref, av_ref, lam_ref, sw_ref, ao_ref),
        _ret_ctx_stages(n_prev, rq_ref, rk_ref, rv_ref, rg_ref, df_ref, db_ref, nw_ref, ro_ref, sf_ref, sb_ref, dmat))


N_DCHUNK = 2 * LAT_TILES_PER_SEQ - 1


def _ret_lat_stages(q_ref, k_ref, v_ref, g_ref, s0f_ref, s0b_ref, df_ref, db_ref, nw_ref, o_ref, strip, k_t):
    b = pl.program_id(0)
    j = pl.program_id(1)

    @pl.when(jnp.logical_and(b == 0, j == 0))
    def _():
        for h in range(HEADS):
            lgf = _log_sigmoid(df_ref[h])
            lgb = _log_sigmoid(db_ref[h])
            for c in range(N_DCHUNK):
                strip[h, c] = RET_SCALE * _decay_matrix(lgf, lgb, T_LAT - TM - c * TM, TM, TM)

    @pl.when(j == 0)
    def _():
        for h in range(HEADS):
            k_t[h] = k_ref[:, h * HD:(h + 1) * HD].astype(F32).T.astype(BF16)

    i = (j * TM + lax.broadcasted_iota(jnp.int32, (TM, 1), 0)).astype(F32)
    c0 = LAT_TILES_PER_SEQ - 1 - j
    heads = range(HEADS)
    sls = [slice(h * HD, (h + 1) * HD) for h in heads]
    q = [q_ref[:, sl] for sl in sls]
    dmat = [jnp.concatenate([strip[h, c0 + c] for c in range(LAT_TILES_PER_SEQ)], axis=-1) for h in heads]
    s = [(_dot(q[h], k_t[h]) * dmat[h]).astype(BF16) for h in heads]
    yield
    o = [_dot(s[h], v_ref[:, sls[h]]) for h in heads]
    yield
    of = [_dot(q[h], s0f_ref[h].astype(BF16)) for h in heads]
    ob = [_dot(q[h], s0b_ref[h].astype(BF16)) for h in heads]
    yield
    for h in heads:
        tot = o[h] + jnp.exp(_log_sigmoid(df_ref[h]) * (i + 1.0)) * of[h]
        tot = tot + jnp.exp(_log_sigmoid(db_ref[h]) * (T_LAT - i)) * ob[h]
        o_ref[:, sls[h]] = _ret_finish(tot, g_ref[:, sls[h]], nw_ref[h])


def _mix_lat_kernel(lam_init, *refs):
    attn_in, ret_in = refs[0:11], refs[11:20]
    ao_ref, ro_ref, ak_t, vall, strip, rk_t = refs[20:]
    _interleave(_attn_lat_stages(lam_init, *attn_in, ao_ref, ak_t, vall))
    _interleave(_ret_lat_stages(*ret_in, ro_ref, strip, rk_t))


def _mix_lat(l, lam_init, zb, zq, cache_k, cache_v, cos, sin, attn_lambda, subln_w,
             state_f, state_b, decay_f, decay_b, ret_norm_w):
    width = HEADS * HD
    lat_row = lambda b, j: CTX_TILES + LAT_TILES_PER_SEQ * b + j
    seq_row = lambda b: N_CTX // T_LAT + b
    q_tile = lambda c: pl.BlockSpec((TM, width), lambda b, j: (lat_row(b, j), c // HEADS))
    seq = lambda c: pl.BlockSpec((T_LAT, width), lambda b, j: (seq_row(b), c // HEADS))
    cache = pl.BlockSpec((None, None, HEADS, PAST, HD), lambda b, j: (b, l, 0, 0, 0))
    state = pl.BlockSpec((None, None, HEADS, HD, HD), lambda b, j: (b, l, 0, 0, 0))
    dec = pl.BlockSpec((HEADS, 1, 1), lambda b, j: (l, 0, 0))
    out_tile = pl.BlockSpec((TM, width), lambda b, j: (LAT_TILES_PER_SEQ * b + j, 0))
    out_shape = jax.ShapeDtypeStruct((N_LAT, width), BF16)
    return pl.pallas_call(
        functools.partial(_mix_lat_kernel, lam_init),
        grid=(N_LAT_SEQ, LAT_TILES_PER_SEQ),
        in_specs=[
            q_tile(QB), seq(KB), seq(VB), cache, cache,
            pl.BlockSpec((T_LAT, HD), lambda b, j: (0, 0)),
            pl.BlockSpec((T_LAT, HD), lambda b, j: (0, 0)),
            pl.BlockSpec((TM, HD), lambda b, j: (j, 0)),
            pl.BlockSpec((TM, HD), lambda b, j: (j, 0)),
            pl.BlockSpec((None, 4, 64), lambda b, j: (l, 0, 0)),
            pl.BlockSpec((None, 1, HD), lambda b, j: (l, 0, 0)),
            q_tile(RQB), seq(RKB), seq(RVB), q_tile(RGB), state, state, dec, dec,
            pl.BlockSpec((HEADS, 1, HD), lambda b, j: (l, 0, 0)),
        ],
        out_specs=[out_tile, out_tile],
        out_shape=[out_shape, out_shape],
        scratch_shapes=[pltpu.VMEM((HEADS, HD, PAST + T_LAT), BF16),
                        pltpu.VMEM((HEADS, PAST + T_LAT, HD), BF16),
                        pltpu.VMEM((HEADS, N_DCHUNK, TM, TM), F32),
                        pltpu.VMEM((HEADS, HD, T_LAT), BF16)],
        compiler_params=_params(2),
        name="mix_lat",
    )(zq, zq, zb, cache_k, cache_v, cos, sin, cos, sin, attn_lambda, subln_w,
      zb, zb, zb, zb, state_f, state_b, decay_f, decay_b, ret_norm_w)


def _merge_kernel(x_pair, *refs):
    refs = list(refs)
    x_ref = refs.pop(0)
    xs_ref = refs.pop(0) if x_pair else None
    (aoc_ref, aol_ref, roc_ref, rol_ref, cb_ref, u_ref, up_ref, un_ref,
     mg0_ref, mg1_ref, mg2_ref, mod_ref, cw_ref, wa_ref, wc_ref, wr_ref, wo_ref, n2_ref, wrt_ref,
     x1_ref, h2_ref, aff_ref) = refs
    i = pl.program_id(0)
    is_ctx = i < CTX_TILES // MERGE_SUB
    m = mod_ref[0]
    gate1 = m[:, 2 * D:3 * D]
    shift2 = m[:, 3 * D:4 * D]
    scale2 = m[:, 4 * D:5 * D]
    cw = cw_ref[...]
    u_all = u_ref[...].astype(F32)
    r = lax.broadcasted_iota(jnp.int32, (TM, 512), 0)
    lane = lax.broadcasted_iota(jnp.int32, (TM, 128), 1)

    subs = range(MERGE_SUB)
    rows = [slice(sub * TM, (sub + 1) * TM) for sub in subs]

    def conv_out(sub):
        j = (i * MERGE_SUB + sub - CTX_TILES) % LAT_TILES_PER_SEQ
        seq_first = jnp.logical_or(is_ctx, j == 0)
        seq_last = jnp.logical_or(is_ctx, j == LAT_TILES_PER_SEQ - 1)
        u = u_all[rows[sub], :]
        up = up_ref[...].astype(F32)[15:16, :] if sub == 0 else u_all[sub * TM - 1:sub * TM, :]
        dn = un_ref[...].astype(F32)[0:1, :] if sub == MERGE_SUB - 1 else u_all[(sub + 1) * TM:(sub + 1) * TM + 1, :]
        up = up * jnp.where(seq_first, 0.0, 1.0)
        dn = dn * jnp.where(seq_last, 0.0, 1.0)
        u_prev = jnp.where(r == 0, up, pltpu.roll(u, 1, 0))
        u_next = jnp.where(r == TM - 1, dn, pltpu.roll(u, TM - 1, 0))
        conv = u_prev * cw[0:1, :] + u * cw[1:2, :] + u_next * cw[2:3, :]
        return (cb_ref[rows[sub], :].astype(F32) * conv).astype(BF16)

    ao = [jnp.where(is_ctx, aoc_ref[rw, :], aol_ref[rw, :]) for rw in rows]
    ro = [jnp.where(is_ctx, roc_ref[rw, :], rol_ref[rw, :]) for rw in rows]
    conv_o = [conv_out(sub) for sub in subs]
    b_attn = [_dot(ao[s], wa_ref[...]) for s in subs]
    b_conv = [_dot(conv_o[s], wc_ref[...]) for s in subs]
    b_ret = [_dot(ro[s], wr_ref[...]) for s in subs]
    merged = [(mg0_ref[rows[s], :].astype(F32) * b_attn[s] + mg1_ref[rows[s], :].astype(F32) * b_conv[s]
               + mg2_ref[rows[s], :].astype(F32) * b_ret[s]).astype(BF16) for s in subs]
    proj = [_dot(merged[s], wo_ref[...]) for s in subs]
    x1 = []
    for s in subs:
        x_in = x_ref[rows[s], :]
        if x_pair:
            x_in = jnp.where(is_ctx, x_in, xs_ref[rows[s], :])
        x1.append(x_in + gate1 * proj[s])
        x1_ref[rows[s], :] = x1[s]
    h2 = [(_rms(x1[s]) * n2_ref[...]) * (1.0 + scale2) + shift2 for s in subs]
    logits = [_dot(h2[s].astype(BF16), wrt_ref[...]) for s in subs]
    valid = lane < N_EXPERTS
    for s in subs:
        _to_slabs(h2_ref.at[pl.ds(s * TM * SLAB, TM * SLAB), :], h2[s])
        lmax = jnp.max(jnp.where(valid, logits[s], -jnp.inf), axis=-1, keepdims=True)
        e = jnp.where(valid, jnp.exp(logits[s] - lmax), 0.0)
        aff_ref[rows[s], :] = e * (1.0 / jnp.sum(e, axis=-1, keepdims=True))


def _merge(l, x, ao, ao_lat, ro, ro_lat, zb, mod, conv_w, w_br_attn, w_br_conv, w_br_ret, w_out, norm2_w,
           w_router_pad):
    mt = MERGE_SUB * TM
    n_steps = N_TOK // mt
    ctx_steps = N_CTX // mt
    n16 = N_TOK // 16
    ctx_br = pl.BlockSpec((mt, 512), lambda i: (jnp.minimum(i, ctx_steps - 1), 0))
    lat_br = pl.BlockSpec((mt, 512), lambda i: (jnp.maximum(i - ctx_steps, 0), 0))
    col = lambda c: pl.BlockSpec((mt, 512), lambda i: (i, c))
    halo_p = lambda c: pl.BlockSpec((16, 512), lambda i: (jnp.maximum(i * (mt // 16) - 1, 0), c))
    halo_n = lambda c: pl.BlockSpec((16, 512), lambda i: (jnp.minimum((i + 1) * (mt // 16), n16 - 1), c))
    mgs = lambda c: pl.BlockSpec((mt, D), lambda i: (i, c))
    wbr = pl.BlockSpec((None, 512, D), lambda i: (l, 0, 0))
    x_pair = isinstance(x, tuple)
    if x_pair:
        x_specs = [pl.BlockSpec((mt, D), lambda i: (jnp.minimum(i, ctx_steps - 1), 0)),
                   pl.BlockSpec((mt, D), lambda i: (jnp.maximum(i - ctx_steps, 0), 0))]
        x_args = list(x)
    else:
        x_specs = [pl.BlockSpec((mt, D), lambda i: (i, 0))]
        x_args = [x]
    return pl.pallas_call(
        functools.partial(_merge_kernel, x_pair),
        grid=(n_steps,),
        in_specs=x_specs + [
            ctx_br, lat_br, ctx_br, lat_br,
            col(CONV_B_COL), col(CONV_U_COL), halo_p(CONV_U_COL), halo_n(CONV_U_COL),
            mgs(MERGE_GATE_COL), mgs(MERGE_GATE_COL + 1), mgs(MERGE_GATE_COL + 2),
            pl.BlockSpec((1, 1, N_MOD * D), lambda i: (l * 8 + _mod_row(i * MERGE_SUB), 0, 0)),
            pl.BlockSpec((None, 3, 512), lambda i: (l, 0, 0)),
            wbr, wbr, wbr,
            pl.BlockSpec((None, D, D), lambda i: (l, 0, 0)),
            pl.BlockSpec((None, 1, D), lambda i: (l, 0, 0)),
            pl.BlockSpec((None, D, 128), lambda i: (l, 0, 0)),
        ],
        out_specs=[
            pl.BlockSpec((mt, D), lambda i: (i, 0)),
            pl.BlockSpec((mt * SLAB, 128), lambda i: (i, 0)),
            pl.BlockSpec((mt, 128), lambda i: (i, 0)),
        ],
        out_shape=[
            jax.ShapeDtypeStruct((N_TOK, D), F32),
            jax.ShapeDtypeStruct((N_TOK * SLAB, 128), F32),
            jax.ShapeDtypeStruct((N_TOK, 128), F32),
        ],
        compiler_params=_params(1),
        name="merge",
    )(*x_args, ao, ao_lat, ro, ro_lat, zb, zb, zb, zb, zb, zb, zb, mod, conv_w,
      w_br_attn, w_br_conv, w_br_ret, w_out, norm2_w, w_router_pad)


def _cumsum_lanes(x, tri):
    run = jnp.zeros((x.shape[0], 1), F32)
    outs = []
    for b in range(x.shape[1] // 128):
        cs = _dot(x[:, b * 128:(b + 1) * 128].astype(BF16), tri) + run
        run = cs[:, 127:128]
        outs.append(cs)
    return jnp.concatenate(outs, axis=-1)


GATE_LANE = (0, 16, 32)
IDX_LANE = 48
BISECT_GROUP = 4


def _topk_kernel(an_ref, tmat_ref, idx_ref, gate_ref):
    an = an_ref[...]
    a = an.T[0:N_EXPERTS, :]
    n = a.shape[1]
    kf = float(CAP)

    hi = an.astype(BF16).astype(F32)
    mid = (an - hi).astype(BF16).astype(F32)
    lo = ((an - hi) - mid).astype(BF16).astype(F32)
    table = (tmat_ref[...].astype(F32) + hi + pltpu.roll(mid, GATE_LANE[1], 1)
             + pltpu.roll(lo, GATE_LANE[2], 1)).astype(BF16)

    def count_gt(thr):
        return jnp.sum(jnp.where(a > thr, 1.0, 0.0), axis=-1, keepdims=True)

    def span(lo, hi):
        inside = jnp.logical_and(a > lo, a <= hi)
        cmax = jnp.max(jnp.where(inside, a, -jnp.inf), axis=-1, keepdims=True)
        cmin = jnp.min(jnp.where(inside, a, jnp.inf), axis=-1, keepdims=True)
        return cmax, cmin

    def cond(c):
        return jnp.logical_and(c[2] > 0, c[3] < 400)

    def body(c):
        lo, hi, _, it = c
        for _ in range(BISECT_GROUP):
            mid = 0.5 * (lo + hi)
            ge = count_gt(mid) >= kf
            lo = jnp.where(ge, mid, lo)
            hi = jnp.where(ge, hi, mid)
        cmax, cmin = span(lo, hi)
        open_rows = jnp.max(jnp.where(cmax != cmin, 1, 0))
        return lo, hi, open_rows, it + 1

    lo0 = jnp.full((N_EXPERTS, 1), -1.0, F32)
    hi0 = jnp.max(a, axis=-1, keepdims=True)
    cmax0, cmin0 = span(lo0, hi0)
    lo, hi, _, _ = lax.while_loop(
        cond, body, (lo0, hi0, jnp.max(jnp.where(cmax0 != cmin0, 1, 0)), jnp.int32(0)))
    thr, _ = span(lo, hi)

    r = lax.broadcasted_iota(jnp.int32, (128, 128), 0)
    c = lax.broadcasted_iota(jnp.int32, (128, 128), 1)
    tri = jnp.where(r <= c, 1.0, 0.0).astype(BF16)
    gt = a > thr
    eq = jnp.where(a == thr, 1.0, 0.0)
    need = kf - count_gt(thr)
    eq_before = _cumsum_lanes(eq, tri) - eq
    sel = jnp.where(jnp.logical_or(gt, jnp.logical_and(eq > 0.0, eq_before < need)), 1.0, 0.0)
    pos = _cumsum_lanes(sel, tri) - 1.0
    slot = jnp.where(sel > 0.0, pos, -1.0).astype(jnp.int32)

    p_iota = lax.broadcasted_iota(jnp.int32, (CAP, 1024), 0)
    lane = lax.broadcasted_iota(jnp.int32, (CAP, 128), 1)
    for e in range(N_EXPERTS):
        acc = jnp.zeros((CAP, 128), F32)
        for cb in range(n // 1024):
            onehot = jnp.where(p_iota == slot[e:e + 1, cb * 1024:(cb + 1) * 1024], 1.0, 0.0).astype(BF16)
            acc = acc + _dot(onehot, table[cb * 1024:(cb + 1) * 1024, :])
        acc_t = acc.T
        idx_ref[0, e] = (acc_t[IDX_LANE:IDX_LANE + 1, :] * 64.0 + acc_t[IDX_LANE + 1:IDX_LANE + 2, :]).astype(jnp.int32)
        g = jnp.zeros((CAP, 1), F32)
        for off in GATE_LANE:
            g = g + jnp.sum(jnp.where(lane == off + e, acc, 0.0), axis=-1, keepdims=True)
        gate_ref[0, e] = g


def _topk(aff_n, tmat):
    out_spec = pl.BlockSpec((1, N_EXPERTS, CAP, 1), lambda s: (s, 0, 0, 0))
    return pl.pallas_call(
        _topk_kernel,
        grid=(2,),
        in_specs=[
            pl.BlockSpec((N_CTX, 128), lambda s: (s, 0)),
            pl.BlockSpec((N_CTX, 128), lambda s: (0, 0)),
        ],
        out_specs=[pl.BlockSpec((1, N_EXPERTS, 1, CAP), lambda s: (s, 0, 0, 0)), out_spec],
        out_shape=[jax.ShapeDtypeStruct((2, N_EXPERTS, 1, CAP), jnp.int32),
                   jax.ShapeDtypeStruct((2, N_EXPERTS, CAP, 1), F32)],
        compiler_params=_params(1),
        name="topk",
    )(aff_n, tmat)


N_FT = FF // TF
GATHER_ROWS = 2 * CAP
ROWS_PER_STEP = GATHER_ROWS // N_FT


def _ffn_kernel(cast_next, idx_ref, h2_hbm, g_ref, wg_ref, wu_ref, wd_ref, *refs):
    if cast_next:
        win_ref, ye_ref, wbf_ref, xg, xb, acc, sems = refs
        wbf_ref[...] = win_ref[...].astype(BF16)
    else:
        ye_ref, xg, xb, acc, sems = refs
    e = pl.program_id(0)
    f = pl.program_id(1)
    slot = e % 2

    def row_copy(expert, s, p, dst_slot):
        row = idx_ref[(s * N_EXPERTS + expert) * CAP + p] + s * N_CTX
        src = h2_hbm.at[pl.ds(pl.multiple_of(row * SLAB, SLAB), SLAB), :]
        dst = xg.at[dst_slot, pl.ds(pl.multiple_of((s * CAP + p) * SLAB, SLAB), SLAB), :]
        return pltpu.make_async_copy(src, dst, sems.at[dst_slot])

    def slot_wait(dst_slot):
        pltpu.make_async_copy(h2_hbm.at[pl.ds(0, GATHER_ROWS * SLAB), :], xg.at[dst_slot],
                              sems.at[dst_slot]).wait()

    @pl.when(jnp.logical_and(e == 0, f == 0))
    def _():
        for r in range(GATHER_ROWS):
            row_copy(0, r // CAP, r % CAP, 0).start()
        acc[...] = jnp.zeros_like(acc)

    @pl.when(f == 0)
    def _():
        slot_wait(slot)
        xb[...] = _from_slabs(xg.at[slot]).astype(BF16)

    nxt = jnp.minimum(e + 1, N_EXPERTS - 1)
    s_nxt = f // (N_FT // 2)
    p0 = (f % (N_FT // 2)) * ROWS_PER_STEP
    for u in range(ROWS_PER_STEP):
        row_copy(nxt, s_nxt, p0 + u, 1 - slot).start()

    x = xb[...]
    hg = _dot(x, wg_ref[...].astype(BF16))
    hu = _dot(x, wu_ref[...].astype(BF16))
    hdn = ((hg * _sigmoid(hg)) * hu).astype(BF16)
    acc[...] = jnp.where(f == 0, 0.0, acc[...]) + _dot(hdn, wd_ref[...].astype(BF16))

    @pl.when(f == N_FT - 1)
    def _():
        for s in range(2):
            _to_slabs(ye_ref.at[s], acc[s * CAP:(s + 1) * CAP, :] * g_ref[s])

    @pl.when(jnp.logical_and(e == N_EXPERTS - 1, f == N_FT - 1))
    def _():
        slot_wait(1 - slot)


def _expert_ffn(l, idx_flat, h2s, gates, w_gate, w_up, w_down, w_in):
    cast_next = l + 1 < DEPTH
    n_steps = N_EXPERTS * N_FT
    in_specs = [
        pl.BlockSpec(memory_space=pl.ANY),
        pl.BlockSpec((2, None, CAP, 1), lambda e, f, idx: (0, e, 0, 0)),
        pl.BlockSpec((None, None, D, TF), lambda e, f, idx: (l, e, 0, f)),
        pl.BlockSpec((None, None, D, TF), lambda e, f, idx: (l, e, 0, f)),
        pl.BlockSpec((None, None, TF, D), lambda e, f, idx: (l, e, f, 0)),
    ]
    args = [idx_flat, h2s, gates, w_gate, w_up, w_down]
    out_specs = [pl.BlockSpec((2, None, CAP * SLAB, 128), lambda e, f, idx: (0, e, 0, 0))]
    out_shape = [jax.ShapeDtypeStruct((2, N_EXPERTS, CAP * SLAB, 128), F32)]
    if cast_next:
        cols = IN_COLS // n_steps
        in_specs.append(pl.BlockSpec((None, D, cols), lambda e, f, idx: (l + 1, 0, e * N_FT + f)))
        args.append(w_in)
        out_specs.append(pl.BlockSpec((D, cols), lambda e, f, idx: (0, e * N_FT + f)))
        out_shape.append(jax.ShapeDtypeStruct((D, IN_COLS), BF16))
    grid_spec = pltpu.PrefetchScalarGridSpec(
        num_scalar_prefetch=1,
        grid=(N_EXPERTS, N_FT),
        in_specs=in_specs,
        out_specs=out_specs,
        scratch_shapes=[
            pltpu.VMEM((2, GATHER_ROWS * SLAB, 128), F32),
            pltpu.VMEM((GATHER_ROWS, D), BF16),
            pltpu.VMEM((GATHER_ROWS, D), F32),
            pltpu.SemaphoreType.DMA((2,)),
        ],
    )
    outs = pl.pallas_call(
        functools.partial(_ffn_kernel, cast_next),
        grid_spec=grid_spec,
        out_shape=out_shape,
        compiler_params=_params(2),
        name="expert_ffn",
    )(*args)
    return (outs[0], outs[1]) if cast_next else (outs[0], None)


SCATTER_UNROLL = 16
SCATTER_EXPERTS = 4


N_SCATTER_STEPS = N_EXPERTS // SCATTER_EXPERTS
RESID_ROWS = 2 * TM
N_RESID_STEPS = N_CTX // RESID_ROWS


def _combine_kernel(l, last, idx_ref, ye_ref, x_ref, mod_ref, *refs):
    if last:
        w_ref, op_ref, os_ref, y_acc = refs
    else:
        o_ref, y_acc = refs
    s = pl.program_id(0)
    g = pl.program_id(1)

    @pl.when(g == 0)
    def _():
        y_acc[...] = jnp.zeros_like(y_acc)

    @pl.when(g < N_SCATTER_STEPS)
    def _():
        for k in range(SCATTER_EXPERTS):
            base = (s * N_EXPERTS + g * SCATTER_EXPERTS + k) * CAP
            for p0 in range(0, CAP, SCATTER_UNROLL):
                rows = [idx_ref[base + p0 + u] for u in range(SCATTER_UNROLL)]
                tiles = [pl.ds(pl.multiple_of(r * SLAB, SLAB), SLAB) for r in rows]
                vals = [y_acc[tiles[u], :] + ye_ref[k, pl.ds((p0 + u) * SLAB, SLAB), :]
                        for u in range(SCATTER_UNROLL)]
                for u in range(SCATTER_UNROLL):
                    y_acc[tiles[u], :] = vals[u]

    @pl.when(g >= N_SCATTER_STEPS)
    def _():
        c = g - N_SCATTER_STEPS
        row = jnp.where(s == 0, 0, 1 + c // (T_LAT // RESID_ROWS))
        gate2 = mod_ref[l * 8 + row][:, 5 * D:6 * D]
        y = _from_slabs(y_acc.at[pl.ds(pl.multiple_of(c * RESID_ROWS * SLAB, RESID_ROWS * SLAB), RESID_ROWS * SLAB), :])
        x = x_ref[...] + gate2 * y
        if last:
            out = _rms(x) * w_ref[...]

            @pl.when(s == 0)
            def _():
                op_ref[...] = out

            @pl.when(s == 1)
            def _():
                os_ref[...] = out
        else:
            o_ref[...] = x


def _combine(l, idx_flat, ye, x1, mod, final_norm_w):
    last = l == DEPTH - 1
    resid = lambda g: jnp.maximum(g - N_SCATTER_STEPS, 0)
    in_specs = [
        pl.BlockSpec((None, SCATTER_EXPERTS, CAP * SLAB, 128),
                     lambda s, g, idx: (s, jnp.minimum(g, N_SCATTER_STEPS - 1), 0, 0)),
        pl.BlockSpec((RESID_ROWS, D), lambda s, g, idx: (s * N_RESID_STEPS + resid(g), 0)),
        pl.BlockSpec((DEPTH * 8, 1, N_MOD * D), lambda s, g, idx: (0, 0, 0)),
    ]
    args = [idx_flat, ye, x1, mod]
    if last:
        in_specs.append(pl.BlockSpec((1, D), lambda s, g, idx: (0, 0)))
        args.append(final_norm_w.reshape(1, D))
        out_specs = [
            pl.BlockSpec((RESID_ROWS, D), lambda s, g, idx: (jnp.where(s == 0, resid(g), N_RESID_STEPS - 1), 0)),
            pl.BlockSpec((RESID_ROWS, D), lambda s, g, idx: (jnp.where(s == 1, resid(g), 0), 0)),
        ]
        out_shape = [jax.ShapeDtypeStruct((N_CTX, D), F32), jax.ShapeDtypeStruct((N_LAT, D), F32)]
    else:
        out_specs = pl.BlockSpec((RESID_ROWS, D), lambda s, g, idx: (s * N_RESID_STEPS + resid(g), 0))
        out_shape = jax.ShapeDtypeStruct((N_TOK, D), F32)
    grid_spec = pltpu.PrefetchScalarGridSpec(
        num_scalar_prefetch=1,
        grid=(2, N_SCATTER_STEPS + N_RESID_STEPS),
        in_specs=in_specs,
        out_specs=out_specs,
        scratch_shapes=[pltpu.VMEM((N_CTX * SLAB, 128), F32)],
    )
    return pl.pallas_call(
        functools.partial(_combine_kernel, l, last),
        grid_spec=grid_spec,
        out_shape=out_shape,
        compiler_params=_params(2),
        name="combine",
    )(*args)


def _rope_tables():
    t = np.arange(T_LAT)
    row = (t // GRID_W).astype(np.float32)
    col = (t % GRID_W).astype(np.float32)
    inv = jnp.asarray(ROPE_BASE, F32) ** (-jnp.arange(N_ROPE_FREQ, dtype=F32) / N_ROPE_FREQ)
    ang_r = jnp.asarray(row)[:, None] * inv
    ang_c = jnp.asarray(col)[:, None] * inv
    def group(ang):
        return jnp.concatenate([ang, ang], axis=-1)
    ang = jnp.concatenate([group(ang_r), group(ang_c), group(ang_r), group(ang_c)], axis=-1)
    sign = np.where(np.arange(HD) % 32 < 16, -1.0, 1.0).astype(np.float32)
    return jnp.cos(ang), jnp.sin(ang) * sign


def _index_table():
    t = np.arange(N_CTX)
    tm = np.zeros((N_CTX, 128), np.float32)
    tm[:, IDX_LANE] = t // 64
    tm[:, IDX_LANE + 1] = t % 64
    return jnp.asarray(tm, BF16)


def kernel(x_prompt, x_sample, c, cache_attn_k, cache_attn_v, state_ret_fwd, state_ret_bwd, c_ctx, w_ada, b_ada, norm1_w, norm2_w, w_in, attn_lambda, attn_subln_w, conv_w, ret_decay_fwd, ret_decay_bwd, ret_norm_w, w_br_attn, w_br_conv, w_br_ret, w_out, w_router, w_exp_gate, w_exp_up, w_exp_down, final_norm_w):
    x = (x_prompt.reshape(N_CTX, D), x_sample.reshape(N_LAT, D))
    cvec = jnp.concatenate([c_ctx[None, :], c, jnp.zeros((3, D), F32)], axis=0)
    mod = _modulation(cvec, w_ada, b_ada).reshape(DEPTH * 8, 1, N_MOD * D)

    w_layer = w_in
    w_br_attn_bf = w_br_attn.astype(BF16)
    w_br_conv_bf = w_br_conv.astype(BF16)
    w_br_ret_bf = w_br_ret.astype(BF16)
    w_out_bf = w_out.astype(BF16)
    w_router_pad = jnp.pad(w_router, ((0, 0), (0, 0), (0, 128 - N_EXPERTS))).astype(BF16)
    norm1 = norm1_w.reshape(DEPTH, 1, D)
    norm2 = norm2_w.reshape(DEPTH, 1, D)
    subln = attn_subln_w.reshape(DEPTH, 1, HD)
    decay_f = ret_decay_fwd.reshape(DEPTH * HEADS, 1, 1)
    decay_b = ret_decay_bwd.reshape(DEPTH * HEADS, 1, 1)
    ret_nw = ret_norm_w.reshape(DEPTH * HEADS, 1, HD)
    cos, sin = _rope_tables()
    tmat = _index_table()

    new_k = new_v = new_sf = new_sb = None
    for l in range(DEPTH):
        lam_init = 0.8 - 0.6 * math.exp(-0.3 * l)
        zb, zq, new_k, new_v = _inproj(l, x, mod, norm1, w_layer, new_k, new_v)
        ao, ro, new_sf, new_sb = _mix_ctx(l, lam_init, zb, attn_lambda, subln, decay_f, decay_b, ret_nw,
                                          new_sf, new_sb)
        ao_lat, ro_lat = _mix_lat(l, lam_init, zb, zq, cache_attn_k, cache_attn_v, cos, sin, attn_lambda, subln,
                                  state_ret_fwd, state_ret_bwd, decay_f, decay_b, ret_nw)
        x, h2s, aff_n = _merge(l, x, ao, ao_lat, ro, ro_lat, zb, mod, conv_w, w_br_attn_bf, w_br_conv_bf,
                               w_br_ret_bf, w_out_bf, norm2, w_router_pad)
        idx, gates = _topk(aff_n, tmat)
        idx_flat = idx.reshape(2 * N_EXPERTS * CAP)
        ye, w_layer = _expert_ffn(l, idx_flat, h2s, gates, w_exp_gate, w_exp_up, w_exp_down, w_in)
        x = _combine(l, idx_flat, ye, x, mod, final_norm_w)

    y_prompt, y_sample = x
    return (y_prompt.reshape(N_CTX_SEQ, T_CTX, D), y_sample.reshape(N_LAT_SEQ, T_LAT, D), new_k, new_v,
            new_sf, new_sb)
```

```python
import functools
import math

import jax
import jax.numpy as jnp
import numpy as np
from jax import lax
from jax.experimental import pallas as pl
from jax.experimental.pallas import tpu as pltpu

F32 = jnp.float32
BF16 = jnp.bfloat16

D = 1024
DEPTH = 2
N_CTX_SEQ = 16
T_CTX = 256
N_LAT_SEQ = 4
T_LAT = 1024
PAST = 256
N_CTX = N_CTX_SEQ * T_CTX
N_LAT = N_LAT_SEQ * T_LAT
N_TOK = N_CTX + N_LAT
TM = 256
N_TILES = N_TOK // TM
CTX_TILES = N_CTX // TM
LAT_TILES_PER_SEQ = T_LAT // TM
HEADS = 4
HD = 128
GRID_W = 64
N_ROPE_FREQ = 16
ROPE_BASE = 10000.0
IN_COLS = 8192
N_MOD = 6
N_EXPERTS = 16
CAP = 512
FF = 2048
TF = 512
EPS = 1e-6
RET_SCALE = HD ** -0.5
ATTN_SCALE = 64 ** -0.5
LOG2_E = math.log2(math.e)
SLAB = D // 128
VMEM_LIMIT = 56 * 1024 * 1024

QB, KB, VB = 0, 4, 8
RQB, RKB, RVB, RGB = 24, 28, 32, 36
CONV_B_COL, CONV_U_COL = 3, 4
MERGE_GATE_COL = 5
MERGE_SUB = 2


def _sigmoid(x):
    return 0.5 * jnp.tanh(0.5 * x) + 0.5


def _log_sigmoid(x):
    return jnp.minimum(x, 0.0) - jnp.log(1.0 + jnp.exp(-jnp.abs(x)))


def _rms(x):
    return x * lax.rsqrt(jnp.mean(x * x, axis=-1, keepdims=True) + EPS)


def _dot(a, b):
    return jnp.dot(a, b, preferred_element_type=F32)


def _dot_nt(a, b):
    return lax.dot_general(a, b, (((1,), (1,)), ((), ())), preferred_element_type=F32)


def _dot_tn(a, b):
    return lax.dot_general(a, b, (((0,), (0,)), ((), ())), preferred_element_type=F32)


def _to_slabs(ref, x):
    n = x.shape[0]
    for s in range(SLAB):
        ref[pl.ds(s, n, stride=SLAB), :] = x[:, s * 128:(s + 1) * 128]


def _from_slabs(ref):
    n = ref.shape[0] // SLAB
    return jnp.concatenate([ref[pl.ds(s, n, stride=SLAB), :] for s in range(SLAB)], axis=-1)


def _mod_row(i):
    return jnp.where(i < CTX_TILES, 0, 1 + (i - CTX_TILES) // LAT_TILES_PER_SEQ)


def _params(n_axes):
    return pltpu.CompilerParams(
        dimension_semantics=("arbitrary",) * n_axes, vmem_limit_bytes=VMEM_LIMIT)


def _mod_kernel(c_ref, w_ref, b_ref, o_ref):
    c = c_ref[...]
    s = (c * _sigmoid(c)).astype(BF16)
    o_ref[...] = _dot(s, w_ref[...].astype(BF16)) + b_ref[...]


def _modulation(cvec, w_ada, b_ada):
    tn = 1024
    return pl.pallas_call(
        _mod_kernel,
        grid=(DEPTH, N_MOD * D // tn),
        in_specs=[
            pl.BlockSpec((8, D), lambda l, n: (0, 0)),
            pl.BlockSpec((None, D, tn), lambda l, n: (l, 0, n)),
            pl.BlockSpec((None, 1, tn), lambda l, n: (l, 0, n)),
        ],
        out_specs=pl.BlockSpec((None, 8, tn), lambda l, n: (l, 0, n)),
        out_shape=jax.ShapeDtypeStruct((DEPTH, 8, N_MOD * D), F32),
        compiler_params=_params(2),
        name="modulation",
    )(cvec, w_ada, b_ada.reshape(DEPTH, 1, N_MOD * D))


W_CHUNK = 1024


def _inproj_kernel(x_pair, n_prev, f32_w_layer, *refs):
    refs = list(refs)
    x_ref = refs.pop(0)
    if x_pair:
        xs_ref = refs.pop(0)
    mod_ref, n1_ref, w_ref = refs.pop(0), refs.pop(0), refs.pop(0)
    if n_prev:
        kp_ref, vp_ref = refs.pop(0), refs.pop(0)
    i = pl.program_id(0)
    if f32_w_layer is None:
        zb_ref, zq_ref, kn_ref, vn_ref = refs
    else:
        zb_ref, zq_ref, kn_ref, vn_ref, w_bf, stage, sems = refs

        def chunk_copy(c):
            return pltpu.make_async_copy(w_ref.at[f32_w_layer, :, pl.ds(c * W_CHUNK, W_CHUNK)],
                                         stage.at[c % 2], sems.at[c % 2])

        @pl.when(i == 0)
        def _():
            n_chunks = IN_COLS // W_CHUNK
            chunk_copy(0).start()
            for c in range(n_chunks):
                if c + 1 < n_chunks:
                    chunk_copy(c + 1).start()
                chunk_copy(c).wait()
                w_bf[:, c * W_CHUNK:(c + 1) * W_CHUNK] = stage[c % 2].astype(BF16)

        w_ref = w_bf

    def normalize():
        x = x_ref[...]
        if x_pair:
            x = jnp.where(i < N_TILES - CTX_TILES, xs_ref[...], x)
        m = mod_ref[0]
        shift1 = m[:, 0:D]
        scale1 = m[:, D:2 * D]
        return ((_rms(x) * n1_ref[...]) * (1.0 + scale1) + shift1).astype(BF16)

    def project(h):
        cw = 1024
        for c in range(IN_COLS // cw):
            z = _dot(h, w_ref[:, c * cw:(c + 1) * cw])
            if c == CONV_U_COL // 2:
                z = jnp.concatenate([z[:, 0:512] * z[:, 512:cw], z[:, 512:cw]], axis=-1)
            elif c == RGB // 8:
                g = z[:, 512:cw]
                z = jnp.concatenate([z[:, 0:512], g * _sigmoid(g)], axis=-1)
            elif c >= MERGE_GATE_COL:
                z = _sigmoid(z)
            if c == 0:
                zb_ref[:, 0:cw] = jnp.concatenate([z[:, 0:512] * (ATTN_SCALE * LOG2_E), z[:, 512:cw]],
                                                  axis=-1).astype(BF16)
            else:
                zb_ref[:, c * cw:(c + 1) * cw] = z.astype(BF16)
            if c == 0:
                zq_ref[...] = z
                for hh in range(HEADS):
                    kn_ref[n_prev, hh] = z[:, 512 + hh * HD:512 + (hh + 1) * HD]
            if c == 1:
                for hh in range(HEADS):
                    vn_ref[n_prev, hh] = z[:, hh * HD:(hh + 1) * HD]
        if n_prev:
            kn_ref[0:n_prev] = kp_ref[...]
            vn_ref[0:n_prev] = vp_ref[...]

    project(normalize())


def _inproj(l, x, mod, norm1_w, w, k_prev, v_prev):
    t = lambda i: (i + CTX_TILES) % N_TILES
    n_lat = N_TILES - CTX_TILES
    norm_tile = pl.BlockSpec((TM, D), lambda i: (t(i), 0))
    ctx_i = lambda i: jnp.maximum(i - n_lat, 0)
    kv_spec = lambda m: pl.BlockSpec((None, m, HEADS, T_CTX, HD), lambda i: (ctx_i(i), 0, 0, 0, 0))
    kv_shape = jax.ShapeDtypeStruct((N_CTX_SEQ, l + 1, HEADS, T_CTX, HD), F32)
    x_pair = isinstance(x, tuple)
    if x_pair:
        in_specs = [pl.BlockSpec((TM, D), lambda i: (jnp.maximum(i - n_lat, 0), 0)),
                    pl.BlockSpec((TM, D), lambda i: (jnp.minimum(i, n_lat - 1), 0))]
        args = list(x)
    else:
        in_specs = [norm_tile]
        args = [x]
    f32_w = w.dtype == F32
    in_specs += [
        pl.BlockSpec((1, 1, N_MOD * D), lambda i: (l * 8 + _mod_row(t(i)), 0, 0)),
        pl.BlockSpec((None, 1, D), lambda i: (l, 0, 0)),
        pl.BlockSpec(memory_space=pl.ANY) if f32_w else
        pl.BlockSpec((D, IN_COLS), lambda i: (0, 0), pipeline_mode=pl.Buffered(1)),
    ]
    args += [mod, norm1_w, w]
    scratch = [pltpu.VMEM((D, IN_COLS), BF16), pltpu.VMEM((2, D, W_CHUNK), F32),
               pltpu.SemaphoreType.DMA((2,))] if f32_w else []
    if l:
        in_specs += [kv_spec(l), kv_spec(l)]
        args += [k_prev, v_prev]
    out_specs = [pl.BlockSpec((TM, IN_COLS), lambda i: (t(i), 0)),
                 pl.BlockSpec((TM, 1024), lambda i: (t(i), 0)),
                 kv_spec(l + 1), kv_spec(l + 1)]
    out_shape = [jax.ShapeDtypeStruct((N_TOK, IN_COLS), BF16),
                 jax.ShapeDtypeStruct((N_TOK, 1024), F32),
                 kv_shape, kv_shape]
    return pl.pallas_call(
        functools.partial(_inproj_kernel, x_pair, l, l if f32_w else None),
        grid=(N_TILES,),
        in_specs=in_specs,
        out_specs=out_specs,
        out_shape=out_shape,
        scratch_shapes=scratch,
        compiler_params=_params(1),
        name="inproj",
    )(*args)


def _lambda(lam_ref, lam_init):
    lv = lam_ref[...]
    a = jnp.sum(lv[0:1] * lv[1:2], axis=-1, keepdims=True)
    b = jnp.sum(lv[2:3] * lv[3:4], axis=-1, keepdims=True)
    return jnp.exp(a) - jnp.exp(b) + lam_init


def _interleave(*stage_generators):
    live = list(stage_generators)
    while live:
        for g in list(live):
            try:
                next(g)
            except StopIteration:
                live.remove(g)


def _diff_attention(lam_init, qs, keys, vs, lam, sw, keys_transposed, o_ref, sls):
    lane = lax.broadcasted_iota(jnp.int32, qs[0].shape, 1)
    score = _dot if keys_transposed else _dot_nt
    maps = [(h, jnp.where(keep, q, jnp.zeros_like(q)))
            for h, q in enumerate(qs) for keep in (lane < 64, lane >= 64)]
    s = [score(qm, keys[h]) for h, qm in maps]
    yield
    e = [jnp.exp2(x - jnp.max(x, axis=-1, keepdims=True)) for x in s]
    yield
    r = [1.0 / jnp.sum(x, axis=-1, keepdims=True) for x in e]
    a = [(e[2 * h] * r[2 * h] - e[2 * h + 1] * (lam * r[2 * h + 1])).astype(BF16) for h in range(len(qs))]
    yield
    o = [_dot(a[h], vs[h]) for h in range(len(qs))]
    yield
    for sl, x in zip(sls, o):
        o_ref[:, sl] = ((_rms(x) * sw) * (1.0 - lam_init)).astype(BF16)


def _scaled_q(q):
    return (q * (ATTN_SCALE * LOG2_E)).astype(BF16)


def _attn_ctx_stages(lam_init, q_ref, k_ref, v_ref, lam_ref, sw_ref, o_ref):
    lam = _lambda(lam_ref, lam_init)
    sls = [slice(h * HD, (h + 1) * HD) for h in range(HEADS)]
    yield from _diff_attention(lam_init, [q_ref[:, sl] for sl in sls], [k_ref[:, sl] for sl in sls],
                               [v_ref[:, sl] for sl in sls], lam, sw_ref[...], False, o_ref, sls)


def _rope(x, cos, sin_signed):
    lane = lax.broadcasted_iota(jnp.int32, x.shape, 1)
    partner = jnp.where(lane % 32 < 16, pltpu.roll(x, 112, 1), pltpu.roll(x, 16, 1))
    return x * cos + partner * sin_signed


def _attn_lat_stages(lam_init, q_ref, k_ref, v_ref, ck_ref, cv_ref, cos_ref, sin_ref,
                     cosq_ref, sinq_ref, lam_ref, sw_ref, o_ref, k_t, vall):
    @pl.when(pl.program_id(1) == 0)
    def _():
        for h in range(HEADS):
            sl = slice(h * HD, (h + 1) * HD)
            k_t[h, :, 0:PAST] = ck_ref[h].T.astype(BF16)
            k_t[h, :, PAST:] = _rope(k_ref[:, sl], cos_ref[...], sin_ref[...]).T.astype(BF16)
            vall[h, 0:PAST, :] = cv_ref[h].astype(BF16)
            vall[h, PAST:, :] = v_ref[:, sl]

    lam = _lambda(lam_ref, lam_init)
    sls = [slice(h * HD, (h + 1) * HD) for h in range(HEADS)]
    qs = [_scaled_q(_rope(q_ref[:, sl], cosq_ref[...], sinq_ref[...])) for sl in sls]
    yield from _diff_attention(lam_init, qs, [k_t[h] for h in range(HEADS)], [vall[h] for h in range(HEADS)],
                               lam, sw_ref[...], True, o_ref, sls)


def _mix_ctx(l, lam_init, zb, attn_lambda, subln_w, decay_f, decay_b, ret_norm_w, sf_prev, sb_prev):
    width = HEADS * HD
    col = lambda c: pl.BlockSpec((T_CTX, width), lambda b: (b, c // HEADS))
    dec = pl.BlockSpec((HEADS, 1, 1), lambda b: (l, 0, 0))
    st_spec = lambda n: pl.BlockSpec((None, n, HEADS, HD, HD), lambda b: (b, 0, 0, 0, 0))
    st_shape = jax.ShapeDtypeStruct((N_CTX_SEQ, l + 1, HEADS, HD, HD), F32)
    out_tile = pl.BlockSpec((T_CTX, width), lambda b: (b, 0))
    out_shape = jax.ShapeDtypeStruct((N_CTX, width), BF16)
    in_specs = [
        col(QB), col(KB), col(VB),
        pl.BlockSpec((None, 4, 64), lambda b: (l, 0, 0)),
        pl.BlockSpec((None, 1, HD), lambda b: (l, 0, 0)),
        col(RQB), col(RKB), col(RVB), col(RGB),
        dec, dec, pl.BlockSpec((HEADS, 1, HD), lambda b: (l, 0, 0)),
    ]
    args = [zb, zb, zb, attn_lambda, subln_w, zb, zb, zb, zb, decay_f, decay_b, ret_norm_w]
    if l:
        in_specs += [st_spec(l), st_spec(l)]
        args += [sf_prev, sb_prev]
    return pl.pallas_call(
        functools.partial(_mix_ctx_kernel, lam_init, l),
        grid=(N_CTX_SEQ,),
        in_specs=in_specs,
        out_specs=[out_tile, out_tile, st_spec(l + 1), st_spec(l + 1)],
        out_shape=[out_shape, out_shape, st_shape, st_shape],
        scratch_shapes=[pltpu.VMEM((HEADS, T_CTX, T_CTX), F32)],
        compiler_params=_params(1),
        name="mix_ctx",
    )(*args)


def _decay_matrix(lgf, lgb, row0, tq, tk):
    i = row0 + lax.broadcasted_iota(jnp.int32, (tq, tk), 0)
    j = lax.broadcasted_iota(jnp.int32, (tq, tk), 1)
    d = (i - j).astype(F32)
    if row0 - (tk - 1) > 0:
        return jnp.exp(lgf * d)
    if row0 + (tq - 1) < 0:
        return jnp.exp(lgb * (-d))
    fwd = jnp.where(d >= 0.0, jnp.exp(lgf * jnp.maximum(d, 0.0)), 0.0)
    bwd = jnp.where(d <= 0.0, jnp.exp(lgb * jnp.maximum(-d, 0.0)), 0.0)
    return fwd + bwd


def _ret_finish(o, g, nw):
    return (g.astype(F32) * (_rms(o) * nw)).astype(BF16)


def _ret_ctx_stages(n_prev, q_ref, k_ref, v_ref, g_ref, df_ref, db_ref, nw_ref, o_ref, sf_ref, sb_ref, dmat):
    j = lax.broadcasted_iota(jnp.int32, (T_CTX, 1), 0).astype(F32)
    heads = range(HEADS)
    sls = [slice(h * HD, (h + 1) * HD) for h in heads]
    q = [q_ref[:, sl] for sl in sls]
    k = [k_ref[:, sl] for sl in sls]
    v = [v_ref[:, sl] for sl in sls]
    s = [(_dot_nt(q[h], k[h]) * dmat[h]).astype(BF16) for h in heads]
    yield
    o = [_dot(s[h], v[h]) for h in heads]
    yield
    lgf = [_log_sigmoid(df_ref[h]) for h in heads]
    lgb = [_log_sigmoid(db_ref[h]) for h in heads]
    kf = [k[h].astype(F32) * RET_SCALE for h in heads]
    kfw = [(kf[h] * jnp.exp(lgf[h] * (T_CTX - 1.0 - j))).astype(BF16) for h in heads]
    kbw = [(kf[h] * jnp.exp(lgb[h] * j)).astype(BF16) for h in heads]
    yield
    for h in heads:
        sf_ref[n_prev, h] = _dot_tn(kfw[h], v[h])
        sb_ref[n_prev, h] = _dot_tn(kbw[h], v[h])
    yield
    for h in heads:
        o_ref[:, sls[h]] = _ret_finish(o[h], g_ref[:, sls[h]], nw_ref[h])


def _mix_ctx_kernel(lam_init, n_prev, *refs):
    refs = list(refs)
    aq_ref, ak_ref, av_ref, lam_ref, sw_ref = refs[0:5]
    rq_ref, rk_ref, rv_ref, rg_ref, df_ref, db_ref, nw_ref = refs[5:12]
    refs = refs[12:]
    if n_prev:
        sfp_ref, sbp_ref = refs.pop(0), refs.pop(0)
    ao_ref, ro_ref, sf_ref, sb_ref, dmat = refs

    @pl.when(pl.program_id(0) == 0)
    def _():
        for h in range(HEADS):
            dmat[h] = RET_SCALE * _decay_matrix(_log_sigmoid(df_ref[h]), _log_sigmoid(db_ref[h]), 0, T_CTX, T_CTX)

    if n_prev:
        sf_ref[0:n_prev] = sfp_ref[...]
        sb_ref[0:n_prev] = sbp_ref[...]
    _interleave(
        _attn_ctx_stages(lam_init, aq_ref, ak_ref, av_ref, lam_ref, sw_ref, ao_ref),
        _ret_ctx_stages(n_prev, rq_ref, rk_ref, rv_ref, rg_ref, df_ref, db_ref, nw_ref, ro_ref, sf_ref, sb_ref, dmat))


N_DCHUNK = 2 * LAT_TILES_PER_SEQ - 1


def _ret_lat_stages(q_ref, k_ref, v_ref, g_ref, s0f_ref, s0b_ref, df_ref, db_ref, nw_ref, o_ref, strip, k_t):
    b = pl.program_id(0)
    j = pl.program_id(1)

    @pl.when(jnp.logical_and(b == 0, j == 0))
    def _():
        for h in range(HEADS):
            lgf = _log_sigmoid(df_ref[h])
            lgb = _log_sigmoid(db_ref[h])
            for c in range(N_DCHUNK):
                strip[h, c] = RET_SCALE * _decay_matrix(lgf, lgb, T_LAT - TM - c * TM, TM, TM)

    @pl.when(j == 0)
    def _():
        for h in range(HEADS):
            k_t[h] = k_ref[:, h * HD:(h + 1) * HD].astype(F32).T.astype(BF16)

    i = (j * TM + lax.broadcasted_iota(jnp.int32, (TM, 1), 0)).astype(F32)
    c0 = LAT_TILES_PER_SEQ - 1 - j
    heads = range(HEADS)
    sls = [slice(h * HD, (h + 1) * HD) for h in heads]
    q = [q_ref[:, sl] for sl in sls]
    dmat = [jnp.concatenate([strip[h, c0 + c] for c in range(LAT_TILES_PER_SEQ)], axis=-1) for h in heads]
    s = [(_dot(q[h], k_t[h]) * dmat[h]).astype(BF16) for h in heads]
    yield
    o = [_dot(s[h], v_ref[:, sls[h]]) for h in heads]
    yield
    of = [_dot(q[h], s0f_ref[h].astype(BF16)) for h in heads]
    ob = [_dot(q[h], s0b_ref[h].astype(BF16)) for h in heads]
    yield
    for h in heads:
        tot = o[h] + jnp.exp(_log_sigmoid(df_ref[h]) * (i + 1.0)) * of[h]
        tot = tot + jnp.exp(_log_sigmoid(db_ref[h]) * (T_LAT - i)) * ob[h]
        o_ref[:, sls[h]] = _ret_finish(tot, g_ref[:, sls[h]], nw_ref[h])


def _mix_lat_kernel(lam_init, *refs):
    attn_in, ret_in = refs[0:11], refs[11:20]
    ao_ref, ro_ref, ak_t, vall, strip, rk_t = refs[20:]
    _interleave(_attn_lat_stages(lam_init, *attn_in, ao_ref, ak_t, vall))
    _interleave(_ret_lat_stages(*ret_in, ro_ref, strip, rk_t))


def _mix_lat(l, lam_init, zb, zq, cache_k, cache_v, cos, sin, attn_lambda, subln_w,
             state_f, state_b, decay_f, decay_b, ret_norm_w):
    width = HEADS * HD
    lat_row = lambda b, j: CTX_TILES + LAT_TILES_PER_SEQ * b + j
    seq_row = lambda b: N_CTX // T_LAT + b
    q_tile = lambda c: pl.BlockSpec((TM, width), lambda b, j: (lat_row(b, j), c // HEADS))
    seq = lambda c: pl.BlockSpec((T_LAT, width), lambda b, j: (seq_row(b), c // HEADS))
    cache = pl.BlockSpec((None, None, HEADS, PAST, HD), lambda b, j: (b, l, 0, 0, 0))
    state = pl.BlockSpec((None, None, HEADS, HD, HD), lambda b, j: (b, l, 0, 0, 0))
    dec = pl.BlockSpec((HEADS, 1, 1), lambda b, j: (l, 0, 0))
    out_tile = pl.BlockSpec((TM, width), lambda b, j: (LAT_TILES_PER_SEQ * b + j, 0))
    out_shape = jax.ShapeDtypeStruct((N_LAT, width), BF16)
    return pl.pallas_call(
        functools.partial(_mix_lat_kernel, lam_init),
        grid=(N_LAT_SEQ, LAT_TILES_PER_SEQ),
        in_specs=[
            q_tile(QB), seq(KB), seq(VB), cache, cache,
            pl.BlockSpec((T_LAT, HD), lambda b, j: (0, 0)),
            pl.BlockSpec((T_LAT, HD), lambda b, j: (0, 0)),
            pl.BlockSpec((TM, HD), lambda b, j: (j, 0)),
            pl.BlockSpec((TM, HD), lambda b, j: (j, 0)),
            pl.BlockSpec((None, 4, 64), lambda b, j: (l, 0, 0)),
            pl.BlockSpec((None, 1, HD), lambda b, j: (l, 0, 0)),
            q_tile(RQB), seq(RKB), seq(RVB), q_tile(RGB), state, state, dec, dec,
            pl.BlockSpec((HEADS, 1, HD), lambda b, j: (l, 0, 0)),
        ],
        out_specs=[out_tile, out_tile],
        out_shape=[out_shape, out_shape],
        scratch_shapes=[pltpu.VMEM((HEADS, HD, PAST + T_LAT), BF16),
                        pltpu.VMEM((HEADS, PAST + T_LAT, HD), BF16),
                        pltpu.VMEM((HEADS, N_DCHUNK, TM, TM), F32),
                        pltpu.VMEM((HEADS, HD, T_LAT), BF16)],
        compiler_params=_params(2),
        name="mix_lat",
    )(zq, zq, zb, cache_k, cache_v, cos, sin, cos, sin, attn_lambda, subln_w,
      zb, zb, zb, zb, state_f, state_b, decay_f, decay_b, ret_norm_w)


def _merge_kernel(x_pair, *refs):
    refs = list(refs)
    x_ref = refs.pop(0)
    xs_ref = refs.pop(0) if x_pair else None
    (aoc_ref, aol_ref, roc_ref, rol_ref, cb_ref, u_ref, up_ref, un_ref,
     mg0_ref, mg1_ref, mg2_ref, mod_ref, cw_ref, wa_ref, wc_ref, wr_ref, wo_ref, n2_ref, wrt_ref,
     x1_ref, h2_ref, aff_ref) = refs
    i = pl.program_id(0)
    is_ctx = i < CTX_TILES // MERGE_SUB
    m = mod_ref[0]
    gate1 = m[:, 2 * D:3 * D]
    shift2 = m[:, 3 * D:4 * D]
    scale2 = m[:, 4 * D:5 * D]
    cw = cw_ref[...]
    u_all = u_ref[...].astype(F32)
    r = lax.broadcasted_iota(jnp.int32, (TM, 512), 0)
    lane = lax.broadcasted_iota(jnp.int32, (TM, 128), 1)

    subs = range(MERGE_SUB)
    rows = [slice(sub * TM, (sub + 1) * TM) for sub in subs]

    def conv_out(sub):
        j = (i * MERGE_SUB + sub - CTX_TILES) % LAT_TILES_PER_SEQ
        seq_first = jnp.logical_or(is_ctx, j == 0)
        seq_last = jnp.logical_or(is_ctx, j == LAT_TILES_PER_SEQ - 1)
        u = u_all[rows[sub], :]
        up = up_ref[...].astype(F32)[15:16, :] if sub == 0 else u_all[sub * TM - 1:sub * TM, :]
        dn = un_ref[...].astype(F32)[0:1, :] if sub == MERGE_SUB - 1 else u_all[(sub + 1) * TM:(sub + 1) * TM + 1, :]
        up = up * jnp.where(seq_first, 0.0, 1.0)
        dn = dn * jnp.where(seq_last, 0.0, 1.0)
        u_prev = jnp.where(r == 0, up, pltpu.roll(u, 1, 0))
        u_next = jnp.where(r == TM - 1, dn, pltpu.roll(u, TM - 1, 0))
        conv = u_prev * cw[0:1, :] + u * cw[1:2, :] + u_next * cw[2:3, :]
        return (cb_ref[rows[sub], :].astype(F32) * conv).astype(BF16)

    ao = [jnp.where(is_ctx, aoc_ref[rw, :], aol_ref[rw, :]) for rw in rows]
    ro = [jnp.where(is_ctx, roc_ref[rw, :], rol_ref[rw, :]) for rw in rows]
    conv_o = [conv_out(sub) for sub in subs]
    b_attn = [_dot(ao[s], wa_ref[...]) for s in subs]
    b_conv = [_dot(conv_o[s], wc_ref[...]) for s in subs]
    b_ret = [_dot(ro[s], wr_ref[...]) for s in subs]
    merged = [(mg0_ref[rows[s], :].astype(F32) * b_attn[s] + mg1_ref[rows[s], :].astype(F32) * b_conv[s]
               + mg2_ref[rows[s], :].astype(F32) * b_ret[s]).astype(BF16) for s in subs]
    proj = [_dot(merged[s], wo_ref[...]) for s in subs]
    x1 = []
    for s in subs:
        x_in = x_ref[rows[s], :]
        if x_pair:
            x_in = jnp.where(is_ctx, x_in, xs_ref[rows[s], :])
        x1.append(x_in + gate1 * proj[s])
        x1_ref[rows[s], :] = x1[s]
    h2 = [(_rms(x1[s]) * n2_ref[...]) * (1.0 + scale2) + shift2 for s in subs]
    logits = [_dot(h2[s].astype(BF16), wrt_ref[...]) for s in subs]
    valid = lane < N_EXPERTS
    for s in subs:
        _to_slabs(h2_ref.at[pl.ds(s * TM * SLAB, TM * SLAB), :], h2[s])
        lmax = jnp.max(jnp.where(valid, logits[s], -jnp.inf), axis=-1, keepdims=True)
        e = jnp.where(valid, jnp.exp(logits[s] - lmax), 0.0)
        aff_ref[rows[s], :] = e * (1.0 / jnp.sum(e, axis=-1, keepdims=True))


def _merge(l, x, ao, ao_lat, ro, ro_lat, zb, mod, conv_w, w_br_attn, w_br_conv, w_br_ret, w_out, norm2_w,
           w_router_pad):
    mt = MERGE_SUB * TM
    n_steps = N_TOK // mt
    ctx_steps = N_CTX // mt
    n16 = N_TOK // 16
    ctx_br = pl.BlockSpec((mt, 512), lambda i: (jnp.minimum(i, ctx_steps - 1), 0))
    lat_br = pl.BlockSpec((mt, 512), lambda i: (jnp.maximum(i - ctx_steps, 0), 0))
    col = lambda c: pl.BlockSpec((mt, 512), lambda i: (i, c))
    halo_p = lambda c: pl.BlockSpec((16, 512), lambda i: (jnp.maximum(i * (mt // 16) - 1, 0), c))
    halo_n = lambda c: pl.BlockSpec((16, 512), lambda i: (jnp.minimum((i + 1) * (mt // 16), n16 - 1), c))
    mgs = lambda c: pl.BlockSpec((mt, D), lambda i: (i, c))
    wbr = pl.BlockSpec((None, 512, D), lambda i: (l, 0, 0))
    x_pair = isinstance(x, tuple)
    if x_pair:
        x_specs = [pl.BlockSpec((mt, D), lambda i: (jnp.minimum(i, ctx_steps - 1), 0)),
                   pl.BlockSpec((mt, D), lambda i: (jnp.maximum(i - ctx_steps, 0), 0))]
        x_args = list(x)
    else:
        x_specs = [pl.BlockSpec((mt, D), lambda i: (i, 0))]
        x_args = [x]
    return pl.pallas_call(
        functools.partial(_merge_kernel, x_pair),
        grid=(n_steps,),
        in_specs=x_specs + [
            ctx_br, lat_br, ctx_br, lat_br,
            col(CONV_B_COL), col(CONV_U_COL), halo_p(CONV_U_COL), halo_n(CONV_U_COL),
            mgs(MERGE_GATE_COL), mgs(MERGE_GATE_COL + 1), mgs(MERGE_GATE_COL + 2),
            pl.BlockSpec((1, 1, N_MOD * D), lambda i: (l * 8 + _mod_row(i * MERGE_SUB), 0, 0)),
            pl.BlockSpec((None, 3, 512), lambda i: (l, 0, 0)),
            wbr, wbr, wbr,
            pl.BlockSpec((None, D, D), lambda i: (l, 0, 0)),
            pl.BlockSpec((None, 1, D), lambda i: (l, 0, 0)),
            pl.BlockSpec((None, D, 128), lambda i: (l, 0, 0)),
        ],
        out_specs=[
            pl.BlockSpec((mt, D), lambda i: (i, 0)),
            pl.BlockSpec((mt * SLAB, 128), lambda i: (i, 0)),
            pl.BlockSpec((mt, 128), lambda i: (i, 0)),
        ],
        out_shape=[
            jax.ShapeDtypeStruct((N_TOK, D), F32),
            jax.ShapeDtypeStruct((N_TOK * SLAB, 128), F32),
            jax.ShapeDtypeStruct((N_TOK, 128), F32),
        ],
        compiler_params=_params(1),
        name="merge",
    )(*x_args, ao, ao_lat, ro, ro_lat, zb, zb, zb, zb, zb, zb, zb, mod, conv_w,
      w_br_attn, w_br_conv, w_br_ret, w_out, norm2_w, w_router_pad)


def _cumsum_lanes(x, tri):
    run = jnp.zeros((x.shape[0], 1), F32)
    outs = []
    for b in range(x.shape[1] // 128):
        cs = _dot(x[:, b * 128:(b + 1) * 128].astype(BF16), tri) + run
        run = cs[:, 127:128]
        outs.append(cs)
    return jnp.concatenate(outs, axis=-1)


GATE_LANE = (0, 16, 32)
IDX_LANE = 48
BISECT_GROUP = 4


def _topk_kernel(an_ref, tmat_ref, idx_ref, gate_ref):
    an = an_ref[...]
    a = an.T[0:N_EXPERTS, :]
    n = a.shape[1]
    kf = float(CAP)

    hi = an.astype(BF16).astype(F32)
    mid = (an - hi).astype(BF16).astype(F32)
    lo = ((an - hi) - mid).astype(BF16).astype(F32)
    table = (tmat_ref[...].astype(F32) + hi + pltpu.roll(mid, GATE_LANE[1], 1)
             + pltpu.roll(lo, GATE_LANE[2], 1)).astype(BF16)

    def count_gt(thr):
        return jnp.sum(jnp.where(a > thr, 1.0, 0.0), axis=-1, keepdims=True)

    def span(lo, hi):
        inside = jnp.logical_and(a > lo, a <= hi)
        cmax = jnp.max(jnp.where(inside, a, -jnp.inf), axis=-1, keepdims=True)
        cmin = jnp.min(jnp.where(inside, a, jnp.inf), axis=-1, keepdims=True)
        return cmax, cmin

    def cond(c):
        return jnp.logical_and(c[2] > 0, c[3] < 400)

    def body(c):
        lo, hi, _, it = c
        for _ in range(BISECT_GROUP):
            mid = 0.5 * (lo + hi)
            ge = count_gt(mid) >= kf
            lo = jnp.where(ge, mid, lo)
            hi = jnp.where(ge, hi, mid)
        cmax, cmin = span(lo, hi)
        open_rows = jnp.max(jnp.where(cmax != cmin, 1, 0))
        return lo, hi, open_rows, it + 1

    lo0 = jnp.full((N_EXPERTS, 1), -1.0, F32)
    hi0 = jnp.max(a, axis=-1, keepdims=True)
    cmax0, cmin0 = span(lo0, hi0)
    lo, hi, _, _ = lax.while_loop(
        cond, body, (lo0, hi0, jnp.max(jnp.where(cmax0 != cmin0, 1, 0)), jnp.int32(0)))
    thr, _ = span(lo, hi)

    r = lax.broadcasted_iota(jnp.int32, (128, 128), 0)
    c = lax.broadcasted_iota(jnp.int32, (128, 128), 1)
    tri = jnp.where(r <= c, 1.0, 0.0).astype(BF16)
    gt = a > thr
    eq = jnp.where(a == thr, 1.0, 0.0)
    need = kf - count_gt(thr)
    eq_before = _cumsum_lanes(eq, tri) - eq
    sel = jnp.where(jnp.logical_or(gt, jnp.logical_and(eq > 0.0, eq_before < need)), 1.0, 0.0)
    pos = _cumsum_lanes(sel, tri) - 1.0
    slot = jnp.where(sel > 0.0, pos, -1.0).astype(jnp.int32)

    p_iota = lax.broadcasted_iota(jnp.int32, (CAP, 1024), 0)
    lane = lax.broadcasted_iota(jnp.int32, (CAP, 128), 1)
    for e in range(N_EXPERTS):
        acc = jnp.zeros((CAP, 128), F32)
        for cb in range(n // 1024):
            onehot = jnp.where(p_iota == slot[e:e + 1, cb * 1024:(cb + 1) * 1024], 1.0, 0.0).astype(BF16)
            acc = acc + _dot(onehot, table[cb * 1024:(cb + 1) * 1024, :])
        acc_t = acc.T
        idx_ref[0, e] = (acc_t[IDX_LANE:IDX_LANE + 1, :] * 64.0 + acc_t[IDX_LANE + 1:IDX_LANE + 2, :]).astype(jnp.int32)
        g = jnp.zeros((CAP, 1), F32)
        for off in GATE_LANE:
            g = g + jnp.sum(jnp.where(lane == off + e, acc, 0.0), axis=-1, keepdims=True)
        gate_ref[0, e] = g


def _topk(aff_n, tmat):
    out_spec = pl.BlockSpec((1, N_EXPERTS, CAP, 1), lambda s: (s, 0, 0, 0))
    return pl.pallas_call(
        _topk_kernel,
        grid=(2,),
        in_specs=[
            pl.BlockSpec((N_CTX, 128), lambda s: (s, 0)),
            pl.BlockSpec((N_CTX, 128), lambda s: (0, 0)),
        ],
        out_specs=[pl.BlockSpec((1, N_EXPERTS, 1, CAP), lambda s: (s, 0, 0, 0)), out_spec],
        out_shape=[jax.ShapeDtypeStruct((2, N_EXPERTS, 1, CAP), jnp.int32),
                   jax.ShapeDtypeStruct((2, N_EXPERTS, CAP, 1), F32)],
        compiler_params=_params(1),
        name="topk",
    )(aff_n, tmat)


N_FT = FF // TF
GATHER_ROWS = 2 * CAP
ROWS_PER_STEP = GATHER_ROWS // N_FT


def _ffn_kernel(cast_next, idx_ref, h2_hbm, g_ref, wg_ref, wu_ref, wd_ref, *refs):
    if cast_next:
        win_ref, ye_ref, wbf_ref, xg, xb, acc, sems = refs
        wbf_ref[...] = win_ref[...].astype(BF16)
    else:
        ye_ref, xg, xb, acc, sems = refs
    e = pl.program_id(0)
    f = pl.program_id(1)
    slot = e % 2

    def row_copy(expert, s, p, dst_slot):
        row = idx_ref[(s * N_EXPERTS + expert) * CAP + p] + s * N_CTX
        src = h2_hbm.at[pl.ds(pl.multiple_of(row * SLAB, SLAB), SLAB), :]
        dst = xg.at[dst_slot, pl.ds(pl.multiple_of((s * CAP + p) * SLAB, SLAB), SLAB), :]
        return pltpu.make_async_copy(src, dst, sems.at[dst_slot])

    def slot_wait(dst_slot):
        pltpu.make_async_copy(h2_hbm.at[pl.ds(0, GATHER_ROWS * SLAB), :], xg.at[dst_slot],
                              sems.at[dst_slot]).wait()

    @pl.when(jnp.logical_and(e == 0, f == 0))
    def _():
        for r in range(GATHER_ROWS):
            row_copy(0, r // CAP, r % CAP, 0).start(priority=r % 2)
        acc[...] = jnp.zeros_like(acc)

    @pl.when(f == 0)
    def _():
        slot_wait(slot)
        xb[...] = _from_slabs(xg.at[slot]).astype(BF16)

    nxt = jnp.minimum(e + 1, N_EXPERTS - 1)
    s_nxt = f // (N_FT // 2)
    p0 = (f % (N_FT // 2)) * ROWS_PER_STEP
    for u in range(ROWS_PER_STEP):
        row_copy(nxt, s_nxt, p0 + u, 1 - slot).start(priority=u % 2)

    x = xb[...]
    hg = _dot(x, wg_ref[...].astype(BF16))
    hu = _dot(x, wu_ref[...].astype(BF16))
    hdn = ((hg * _sigmoid(hg)) * hu).astype(BF16)
    acc[...] = jnp.where(f == 0, 0.0, acc[...]) + _dot(hdn, wd_ref[...].astype(BF16))

    @pl.when(f == N_FT - 1)
    def _():
        for s in range(2):
            _to_slabs(ye_ref.at[s], acc[s * CAP:(s + 1) * CAP, :] * g_ref[s])

    @pl.when(jnp.logical_and(e == N_EXPERTS - 1, f == N_FT - 1))
    def _():
        slot_wait(1 - slot)


def _expert_ffn(l, idx_flat, h2s, gates, w_gate, w_up, w_down, w_in):
    cast_next = l + 1 < DEPTH
    n_steps = N_EXPERTS * N_FT
    in_specs = [
        pl.BlockSpec(memory_space=pl.ANY),
        pl.BlockSpec((2, None, CAP, 1), lambda e, f, idx: (0, e, 0, 0)),
        pl.BlockSpec((None, None, D, TF), lambda e, f, idx: (l, e, 0, f)),
        pl.BlockSpec((None, None, D, TF), lambda e, f, idx: (l, e, 0, f)),
        pl.BlockSpec((None, None, TF, D), lambda e, f, idx: (l, e, f, 0)),
    ]
    args = [idx_flat, h2s, gates, w_gate, w_up, w_down]
    out_specs = [pl.BlockSpec((2, None, CAP * SLAB, 128), lambda e, f, idx: (0, e, 0, 0))]
    out_shape = [jax.ShapeDtypeStruct((2, N_EXPERTS, CAP * SLAB, 128), F32)]
    if cast_next:
        cols = IN_COLS // n_steps
        in_specs.append(pl.BlockSpec((None, D, cols), lambda e, f, idx: (l + 1, 0, e * N_FT + f)))
        args.append(w_in)
        out_specs.append(pl.BlockSpec((D, cols), lambda e, f, idx: (0, e * N_FT + f)))
        out_shape.append(jax.ShapeDtypeStruct((D, IN_COLS), BF16))
    grid_spec = pltpu.PrefetchScalarGridSpec(
        num_scalar_prefetch=1,
        grid=(N_EXPERTS, N_FT),
        in_specs=in_specs,
        out_specs=out_specs,
        scratch_shapes=[
            pltpu.VMEM((2, GATHER_ROWS * SLAB, 128), F32),
            pltpu.VMEM((GATHER_ROWS, D), BF16),
            pltpu.VMEM((GATHER_ROWS, D), F32),
            pltpu.SemaphoreType.DMA((2,)),
        ],
    )
    outs = pl.pallas_call(
        functools.partial(_ffn_kernel, cast_next),
        grid_spec=grid_spec,
        out_shape=out_shape,
        compiler_params=_params(2),
        name="expert_ffn",
    )(*args)
    return (outs[0], outs[1]) if cast_next else (outs[0], None)


SCATTER_UNROLL = 16
SCATTER_EXPERTS = 4


N_SCATTER_STEPS = N_EXPERTS // SCATTER_EXPERTS
RESID_ROWS = 2 * TM
N_RESID_STEPS = N_CTX // RESID_ROWS


def _combine_kernel(l, last, idx_ref, ye_ref, x_ref, mod_ref, *refs):
    if last:
        w_ref, op_ref, os_ref, y_acc = refs
    else:
        o_ref, y_acc = refs
    s = pl.program_id(0)
    g = pl.program_id(1)

    @pl.when(g == 0)
    def _():
        y_acc[...] = jnp.zeros_like(y_acc)

    @pl.when(g < N_SCATTER_STEPS)
    def _():
        for k in range(SCATTER_EXPERTS):
            base = (s * N_EXPERTS + g * SCATTER_EXPERTS + k) * CAP
            for p0 in range(0, CAP, SCATTER_UNROLL):
                rows = [idx_ref[base + p0 + u] for u in range(SCATTER_UNROLL)]
                tiles = [pl.ds(pl.multiple_of(r * SLAB, SLAB), SLAB) for r in rows]
                vals = [y_acc[tiles[u], :] + ye_ref[k, pl.ds((p0 + u) * SLAB, SLAB), :]
                        for u in range(SCATTER_UNROLL)]
                for u in range(SCATTER_UNROLL):
                    y_acc[tiles[u], :] = vals[u]

    @pl.when(g >= N_SCATTER_STEPS)
    def _():
        c = g - N_SCATTER_STEPS
        row = jnp.where(s == 0, 0, 1 + c // (T_LAT // RESID_ROWS))
        gate2 = mod_ref[l * 8 + row][:, 5 * D:6 * D]
        y = _from_slabs(y_acc.at[pl.ds(pl.multiple_of(c * RESID_ROWS * SLAB, RESID_ROWS * SLAB), RESID_ROWS * SLAB), :])
        x = x_ref[...] + gate2 * y
        if last:
            out = _rms(x) * w_ref[...]

            @pl.when(s == 0)
            def _():
                op_ref[...] = out

            @pl.when(s == 1)
            def _():
                os_ref[...] = out
        else:
            o_ref[...] = x


def _combine(l, idx_flat, ye, x1, mod, final_norm_w):
    last = l == DEPTH - 1
    resid = lambda g: jnp.maximum(g - N_SCATTER_STEPS, 0)
    in_specs = [
        pl.BlockSpec((None, SCATTER_EXPERTS, CAP * SLAB, 128),
                     lambda s, g, idx: (s, jnp.minimum(g, N_SCATTER_STEPS - 1), 0, 0)),
        pl.BlockSpec((RESID_ROWS, D), lambda s, g, idx: (s * N_RESID_STEPS + resid(g), 0)),
        pl.BlockSpec((DEPTH * 8, 1, N_MOD * D), lambda s, g, idx: (0, 0, 0)),
    ]
    args = [idx_flat, ye, x1, mod]
    if last:
        in_specs.append(pl.BlockSpec((1, D), lambda s, g, idx: (0, 0)))
        args.append(final_norm_w.reshape(1, D))
        out_specs = [
            pl.BlockSpec((RESID_ROWS, D), lambda s, g, idx: (jnp.where(s == 0, resid(g), N_RESID_STEPS - 1), 0)),
            pl.BlockSpec((RESID_ROWS, D), lambda s, g, idx: (jnp.where(s == 1, resid(g), 0), 0)),
        ]
        out_shape = [jax.ShapeDtypeStruct((N_CTX, D), F32), jax.ShapeDtypeStruct((N_LAT, D), F32)]
    else:
        out_specs = pl.BlockSpec((RESID_ROWS, D), lambda s, g, idx: (s * N_RESID_STEPS + resid(g), 0))
        out_shape = jax.ShapeDtypeStruct((N_TOK, D), F32)
    grid_spec = pltpu.PrefetchScalarGridSpec(
        num_scalar_prefetch=1,
        grid=(2, N_SCATTER_STEPS + N_RESID_STEPS),
        in_specs=in_specs,
        out_specs=out_specs,
        scratch_shapes=[pltpu.VMEM((N_CTX * SLAB, 128), F32)],
    )
    return pl.pallas_call(
        functools.partial(_combine_kernel, l, last),
        grid_spec=grid_spec,
        out_shape=out_shape,
        compiler_params=_params(2),
        name="combine",
    )(*args)


def _rope_tables():
    t = np.arange(T_LAT)
    row = (t // GRID_W).astype(np.float32)
    col = (t % GRID_W).astype(np.float32)
    inv = jnp.asarray(ROPE_BASE, F32) ** (-jnp.arange(N_ROPE_FREQ, dtype=F32) / N_ROPE_FREQ)
    ang_r = jnp.asarray(row)[:, None] * inv
    ang_c = jnp.asarray(col)[:, None] * inv
    def group(ang):
        return jnp.concatenate([ang, ang], axis=-1)
    ang = jnp.concatenate([group(ang_r), group(ang_c), group(ang_r), group(ang_c)], axis=-1)
    sign = np.where(np.arange(HD) % 32 < 16, -1.0, 1.0).astype(np.float32)
    return jnp.cos(ang), jnp.sin(ang) * sign


def _index_table():
    t = np.arange(N_CTX)
    tm = np.zeros((N_CTX, 128), np.float32)
    tm[:, IDX_LANE] = t // 64
    tm[:, IDX_LANE + 1] = t % 64
    return jnp.asarray(tm, BF16)


def kernel(x_prompt, x_sample, c, cache_attn_k, cache_attn_v, state_ret_fwd, state_ret_bwd, c_ctx, w_ada, b_ada, norm1_w, norm2_w, w_in, attn_lambda, attn_subln_w, conv_w, ret_decay_fwd, ret_decay_bwd, ret_norm_w, w_br_attn, w_br_conv, w_br_ret, w_out, w_router, w_exp_gate, w_exp_up, w_exp_down, final_norm_w):
    x = (x_prompt.reshape(N_CTX, D), x_sample.reshape(N_LAT, D))
    cvec = jnp.concatenate([c_ctx[None, :], c, jnp.zeros((3, D), F32)], axis=0)
    mod = _modulation(cvec, w_ada, b_ada).reshape(DEPTH * 8, 1, N_MOD * D)

    w_layer = w_in
    w_br_attn_bf = w_br_attn.astype(BF16)
    w_br_conv_bf = w_br_conv.astype(BF16)
    w_br_ret_bf = w_br_ret.astype(BF16)
    w_out_bf = w_out.astype(BF16)
    w_router_pad = jnp.pad(w_router, ((0, 0), (0, 0), (0, 128 - N_EXPERTS))).astype(BF16)
    norm1 = norm1_w.reshape(DEPTH, 1, D)
    norm2 = norm2_w.reshape(DEPTH, 1, D)
    subln = attn_subln_w.reshape(DEPTH, 1, HD)
    decay_f = ret_decay_fwd.reshape(DEPTH * HEADS, 1, 1)
    decay_b = ret_decay_bwd.reshape(DEPTH * HEADS, 1, 1)
    ret_nw = ret_norm_w.reshape(DEPTH * HEADS, 1, HD)
    cos, sin = _rope_tables()
    tmat = _index_table()

    new_k = new_v = new_sf = new_sb = None
    for l in range(DEPTH):
        lam_init = 0.8 - 0.6 * math.exp(-0.3 * l)
        zb, zq, new_k, new_v = _inproj(l, x, mod, norm1, w_layer, new_k, new_v)
        ao, ro, new_sf, new_sb = _mix_ctx(l, lam_init, zb, attn_lambda, subln, decay_f, decay_b, ret_nw,
                                          new_sf, new_sb)
        ao_lat, ro_lat = _mix_lat(l, lam_init, zb, zq, cache_attn_k, cache_attn_v, cos, sin, attn_lambda, subln,
                                  state_ret_fwd, state_ret_bwd, decay_f, decay_b, ret_nw)
        x, h2s, aff_n = _merge(l, x, ao, ao_lat, ro, ro_lat, zb, mod, conv_w, w_br_attn_bf, w_br_conv_bf,
                               w_br_ret_bf, w_out_bf, norm2, w_router_pad)
        idx, gates = _topk(aff_n, tmat)
        idx_flat = idx.reshape(2 * N_EXPERTS * CAP)
        ye, w_layer = _expert_ffn(l, idx_flat, h2s, gates, w_exp_gate, w_exp_up, w_exp_down, w_in)
        x = _combine(l, idx_flat, ye, x, mod, final_norm_w)

    y_prompt, y_sample = x
    return (y_prompt.reshape(N_CTX_SEQ, T_CTX, D), y_sample.reshape(N_LAT_SEQ, T_LAT, D), new_k, new_v,
            new_sf, new_sb)
```
